```python
import jax, jax.numpy as jnp
from jax import lax
import numpy as np

D_MODEL = 1024
BATCH = 16
SEQ = 256
DEPTH = 2
DEC_BATCH = 4
DEC_SEQ = 2048
PAST_LEN = 512

GRID_W = 64
N_HEADS = 8
N_KV_HEADS = 2
HEAD_DIM = 64
GQA_GROUP = N_HEADS // N_KV_HEADS
ATTN_WIDTH = N_HEADS * HEAD_DIM
KV_WIDTH = N_KV_HEADS * HEAD_DIM
WINDOW = 128
BLOCK = 128
ROPE_BASE = 10000.0
CONV_WIDTH = 512
CHUNK = 128
GMLP_WIDTH = 1024
GMLP_GROUPS = 8
GMLP_GROUP_DIM = GMLP_WIDTH // GMLP_GROUPS
D_FF = 2816
EPS = 1e-6
NEG_INF = -1e30

N_EVEN = (DEPTH + 1) // 2
N_ODD = DEPTH // 2
N_ATTN_LAYERS = N_EVEN
EVEN_IN_WIDTH = ATTN_WIDTH + 2 * KV_WIDTH + 3 * CONV_WIDTH

kernel_name = 'hybrid_prefix_diffusion_step'


def rms_norm(x, g):
    xf = x.astype(jnp.float32)
    y = xf * lax.rsqrt(jnp.mean(xf * xf, axis=-1, keepdims=True) + EPS)
    return (y * g.astype(jnp.float32)).astype(x.dtype)


def dwconv3(x, w):
    xp = jnp.pad(x, ((0, 0), (1, 1), (0, 0)))
    return xp[:, :-2] * w[0] + xp[:, 1:-1] * w[1] + xp[:, 2:] * w[2]


def grid_angles(n_tokens):
    rows = n_tokens // GRID_W
    row = jnp.repeat(jnp.arange(rows), GRID_W).astype(jnp.float32)
    col = jnp.tile(jnp.arange(GRID_W), rows).astype(jnp.float32)
    n_freq = HEAD_DIM // 4
    inv = ROPE_BASE ** (-jnp.arange(n_freq, dtype=jnp.float32) / n_freq)
    return row[:, None] * inv, col[:, None] * inv


def rope_half(x, ang):
    n = ang.shape[-1]
    cos = jnp.cos(ang)[None, :, None, :]
    sin = jnp.sin(ang)[None, :, None, :]
    x1 = x[..., :n].astype(jnp.float32)
    x2 = x[..., n:].astype(jnp.float32)
    return jnp.concatenate([x1 * cos - x2 * sin, x2 * cos + x1 * sin], axis=-1).astype(x.dtype)


def axial_rope(x, row_ang, col_ang):
    half = HEAD_DIM // 2
    return jnp.concatenate([rope_half(x[..., :half], row_ang), rope_half(x[..., half:], col_ang)], axis=-1)


def ctx_self_attention(q, k, v, sink):
    B, S = q.shape[0], q.shape[1]
    nb = S // BLOCK
    qb = jnp.moveaxis(q.reshape(B, nb, BLOCK, N_KV_HEADS, GQA_GROUP, HEAD_DIM), 1, 0)
    sink_b = sink.astype(jnp.float32).reshape(1, N_KV_HEADS, GQA_GROUP, 1, 1)
    scale = HEAD_DIM ** -0.5

    def one_block(qi):
        s = jnp.einsum('bqhgd,bkhd->bhgqk', qi, k).astype(jnp.float32) * scale
        s = jnp.concatenate([jnp.broadcast_to(sink_b, s.shape[:-1] + (1,)), s], axis=-1)
        p = jax.nn.softmax(s, axis=-1)[..., 1:].astype(v.dtype)
        return jnp.einsum('bhgqk,bkhd->bqhgd', p, v)

    o = lax.map(one_block, qb)
    return jnp.moveaxis(o, 0, 1).reshape(B, S, ATTN_WIDTH)


def latent_window_attention(q, k, v, kc, vc, sink):
    B, L = q.shape[0], q.shape[1]
    nb = L // BLOCK
    scale = HEAD_DIM ** -0.5
    qb = q.reshape(B, nb, BLOCK, N_KV_HEADS, GQA_GROUP, HEAD_DIM)
    pad = ((0, 0), (BLOCK, BLOCK), (0, 0), (0, 0))
    idx = jnp.arange(nb)[:, None] * BLOCK + jnp.arange(3 * BLOCK)[None, :]
    kw = jnp.pad(k, pad)[:, idx]
    vw = jnp.pad(v, pad)[:, idx]
    qpos = jnp.arange(nb)[:, None, None] * BLOCK + jnp.arange(BLOCK)[None, :, None]
    kpos = idx[:, None, :] - BLOCK
    valid = (jnp.abs(qpos - kpos) <= WINDOW) & (kpos >= 0) & (kpos < L)
    s_loc = jnp.einsum('bnqhgd,bnkhd->bnhgqk', qb, kw).astype(jnp.float32) * scale
    s_loc = jnp.where(valid[None, :, None, None], s_loc, NEG_INF)
    s_ctx = jnp.einsum('bnqhgd,bkhd->bnhgqk', qb, kc).astype(jnp.float32) * scale
    sink_b = jnp.broadcast_to(sink.astype(jnp.float32).reshape(1, 1, N_KV_HEADS, GQA_GROUP, 1, 1),
                              s_loc.shape[:-1] + (1,))
    p = jax.nn.softmax(jnp.concatenate([sink_b, s_loc, s_ctx], axis=-1), axis=-1)
    p_loc = p[..., 1:1 + 3 * BLOCK].astype(v.dtype)
    p_ctx = p[..., 1 + 3 * BLOCK:].astype(v.dtype)
    o = (jnp.einsum('bnhgqk,bnkhd->bnqhgd', p_loc, vw)
         + jnp.einsum('bnhgqk,bkhd->bnqhgd', p_ctx, vc))
    return o.reshape(B, L, ATTN_WIDTH)


def even_projections(h, w_in, q_g, k_g):
    B, T = h.shape[0], h.shape[1]
    z = h @ w_in
    cuts = np.cumsum([ATTN_WIDTH, KV_WIDTH, KV_WIDTH, CONV_WIDTH, CONV_WIDTH]).tolist()
    q, k, v, b_gate, c_gate, hv = jnp.split(z, cuts, axis=-1)
    q = rms_norm(q.reshape(B, T, N_HEADS, HEAD_DIM), q_g)
    k = rms_norm(k.reshape(B, T, N_KV_HEADS, HEAD_DIM), k_g)
    v = v.reshape(B, T, N_KV_HEADS, HEAD_DIM)
    return q, k, v, b_gate, c_gate, hv


def short_conv(b_gate, c_gate, hv, w):
    return b_gate * dwconv3(c_gate * hv, w)


def gmlp_mixer(h, w_in, v_g, w_s, b_s, w_out):
    B, T = h.shape[0], h.shape[1]
    nc = T // CHUNK
    u, v = jnp.split(jax.nn.gelu(h @ w_in), 2, axis=-1)
    v = rms_norm(v, v_g).reshape(B, nc, CHUNK, GMLP_GROUPS, GMLP_GROUP_DIM)
    s = jnp.einsum('gts,bnsgc->bntgc', w_s, v) + b_s.T[:, :, None]
    return (u * s.reshape(B, T, GMLP_WIDTH)) @ w_out


def conv_ffn(h, w_up, conv_w, w_down):
    z = dwconv3(h @ w_up, conv_w)
    g, val = jnp.split(z, 2, axis=-1)
    return (jax.nn.silu(g) * val) @ w_down


def adaln(cond, w, b):
    m = jax.nn.silu(cond) @ w + b
    return [t[:, None, :] for t in jnp.split(m, 6, axis=-1)]


def modulate(x, g, shift, scale):
    return rms_norm(x, g) * (1 + scale) + shift


def setup_inputs(seed: int = 0) -> dict:
    key = jax.random.key(seed)
    ks = jax.random.split(key, 24)

    def nrm(k, shape, scale):
        return jax.random.normal(k, shape, jnp.float32) * scale

    D = D_MODEL
    return {
        'x_prompt': nrm(ks[0], (BATCH, SEQ, D), 1.0),
        'x_sample': nrm(ks[1], (DEC_BATCH, DEC_SEQ, D), 1.0),
        'cache_k': nrm(ks[2], (DEC_BATCH, N_ATTN_LAYERS, PAST_LEN, N_KV_HEADS, HEAD_DIM), 1.0),
        'cache_v': nrm(ks[3], (DEC_BATCH, N_ATTN_LAYERS, PAST_LEN, N_KV_HEADS, HEAD_DIM), 1.0),
        'c': nrm(ks[4], (DEC_BATCH, D), 1.0),
        'c_ctx': nrm(ks[5], (D,), 1.0),
        'ada_w': nrm(ks[6], (DEPTH, D, 6 * D), 0.5 * D ** -0.5),
        'ada_b': nrm(ks[7], (DEPTH, 6 * D), 0.02),
        'norm_mix_g': 1.0 + nrm(ks[8], (DEPTH, D), 0.05),
        'norm_ffn_g': 1.0 + nrm(ks[9], (DEPTH, D), 0.05),
        'w_in_even': nrm(ks[10], (N_EVEN, D, EVEN_IN_WIDTH), D ** -0.5),
        'q_norm_g': 1.0 + nrm(ks[11], (N_EVEN, HEAD_DIM), 0.05),
        'k_norm_g': 1.0 + nrm(ks[12], (N_EVEN, HEAD_DIM), 0.05),
        'sink_logit': nrm(ks[13], (N_EVEN, N_HEADS), 0.5),
        'short_conv_w': nrm(ks[14], (N_EVEN, 3, CONV_WIDTH), 3 ** -0.5),
        'w_out_even': nrm(ks[15], (N_EVEN, ATTN_WIDTH + CONV_WIDTH, D), (ATTN_WIDTH + CONV_WIDTH) ** -0.5),
        'w_in_odd': nrm(ks[16], (N_ODD, D, 2 * GMLP_WIDTH), D ** -0.5),
        'gmlp_norm_g': 1.0 + nrm(ks[17], (N_ODD, GMLP_WIDTH), 0.05),
        'w_spatial': nrm(ks[18], (N_ODD, GMLP_GROUPS, CHUNK, CHUNK), 0.5 * CHUNK ** -0.5),
        'b_spatial': 1.0 + nrm(ks[19], (N_ODD, GMLP_GROUPS, CHUNK), 0.1),
        'w_out_odd': nrm(ks[20], (N_ODD, GMLP_WIDTH, D), GMLP_WIDTH ** -0.5),
        'w_up': nrm(ks[21], (DEPTH, D, 2 * D_FF), D ** -0.5),
        'ffn_conv_w': nrm(ks[22], (DEPTH, 3, 2 * D_FF), 3 ** -0.5),
        'w_down': nrm(ks[23], (DEPTH, D_FF, D), D_FF ** -0.5),
    }


def reference(x_prompt, x_sample, cache_k, cache_v, c, c_ctx, ada_w, ada_b, norm_mix_g, norm_ffn_g,
              w_in_even, q_norm_g, k_norm_g, sink_logit, short_conv_w, w_out_even,
              w_in_odd, gmlp_norm_g, w_spatial, b_spatial, w_out_odd, w_up, ffn_conv_w, w_down):
    row_ang, col_ang = grid_angles(x_sample.shape[1])
    cond_ctx = c_ctx[None, :]
    xp, xs = x_prompt, x_sample
    new_k, new_v = [], []
    for l in range(DEPTH):
        shp, scp, gtp, shp2, scp2, gtp2 = adaln(cond_ctx, ada_w[l], ada_b[l])
        shs, scs, gts, shs2, scs2, gts2 = adaln(c, ada_w[l], ada_b[l])
        hp = modulate(xp, norm_mix_g[l], shp, scp)
        hs = modulate(xs, norm_mix_g[l], shs, scs)
        if l % 2 == 0:
            e = l // 2
            qp, kp, vp, bp, cp, up = even_projections(hp, w_in_even[e], q_norm_g[e], k_norm_g[e])
            mix_p = jnp.concatenate([ctx_self_attention(qp, kp, vp, sink_logit[e]),
                                     short_conv(bp, cp, up, short_conv_w[e])], axis=-1) @ w_out_even[e]
            new_k.append(kp)
            new_v.append(vp)
            qs, ks_, vs, bs, cs, us = even_projections(hs, w_in_even[e], q_norm_g[e], k_norm_g[e])
            qs = axial_rope(qs, row_ang, col_ang)
            ks_ = axial_rope(ks_, row_ang, col_ang)
            attn_s = latent_window_attention(qs, ks_, vs, cache_k[:, e], cache_v[:, e], sink_logit[e])
            mix_s = jnp.concatenate([attn_s, short_conv(bs, cs, us, short_conv_w[e])], axis=-1) @ w_out_even[e]
        else:
            o = l // 2
            mix_p = gmlp_mixer(hp, w_in_odd[o], gmlp_norm_g[o], w_spatial[o], b_spatial[o], w_out_odd[o])
            mix_s = gmlp_mixer(hs, w_in_odd[o], gmlp_norm_g[o], w_spatial[o], b_spatial[o], w_out_odd[o])
        xp = xp + gtp * mix_p
        xs = xs + gts * mix_s
        xp = xp + gtp2 * conv_ffn(modulate(xp, norm_ffn_g[l], shp2, scp2), w_up[l], ffn_conv_w[l], w_down[l])
        xs = xs + gts2 * conv_ffn(modulate(xs, norm_ffn_g[l], shs2, scs2), w_up[l], ffn_conv_w[l], w_down[l])
    new_cache_k = jnp.stack(new_k, axis=1)
    new_cache_v = jnp.stack(new_v, axis=1)
    return (xp, xs, new_cache_k, new_cache_v)
```

```python
import functools

import jax
import jax.numpy as jnp
from jax import lax
from jax.experimental import pallas as pl
from jax.experimental.pallas import tpu as pltpu

F32 = jnp.float32
BF16 = jnp.bfloat16

D_MODEL = 1024
BATCH = 16
SEQ = 256
DEPTH = 2
DEC_BATCH = 4
DEC_SEQ = 2048
PAST_LEN = 512
GRID_W = 64
N_HEADS = 8
N_KV_HEADS = 2
HEAD_DIM = 64
GQA_GROUP = N_HEADS // N_KV_HEADS
ATTN_WIDTH = N_HEADS * HEAD_DIM
KV_WIDTH = N_KV_HEADS * HEAD_DIM
WINDOW = 128
BLOCK = 128
ROPE_BASE = 10000.0
CONV_WIDTH = 512
CHUNK = 128
GMLP_WIDTH = 1024
GMLP_GROUPS = 8
GMLP_GROUP_DIM = GMLP_WIDTH // GMLP_GROUPS
D_FF = 2816
EPS = 1e-6
NEG_INF = -1e30

LANES = 128
SUBLANES_F32 = 8
SUBLANES_BF16 = 16
MXU_WIDTH = 256
VMEM_LIMIT_BYTES = 56 * 1024 * 1024

COND_ROWS = 8
CTX_ROW = DEC_BATCH
FFN_HALO = SUBLANES_BF16
FFN_COLS = MXU_WIDTH
ADA_TN = 1536


def _params(n_axes):
    return pltpu.CompilerParams(dimension_semantics=("parallel",) * n_axes,
                                vmem_limit_bytes=VMEM_LIMIT_BYTES)


def _resident(shape):
    zeros = (0,) * len(shape)
    return pl.BlockSpec(shape, lambda *_: zeros, pipeline_mode=pl.Buffered(1))


def _dot(a, b):
    return jnp.dot(a, b, preferred_element_type=F32)


def _modulate(x, g, shift, scale):
    ms = jnp.mean(x * x, axis=-1, keepdims=True)
    return (x * lax.rsqrt(ms + EPS) * g) * (1.0 + scale) + shift


def _mod_chunk(mod_ref, k):
    return mod_ref[0, :, k * D_MODEL:(k + 1) * D_MODEL]


def _adaln_kernel(cond_ref, w_ref, b_ref, o_ref):
    a = jax.nn.silu(cond_ref[...]).astype(BF16)
    o_ref[0] = _dot(a, w_ref[0].astype(BF16)) + b_ref[0]


def _adaln(cond, ada_w, ada_b):
    n_out = 6 * D_MODEL
    return pl.pallas_call(
        _adaln_kernel,
        grid=(DEPTH, n_out // ADA_TN),
        in_specs=[pl.BlockSpec((COND_ROWS, D_MODEL), lambda l, j: (0, 0)),
                  pl.BlockSpec((1, D_MODEL, ADA_TN), lambda l, j: (l, 0, j)),
                  pl.BlockSpec((1, 1, ADA_TN), lambda l, j: (l, 0, j))],
        out_specs=pl.BlockSpec((1, COND_ROWS, ADA_TN), lambda l, j: (l, 0, j)),
        out_shape=jax.ShapeDtypeStruct((DEPTH, COND_ROWS, n_out), F32),
        compiler_params=_params(2),
        name="adaln",
    )(cond, ada_w, ada_b.reshape(DEPTH, 1, n_out))


def _head_rms(z, gain):
    n = z.shape[1]
    w = min(n, MXU_WIDTH)
    r = lax.broadcasted_iota(jnp.int32, (w, w), 0) // HEAD_DIM
    c = lax.broadcasted_iota(jnp.int32, (w, w), 1) // HEAD_DIM
    ones = (r == c).astype(BF16)
    sq = (z * z).astype(BF16)
    ss = jnp.concatenate([_dot(sq[:, k:k + w], ones) for k in range(0, n, w)], axis=1)
    return z * lax.rsqrt(ss * (1.0 / HEAD_DIM) + EPS) * gain


def _rope(z, cos, sin_lo, sin_hi):
    outs = []
    for k in range(0, z.shape[1], LANES):
        blk = z[:, k:k + LANES]
        outs.append(blk * cos + pltpu.roll(blk, LANES - 16, 1) * sin_lo + pltpu.roll(blk, 16, 1) * sin_hi)
    return jnp.concatenate(outs, axis=1)


def _in_even_kernel(*refs, rope):
    if rope:
        (x_ref, mod_ref, g_ref, w_ref, qg_ref, kg_ref, cos_ref, slo_ref, shi_ref,
         q_ref, k_ref, v_ref, b_ref, ch_ref) = refs
    else:
        x_ref, mod_ref, g_ref, w_ref, qg_ref, kg_ref, q_ref, k_ref, v_ref, b_ref, ch_ref = refs
    h = _modulate(x_ref[...], g_ref[...], _mod_chunk(mod_ref, 0), _mod_chunk(mod_ref, 1)).astype(BF16)
    c0 = 0
    c1 = c0 + ATTN_WIDTH
    c2 = c1 + 2 * KV_WIDTH
    c3 = c2 + CONV_WIDTH
    c4 = c3 + CONV_WIDTH
    c5 = c4 + CONV_WIDTH
    q = _head_rms(_dot(h, w_ref[:, c0:c1]), qg_ref[...] * HEAD_DIM ** -0.5)
    zkv = _dot(h, w_ref[:, c1:c2])
    k = _head_rms(zkv[:, :KV_WIDTH], kg_ref[...])
    if rope:
        cos, slo, shi = cos_ref[...], slo_ref[...], shi_ref[...]
        q = _rope(q, cos, slo, shi)
        k = _rope(k, cos, slo, shi)
    q_ref[...] = q
    k_ref[...] = k
    v_ref[...] = zkv[:, KV_WIDTH:]
    b_ref[...] = _dot(h, w_ref[:, c2:c3])
    ch_ref[...] = _dot(h, w_ref[:, c3:c4]) * _dot(h, w_ref[:, c4:c5])


def _in_even(x, mod_l, mod_row, g, w_bf, qg, kg, rope_tabs, tm, seq_len):
    T = x.shape[0]
    n_in = w_bf.shape[1]
    tiles_per_seq = seq_len // tm
    tok = lambda w: pl.BlockSpec((tm, w), lambda i: (i, 0))
    in_specs = [tok(D_MODEL),
                pl.BlockSpec((1, 1, 6 * D_MODEL), lambda i: (mod_row(i), 0, 0)),
                _resident((1, D_MODEL)), _resident((D_MODEL, n_in)),
                _resident((1, ATTN_WIDTH)), _resident((1, KV_WIDTH))]
    args = [x, mod_l, g, w_bf, qg, kg]
    if rope_tabs is not None:
        in_specs += [pl.BlockSpec((tm, LANES), lambda i: (i % tiles_per_seq, 0))] * 3
        args += list(rope_tabs)
    widths = (ATTN_WIDTH, KV_WIDTH, KV_WIDTH, CONV_WIDTH, CONV_WIDTH)
    return pl.pallas_call(
        functools.partial(_in_even_kernel, rope=rope_tabs is not None),
        grid=(T // tm,),
        in_specs=in_specs,
        out_specs=[tok(w) for w in widths],
        out_shape=[jax.ShapeDtypeStruct((T, w), F32) for w in widths],
        compiler_params=_params(1),
        name="in_even",
    )(*args)


def _attend(q, kcat, vcat, valid, sink_ref):
    tq = q.shape[0]
    lane_lo = lax.broadcasted_iota(jnp.int32, (tq, LANES), 1) < HEAD_DIM
    pairs = []
    for pair in range(N_HEADS // 2):
        kv = (2 * pair) // GQA_GROUP
        qp = q[:, pair * LANES:(pair + 1) * LANES]
        qr = pltpu.roll(qp, HEAD_DIM, 1)
        halves = []
        for half in range(2):
            src = qp if half == kv else qr
            qh = jnp.where(lane_lo, src, 0.0) if kv == 0 else jnp.where(lane_lo, 0.0, src)
            s = lax.dot_general(qh.astype(BF16), kcat, (((1,), (1,)), ((), ())), preferred_element_type=F32)
            if valid is not None:
                s = jnp.where(valid, s, NEG_INF)
            sink = sink_ref[2 * pair + half]
            m = jnp.maximum(jnp.max(s, axis=-1, keepdims=True), sink)
            p = jnp.exp(s - m)
            denom = jnp.sum(p, axis=-1, keepdims=True) + jnp.exp(sink - m)
            o = _dot(p.astype(BF16), vcat) / denom
            halves.append(o if half == kv else pltpu.roll(o, HEAD_DIM, 1))
        pairs.append(jnp.where(lane_lo, halves[0], halves[1]))
    return jnp.concatenate(pairs, axis=1)


def _attn_ctx_kernel(sink_ref, q_ref, k_ref, v_ref, o_ref):
    o_ref[...] = _attend(q_ref[...], k_ref[...].astype(BF16), v_ref[...].astype(BF16), None,
                         sink_ref).astype(BF16)


def _attn_ctx(q, k, v, sink):
    T = q.shape[0]
    tok = lambda w: pl.BlockSpec((SEQ, w), lambda i: (i, 0))
    return pl.pallas_call(
        _attn_ctx_kernel,
        grid=(T // SEQ,),
        in_specs=[pl.BlockSpec(memory_space=pltpu.SMEM), tok(ATTN_WIDTH), tok(KV_WIDTH), tok(KV_WIDTH)],
        out_specs=tok(ATTN_WIDTH),
        out_shape=jax.ShapeDtypeStruct((T, ATTN_WIDTH), BF16),
        compiler_params=_params(1),
        name="attn_ctx",
    )(sink, q, k, v)


def _attn_win_kernel(sink_ref, q_ref, kp_ref, kc_ref, kn_ref, vp_ref, vc_ref, vn_ref, ck_ref, cv_ref, o_ref,
                     *, n_blocks):
    i = pl.program_id(1)
    kcat = jnp.concatenate([kp_ref[...], kc_ref[...], kn_ref[...], ck_ref[0]], axis=0).astype(BF16)
    vcat = jnp.concatenate([vp_ref[...], vc_ref[...], vn_ref[...], cv_ref[0]], axis=0).astype(BF16)
    nk = kcat.shape[0]
    r = lax.broadcasted_iota(jnp.int32, (BLOCK, nk), 0)
    c = lax.broadcasted_iota(jnp.int32, (BLOCK, nk), 1)
    first_prev = r + jnp.where(i > 0, 0, BLOCK)
    last_next = r + 2 * BLOCK - jnp.where(i < n_blocks - 1, 0, BLOCK)
    masked = ((c < BLOCK) & (c < first_prev)) | ((c >= 2 * BLOCK) & (c < 3 * BLOCK) & (c > last_next))
    valid = jnp.logical_not(masked)
    o_ref[...] = _attend(q_ref[...], kcat, vcat, valid, sink_ref).astype(BF16)


def _attn_win(q, k, v, ck, cv, sink):
    T = q.shape[0]
    nb = DEC_SEQ // BLOCK
    blk = lambda w, f: pl.BlockSpec((BLOCK, w), lambda b, i: (b * nb + f(i), 0))
    prev = lambda i: jnp.maximum(i - 1, 0)
    cur = lambda i: i
    nxt = lambda i: jnp.minimum(i + 1, nb - 1)
    ctx = pl.BlockSpec((1, PAST_LEN, KV_WIDTH), lambda b, i: (b, 0, 0))
    return pl.pallas_call(
        functools.partial(_attn_win_kernel, n_blocks=nb),
        grid=(T // DEC_SEQ, nb),
        in_specs=[pl.BlockSpec(memory_space=pltpu.SMEM), blk(ATTN_WIDTH, cur),
                  blk(KV_WIDTH, prev), blk(KV_WIDTH, cur), blk(KV_WIDTH, nxt),
                  blk(KV_WIDTH, prev), blk(KV_WIDTH, cur), blk(KV_WIDTH, nxt), ctx, ctx],
        out_specs=blk(ATTN_WIDTH, cur),
        out_shape=jax.ShapeDtypeStruct((T, ATTN_WIDTH), BF16),
        compiler_params=_params(2),
        name="attn_win",
    )(sink, q, k, k, k, v, v, v, ck, cv)


def _mix_out_kernel(attn_ref, b_ref, ch_ref, chp_ref, chn_ref, x_ref, mod_ref, cw_ref, w_ref, o_ref,
                    *, tiles_per_seq):
    tm = x_ref.shape[0]
    pos = pl.program_id(0) % tiles_per_seq
    ch = ch_ref[...]
    prev_row = jnp.where(pos > 0, chp_ref[SUBLANES_F32 - 1:SUBLANES_F32, :], 0.0)
    next_row = jnp.where(pos < tiles_per_seq - 1, chn_ref[0:1, :], 0.0)
    row = lax.broadcasted_iota(jnp.int32, ch.shape, 0)
    dn = jnp.where(row == 0, prev_row, pltpu.roll(ch, 1, 0))
    up = jnp.where(row == tm - 1, next_row, pltpu.roll(ch, tm - 1, 0))
    conv = dn * cw_ref[0:1, :] + ch * cw_ref[1:2, :] + up * cw_ref[2:3, :]
    sc = (b_ref[...] * conv).astype(BF16)
    mix = _dot(attn_ref[...], w_ref[:ATTN_WIDTH, :]) + _dot(sc, w_ref[ATTN_WIDTH:, :])
    o_ref[...] = x_ref[...] + _mod_chunk(mod_ref, 2) * mix


def _mix_out(attn, b, ch, x, mod_l, mod_row, conv_w, w_bf, tm, seq_len):
    T = x.shape[0]
    tiles_per_seq = seq_len // tm
    rows = tm // SUBLANES_F32
    n_halo = T // SUBLANES_F32
    tok = lambda w: pl.BlockSpec((tm, w), lambda i: (i, 0))
    halo_prev = pl.BlockSpec((SUBLANES_F32, CONV_WIDTH), lambda i: (jnp.maximum(i * rows - 1, 0), 0))
    halo_next = pl.BlockSpec((SUBLANES_F32, CONV_WIDTH), lambda i: (jnp.minimum((i + 1) * rows, n_halo - 1), 0))
    return pl.pallas_call(
        functools.partial(_mix_out_kernel, tiles_per_seq=tiles_per_seq),
        grid=(T // tm,),
        in_specs=[tok(ATTN_WIDTH), tok(CONV_WIDTH), tok(CONV_WIDTH), halo_prev, halo_next, tok(D_MODEL),
                  pl.BlockSpec((1, 1, 6 * D_MODEL), lambda i: (mod_row(i), 0, 0)),
                  _resident((3, CONV_WIDTH)), _resident((ATTN_WIDTH + CONV_WIDTH, D_MODEL))],
        out_specs=tok(D_MODEL),
        out_shape=jax.ShapeDtypeStruct((T, D_MODEL), F32),
        compiler_params=_params(1),
        name="mix_out",
    )(attn, b, ch, ch, ch, x, mod_l, conv_w, w_bf)


def _gelu_tanh(x):
    return 0.5 * x * (1.0 + jnp.tanh(0.7978845608028654 * (x + 0.044715 * (x * x * x))))


def _gmlp_kernel(x_ref, mod_ref, g_ref, win_ref, vg_ref, ws_ref, bs_ref, wout_ref, o_ref, gated_scr):
    x = x_ref[...]
    n_chunks = x.shape[0] // CHUNK
    h = _modulate(x, g_ref[...], _mod_chunk(mod_ref, 0), _mod_chunk(mod_ref, 1)).astype(BF16)
    u = _gelu_tanh(_dot(h, win_ref[:, :GMLP_WIDTH]))
    v = _gelu_tanh(_dot(h, win_ref[:, GMLP_WIDTH:]))
    ms = jnp.mean(v * v, axis=-1, keepdims=True)
    v = (v * lax.rsqrt(ms + EPS) * vg_ref[...]).astype(BF16)
    for grp in range(GMLP_GROUPS):
        lanes = slice(grp * GMLP_GROUP_DIM, (grp + 1) * GMLP_GROUP_DIM)
        rhs = jnp.concatenate([v[n * CHUNK:(n + 1) * CHUNK, lanes] for n in range(n_chunks)], axis=1)
        s = _dot(ws_ref[grp], rhs)
        for n in range(n_chunks):
            rows = slice(n * CHUNK, (n + 1) * CHUNK)
            s_n = s[:, n * GMLP_GROUP_DIM:(n + 1) * GMLP_GROUP_DIM] + bs_ref[grp]
            gated_scr[rows, lanes] = (u[rows, lanes] * s_n).astype(BF16)
    o_ref[...] = x + _mod_chunk(mod_ref, 2) * _dot(gated_scr[...], wout_ref[...])


def _gmlp(x, mod_l, mod_row, g, win_bf, vg, ws_bf, bs_full, wout_bf, tm):
    T = x.shape[0]
    tok = pl.BlockSpec((tm, D_MODEL), lambda i: (i, 0))
    return pl.pallas_call(
        _gmlp_kernel,
        grid=(T // tm,),
        in_specs=[tok, pl.BlockSpec((1, 1, 6 * D_MODEL), lambda i: (mod_row(i), 0, 0)),
                  _resident((1, D_MODEL)), _resident((D_MODEL, 2 * GMLP_WIDTH)), _resident((1, GMLP_WIDTH)),
                  _resident((GMLP_GROUPS, CHUNK, CHUNK)), _resident((GMLP_GROUPS, CHUNK, GMLP_GROUP_DIM)),
                  _resident((GMLP_WIDTH, D_MODEL))],
        out_specs=tok,
        out_shape=jax.ShapeDtypeStruct((T, D_MODEL), F32),
        scratch_shapes=[pltpu.VMEM((tm, GMLP_WIDTH), BF16)],
        compiler_params=_params(1),
        name="gmlp",
    )(x, mod_l, g, win_bf, vg, ws_bf, bs_full, wout_bf)


def _ffn_kernel(xp_ref, x_ref, xn_ref, mod_ref, g_ref, wup_ref, cw_ref, wdn_ref, o_ref, h_scr, act_scr,
                *, tiles_per_seq):
    tm = x_ref.shape[0]
    rows = tm + 2 * FFN_HALO
    pos = pl.program_id(0) % tiles_per_seq
    g, shift, scale = g_ref[...], _mod_chunk(mod_ref, 3), _mod_chunk(mod_ref, 4)
    x = x_ref[...]
    hp = jnp.where(pos > 0, _modulate(xp_ref[...], g, shift, scale), 0.0)
    hn = jnp.where(pos < tiles_per_seq - 1, _modulate(xn_ref[...], g, shift, scale), 0.0)
    h_scr[:FFN_HALO, :] = hp.astype(BF16)
    h_scr[FFN_HALO:FFN_HALO + tm, :] = _modulate(x, g, shift, scale).astype(BF16)
    h_scr[FFN_HALO + tm:, :] = hn.astype(BF16)
    h = h_scr[...]
    mid = slice(FFN_HALO, FFN_HALO + tm)

    def conv(z, w):
        return (pltpu.roll(z, 1, 0)[mid] * w[0:1] + z[mid] * w[1:2] + pltpu.roll(z, rows - 1, 0)[mid] * w[2:3])

    for j in range(0, D_FF, FFN_COLS):
        gate_cols = slice(j, j + FFN_COLS)
        val_cols = slice(D_FF + j, D_FF + j + FFN_COLS)
        zg = conv(_dot(h, wup_ref[:, gate_cols]), cw_ref[:, gate_cols])
        zv = conv(_dot(h, wup_ref[:, val_cols]), cw_ref[:, val_cols])
        act_scr[:, gate_cols] = (jax.nn.silu(zg) * zv).astype(BF16)
    o_ref[...] = x + _mod_chunk(mod_ref, 5) * _dot(act_scr[...], wdn_ref[...])


def _ffn(x, mod_l, mod_row, g, wup_bf, conv_w, wdn_bf, tm, seq_len):
    T = x.shape[0]
    tiles_per_seq = seq_len // tm
    rows = tm // FFN_HALO
    n_halo = T // FFN_HALO
    tok = pl.BlockSpec((tm, D_MODEL), lambda i: (i, 0))
    halo_prev = pl.BlockSpec((FFN_HALO, D_MODEL), lambda i: (jnp.maximum(i * rows - 1, 0), 0))
    halo_next = pl.BlockSpec((FFN_HALO, D_MODEL), lambda i: (jnp.minimum((i + 1) * rows, n_halo - 1), 0))
    return pl.pallas_call(
        functools.partial(_ffn_kernel, tiles_per_seq=tiles_per_seq),
        grid=(T // tm,),
        in_specs=[halo_prev, tok, halo_next,
                  pl.BlockSpec((1, 1, 6 * D_MODEL), lambda i: (mod_row(i), 0, 0)),
                  _resident((1, D_MODEL)), _resident((D_MODEL, 2 * D_FF)), _resident((3, 2 * D_FF)),
                  _resident((D_FF, D_MODEL))],
        out_specs=tok,
        out_shape=jax.ShapeDtypeStruct((T, D_MODEL), F32),
        scratch_shapes=[pltpu.VMEM((tm + 2 * FFN_HALO, D_MODEL), BF16), pltpu.VMEM((tm, D_FF), BF16)],
        compiler_params=_params(1),
        name="conv_ffn",
    )(x, x, x, mod_l, g, wup_bf, conv_w, wdn_bf)


def _rope_tables(n_tokens):
    t = jnp.arange(n_tokens)
    n_freq = HEAD_DIM // 4
    inv = ROPE_BASE ** (-jnp.arange(n_freq, dtype=F32) / n_freq)
    row_ang = (t // GRID_W).astype(F32)[:, None] * inv
    col_ang = (t % GRID_W).astype(F32)[:, None] * inv
    ang = jnp.concatenate([row_ang, row_ang, col_ang, col_ang], axis=1)
    ang = jnp.tile(ang, (1, LANES // HEAD_DIM))
    first = (jnp.arange(LANES) % (2 * n_freq)) < n_freq
    cos, sin = jnp.cos(ang), jnp.sin(ang)
    return cos, jnp.where(first, -sin, 0.0), jnp.where(first, 0.0, sin)


def kernel(x_prompt, x_sample, cache_k, cache_v, c, c_ctx, ada_w, ada_b, norm_mix_g, norm_ffn_g, w_in_even,
           q_norm_g, k_norm_g, sink_logit, short_conv_w, w_out_even, w_in_odd, gmlp_norm_g, w_spatial,
           b_spatial, w_out_odd, w_up, ffn_conv_w, w_down):
    n_p, n_s = BATCH * SEQ, DEC_BATCH * DEC_SEQ
    xp = x_prompt.reshape(n_p, D_MODEL)
    xs = x_sample.reshape(n_s, D_MODEL)
    cond = jnp.concatenate([c, c_ctx[None, :], jnp.zeros((COND_ROWS - DEC_BATCH - 1, D_MODEL), F32)], axis=0)
    mod = _adaln(cond, ada_w, ada_b)
    rope_tabs = _rope_tables(DEC_SEQ)

    tm_p, tm_s = SEQ, 512
    row_p = lambda i: CTX_ROW
    row_s = lambda i: i // (DEC_SEQ // tm_s)
    row_vec = lambda a: a.reshape(1, -1)
    new_k, new_v = [], []
    for l in range(DEPTH):
        mod_l = mod[l].reshape(COND_ROWS, 1, 6 * D_MODEL)
        g_mix = row_vec(norm_mix_g[l])
        if l % 2 == 0:
            e = l // 2
            w_in = w_in_even[e].astype(BF16)
            w_out = w_out_even[e].astype(BF16)
            qg = row_vec(jnp.tile(q_norm_g[e], N_HEADS))
            kg = row_vec(jnp.tile(k_norm_g[e], N_KV_HEADS))
            sink = sink_logit[e]
            cw = short_conv_w[e]
            qp, kp, vp, bp, chp = _in_even(xp, mod_l, row_p, g_mix, w_in, qg, kg, None, tm_p, SEQ)
            new_k.append(kp.reshape(BATCH, SEQ, N_KV_HEADS, HEAD_DIM))
            new_v.append(vp.reshape(BATCH, SEQ, N_KV_HEADS, HEAD_DIM))
            attn_p = _attn_ctx(qp, kp, vp, sink)
            xp = _mix_out(attn_p, bp, chp, xp, mod_l, row_p, cw, w_out, tm_p, SEQ)
            qs, ks, vs, bs, chs = _in_even(xs, mod_l, row_s, g_mix, w_in, qg, kg, rope_tabs, tm_s, DEC_SEQ)
            ck = cache_k[:, e].reshape(DEC_BATCH, PAST_LEN, KV_WIDTH)
            cv = cache_v[:, e].reshape(DEC_BATCH, PAST_LEN, KV_WIDTH)
            attn_s = _attn_win(qs, ks, vs, ck, cv, sink)
            xs = _mix_out(attn_s, bs, chs, xs, mod_l, row_s, cw, w_out, tm_s, DEC_SEQ)
        else:
            o = l // 2
            w_in = w_in_odd[o].astype(BF16)
            w_out = w_out_odd[o].astype(BF16)
            vg = row_vec(gmlp_norm_g[o])
            ws = w_spatial[o].astype(BF16)
            bs_full = jnp.broadcast_to(b_spatial[o][:, :, None], (GMLP_GROUPS, CHUNK, GMLP_GROUP_DIM))
            xp = _gmlp(xp, mod_l, row_p, g_mix, w_in, vg, ws, bs_full, w_out, tm_p)
            xs = _gmlp(xs, mod_l, row_s, g_mix, w_in, vg, ws, bs_full, w_out, tm_s)
        g_ffn = row_vec(norm_ffn_g[l])
        wup = w_up[l].astype(BF16)
        wdn = w_down[l].astype(BF16)
        xp = _ffn(xp, mod_l, row_p, g_ffn, wup, ffn_conv_w[l], wdn, tm_p, SEQ)
        xs = _ffn(xs, mod_l, row_s, g_ffn, wup, ffn_conv_w[l], wdn, tm_s, DEC_SEQ)
    return (xp.reshape(BATCH, SEQ, D_MODEL), xs.reshape(DEC_BATCH, DEC_SEQ, D_MODEL),
            jnp.stack(new_k, axis=1), jnp.stack(new_v, axis=1))
```

```python
import functools

import jax
import jax.numpy as jnp
from jax import lax
from jax.experimental import pallas as pl
from jax.experimental.pallas import tpu as pltpu

F32 = jnp.float32
BF16 = jnp.bfloat16

D_MODEL = 1024
BATCH = 16
SEQ = 256
DEPTH = 2
DEC_BATCH = 4
DEC_SEQ = 2048
PAST_LEN = 512
GRID_W = 64
N_HEADS = 8
N_KV_HEADS = 2
HEAD_DIM = 64
GQA_GROUP = N_HEADS // N_KV_HEADS
ATTN_WIDTH = N_HEADS * HEAD_DIM
KV_WIDTH = N_KV_HEADS * HEAD_DIM
WINDOW = 128
BLOCK = 128
ROPE_BASE = 10000.0
CONV_WIDTH = 512
CHUNK = 128
GMLP_WIDTH = 1024
GMLP_GROUPS = 8
GMLP_GROUP_DIM = GMLP_WIDTH // GMLP_GROUPS
D_FF = 2816
EPS = 1e-6
NEG_INF = -1e30
LOG2_E = 1.4426950408889634

LANES = 128
SUBLANES_F32 = 8
SUBLANES_BF16 = 16
MXU_WIDTH = 256
VMEM_LIMIT_BYTES = 56 * 1024 * 1024

COND_ROWS = 8
CTX_ROW = DEC_BATCH
FFN_HALO = SUBLANES_BF16
FFN_COLS = MXU_WIDTH
ADA_TN = 1536
ATTN_Q_BLOCKS = 2
SUM_ROWS = SUBLANES_BF16
SCORE_CAP = 3.0e38


def _params(n_axes):
    return pltpu.CompilerParams(dimension_semantics=("parallel",) * n_axes,
                                vmem_limit_bytes=VMEM_LIMIT_BYTES)


def _resident(shape):
    zeros = (0,) * len(shape)
    return pl.BlockSpec(shape, lambda *_: zeros, pipeline_mode=pl.Buffered(1))


def _dot(a, b):
    return jnp.dot(a, b, preferred_element_type=F32)


def _modulate(x, g, shift, scale):
    ms = jnp.mean(x * x, axis=-1, keepdims=True)
    return (x * lax.rsqrt(ms + EPS) * g) * (1.0 + scale) + shift


def _mod_chunk(mod_ref, k):
    return mod_ref[0, :, k * D_MODEL:(k + 1) * D_MODEL]


def _adaln_kernel(cond_ref, w_ref, b_ref, o_ref):
    a = jax.nn.silu(cond_ref[...]).astype(BF16)
    o_ref[0] = _dot(a, w_ref[0].astype(BF16)) + b_ref[0]


def _adaln(cond, ada_w, ada_b):
    n_out = 6 * D_MODEL
    return pl.pallas_call(
        _adaln_kernel,
        grid=(DEPTH, n_out // ADA_TN),
        in_specs=[pl.BlockSpec((COND_ROWS, D_MODEL), lambda l, j: (0, 0)),
                  pl.BlockSpec((1, D_MODEL, ADA_TN), lambda l, j: (l, 0, j)),
                  pl.BlockSpec((1, 1, ADA_TN), lambda l, j: (l, 0, j))],
        out_specs=pl.BlockSpec((1, COND_ROWS, ADA_TN), lambda l, j: (l, 0, j)),
        out_shape=jax.ShapeDtypeStruct((DEPTH, COND_ROWS, n_out), F32),
        compiler_params=_params(2),
        name="adaln",
    )(cond, ada_w, ada_b.reshape(DEPTH, 1, n_out))


def _head_rms(z, gain):
    n = z.shape[1]
    w = min(n, MXU_WIDTH)
    r = lax.broadcasted_iota(jnp.int32, (w, w), 0) // HEAD_DIM
    c = lax.broadcasted_iota(jnp.int32, (w, w), 1) // HEAD_DIM
    ones = (r == c).astype(BF16)
    sq = (z * z).astype(BF16)
    ss = jnp.concatenate([_dot(sq[:, k:k + w], ones) for k in range(0, n, w)], axis=1)
    return z * lax.rsqrt(ss * (1.0 / HEAD_DIM) + EPS) * gain


def _rope(z, cos, sin_lo, sin_hi):
    outs = []
    for k in range(0, z.shape[1], LANES):
        blk = z[:, k:k + LANES]
        outs.append(blk * cos + pltpu.roll(blk, LANES - 16, 1) * sin_lo + pltpu.roll(blk, 16, 1) * sin_hi)
    return jnp.concatenate(outs, axis=1)


def _in_even_kernel(*refs, rope):
    if rope:
        (x_ref, mod_ref, g_ref, w_ref, qg_ref, kg_ref, cos_ref, slo_ref, shi_ref,
         q_ref, k_ref, v_ref, b_ref, ch_ref) = refs
    else:
        x_ref, mod_ref, g_ref, w_ref, qg_ref, kg_ref, q_ref, k_ref, v_ref, b_ref, ch_ref = refs
    h = _modulate(x_ref[...], g_ref[...], _mod_chunk(mod_ref, 0), _mod_chunk(mod_ref, 1)).astype(BF16)
    c0 = 0
    c1 = c0 + ATTN_WIDTH
    c2 = c1 + 2 * KV_WIDTH
    c3 = c2 + CONV_WIDTH
    c4 = c3 + CONV_WIDTH
    c5 = c4 + CONV_WIDTH
    q = _head_rms(_dot(h, w_ref[:, c0:c1]), qg_ref[...] * (HEAD_DIM ** -0.5 * LOG2_E))
    zkv = _dot(h, w_ref[:, c1:c2])
    k = _head_rms(zkv[:, :KV_WIDTH], kg_ref[...])
    if rope:
        cos, slo, shi = cos_ref[...], slo_ref[...], shi_ref[...]
        q = _rope(q, cos, slo, shi)
        k = _rope(k, cos, slo, shi)
    q_ref[...] = q
    k_ref[...] = k
    v_ref[...] = zkv[:, KV_WIDTH:]
    b_ref[...] = _dot(h, w_ref[:, c2:c3])
    ch_ref[...] = _dot(h, w_ref[:, c3:c4]) * _dot(h, w_ref[:, c4:c5])


def _in_even(x, mod_l, mod_row, g, w_bf, qg, kg, rope_tabs, tm, seq_len):
    T = x.shape[0]
    n_in = w_bf.shape[1]
    tiles_per_seq = seq_len // tm
    tok = lambda w: pl.BlockSpec((tm, w), lambda i: (i, 0))
    in_specs = [tok(D_MODEL),
                pl.BlockSpec((1, 1, 6 * D_MODEL), lambda i: (mod_row(i), 0, 0)),
                _resident((1, D_MODEL)), _resident((D_MODEL, n_in)),
                _resident((1, ATTN_WIDTH)), _resident((1, KV_WIDTH))]
    args = [x, mod_l, g, w_bf, qg, kg]
    if rope_tabs is not None:
        in_specs += [pl.BlockSpec((tm, LANES), lambda i: (i % tiles_per_seq, 0))] * 3
        args += list(rope_tabs)
    widths = (ATTN_WIDTH, KV_WIDTH, KV_WIDTH, CONV_WIDTH, CONV_WIDTH)
    return pl.pallas_call(
        functools.partial(_in_even_kernel, rope=rope_tabs is not None),
        grid=(T // tm,),
        in_specs=in_specs,
        out_specs=[tok(w) for w in widths],
        out_shape=[jax.ShapeDtypeStruct((T, w), F32) for w in widths],
        compiler_params=_params(1),
        name="in_even",
    )(*args)


def _stack_heads(q):
    lane_lo = lax.broadcasted_iota(jnp.int32, (BLOCK, LANES), 1) < HEAD_DIM
    heads = []
    for pair in range(N_HEADS // 2):
        kv = (2 * pair) // GQA_GROUP
        qp = q[:, pair * LANES:(pair + 1) * LANES]
        qr = pltpu.roll(qp, HEAD_DIM, 1)
        for half in range(2):
            src = qp if half == kv else qr
            heads.append(jnp.where(lane_lo, src, 0.0) if kv == 0 else jnp.where(lane_lo, 0.0, src))
    return jnp.concatenate(heads, axis=0).astype(BF16)


def _transposed_values(v_blocks):
    vt = jnp.concatenate([v[j:j + BLOCK].T for v in v_blocks for j in range(0, v.shape[0], BLOCK)], axis=1)
    return jnp.concatenate([vt.astype(BF16), jnp.ones((SUM_ROWS, vt.shape[1]), BF16)], axis=0)


def _attend(q, kcat, vt, cap, sink_row):
    st = lax.dot_general(kcat, _stack_heads(q), (((1,), (1,)), ((), ())), preferred_element_type=F32)
    if cap is not None:
        n_band = cap.shape[0]
        capped = jnp.minimum(st[:n_band], jnp.concatenate([cap] * N_HEADS, axis=1))
        st = jnp.concatenate([capped, st[n_band:]], axis=0)
    m = jnp.maximum(jnp.max(st, axis=0, keepdims=True), sink_row)
    pt = jnp.exp2(st - m).astype(BF16)
    ot = _dot(vt, pt)
    denom = ot[KV_WIDTH:KV_WIDTH + 1] + jnp.exp2(sink_row - m)
    ot = ot[:KV_WIDTH] / denom
    pairs = []
    for pair in range(N_HEADS // 2):
        kv = (2 * pair) // GQA_GROUP
        dims = slice(kv * HEAD_DIM, (kv + 1) * HEAD_DIM)
        both = [ot[dims, (2 * pair + half) * BLOCK:(2 * pair + half + 1) * BLOCK] for half in range(2)]
        pairs.append(jnp.concatenate(both, axis=0).T)
    return jnp.concatenate(pairs, axis=1)


def _attn_ctx_kernel(sink_ref, q_ref, k_ref, v_ref, o_ref):
    kcat = k_ref[...].astype(BF16)
    vt = _transposed_values([v_ref[...]])
    sink_row = sink_ref[...] * LOG2_E
    for j in range(0, SEQ, BLOCK):
        o_ref[j:j + BLOCK, :] = _attend(q_ref[j:j + BLOCK, :], kcat, vt, None, sink_row).astype(BF16)


def _attn_ctx(q, k, v, sink_row):
    T = q.shape[0]
    tok = lambda w: pl.BlockSpec((SEQ, w), lambda i: (i, 0))
    return pl.pallas_call(
        _attn_ctx_kernel,
        grid=(T // SEQ,),
        in_specs=[_resident((1, N_HEADS * BLOCK)), tok(ATTN_WIDTH), tok(KV_WIDTH), tok(KV_WIDTH)],
        out_specs=tok(ATTN_WIDTH),
        out_shape=jax.ShapeDtypeStruct((T, ATTN_WIDTH), BF16),
        compiler_params=_params(1),
        name="attn_ctx",
    )(sink_row, q, k, v)


def _band_cap(has_prev, has_next):
    c = lax.broadcasted_iota(jnp.int32, (3 * BLOCK, BLOCK), 0)
    r = lax.broadcasted_iota(jnp.int32, (3 * BLOCK, BLOCK), 1)
    first_prev = r + jnp.where(has_prev, 0, BLOCK)
    last_next = r + 2 * BLOCK - jnp.where(has_next, 0, BLOCK)
    masked = ((c < BLOCK) & (c < first_prev)) | ((c >= 2 * BLOCK) & (c > last_next))
    return jnp.where(masked, NEG_INF, SCORE_CAP)


def _attn_win_kernel(sink_ref, q_ref, kp_ref, kc_ref, kn_ref, vp_ref, vc_ref, vn_ref, ck_ref, cv_ref, o_ref,
                     *, n_steps):
    i = pl.program_id(1)
    kc, vc = kc_ref[...], vc_ref[...]
    k_blocks = [kp_ref[...], kc[:BLOCK], kc[BLOCK:], kn_ref[...]]
    v_blocks = [vp_ref[...], vc[:BLOCK], vc[BLOCK:], vn_ref[...]]
    sink_row = sink_ref[...] * LOG2_E
    for j in range(ATTN_Q_BLOCKS):
        kcat = jnp.concatenate(k_blocks[j:j + 3] + [ck_ref[0]], axis=0).astype(BF16)
        vt = _transposed_values(v_blocks[j:j + 3] + [cv_ref[0]])
        cap = _band_cap(i > 0 if j == 0 else True, True if j == 0 else i < n_steps - 1)
        rows = slice(j * BLOCK, (j + 1) * BLOCK)
        o_ref[rows, :] = _attend(q_ref[rows, :], kcat, vt, cap, sink_row).astype(BF16)


def _attn_win(q, k, v, ck, cv, sink_row):
    T = q.shape[0]
    tq = ATTN_Q_BLOCKS * BLOCK
    n_steps = DEC_SEQ // tq
    nb = DEC_SEQ // BLOCK
    cur = lambda w: pl.BlockSpec((tq, w), lambda b, i: (b * n_steps + i, 0))
    prev = pl.BlockSpec((BLOCK, KV_WIDTH), lambda b, i: (b * nb + jnp.maximum(ATTN_Q_BLOCKS * i - 1, 0), 0))
    nxt = pl.BlockSpec((BLOCK, KV_WIDTH),
                       lambda b, i: (b * nb + jnp.minimum(ATTN_Q_BLOCKS * (i + 1), nb - 1), 0))
    ctx = pl.BlockSpec((1, PAST_LEN, KV_WIDTH), lambda b, i: (b, 0, 0))
    return pl.pallas_call(
        functools.partial(_attn_win_kernel, n_steps=n_steps),
        grid=(T // DEC_SEQ, n_steps),
        in_specs=[_resident((1, N_HEADS * BLOCK)), cur(ATTN_WIDTH),
                  prev, cur(KV_WIDTH), nxt, prev, cur(KV_WIDTH), nxt, ctx, ctx],
        out_specs=cur(ATTN_WIDTH),
        out_shape=jax.ShapeDtypeStruct((T, ATTN_WIDTH), BF16),
        compiler_params=_params(2),
        name="attn_win",
    )(sink_row, q, k, k, k, v, v, v, ck, cv)


def _mix_out_kernel(attn_ref, b_ref, ch_ref, chp_ref, chn_ref, x_ref, mod_ref, cw_ref, w_ref, o_ref,
                    *, tiles_per_seq):
    tm = x_ref.shape[0]
    pos = pl.program_id(0) % tiles_per_seq
    ch = ch_ref[...]
    prev_row = jnp.where(pos > 0, chp_ref[SUBLANES_F32 - 1:SUBLANES_F32, :], 0.0)
    next_row = jnp.where(pos < tiles_per_seq - 1, chn_ref[0:1, :], 0.0)
    row = lax.broadcasted_iota(jnp.int32, ch.shape, 0)
    dn = jnp.where(row == 0, prev_row, pltpu.roll(ch, 1, 0))
    up = jnp.where(row == tm - 1, next_row, pltpu.roll(ch, tm - 1, 0))
    conv = dn * cw_ref[0:1, :] + ch * cw_ref[1:2, :] + up * cw_ref[2:3, :]
    sc = (b_ref[...] * conv).astype(BF16)
    mix = _dot(attn_ref[...], w_ref[:ATTN_WIDTH, :]) + _dot(sc, w_ref[ATTN_WIDTH:, :])
    o_ref[...] = x_ref[...] + _mod_chunk(mod_ref, 2) * mix


def _mix_out(attn, b, ch, x, mod_l, mod_row, conv_w, w_bf, tm, seq_len):
    T = x.shape[0]
    tiles_per_seq = seq_len // tm
    rows = tm // SUBLANES_F32
    n_halo = T // SUBLANES_F32
    tok = lambda w: pl.BlockSpec((tm, w), lambda i: (i, 0))
    halo_prev = pl.BlockSpec((SUBLANES_F32, CONV_WIDTH), lambda i: (jnp.maximum(i * rows - 1, 0), 0))
    halo_next = pl.BlockSpec((SUBLANES_F32, CONV_WIDTH), lambda i: (jnp.minimum((i + 1) * rows, n_halo - 1), 0))
    return pl.pallas_call(
        functools.partial(_mix_out_kernel, tiles_per_seq=tiles_per_seq),
        grid=(T // tm,),
        in_specs=[tok(ATTN_WIDTH), tok(CONV_WIDTH), tok(CONV_WIDTH), halo_prev, halo_next, tok(D_MODEL),
                  pl.BlockSpec((1, 1, 6 * D_MODEL), lambda i: (mod_row(i), 0, 0)),
                  _resident((3, CONV_WIDTH)), _resident((ATTN_WIDTH + CONV_WIDTH, D_MODEL))],
        out_specs=tok(D_MODEL),
        out_shape=jax.ShapeDtypeStruct((T, D_MODEL), F32),
        compiler_params=_params(1),
        name="mix_out",
    )(attn, b, ch, ch, ch, x, mod_l, conv_w, w_bf)


def _gelu_tanh(x):
    return 0.5 * x * (1.0 + jnp.tanh(0.7978845608028654 * (x + 0.044715 * (x * x * x))))


def _gmlp_kernel(x_ref, mod_ref, g_ref, win_ref, vg_ref, ws_ref, bs_ref, wout_ref, o_ref, gated_scr):
    x = x_ref[...]
    n_chunks = x.shape[0] // CHUNK
    h = _modulate(x, g_ref[...], _mod_chunk(mod_ref, 0), _mod_chunk(mod_ref, 1)).astype(BF16)
    u = _gelu_tanh(_dot(h, win_ref[:, :GMLP_WIDTH]))
    v = _gelu_tanh(_dot(h, win_ref[:, GMLP_WIDTH:]))
    ms = jnp.mean(v * v, axis=-1, keepdims=True)
    v = (v * lax.rsqrt(ms + EPS) * vg_ref[...]).astype(BF16)
    for grp in range(GMLP_GROUPS):
        lanes = slice(grp * GMLP_GROUP_DIM, (grp + 1) * GMLP_GROUP_DIM)
        rhs = jnp.concatenate([v[n * CHUNK:(n + 1) * CHUNK, lanes] for n in range(n_chunks)], axis=1)
        s = _dot(ws_ref[grp], rhs)
        for n in range(n_chunks):
            rows = slice(n * CHUNK, (n + 1) * CHUNK)
            s_n = s[:, n * GMLP_GROUP_DIM:(n + 1) * GMLP_GROUP_DIM] + bs_ref[grp]
            gated_scr[rows, lanes] = (u[rows, lanes] * s_n).astype(BF16)
    o_ref[...] = x + _mod_chunk(mod_ref, 2) * _dot(gated_scr[...], wout_ref[...])


def _gmlp(x, mod_l, mod_row, g, win_bf, vg, ws_bf, bs_full, wout_bf, tm):
    T = x.shape[0]
    tok = pl.BlockSpec((tm, D_MODEL), lambda i: (i, 0))
    return pl.pallas_call(
        _gmlp_kernel,
        grid=(T // tm,),
        in_specs=[tok, pl.BlockSpec((1, 1, 6 * D_MODEL), lambda i: (mod_row(i), 0, 0)),
                  _resident((1, D_MODEL)), _resident((D_MODEL, 2 * GMLP_WIDTH)), _resident((1, GMLP_WIDTH)),
                  _resident((GMLP_GROUPS, CHUNK, CHUNK)), _resident((GMLP_GROUPS, CHUNK, GMLP_GROUP_DIM)),
                  _resident((GMLP_WIDTH, D_MODEL))],
        out_specs=tok,
        out_shape=jax.ShapeDtypeStruct((T, D_MODEL), F32),
        scratch_shapes=[pltpu.VMEM((tm, GMLP_WIDTH), BF16)],
        compiler_params=_params(1),
        name="gmlp",
    )(x, mod_l, g, win_bf, vg, ws_bf, bs_full, wout_bf)


def _ffn_kernel(*refs, tiles_per_seq, seq_len):
    halo = FFN_HALO if tiles_per_seq > 1 else 0
    if halo:
        xp_ref, x_ref, xn_ref, mod_ref, g_ref, wup_ref, cw_ref, wdn_ref, o_ref, act_scr, h_scr = refs
    else:
        x_ref, mod_ref, g_ref, wup_ref, cw_ref, wdn_ref, o_ref, act_scr = refs
    tm = x_ref.shape[0]
    rows = tm + 2 * halo
    g, shift, scale = g_ref[...], _mod_chunk(mod_ref, 3), _mod_chunk(mod_ref, 4)
    x = x_ref[...]
    if halo:
        pos = pl.program_id(0) % tiles_per_seq
        hp = jnp.where(pos > 0, _modulate(xp_ref[...], g, shift, scale), 0.0)
        hn = jnp.where(pos < tiles_per_seq - 1, _modulate(xn_ref[...], g, shift, scale), 0.0)
        h_scr[:halo, :] = hp.astype(BF16)
        h_scr[halo:halo + tm, :] = _modulate(x, g, shift, scale).astype(BF16)
        h_scr[halo + tm:, :] = hn.astype(BF16)
        h = h_scr[...]
    else:
        h = _modulate(x, g, shift, scale).astype(BF16)
        seq_row = lax.broadcasted_iota(jnp.int32, (tm, FFN_COLS), 0) % seq_len
        has_prev = seq_row != 0
        has_next = seq_row != seq_len - 1
    mid = slice(halo, halo + tm)

    def conv(z, w):
        dn = pltpu.roll(z, 1, 0)[mid]
        up = pltpu.roll(z, rows - 1, 0)[mid]
        if not halo:
            dn = jnp.where(has_prev, dn, 0.0)
            up = jnp.where(has_next, up, 0.0)
        return dn * w[0:1] + z[mid] * w[1:2] + up * w[2:3]

    for j in range(0, D_FF, FFN_COLS):
        gate_cols = slice(j, j + FFN_COLS)
        val_cols = slice(D_FF + j, D_FF + j + FFN_COLS)
        zg = conv(_dot(h, wup_ref[:, gate_cols]), cw_ref[:, gate_cols])
        zv = conv(_dot(h, wup_ref[:, val_cols]), cw_ref[:, val_cols])
        act_scr[:, gate_cols] = (jax.nn.silu(zg) * zv).astype(BF16)
    o_ref[...] = x + _mod_chunk(mod_ref, 5) * _dot(act_scr[...], wdn_ref[...])


def _ffn(x, mod_l, mod_row, g, wup_bf, conv_w, wdn_bf, tm, seq_len):
    T = x.shape[0]
    tiles_per_seq = max(seq_len // tm, 1)
    assert tm % seq_len == 0 or seq_len % tm == 0
    rows = tm // FFN_HALO
    n_halo = T // FFN_HALO
    tok = pl.BlockSpec((tm, D_MODEL), lambda i: (i, 0))
    in_specs = [tok, pl.BlockSpec((1, 1, 6 * D_MODEL), lambda i: (mod_row(i), 0, 0)),
                _resident((1, D_MODEL)), _resident((D_MODEL, 2 * D_FF)), _resident((3, 2 * D_FF)),
                _resident((D_FF, D_MODEL))]
    args = [x, mod_l, g, wup_bf, conv_w, wdn_bf]
    scratch = [pltpu.VMEM((tm, D_FF), BF16)]
    if tiles_per_seq > 1:
        halo_prev = pl.BlockSpec((FFN_HALO, D_MODEL), lambda i: (jnp.maximum(i * rows - 1, 0), 0))
        halo_next = pl.BlockSpec((FFN_HALO, D_MODEL), lambda i: (jnp.minimum((i + 1) * rows, n_halo - 1), 0))
        in_specs = [halo_prev, in_specs[0], halo_next] + in_specs[1:]
        args = [x, x, x] + args[1:]
        scratch.append(pltpu.VMEM((tm + 2 * FFN_HALO, D_MODEL), BF16))
    return pl.pallas_call(
        functools.partial(_ffn_kernel, tiles_per_seq=tiles_per_seq, seq_len=seq_len),
        grid=(T // tm,),
        in_specs=in_specs,
        out_specs=tok,
        out_shape=jax.ShapeDtypeStruct((T, D_MODEL), F32),
        scratch_shapes=scratch,
        compiler_params=_params(1),
        name="conv_ffn",
    )(*args)


def _rope_tables(n_tokens):
    t = jnp.arange(n_tokens)
    n_freq = HEAD_DIM // 4
    inv = ROPE_BASE ** (-jnp.arange(n_freq, dtype=F32) / n_freq)
    row_ang = (t // GRID_W).astype(F32)[:, None] * inv
    col_ang = (t % GRID_W).astype(F32)[:, None] * inv
    ang = jnp.concatenate([row_ang, row_ang, col_ang, col_ang], axis=1)
    ang = jnp.tile(ang, (1, LANES // HEAD_DIM))
    first = (jnp.arange(LANES) % (2 * n_freq)) < n_freq
    cos, sin = jnp.cos(ang), jnp.sin(ang)
    return cos, jnp.where(first, -sin, 0.0), jnp.where(first, 0.0, sin)


def kernel(x_prompt, x_sample, cache_k, cache_v, c, c_ctx, ada_w, ada_b, norm_mix_g, norm_ffn_g, w_in_even,
           q_norm_g, k_norm_g, sink_logit, short_conv_w, w_out_even, w_in_odd, gmlp_norm_g, w_spatial,
           b_spatial, w_out_odd, w_up, ffn_conv_w, w_down):
    n_p, n_s = BATCH * SEQ, DEC_BATCH * DEC_SEQ
    xp = x_prompt.reshape(n_p, D_MODEL)
    xs = x_sample.reshape(n_s, D_MODEL)
    cond = jnp.concatenate([c, c_ctx[None, :], jnp.zeros((COND_ROWS - DEC_BATCH - 1, D_MODEL), F32)], axis=0)
    mod = _adaln(cond, ada_w, ada_b)
    rope_tabs = _rope_tables(DEC_SEQ)

    tm_p, tm_s = 512, 512
    row_p = lambda i: CTX_ROW
    row_s = lambda i: i // (DEC_SEQ // tm_s)
    row_vec = lambda a: a.reshape(1, -1)
    new_k, new_v = [], []
    for l in range(DEPTH):
        mod_l = mod[l].reshape(COND_ROWS, 1, 6 * D_MODEL)
        g_mix = row_vec(norm_mix_g[l])
        if l % 2 == 0:
            e = l // 2
            w_in = w_in_even[e].astype(BF16)
            w_out = w_out_even[e].astype(BF16)
            qg = row_vec(jnp.tile(q_norm_g[e], N_HEADS))
            kg = row_vec(jnp.tile(k_norm_g[e], N_KV_HEADS))
            sink = jnp.repeat(sink_logit[e], BLOCK).reshape(1, N_HEADS * BLOCK)
            cw = short_conv_w[e]
            qp, kp, vp, bp, chp = _in_even(xp, mod_l, row_p, g_mix, w_in, qg, kg, None, tm_p, SEQ)
            new_k.append(kp.reshape(BATCH, SEQ, N_KV_HEADS, HEAD_DIM))
            new_v.append(vp.reshape(BATCH, SEQ, N_KV_HEADS, HEAD_DIM))
            attn_p = _attn_ctx(qp, kp, vp, sink)
            xp = _mix_out(attn_p, bp, chp, xp, mod_l, row_p, cw, w_out, SEQ, SEQ)
            qs, ks, vs, bs, chs = _in_even(xs, mod_l, row_s, g_mix, w_in, qg, kg, rope_tabs, tm_s, DEC_SEQ)
            ck = cache_k[:, e].reshape(DEC_BATCH, PAST_LEN, KV_WIDTH)
            cv = cache_v[:, e].reshape(DEC_BATCH, PAST_LEN, KV_WIDTH)
            attn_s = _attn_win(qs, ks, vs, ck, cv, sink)
            xs = _mix_out(attn_s, bs, chs, xs, mod_l, row_s, cw, w_out, tm_s, DEC_SEQ)
        else:
            o = l // 2
            w_in = w_in_odd[o].astype(BF16)
            w_out = w_out_odd[o].astype(BF16)
            vg = row_vec(gmlp_norm_g[o])
            ws = w_spatial[o].astype(BF16)
            bs_full = jnp.broadcast_to(b_spatial[o][:, :, None], (GMLP_GROUPS, CHUNK, GMLP_GROUP_DIM))
            xp = _gmlp(xp, mod_l, row_p, g_mix, w_in, vg, ws, bs_full, w_out, tm_p)
            xs = _gmlp(xs, mod_l, row_s, g_mix, w_in, vg, ws, bs_full, w_out, tm_s)
        g_ffn = row_vec(norm_ffn_g[l])
        wup = w_up[l].astype(BF16)
        wdn = w_down[l].astype(BF16)
        xp = _ffn(xp, mod_l, row_p, g_ffn, wup, ffn_conv_w[l], wdn, tm_p, SEQ)
        xs = _ffn(xs, mod_l, row_s, g_ffn, wup, ffn_conv_w[l], wdn, tm_s, DEC_SEQ)
    return (xp.reshape(BATCH, SEQ, D_MODEL), xs.reshape(DEC_BATCH, DEC_SEQ, D_MODEL),
            jnp.stack(new_k, axis=1), jnp.stack(new_v, axis=1))
```

```python
import functools

import jax
import jax.numpy as jnp
from jax import lax
from jax.experimental import pallas as pl
from jax.experimental.pallas import tpu as pltpu

F32 = jnp.float32
BF16 = jnp.bfloat16

D_MODEL = 1024
BATCH = 16
SEQ = 256
DEPTH = 2
DEC_BATCH = 4
DEC_SEQ = 2048
PAST_LEN = 512
GRID_W = 64
N_HEADS = 8
N_KV_HEADS = 2
HEAD_DIM = 64
GQA_GROUP = N_HEADS // N_KV_HEADS
ATTN_WIDTH = N_HEADS * HEAD_DIM
KV_WIDTH = N_KV_HEADS * HEAD_DIM
WINDOW = 128
BLOCK = 128
ROPE_BASE = 10000.0
CONV_WIDTH = 512
CHUNK = 128
GMLP_WIDTH = 1024
GMLP_GROUPS = 8
GMLP_GROUP_DIM = GMLP_WIDTH // GMLP_GROUPS
D_FF = 2816
EPS = 1e-6
NEG_INF = -1e30
N_EVEN = (DEPTH + 1) // 2
N_ODD = DEPTH // 2
LOG2_E = 1.4426950408889634

LANES = 128
SUBLANES_F32 = 8
SUBLANES_BF16 = 16
MXU_WIDTH = 256
VMEM_LIMIT_BYTES = 56 * 1024 * 1024

COND_ROWS = 8
CTX_ROW = DEC_BATCH
FFN_HALO = SUBLANES_BF16
FFN_COLS = MXU_WIDTH
CAST_BLOCK_BYTES = 3 * 1024 * 1024
ADA_TN = 1536
ATTN_Q_BLOCKS = 2
SUM_ROWS = SUBLANES_BF16
SCORE_CAP = 3.0e38


def _params(n_axes):
    return pltpu.CompilerParams(dimension_semantics=("parallel",) * n_axes,
                                vmem_limit_bytes=VMEM_LIMIT_BYTES)


def _resident(shape):
    zeros = (0,) * len(shape)
    return pl.BlockSpec(shape, lambda *_: zeros, pipeline_mode=pl.Buffered(1))


def _layer_slab(shape, layer):
    return pl.BlockSpec((1,) + tuple(shape[1:]), lambda *_: (layer, 0, 0), pipeline_mode=pl.Buffered(1))


def _dot(a, b):
    return jnp.dot(a, b, preferred_element_type=F32)


def _cast_kernel(x_ref, o_ref):
    o_ref[...] = x_ref[...].astype(BF16)


def _to_bf16(w):
    n, rows, cols = w.shape
    fits = [r for r in range(SUBLANES_BF16, rows + 1, SUBLANES_BF16)
            if rows % r == 0 and r * cols * 4 <= CAST_BLOCK_BYTES]
    tr = max(fits)
    spec = pl.BlockSpec((1, tr, cols), lambda i, j: (i, j, 0))
    return pl.pallas_call(
        _cast_kernel, grid=(n, rows // tr), in_specs=[spec], out_specs=spec,
        out_shape=jax.ShapeDtypeStruct(w.shape, BF16), compiler_params=_params(2), name="to_bf16",
    )(w)


def _modulate(x, g, shift, scale):
    ms = jnp.mean(x * x, axis=-1, keepdims=True)
    return (x * lax.rsqrt(ms + EPS) * g) * (1.0 + scale) + shift


def _mod_chunk(mod_ref, k):
    return mod_ref[0, :, k * D_MODEL:(k + 1) * D_MODEL]


def _adaln_kernel(cond_ref, w_ref, b_ref, o_ref):
    a = jax.nn.silu(cond_ref[...]).astype(BF16)
    o_ref[0] = _dot(a, w_ref[0].astype(BF16)) + b_ref[0]


def _adaln(cond, ada_w, ada_b):
    n_out = 6 * D_MODEL
    return pl.pallas_call(
        _adaln_kernel,
        grid=(DEPTH, n_out // ADA_TN),
        in_specs=[pl.BlockSpec((COND_ROWS, D_MODEL), lambda l, j: (0, 0)),
                  pl.BlockSpec((1, D_MODEL, ADA_TN), lambda l, j: (l, 0, j)),
                  pl.BlockSpec((1, 1, ADA_TN), lambda l, j: (l, 0, j))],
        out_specs=pl.BlockSpec((1, COND_ROWS, ADA_TN), lambda l, j: (l, 0, j)),
        out_shape=jax.ShapeDtypeStruct((DEPTH, COND_ROWS, n_out), F32),
        compiler_params=_params(2),
        name="adaln",
    )(cond, ada_w, ada_b.reshape(DEPTH, 1, n_out))


def _head_rms(z, gain):
    n = z.shape[1]
    w = min(n, MXU_WIDTH)
    r = lax.broadcasted_iota(jnp.int32, (w, w), 0) // HEAD_DIM
    c = lax.broadcasted_iota(jnp.int32, (w, w), 1) // HEAD_DIM
    ones = (r == c).astype(BF16)
    sq = (z * z).astype(BF16)
    ss = jnp.concatenate([_dot(sq[:, k:k + w], ones) for k in range(0, n, w)], axis=1)
    return z * lax.rsqrt(ss * (1.0 / HEAD_DIM) + EPS) * gain


def _rope(z, cos, sin_lo, sin_hi):
    outs = []
    for k in range(0, z.shape[1], LANES):
        blk = z[:, k:k + LANES]
        outs.append(blk * cos + pltpu.roll(blk, LANES - 16, 1) * sin_lo + pltpu.roll(blk, 16, 1) * sin_hi)
    return jnp.concatenate(outs, axis=1)


def _in_even_kernel(*refs, rope):
    if rope:
        (x_ref, mod_ref, g_ref, w_ref, qg_ref, kg_ref, cos_ref, slo_ref, shi_ref,
         q_ref, k_ref, v_ref, b_ref, ch_ref) = refs
    else:
        x_ref, mod_ref, g_ref, w_ref, qg_ref, kg_ref, q_ref, k_ref, v_ref, b_ref, ch_ref = refs
    h = _modulate(x_ref[...], g_ref[...], _mod_chunk(mod_ref, 0), _mod_chunk(mod_ref, 1)).astype(BF16)
    c0 = 0
    c1 = c0 + ATTN_WIDTH
    c2 = c1 + 2 * KV_WIDTH
    c3 = c2 + CONV_WIDTH
    c4 = c3 + CONV_WIDTH
    c5 = c4 + CONV_WIDTH
    q = _head_rms(_dot(h, w_ref[:, c0:c1]), qg_ref[...] * (HEAD_DIM ** -0.5 * LOG2_E))
    zkv = _dot(h, w_ref[:, c1:c2])
    k = _head_rms(zkv[:, :KV_WIDTH], kg_ref[...])
    if rope:
        cos, slo, shi = cos_ref[...], slo_ref[...], shi_ref[...]
        q = _rope(q, cos, slo, shi)
        k = _rope(k, cos, slo, shi)
    q_ref[...] = q
    k_ref[...] = k
    v_ref[...] = zkv[:, KV_WIDTH:]
    b_ref[...] = _dot(h, w_ref[:, c2:c3])
    ch_ref[...] = _dot(h, w_ref[:, c3:c4]) * _dot(h, w_ref[:, c4:c5])


def _in_even(x, mod_l, mod_row, g, w_bf, qg, kg, rope_tabs, tm, seq_len):
    T = x.shape[0]
    n_in = w_bf.shape[1]
    tiles_per_seq = seq_len // tm
    tok = lambda w: pl.BlockSpec((tm, w), lambda i: (i, 0))
    in_specs = [tok(D_MODEL),
                pl.BlockSpec((1, 1, 6 * D_MODEL), lambda i: (mod_row(i), 0, 0)),
                _resident((1, D_MODEL)), _resident((D_MODEL, n_in)),
                _resident((1, ATTN_WIDTH)), _resident((1, KV_WIDTH))]
    args = [x, mod_l, g, w_bf, qg, kg]
    if rope_tabs is not None:
        in_specs += [pl.BlockSpec((tm, LANES), lambda i: (i % tiles_per_seq, 0))] * 3
        args += list(rope_tabs)
    widths = (ATTN_WIDTH, KV_WIDTH, KV_WIDTH, CONV_WIDTH, CONV_WIDTH)
    return pl.pallas_call(
        functools.partial(_in_even_kernel, rope=rope_tabs is not None),
        grid=(T // tm,),
        in_specs=in_specs,
        out_specs=[tok(w) for w in widths],
        out_shape=[jax.ShapeDtypeStruct((T, w), F32) for w in widths],
        compiler_params=_params(1),
        name="in_even",
    )(*args)


def _stack_heads(q):
    lane_lo = lax.broadcasted_iota(jnp.int32, (BLOCK, LANES), 1) < HEAD_DIM
    heads = []
    for pair in range(N_HEADS // 2):
        kv = (2 * pair) // GQA_GROUP
        qp = q[:, pair * LANES:(pair + 1) * LANES]
        qr = pltpu.roll(qp, HEAD_DIM, 1)
        for half in range(2):
            src = qp if half == kv else qr
            heads.append(jnp.where(lane_lo, src, 0.0) if kv == 0 else jnp.where(lane_lo, 0.0, src))
    return jnp.concatenate(heads, axis=0).astype(BF16)


def _transposed_values(v_blocks):
    vt = jnp.concatenate([v[j:j + BLOCK].T for v in v_blocks for j in range(0, v.shape[0], BLOCK)], axis=1)
    return jnp.concatenate([vt.astype(BF16), jnp.ones((SUM_ROWS, vt.shape[1]), BF16)], axis=0)


def _attend(q, kcat, vt, cap, sink_row):
    st = lax.dot_general(kcat, _stack_heads(q), (((1,), (1,)), ((), ())), preferred_element_type=F32)
    if cap is not None:
        n_band = cap.shape[0]
        capped = jnp.minimum(st[:n_band], jnp.concatenate([cap] * N_HEADS, axis=1))
        st = jnp.concatenate([capped, st[n_band:]], axis=0)
    m = jnp.maximum(jnp.max(st, axis=0, keepdims=True), sink_row)
    pt = jnp.exp2(st - m).astype(BF16)
    ot = _dot(vt, pt)
    denom = ot[KV_WIDTH:KV_WIDTH + 1] + jnp.exp2(sink_row - m)
    ot = ot[:KV_WIDTH] / denom
    pairs = []
    for pair in range(N_HEADS // 2):
        kv = (2 * pair) // GQA_GROUP
        dims = slice(kv * HEAD_DIM, (kv + 1) * HEAD_DIM)
        both = [ot[dims, (2 * pair + half) * BLOCK:(2 * pair + half + 1) * BLOCK] for half in range(2)]
        pairs.append(jnp.concatenate(both, axis=0).T)
    return jnp.concatenate(pairs, axis=1)


def _attn_ctx_kernel(sink_ref, q_ref, k_ref, v_ref, o_ref):
    kcat = k_ref[...].astype(BF16)
    vt = _transposed_values([v_ref[...]])
    sink_row = sink_ref[...] * LOG2_E
    for j in range(0, SEQ, BLOCK):
        o_ref[j:j + BLOCK, :] = _attend(q_ref[j:j + BLOCK, :], kcat, vt, None, sink_row).astype(BF16)


def _attn_ctx(q, k, v, sink_row):
    T = q.shape[0]
    tok = lambda w: pl.BlockSpec((SEQ, w), lambda i: (i, 0))
    return pl.pallas_call(
        _attn_ctx_kernel,
        grid=(T // SEQ,),
        in_specs=[_resident((1, N_HEADS * BLOCK)), tok(ATTN_WIDTH), tok(KV_WIDTH), tok(KV_WIDTH)],
        out_specs=tok(ATTN_WIDTH),
        out_shape=jax.ShapeDtypeStruct((T, ATTN_WIDTH), BF16),
        compiler_params=_params(1),
        name="attn_ctx",
    )(sink_row, q, k, v)


def _band_cap(has_prev, has_next):
    c = lax.broadcasted_iota(jnp.int32, (3 * BLOCK, BLOCK), 0)
    r = lax.broadcasted_iota(jnp.int32, (3 * BLOCK, BLOCK), 1)
    first_prev = r + jnp.where(has_prev, 0, BLOCK)
    last_next = r + 2 * BLOCK - jnp.where(has_next, 0, BLOCK)
    masked = ((c < BLOCK) & (c < first_prev)) | ((c >= 2 * BLOCK) & (c > last_next))
    return jnp.where(masked, NEG_INF, SCORE_CAP)


def _attn_win_kernel(sink_ref, q_ref, kp_ref, kc_ref, kn_ref, vp_ref, vc_ref, vn_ref, ck_ref, cv_ref, o_ref,
                     *, n_steps):
    i = pl.program_id(1)
    kc, vc = kc_ref[...], vc_ref[...]
    k_blocks = [kp_ref[...], kc[:BLOCK], kc[BLOCK:], kn_ref[...]]
    v_blocks = [vp_ref[...], vc[:BLOCK], vc[BLOCK:], vn_ref[...]]
    sink_row = sink_ref[...] * LOG2_E
    for j in range(ATTN_Q_BLOCKS):
        kcat = jnp.concatenate(k_blocks[j:j + 3] + [ck_ref[0]], axis=0).astype(BF16)
        vt = _transposed_values(v_blocks[j:j + 3] + [cv_ref[0]])
        cap = _band_cap(i > 0 if j == 0 else True, True if j == 0 else i < n_steps - 1)
        rows = slice(j * BLOCK, (j + 1) * BLOCK)
        o_ref[rows, :] = _attend(q_ref[rows, :], kcat, vt, cap, sink_row).astype(BF16)


def _attn_win(q, k, v, ck, cv, sink_row):
    T = q.shape[0]
    tq = ATTN_Q_BLOCKS * BLOCK
    n_steps = DEC_SEQ // tq
    nb = DEC_SEQ // BLOCK
    cur = lambda w: pl.BlockSpec((tq, w), lambda b, i: (b * n_steps + i, 0))
    prev = pl.BlockSpec((BLOCK, KV_WIDTH), lambda b, i: (b * nb + jnp.maximum(ATTN_Q_BLOCKS * i - 1, 0), 0))
    nxt = pl.BlockSpec((BLOCK, KV_WIDTH),
                       lambda b, i: (b * nb + jnp.minimum(ATTN_Q_BLOCKS * (i + 1), nb - 1), 0))
    ctx = pl.BlockSpec((1, PAST_LEN, KV_WIDTH), lambda b, i: (b, 0, 0))
    return pl.pallas_call(
        functools.partial(_attn_win_kernel, n_steps=n_steps),
        grid=(T // DEC_SEQ, n_steps),
        in_specs=[_resident((1, N_HEADS * BLOCK)), cur(ATTN_WIDTH),
                  prev, cur(KV_WIDTH), nxt, prev, cur(KV_WIDTH), nxt, ctx, ctx],
        out_specs=cur(ATTN_WIDTH),
        out_shape=jax.ShapeDtypeStruct((T, ATTN_WIDTH), BF16),
        compiler_params=_params(2),
        name="attn_win",
    )(sink_row, q, k, k, k, v, v, v, ck, cv)


def _mix_out_kernel(attn_ref, b_ref, ch_ref, chp_ref, chn_ref, x_ref, mod_ref, cw_ref, w_ref, o_ref,
                    *, tiles_per_seq):
    tm = x_ref.shape[0]
    pos = pl.program_id(0) % tiles_per_seq
    ch = ch_ref[...]
    prev_row = jnp.where(pos > 0, chp_ref[SUBLANES_F32 - 1:SUBLANES_F32, :], 0.0)
    next_row = jnp.where(pos < tiles_per_seq - 1, chn_ref[0:1, :], 0.0)
    row = lax.broadcasted_iota(jnp.int32, ch.shape, 0)
    dn = jnp.where(row == 0, prev_row, pltpu.roll(ch, 1, 0))
    up = jnp.where(row == tm - 1, next_row, pltpu.roll(ch, tm - 1, 0))
    conv = dn * cw_ref[0:1, :] + ch * cw_ref[1:2, :] + up * cw_ref[2:3, :]
    sc = (b_ref[...] * conv).astype(BF16)
    mix = _dot(attn_ref[...], w_ref[:ATTN_WIDTH, :]) + _dot(sc, w_ref[ATTN_WIDTH:, :])
    o_ref[...] = x_ref[...] + _mod_chunk(mod_ref, 2) * mix


def _mix_out(attn, b, ch, x, mod_l, mod_row, conv_w, w_bf, tm, seq_len):
    T = x.shape[0]
    tiles_per_seq = seq_len // tm
    rows = tm // SUBLANES_F32
    n_halo = T // SUBLANES_F32
    tok = lambda w: pl.BlockSpec((tm, w), lambda i: (i, 0))
    halo_prev = pl.BlockSpec((SUBLANES_F32, CONV_WIDTH), lambda i: (jnp.maximum(i * rows - 1, 0), 0))
    halo_next = pl.BlockSpec((SUBLANES_F32, CONV_WIDTH), lambda i: (jnp.minimum((i + 1) * rows, n_halo - 1), 0))
    return pl.pallas_call(
        functools.partial(_mix_out_kernel, tiles_per_seq=tiles_per_seq),
        grid=(T // tm,),
        in_specs=[tok(ATTN_WIDTH), tok(CONV_WIDTH), tok(CONV_WIDTH), halo_prev, halo_next, tok(D_MODEL),
                  pl.BlockSpec((1, 1, 6 * D_MODEL), lambda i: (mod_row(i), 0, 0)),
                  _resident((3, CONV_WIDTH)), _resident((ATTN_WIDTH + CONV_WIDTH, D_MODEL))],
        out_specs=tok(D_MODEL),
        out_shape=jax.ShapeDtypeStruct((T, D_MODEL), F32),
        compiler_params=_params(1),
        name="mix_out",
    )(attn, b, ch, ch, ch, x, mod_l, conv_w, w_bf)


def _gelu_tanh(x):
    k = 0.7978845608028654
    half = 0.5 * x
    return half + half * jnp.tanh(x * (k + (k * 0.044715) * (x * x)))


def _gmlp_kernel(x_ref, mod_ref, g_ref, win_ref, vg_ref, ws_ref, bs_ref, wout_ref, o_ref, gated_scr):
    x = x_ref[...]
    n_chunks = x.shape[0] // CHUNK
    h = _modulate(x, g_ref[...], _mod_chunk(mod_ref, 0), _mod_chunk(mod_ref, 1)).astype(BF16)
    u = _gelu_tanh(_dot(h, win_ref[:, :GMLP_WIDTH]))
    v = _gelu_tanh(_dot(h, win_ref[:, GMLP_WIDTH:]))
    ms = jnp.mean(v * v, axis=-1, keepdims=True)
    v = (v * lax.rsqrt(ms + EPS) * vg_ref[...]).astype(BF16)
    for grp in range(GMLP_GROUPS):
        lanes = slice(grp * GMLP_GROUP_DIM, (grp + 1) * GMLP_GROUP_DIM)
        rhs = jnp.concatenate([v[n * CHUNK:(n + 1) * CHUNK, lanes] for n in range(n_chunks)], axis=1)
        s = _dot(ws_ref[grp], rhs)
        for n in range(n_chunks):
            rows = slice(n * CHUNK, (n + 1) * CHUNK)
            s_n = s[:, n * GMLP_GROUP_DIM:(n + 1) * GMLP_GROUP_DIM] + bs_ref[grp]
            gated_scr[rows, lanes] = (u[rows, lanes] * s_n).astype(BF16)
    o_ref[...] = x + _mod_chunk(mod_ref, 2) * _dot(gated_scr[...], wout_ref[...])


def _gmlp(x, mod_l, mod_row, g, win_bf, vg, ws_bf, bs_full, wout_bf, tm):
    T = x.shape[0]
    tok = pl.BlockSpec((tm, D_MODEL), lambda i: (i, 0))
    return pl.pallas_call(
        _gmlp_kernel,
        grid=(T // tm,),
        in_specs=[tok, pl.BlockSpec((1, 1, 6 * D_MODEL), lambda i: (mod_row(i), 0, 0)),
                  _resident((1, D_MODEL)), _resident((D_MODEL, 2 * GMLP_WIDTH)), _resident((1, GMLP_WIDTH)),
                  _resident((GMLP_GROUPS, CHUNK, CHUNK)), _resident((GMLP_GROUPS, CHUNK, GMLP_GROUP_DIM)),
                  _resident((GMLP_WIDTH, D_MODEL))],
        out_specs=tok,
        out_shape=jax.ShapeDtypeStruct((T, D_MODEL), F32),
        scratch_shapes=[pltpu.VMEM((tm, GMLP_WIDTH), BF16)],
        compiler_params=_params(1),
        name="gmlp",
    )(x, mod_l, g, win_bf, vg, ws_bf, bs_full, wout_bf)


def _ffn_kernel(*refs, tiles_per_seq, seq_len):
    halo = FFN_HALO if tiles_per_seq > 1 else 0
    if halo:
        xp_ref, x_ref, xn_ref, mod_ref, g_ref, wup_ref, cw_ref, wdn_ref, o_ref, act_scr, h_scr = refs
    else:
        x_ref, mod_ref, g_ref, wup_ref, cw_ref, wdn_ref, o_ref, act_scr = refs
    tm = x_ref.shape[0]
    rows = tm + 2 * halo
    g, shift, scale = g_ref[...], _mod_chunk(mod_ref, 3), _mod_chunk(mod_ref, 4)
    x = x_ref[...]
    if halo:
        pos = pl.program_id(0) % tiles_per_seq
        hp = jnp.where(pos > 0, _modulate(xp_ref[...], g, shift, scale), 0.0)
        hn = jnp.where(pos < tiles_per_seq - 1, _modulate(xn_ref[...], g, shift, scale), 0.0)
        h_scr[:halo, :] = hp.astype(BF16)
        h_scr[halo:halo + tm, :] = _modulate(x, g, shift, scale).astype(BF16)
        h_scr[halo + tm:, :] = hn.astype(BF16)
        h = h_scr[...]
    else:
        h = _modulate(x, g, shift, scale).astype(BF16)
        seq_row = lax.broadcasted_iota(jnp.int32, (tm, FFN_COLS), 0) % seq_len
        has_prev = seq_row != 0
        has_next = seq_row != seq_len - 1
    mid = slice(halo, halo + tm)

    def conv(z, w):
        dn = pltpu.roll(z, 1, 0)[mid]
        up = pltpu.roll(z, rows - 1, 0)[mid]
        if not halo:
            dn = jnp.where(has_prev, dn, 0.0)
            up = jnp.where(has_next, up, 0.0)
        return dn * w[0:1] + z[mid] * w[1:2] + up * w[2:3]

    for j in range(0, D_FF, FFN_COLS):
        gate_cols = slice(j, j + FFN_COLS)
        val_cols = slice(D_FF + j, D_FF + j + FFN_COLS)
        zg = conv(_dot(h, wup_ref[0, :, gate_cols]), cw_ref[0, :, gate_cols])
        zv = conv(_dot(h, wup_ref[0, :, val_cols]), cw_ref[0, :, val_cols])
        act_scr[:, gate_cols] = (jax.nn.silu(zg) * zv).astype(BF16)
    o_ref[...] = x + _mod_chunk(mod_ref, 5) * _dot(act_scr[...], wdn_ref[0])


def _ffn(x, mod_l, mod_row, g, wup_bf, conv_w, wdn_bf, layer, tm, seq_len):
    T = x.shape[0]
    tiles_per_seq = max(seq_len // tm, 1)
    assert tm % seq_len == 0 or seq_len % tm == 0
    rows = tm // FFN_HALO
    n_halo = T // FFN_HALO
    tok = pl.BlockSpec((tm, D_MODEL), lambda i: (i, 0))
    in_specs = [tok, pl.BlockSpec((1, 1, 6 * D_MODEL), lambda i: (mod_row(i), 0, 0)),
                _resident((1, D_MODEL)), _layer_slab(wup_bf.shape, layer), _layer_slab(conv_w.shape, layer),
                _layer_slab(wdn_bf.shape, layer)]
    args = [x, mod_l, g, wup_bf, conv_w, wdn_bf]
    scratch = [pltpu.VMEM((tm, D_FF), BF16)]
    if tiles_per_seq > 1:
        halo_prev = pl.BlockSpec((FFN_HALO, D_MODEL), lambda i: (jnp.maximum(i * rows - 1, 0), 0))
        halo_next = pl.BlockSpec((FFN_HALO, D_MODEL), lambda i: (jnp.minimum((i + 1) * rows, n_halo - 1), 0))
        in_specs = [halo_prev, in_specs[0], halo_next] + in_specs[1:]
        args = [x, x, x] + args[1:]
        scratch.append(pltpu.VMEM((tm + 2 * FFN_HALO, D_MODEL), BF16))
    return pl.pallas_call(
        functools.partial(_ffn_kernel, tiles_per_seq=tiles_per_seq, seq_len=seq_len),
        grid=(T // tm,),
        in_specs=in_specs,
        out_specs=tok,
        out_shape=jax.ShapeDtypeStruct((T, D_MODEL), F32),
        scratch_shapes=scratch,
        compiler_params=_params(1),
        name="conv_ffn",
    )(*args)


def _rope_tables(n_tokens):
    t = jnp.arange(n_tokens)
    n_freq = HEAD_DIM // 4
    inv = ROPE_BASE ** (-jnp.arange(n_freq, dtype=F32) / n_freq)
    row_ang = (t // GRID_W).astype(F32)[:, None] * inv
    col_ang = (t % GRID_W).astype(F32)[:, None] * inv
    ang = jnp.concatenate([row_ang, row_ang, col_ang, col_ang], axis=1)
    ang = jnp.tile(ang, (1, LANES // HEAD_DIM))
    first = (jnp.arange(LANES) % (2 * n_freq)) < n_freq
    cos, sin = jnp.cos(ang), jnp.sin(ang)
    return cos, jnp.where(first, -sin, 0.0), jnp.where(first, 0.0, sin)


def kernel(x_prompt, x_sample, cache_k, cache_v, c, c_ctx, ada_w, ada_b, norm_mix_g, norm_ffn_g, w_in_even,
           q_norm_g, k_norm_g, sink_logit, short_conv_w, w_out_even, w_in_odd, gmlp_norm_g, w_spatial,
           b_spatial, w_out_odd, w_up, ffn_conv_w, w_down):
    n_p, n_s = BATCH * SEQ, DEC_BATCH * DEC_SEQ
    xp = x_prompt.reshape(n_p, D_MODEL)
    xs = x_sample.reshape(n_s, D_MODEL)
    cond = jnp.concatenate([c, c_ctx[None, :], jnp.zeros((COND_ROWS - DEC_BATCH - 1, D_MODEL), F32)], axis=0)
    mod = _adaln(cond, ada_w, ada_b)
    rope_tabs = _rope_tables(DEC_SEQ)
    w_in_even_bf, w_out_even_bf = _to_bf16(w_in_even), _to_bf16(w_out_even)
    w_in_odd_bf, w_out_odd_bf = _to_bf16(w_in_odd), _to_bf16(w_out_odd)
    w_spatial_bf = _to_bf16(w_spatial.reshape(N_ODD, GMLP_GROUPS * CHUNK, CHUNK))
    w_up_bf, w_down_bf = _to_bf16(w_up), _to_bf16(w_down)

    tm_p, tm_s = 512, 512
    row_p = lambda i: CTX_ROW
    row_s = lambda i: i // (DEC_SEQ // tm_s)
    row_vec = lambda a: a.reshape(1, -1)
    new_k, new_v = [], []
    for l in range(DEPTH):
        mod_l = mod[l].reshape(COND_ROWS, 1, 6 * D_MODEL)
        g_mix = row_vec(norm_mix_g[l])
        if l % 2 == 0:
            e = l // 2
            w_in, w_out = w_in_even_bf[e], w_out_even_bf[e]
            qg = row_vec(jnp.tile(q_norm_g[e], N_HEADS))
            kg = row_vec(jnp.tile(k_norm_g[e], N_KV_HEADS))
            sink = jnp.repeat(sink_logit[e], BLOCK).reshape(1, N_HEADS * BLOCK)
            cw = short_conv_w[e]
            qp, kp, vp, bp, chp = _in_even(xp, mod_l, row_p, g_mix, w_in, qg, kg, None, tm_p, SEQ)
            new_k.append(kp.reshape(BATCH, SEQ, N_KV_HEADS, HEAD_DIM))
            new_v.append(vp.reshape(BATCH, SEQ, N_KV_HEADS, HEAD_DIM))
            attn_p = _attn_ctx(qp, kp, vp, sink)
            xp = _mix_out(attn_p, bp, chp, xp, mod_l, row_p, cw, w_out, SEQ, SEQ)
            qs, ks, vs, bs, chs = _in_even(xs, mod_l, row_s, g_mix, w_in, qg, kg, rope_tabs, tm_s, DEC_SEQ)
            ck = cache_k[:, e].reshape(DEC_BATCH, PAST_LEN, KV_WIDTH)
            cv = cache_v[:, e].reshape(DEC_BATCH, PAST_LEN, KV_WIDTH)
            attn_s = _attn_win(qs, ks, vs, ck, cv, sink)
            xs = _mix_out(attn_s, bs, chs, xs, mod_l, row_s, cw, w_out, tm_s, DEC_SEQ)
        else:
            o = l // 2
            w_in, w_out = w_in_odd_bf[o], w_out_odd_bf[o]
            vg = row_vec(gmlp_norm_g[o])
            ws = w_spatial_bf[o].reshape(GMLP_GROUPS, CHUNK, CHUNK)
            bs_full = jnp.broadcast_to(b_spatial[o][:, :, None], (GMLP_GROUPS, CHUNK, GMLP_GROUP_DIM))
            xp = _gmlp(xp, mod_l, row_p, g_mix, w_in, vg, ws, bs_full, w_out, tm_p)
            xs = _gmlp(xs, mod_l, row_s, g_mix, w_in, vg, ws, bs_full, w_out, tm_s)
        g_ffn = row_vec(norm_ffn_g[l])
        xp = _ffn(xp, mod_l, row_p, g_ffn, w_up_bf, ffn_conv_w, w_down_bf, l, tm_p, SEQ)
        xs = _ffn(xs, mod_l, row_s, g_ffn, w_up_bf, ffn_conv_w, w_down_bf, l, tm_s, DEC_SEQ)
    return (xp.reshape(BATCH, SEQ, D_MODEL), xs.reshape(DEC_BATCH, DEC_SEQ, D_MODEL),
            jnp.stack(new_k, axis=1), jnp.stack(new_v, axis=1))
```

```python
import functools

import jax
import jax.numpy as jnp
from jax import lax
from jax.experimental import pallas as pl
from jax.experimental.pallas import tpu as pltpu

F32 = jnp.float32
BF16 = jnp.bfloat16

D_MODEL = 1024
BATCH = 16
SEQ = 256
DEPTH = 2
DEC_BATCH = 4
DEC_SEQ = 2048
PAST_LEN = 512
GRID_W = 64
N_HEADS = 8
N_KV_HEADS = 2
HEAD_DIM = 64
GQA_GROUP = N_HEADS // N_KV_HEADS
ATTN_WIDTH = N_HEADS * HEAD_DIM
KV_WIDTH = N_KV_HEADS * HEAD_DIM
WINDOW = 128
BLOCK = 128
ROPE_BASE = 10000.0
CONV_WIDTH = 512
CHUNK = 128
GMLP_WIDTH = 1024
GMLP_GROUPS = 8
GMLP_GROUP_DIM = GMLP_WIDTH // GMLP_GROUPS
D_FF = 2816
EPS = 1e-6
NEG_INF = -1e30
N_EVEN = (DEPTH + 1) // 2
N_ODD = DEPTH // 2
LOG2_E = 1.4426950408889634

LANES = 128
SUBLANES_F32 = 8
SUBLANES_BF16 = 16
MXU_WIDTH = 256
VMEM_LIMIT_BYTES = 56 * 1024 * 1024

COND_ROWS = 8
CTX_ROW = DEC_BATCH
TOKEN_HALO = SUBLANES_BF16
FFN_COLS = MXU_WIDTH
CAST_BLOCK_BYTES = 3 * 1024 * 1024
ADA_TN = 1536
ATTN_Q_BLOCKS = 2
SUM_ROWS = SUBLANES_BF16
SCORE_CAP = 3.0e38


def _params(n_axes):
    return pltpu.CompilerParams(dimension_semantics=("parallel",) * n_axes,
                                vmem_limit_bytes=VMEM_LIMIT_BYTES)


def _resident(shape):
    zeros = (0,) * len(shape)
    return pl.BlockSpec(shape, lambda *_: zeros, pipeline_mode=pl.Buffered(1))


def _layer_slab(shape, layer):
    return pl.BlockSpec((1,) + tuple(shape[1:]), lambda *_: (layer, 0, 0), pipeline_mode=pl.Buffered(1))


def _mod_spec(mod_row):
    return pl.BlockSpec((1, 1, 6 * D_MODEL), lambda *idx: (mod_row(*idx), 0, 0))


def _dot(a, b):
    return jnp.dot(a, b, preferred_element_type=F32)


def _cast_kernel(x_ref, o_ref):
    o_ref[...] = x_ref[...].astype(BF16)


def _to_bf16(w):
    n, rows, cols = w.shape
    fits = [r for r in range(SUBLANES_BF16, rows + 1, SUBLANES_BF16)
            if rows % r == 0 and r * cols * 4 <= CAST_BLOCK_BYTES]
    tr = max(fits)
    spec = pl.BlockSpec((1, tr, cols), lambda i, j: (i, j, 0))
    return pl.pallas_call(
        _cast_kernel, grid=(n, rows // tr), in_specs=[spec], out_specs=spec,
        out_shape=jax.ShapeDtypeStruct(w.shape, BF16), compiler_params=_params(2), name="to_bf16",
    )(w)


def _modulate(x, g, shift, scale):
    ms = jnp.mean(x * x, axis=-1, keepdims=True)
    return (x * lax.rsqrt(ms + EPS) * g) * (1.0 + scale) + shift


def _mod_chunk(mod_ref, k):
    return mod_ref[0, :, k * D_MODEL:(k + 1) * D_MODEL]


def _halo_rows(tiles_per_seq):
    return TOKEN_HALO if tiles_per_seq > 1 else 0


def _token_specs(T, tm, tiles_per_seq):
    tok = pl.BlockSpec((tm, D_MODEL), lambda i: (i, 0))
    if tiles_per_seq == 1:
        return [tok]
    per_tile = tm // TOKEN_HALO
    last = T // TOKEN_HALO - 1
    prev = pl.BlockSpec((TOKEN_HALO, D_MODEL), lambda i: (jnp.maximum(i * per_tile - 1, 0), 0))
    nxt = pl.BlockSpec((TOKEN_HALO, D_MODEL), lambda i: (jnp.minimum((i + 1) * per_tile, last), 0))
    return [prev, tok, nxt]


def _modulated_rows(x_refs, h_scr, g, shift, scale, tiles_per_seq):
    if tiles_per_seq == 1:
        return _modulate(x_refs[0][...], g, shift, scale).astype(BF16)
    xp_ref, x_ref, xn_ref = x_refs
    tm = x_ref.shape[0]
    pos = pl.program_id(0) % tiles_per_seq
    hp = jnp.where(pos > 0, _modulate(xp_ref[...], g, shift, scale), 0.0)
    hn = jnp.where(pos < tiles_per_seq - 1, _modulate(xn_ref[...], g, shift, scale), 0.0)
    h_scr[:TOKEN_HALO, :] = hp.astype(BF16)
    h_scr[TOKEN_HALO:TOKEN_HALO + tm, :] = _modulate(x_ref[...], g, shift, scale).astype(BF16)
    h_scr[TOKEN_HALO + tm:, :] = hn.astype(BF16)
    return h_scr[...]


def _seq_end_masks(tm, cols, seq_len, tiles_per_seq):
    if tiles_per_seq > 1:
        return None
    seq_row = lax.broadcasted_iota(jnp.int32, (tm, cols), 0) % seq_len
    return seq_row != 0, seq_row != seq_len - 1


def _token_conv3(z, w, tm, halo, masks):
    rows = z.shape[0]
    mid = slice(halo, halo + tm)
    dn = pltpu.roll(z, 1, 0)[mid]
    up = pltpu.roll(z, rows - 1, 0)[mid]
    if masks is not None:
        dn = jnp.where(masks[0], dn, 0.0)
        up = jnp.where(masks[1], up, 0.0)
    return dn * w[0:1] + z[mid] * w[1:2] + up * w[2:3]


def _adaln_kernel(cond_ref, w_ref, b_ref, o_ref):
    a = jax.nn.silu(cond_ref[...]).astype(BF16)
    o_ref[0] = _dot(a, w_ref[0].astype(BF16)) + b_ref[0]


def _adaln(cond, ada_w, ada_b):
    n_out = 6 * D_MODEL
    return pl.pallas_call(
        _adaln_kernel,
        grid=(DEPTH, n_out // ADA_TN),
        in_specs=[pl.BlockSpec((COND_ROWS, D_MODEL), lambda l, j: (0, 0)),
                  pl.BlockSpec((1, D_MODEL, ADA_TN), lambda l, j: (l, 0, j)),
                  pl.BlockSpec((1, 1, ADA_TN), lambda l, j: (l, 0, j))],
        out_specs=pl.BlockSpec((1, COND_ROWS, ADA_TN), lambda l, j: (l, 0, j)),
        out_shape=jax.ShapeDtypeStruct((DEPTH, COND_ROWS, n_out), F32),
        compiler_params=_params(2),
        name="adaln",
    )(cond, ada_w, ada_b.reshape(DEPTH, 1, n_out))


def _head_rms(z, gain):
    n = z.shape[1]
    w = min(n, MXU_WIDTH)
    r = lax.broadcasted_iota(jnp.int32, (w, w), 0) // HEAD_DIM
    c = lax.broadcasted_iota(jnp.int32, (w, w), 1) // HEAD_DIM
    ones = (r == c).astype(BF16)
    sq = (z * z).astype(BF16)
    ss = jnp.concatenate([_dot(sq[:, k:k + w], ones) for k in range(0, n, w)], axis=1)
    return z * lax.rsqrt(ss * (1.0 / HEAD_DIM) + EPS) * gain


def _rope(z, cos, sin_lo, sin_hi):
    outs = []
    for k in range(0, z.shape[1], LANES):
        blk = z[:, k:k + LANES]
        outs.append(blk * cos + pltpu.roll(blk, LANES - 16, 1) * sin_lo + pltpu.roll(blk, 16, 1) * sin_hi)
    return jnp.concatenate(outs, axis=1)


def _in_even_kernel(*refs, rope, tiles_per_seq, seq_len):
    halo = _halo_rows(tiles_per_seq)
    n_x = 3 if halo else 1
    x_refs, refs = refs[:n_x], refs[n_x:]
    mod_ref, g_ref, w_ref, qg_ref, kg_ref, cw_ref = refs[:6]
    refs = refs[6:]
    if rope:
        cos_ref, slo_ref, shi_ref = refs[:3]
        refs = refs[3:]
    q_ref, k_ref, v_ref, sc_ref = refs[:4]
    h_scr = refs[4] if halo else None
    tm = q_ref.shape[0]
    h_all = _modulated_rows(x_refs, h_scr, g_ref[...], _mod_chunk(mod_ref, 0), _mod_chunk(mod_ref, 1),
                            tiles_per_seq)
    h = h_scr[halo:halo + tm, :] if halo else h_all
    c0 = 0
    c1 = c0 + ATTN_WIDTH
    c2 = c1 + 2 * KV_WIDTH
    c3 = c2 + CONV_WIDTH
    c4 = c3 + CONV_WIDTH
    c5 = c4 + CONV_WIDTH
    q = _head_rms(_dot(h, w_ref[:, c0:c1]), qg_ref[...] * (HEAD_DIM ** -0.5 * LOG2_E))
    zkv = _dot(h, w_ref[:, c1:c2])
    k = _head_rms(zkv[:, :KV_WIDTH], kg_ref[...])
    if rope:
        cos, slo, shi = cos_ref[...], slo_ref[...], shi_ref[...]
        q = _rope(q, cos, slo, shi)
        k = _rope(k, cos, slo, shi)
    q_ref[...] = q
    k_ref[...] = k
    v_ref[...] = zkv[:, KV_WIDTH:]
    ch = _dot(h_all, w_ref[:, c3:c4]) * _dot(h_all, w_ref[:, c4:c5])
    conv = _token_conv3(ch, cw_ref[...], tm, halo, _seq_end_masks(tm, CONV_WIDTH, seq_len, tiles_per_seq))
    sc_ref[...] = (_dot(h, w_ref[:, c2:c3]) * conv).astype(BF16)


def _in_even(x, mod_l, mod_row, g, w_bf, qg, kg, conv_w, rope_tabs, tm, seq_len):
    T = x.shape[0]
    n_in = w_bf.shape[1]
    tiles_per_seq = max(seq_len // tm, 1)
    assert tm % seq_len == 0 or seq_len % tm == 0
    tok = lambda w: pl.BlockSpec((tm, w), lambda i: (i, 0))
    x_specs = _token_specs(T, tm, tiles_per_seq)
    in_specs = x_specs + [_mod_spec(mod_row), _resident((1, D_MODEL)), _resident((D_MODEL, n_in)),
                          _resident((1, ATTN_WIDTH)), _resident((1, KV_WIDTH)), _resident((3, CONV_WIDTH))]
    args = [x] * len(x_specs) + [mod_l, g, w_bf, qg, kg, conv_w]
    if rope_tabs is not None:
        in_specs += [pl.BlockSpec((tm, LANES), lambda i: (i % tiles_per_seq, 0))] * 3
        args += list(rope_tabs)
    outs = ((ATTN_WIDTH, F32), (KV_WIDTH, F32), (KV_WIDTH, F32), (CONV_WIDTH, BF16))
    scratch = [pltpu.VMEM((tm + 2 * TOKEN_HALO, D_MODEL), BF16)] if tiles_per_seq > 1 else []
    return pl.pallas_call(
        functools.partial(_in_even_kernel, rope=rope_tabs is not None, tiles_per_seq=tiles_per_seq,
                          seq_len=seq_len),
        grid=(T // tm,),
        in_specs=in_specs,
        out_specs=[tok(w) for w, _ in outs],
        out_shape=[jax.ShapeDtypeStruct((T, w), dt) for w, dt in outs],
        scratch_shapes=scratch,
        compiler_params=_params(1),
        name="in_even",
    )(*args)


def _stack_heads(q):
    lane_lo = lax.broadcasted_iota(jnp.int32, (BLOCK, LANES), 1) < HEAD_DIM
    heads = []
    for pair in range(N_HEADS // 2):
        kv = (2 * pair) // GQA_GROUP
        qp = q[:, pair * LANES:(pair + 1) * LANES]
        qr = pltpu.roll(qp, HEAD_DIM, 1)
        for half in range(2):
            src = qp if half == kv else qr
            heads.append(jnp.where(lane_lo, src, 0.0) if kv == 0 else jnp.where(lane_lo, 0.0, src))
    return jnp.concatenate(heads, axis=0).astype(BF16)


def _transposed_values(v_blocks):
    vt = jnp.concatenate([v[j:j + BLOCK].T for v in v_blocks for j in range(0, v.shape[0], BLOCK)], axis=1)
    return jnp.concatenate([vt.astype(BF16), jnp.ones((SUM_ROWS, vt.shape[1]), BF16)], axis=0)


def _attend(q, kcat, vt, cap, sink_row):
    st = lax.dot_general(kcat, _stack_heads(q), (((1,), (1,)), ((), ())), preferred_element_type=F32)
    if cap is not None:
        n_band = cap.shape[0]
        capped = jnp.minimum(st[:n_band], jnp.concatenate([cap] * N_HEADS, axis=1))
        st = jnp.concatenate([capped, st[n_band:]], axis=0)
    m = jnp.maximum(jnp.max(st, axis=0, keepdims=True), sink_row)
    pt = jnp.exp2(st - m).astype(BF16)
    ot = _dot(vt, pt)
    denom = ot[KV_WIDTH:KV_WIDTH + 1] + jnp.exp2(sink_row - m)
    ot = ot[:KV_WIDTH] / denom
    pairs = []
    for pair in range(N_HEADS // 2):
        kv = (2 * pair) // GQA_GROUP
        dims = slice(kv * HEAD_DIM, (kv + 1) * HEAD_DIM)
        both = [ot[dims, (2 * pair + half) * BLOCK:(2 * pair + half + 1) * BLOCK] for half in range(2)]
        pairs.append(jnp.concatenate(both, axis=0).T)
    return jnp.concatenate(pairs, axis=1)


def _project_out(attn_scr, sc_ref, x_ref, mod_ref, w_ref, o_ref):
    mix = _dot(attn_scr[...], w_ref[:ATTN_WIDTH, :]) + _dot(sc_ref[...], w_ref[ATTN_WIDTH:, :])
    o_ref[...] = x_ref[...] + _mod_chunk(mod_ref, 2) * mix


def _attn_ctx_kernel(sink_ref, q_ref, k_ref, v_ref, sc_ref, x_ref, mod_ref, w_ref, o_ref, attn_scr):
    kcat = k_ref[...].astype(BF16)
    vt = _transposed_values([v_ref[...]])
    sink_row = sink_ref[...] * LOG2_E
    for j in range(0, SEQ, BLOCK):
        attn_scr[j:j + BLOCK, :] = _attend(q_ref[j:j + BLOCK, :], kcat, vt, None, sink_row).astype(BF16)
    _project_out(attn_scr, sc_ref, x_ref, mod_ref, w_ref, o_ref)


def _attn_ctx(q, k, v, sc, x, mod_l, sink_row, w_out_bf):
    T = q.shape[0]
    tok = lambda w: pl.BlockSpec((SEQ, w), lambda i: (i, 0))
    return pl.pallas_call(
        _attn_ctx_kernel,
        grid=(T // SEQ,),
        in_specs=[_resident((1, N_HEADS * BLOCK)), tok(ATTN_WIDTH), tok(KV_WIDTH), tok(KV_WIDTH),
                  tok(CONV_WIDTH), tok(D_MODEL), _mod_spec(lambda i: CTX_ROW),
                  _resident((ATTN_WIDTH + CONV_WIDTH, D_MODEL))],
        out_specs=tok(D_MODEL),
        out_shape=jax.ShapeDtypeStruct((T, D_MODEL), F32),
        scratch_shapes=[pltpu.VMEM((SEQ, ATTN_WIDTH), BF16)],
        compiler_params=_params(1),
        name="attn_ctx",
    )(sink_row, q, k, v, sc, x, mod_l, w_out_bf)


def _band_cap(has_prev, has_next):
    c = lax.broadcasted_iota(jnp.int32, (3 * BLOCK, BLOCK), 0)
    r = lax.broadcasted_iota(jnp.int32, (3 * BLOCK, BLOCK), 1)
    first_prev = r + jnp.where(has_prev, 0, BLOCK)
    last_next = r + 2 * BLOCK - jnp.where(has_next, 0, BLOCK)
    masked = ((c < BLOCK) & (c < first_prev)) | ((c >= 2 * BLOCK) & (c > last_next))
    return jnp.where(masked, NEG_INF, SCORE_CAP)


def _attn_win_kernel(sink_ref, q_ref, kp_ref, kc_ref, kn_ref, vp_ref, vc_ref, vn_ref, ck_ref, cv_ref,
                     sc_ref, x_ref, mod_ref, w_ref, o_ref, attn_scr, *, n_steps):
    i = pl.program_id(1)
    kc, vc = kc_ref[...], vc_ref[...]
    inner = range(0, ATTN_Q_BLOCKS * BLOCK, BLOCK)
    k_blocks = [kp_ref[...]] + [kc[j:j + BLOCK] for j in inner] + [kn_ref[...]]
    v_blocks = [vp_ref[...]] + [vc[j:j + BLOCK] for j in inner] + [vn_ref[...]]
    sink_row = sink_ref[...] * LOG2_E
    for j in range(ATTN_Q_BLOCKS):
        kcat = jnp.concatenate(k_blocks[j:j + 3] + [ck_ref[0]], axis=0).astype(BF16)
        vt = _transposed_values(v_blocks[j:j + 3] + [cv_ref[0]])
        cap = _band_cap(i > 0 if j == 0 else True, i < n_steps - 1 if j == ATTN_Q_BLOCKS - 1 else True)
        rows = slice(j * BLOCK, (j + 1) * BLOCK)
        attn_scr[rows, :] = _attend(q_ref[rows, :], kcat, vt, cap, sink_row).astype(BF16)
    _project_out(attn_scr, sc_ref, x_ref, mod_ref, w_ref, o_ref)


def _attn_win(q, k, v, ck, cv, sc, x, mod_l, sink_row, w_out_bf):
    T = q.shape[0]
    tq = ATTN_Q_BLOCKS * BLOCK
    n_steps = DEC_SEQ // tq
    nb = DEC_SEQ // BLOCK
    cur = lambda w: pl.BlockSpec((tq, w), lambda b, i: (b * n_steps + i, 0))
    prev = pl.BlockSpec((BLOCK, KV_WIDTH), lambda b, i: (b * nb + jnp.maximum(ATTN_Q_BLOCKS * i - 1, 0), 0))
    nxt = pl.BlockSpec((BLOCK, KV_WIDTH),
                       lambda b, i: (b * nb + jnp.minimum(ATTN_Q_BLOCKS * (i + 1), nb - 1), 0))
    ctx = pl.BlockSpec((1, PAST_LEN, KV_WIDTH), lambda b, i: (b, 0, 0))
    return pl.pallas_call(
        functools.partial(_attn_win_kernel, n_steps=n_steps),
        grid=(T // DEC_SEQ, n_steps),
        in_specs=[_resident((1, N_HEADS * BLOCK)), cur(ATTN_WIDTH),
                  prev, cur(KV_WIDTH), nxt, prev, cur(KV_WIDTH), nxt, ctx, ctx,
                  cur(CONV_WIDTH), cur(D_MODEL), _mod_spec(lambda b, i: b),
                  _resident((ATTN_WIDTH + CONV_WIDTH, D_MODEL))],
        out_specs=cur(D_MODEL),
        out_shape=jax.ShapeDtypeStruct((T, D_MODEL), F32),
        scratch_shapes=[pltpu.VMEM((tq, ATTN_WIDTH), BF16)],
        compiler_params=_params(2),
        name="attn_win",
    )(sink_row, q, k, k, k, v, v, v, ck, cv, sc, x, mod_l, w_out_bf)


def _gelu_tanh(x):
    k = 0.7978845608028654
    half = 0.5 * x
    return half + half * jnp.tanh(x * (k + (k * 0.044715) * (x * x)))


def _gmlp_kernel(x_ref, mod_ref, g_ref, win_ref, vg_ref, ws_ref, bs_ref, wout_ref, o_ref, gated_scr):
    x = x_ref[...]
    n_chunks = x.shape[0] // CHUNK
    h = _modulate(x, g_ref[...], _mod_chunk(mod_ref, 0), _mod_chunk(mod_ref, 1)).astype(BF16)
    u = _gelu_tanh(_dot(h, win_ref[:, :GMLP_WIDTH]))
    v = _gelu_tanh(_dot(h, win_ref[:, GMLP_WIDTH:]))
    ms = jnp.mean(v * v, axis=-1, keepdims=True)
    v = (v * lax.rsqrt(ms + EPS) * vg_ref[...]).astype(BF16)
    for grp in range(GMLP_GROUPS):
        lanes = slice(grp * GMLP_GROUP_DIM, (grp + 1) * GMLP_GROUP_DIM)
        rhs = jnp.concatenate([v[n * CHUNK:(n + 1) * CHUNK, lanes] for n in range(n_chunks)], axis=1)
        s = _dot(ws_ref[grp], rhs)
        for n in range(n_chunks):
            rows = slice(n * CHUNK, (n + 1) * CHUNK)
            s_n = s[:, n * GMLP_GROUP_DIM:(n + 1) * GMLP_GROUP_DIM] + bs_ref[grp]
            gated_scr[rows, lanes] = (u[rows, lanes] * s_n).astype(BF16)
    o_ref[...] = x + _mod_chunk(mod_ref, 2) * _dot(gated_scr[...], wout_ref[...])


def _gmlp(x, mod_l, mod_row, g, win_bf, vg, ws_bf, bs_full, wout_bf, tm):
    T = x.shape[0]
    tok = pl.BlockSpec((tm, D_MODEL), lambda i: (i, 0))
    return pl.pallas_call(
        _gmlp_kernel,
        grid=(T // tm,),
        in_specs=[tok, _mod_spec(mod_row),
                  _resident((1, D_MODEL)), _resident((D_MODEL, 2 * GMLP_WIDTH)), _resident((1, GMLP_WIDTH)),
                  _resident((GMLP_GROUPS, CHUNK, CHUNK)), _resident((GMLP_GROUPS, CHUNK, GMLP_GROUP_DIM)),
                  _resident((GMLP_WIDTH, D_MODEL))],
        out_specs=tok,
        out_shape=jax.ShapeDtypeStruct((T, D_MODEL), F32),
        scratch_shapes=[pltpu.VMEM((tm, GMLP_WIDTH), BF16)],
        compiler_params=_params(1),
        name="gmlp",
    )(x, mod_l, g, win_bf, vg, ws_bf, bs_full, wout_bf)


def _ffn_kernel(*refs, tiles_per_seq, seq_len):
    halo = _halo_rows(tiles_per_seq)
    n_x = 3 if halo else 1
    x_refs, refs = refs[:n_x], refs[n_x:]
    mod_ref, g_ref, wup_ref, cw_ref, wdn_ref, o_ref, act_scr = refs[:7]
    h_scr = refs[7] if halo else None
    tm = o_ref.shape[0]
    h = _modulated_rows(x_refs, h_scr, g_ref[...], _mod_chunk(mod_ref, 3), _mod_chunk(mod_ref, 4), tiles_per_seq)
    masks = _seq_end_masks(tm, FFN_COLS, seq_len, tiles_per_seq)
    for j in range(0, D_FF, FFN_COLS):
        gate_cols = slice(j, j + FFN_COLS)
        val_cols = slice(D_FF + j, D_FF + j + FFN_COLS)
        zg = _token_conv3(_dot(h, wup_ref[0, :, gate_cols]), cw_ref[0, :, gate_cols], tm, halo, masks)
        zv = _token_conv3(_dot(h, wup_ref[0, :, val_cols]), cw_ref[0, :, val_cols], tm, halo, masks)
        act_scr[:, gate_cols] = (jax.nn.silu(zg) * zv).astype(BF16)
    x = x_refs[n_x // 2][...]
    o_ref[...] = x + _mod_chunk(mod_ref, 5) * _dot(act_scr[...], wdn_ref[0])


def _ffn(x, mod_l, mod_row, g, wup_bf, conv_w, wdn_bf, layer, tm, seq_len):
    T = x.shape[0]
    tiles_per_seq = max(seq_len // tm, 1)
    assert tm % seq_len == 0 or seq_len % tm == 0
    x_specs = _token_specs(T, tm, tiles_per_seq)
    in_specs = x_specs + [_mod_spec(mod_row), _resident((1, D_MODEL)), _layer_slab(wup_bf.shape, layer),
                          _layer_slab(conv_w.shape, layer), _layer_slab(wdn_bf.shape, layer)]
    scratch = [pltpu.VMEM((tm, D_FF), BF16)]
    if tiles_per_seq > 1:
        scratch.append(pltpu.VMEM((tm + 2 * TOKEN_HALO, D_MODEL), BF16))
    return pl.pallas_call(
        functools.partial(_ffn_kernel, tiles_per_seq=tiles_per_seq, seq_len=seq_len),
        grid=(T // tm,),
        in_specs=in_specs,
        out_specs=pl.BlockSpec((tm, D_MODEL), lambda i: (i, 0)),
        out_shape=jax.ShapeDtypeStruct((T, D_MODEL), F32),
        scratch_shapes=scratch,
        compiler_params=_params(1),
        name="conv_ffn",
    )(*([x] * len(x_specs) + [mod_l, g, wup_bf, conv_w, wdn_bf]))


def _rope_tables(n_tokens):
    t = jnp.arange(n_tokens)
    n_freq = HEAD_DIM // 4
    inv = ROPE_BASE ** (-jnp.arange(n_freq, dtype=F32) / n_freq)
    row_ang = (t // GRID_W).astype(F32)[:, None] * inv
    col_ang = (t % GRID_W).astype(F32)[:, None] * inv
    ang = jnp.concatenate([row_ang, row_ang, col_ang, col_ang], axis=1)
    ang = jnp.tile(ang, (1, LANES // HEAD_DIM))
    first = (jnp.arange(LANES) % (2 * n_freq)) < n_freq
    cos, sin = jnp.cos(ang), jnp.sin(ang)
    return cos, jnp.where(first, -sin, 0.0), jnp.where(first, 0.0, sin)


def kernel(x_prompt, x_sample, cache_k, cache_v, c, c_ctx, ada_w, ada_b, norm_mix_g, norm_ffn_g, w_in_even,
           q_norm_g, k_norm_g, sink_logit, short_conv_w, w_out_even, w_in_odd, gmlp_norm_g, w_spatial,
           b_spatial, w_out_odd, w_up, ffn_conv_w, w_down):
    n_p, n_s = BATCH * SEQ, DEC_BATCH * DEC_SEQ
    xp = x_prompt.reshape(n_p, D_MODEL)
    xs = x_sample.reshape(n_s, D_MODEL)
    cond = jnp.concatenate([c, c_ctx[None, :], jnp.zeros((COND_ROWS - DEC_BATCH - 1, D_MODEL), F32)], axis=0)
    mod = _adaln(cond, ada_w, ada_b)
    rope_tabs = _rope_tables(DEC_SEQ)
    w_in_even_bf, w_out_even_bf = _to_bf16(w_in_even), _to_bf16(w_out_even)
    w_in_odd_bf, w_out_odd_bf = _to_bf16(w_in_odd), _to_bf16(w_out_odd)
    w_spatial_bf = _to_bf16(w_spatial.reshape(N_ODD, GMLP_GROUPS * CHUNK, CHUNK))
    w_up_bf, w_down_bf = _to_bf16(w_up), _to_bf16(w_down)

    tm_p, tm_s = 512, 512
    row_p = lambda i: CTX_ROW
    row_s = lambda i: i // (DEC_SEQ // tm_s)
    row_vec = lambda a: a.reshape(1, -1)
    new_k, new_v = [], []
    for l in range(DEPTH):
        mod_l = mod[l].reshape(COND_ROWS, 1, 6 * D_MODEL)
        g_mix = row_vec(norm_mix_g[l])
        if l % 2 == 0:
            e = l // 2
            w_in, w_out = w_in_even_bf[e], w_out_even_bf[e]
            qg = row_vec(jnp.tile(q_norm_g[e], N_HEADS))
            kg = row_vec(jnp.tile(k_norm_g[e], N_KV_HEADS))
            sink = jnp.repeat(sink_logit[e], BLOCK).reshape(1, N_HEADS * BLOCK)
            cw = short_conv_w[e]
            qp, kp, vp, scp = _in_even(xp, mod_l, row_p, g_mix, w_in, qg, kg, cw, None, tm_p, SEQ)
            new_k.append(kp.reshape(BATCH, SEQ, N_KV_HEADS, HEAD_DIM))
            new_v.append(vp.reshape(BATCH, SEQ, N_KV_HEADS, HEAD_DIM))
            xp = _attn_ctx(qp, kp, vp, scp, xp, mod_l, sink, w_out)
            qs, ks, vs, scs = _in_even(xs, mod_l, row_s, g_mix, w_in, qg, kg, cw, rope_tabs, tm_s, DEC_SEQ)
            ck = cache_k[:, e].reshape(DEC_BATCH, PAST_LEN, KV_WIDTH)
            cv = cache_v[:, e].reshape(DEC_BATCH, PAST_LEN, KV_WIDTH)
            xs = _attn_win(qs, ks, vs, ck, cv, scs, xs, mod_l, sink, w_out)
        else:
            o = l // 2
            w_in, w_out = w_in_odd_bf[o], w_out_odd_bf[o]
            vg = row_vec(gmlp_norm_g[o])
            ws = w_spatial_bf[o].reshape(GMLP_GROUPS, CHUNK, CHUNK)
            bs_full = jnp.broadcast_to(b_spatial[o][:, :, None], (GMLP_GROUPS, CHUNK, GMLP_GROUP_DIM))
            xp = _gmlp(xp, mod_l, row_p, g_mix, w_in, vg, ws, bs_full, w_out, tm_p)
            xs = _gmlp(xs, mod_l, row_s, g_mix, w_in, vg, ws, bs_full, w_out, tm_s)
        g_ffn = row_vec(norm_ffn_g[l])
        xp = _ffn(xp, mod_l, row_p, g_ffn, w_up_bf, ffn_conv_w, w_down_bf, l, tm_p, SEQ)
        xs = _ffn(xs, mod_l, row_s, g_ffn, w_up_bf, ffn_conv_w, w_down_bf, l, tm_s, DEC_SEQ)
    return (xp.reshape(BATCH, SEQ, D_MODEL), xs.reshape(DEC_BATCH, DEC_SEQ, D_MODEL),
            jnp.stack(new_k, axis=1), jnp.stack(new_v, axis=1))
```

```python
import functools

import jax
import jax.numpy as jnp
from jax import lax
from jax.experimental import pallas as pl
from jax.experimental.pallas import tpu as pltpu

F32 = jnp.float32
BF16 = jnp.bfloat16

D_MODEL = 1024
BATCH = 16
SEQ = 256
DEPTH = 2
DEC_BATCH = 4
DEC_SEQ = 2048
PAST_LEN = 512
GRID_W = 64
N_HEADS = 8
N_KV_HEADS = 2
HEAD_DIM = 64
GQA_GROUP = N_HEADS // N_KV_HEADS
ATTN_WIDTH = N_HEADS * HEAD_DIM
KV_WIDTH = N_KV_HEADS * HEAD_DIM
WINDOW = 128
BLOCK = 128
ROPE_BASE = 10000.0
CONV_WIDTH = 512
CHUNK = 128
GMLP_WIDTH = 1024
GMLP_GROUPS = 8
GMLP_GROUP_DIM = GMLP_WIDTH // GMLP_GROUPS
D_FF = 2816
EPS = 1e-6
NEG_INF = -1e30
N_EVEN = (DEPTH + 1) // 2
N_ODD = DEPTH // 2
LOG2_E = 1.4426950408889634

LANES = 128
SUBLANES_F32 = 8
SUBLANES_BF16 = 16
MXU_WIDTH = 256
VMEM_LIMIT_BYTES = 56 * 1024 * 1024

COND_ROWS = 8
CTX_ROW = DEC_BATCH
TOKEN_HALO = SUBLANES_BF16
FFN_COLS = MXU_WIDTH
CAST_BLOCK_BYTES = 3 * 1024 * 1024
ADA_TN = 1536
ATTN_Q_BLOCKS = 4
SUM_ROWS = SUBLANES_BF16
SCORE_CAP = 3.0e38


def _params(n_axes):
    return pltpu.CompilerParams(dimension_semantics=("parallel",) * n_axes,
                                vmem_limit_bytes=VMEM_LIMIT_BYTES)


def _resident(shape):
    zeros = (0,) * len(shape)
    return pl.BlockSpec(shape, lambda *_: zeros, pipeline_mode=pl.Buffered(1))


def _layer_slab(shape, layer):
    return pl.BlockSpec((1,) + tuple(shape[1:]), lambda *_: (layer, 0, 0), pipeline_mode=pl.Buffered(1))


def _mod_spec(mod_row):
    return pl.BlockSpec((1, 1, 6 * D_MODEL), lambda *idx: (mod_row(*idx), 0, 0))


def _dot(a, b):
    return jnp.dot(a, b, preferred_element_type=F32)


def _cast_kernel(x_ref, o_ref):
    o_ref[...] = x_ref[...].astype(BF16)


def _to_bf16(w):
    n, rows, cols = w.shape
    fits = [r for r in range(SUBLANES_BF16, rows + 1, SUBLANES_BF16)
            if rows % r == 0 and r * cols * 4 <= CAST_BLOCK_BYTES]
    tr = max(fits)
    spec = pl.BlockSpec((1, tr, cols), lambda i, j: (i, j, 0))
    return pl.pallas_call(
        _cast_kernel, grid=(n, rows // tr), in_specs=[spec], out_specs=spec,
        out_shape=jax.ShapeDtypeStruct(w.shape, BF16), compiler_params=_params(2), name="to_bf16",
    )(w)


def _modulate(x, g, shift, scale):
    ms = jnp.mean(x * x, axis=-1, keepdims=True)
    return (x * lax.rsqrt(ms + EPS) * g) * (1.0 + scale) + shift


def _mod_chunk(mod_ref, k):
    return mod_ref[0, :, k * D_MODEL:(k + 1) * D_MODEL]


def _halo_rows(tiles_per_seq):
    return TOKEN_HALO if tiles_per_seq > 1 else 0


def _token_specs(T, tm, tiles_per_seq):
    tok = pl.BlockSpec((tm, D_MODEL), lambda i: (i, 0))
    if tiles_per_seq == 1:
        return [tok]
    per_tile = tm // TOKEN_HALO
    last = T // TOKEN_HALO - 1
    prev = pl.BlockSpec((TOKEN_HALO, D_MODEL), lambda i: (jnp.maximum(i * per_tile - 1, 0), 0))
    nxt = pl.BlockSpec((TOKEN_HALO, D_MODEL), lambda i: (jnp.minimum((i + 1) * per_tile, last), 0))
    return [prev, tok, nxt]


def _modulated_rows(x_refs, h_scr, g, shift, scale, tiles_per_seq):
    if tiles_per_seq == 1:
        return _modulate(x_refs[0][...], g, shift, scale).astype(BF16)
    xp_ref, x_ref, xn_ref = x_refs
    tm = x_ref.shape[0]
    pos = pl.program_id(0) % tiles_per_seq
    hp = jnp.where(pos > 0, _modulate(xp_ref[...], g, shift, scale), 0.0)
    hn = jnp.where(pos < tiles_per_seq - 1, _modulate(xn_ref[...], g, shift, scale), 0.0)
    h_scr[:TOKEN_HALO, :] = hp.astype(BF16)
    h_scr[TOKEN_HALO:TOKEN_HALO + tm, :] = _modulate(x_ref[...], g, shift, scale).astype(BF16)
    h_scr[TOKEN_HALO + tm:, :] = hn.astype(BF16)
    return h_scr[...]


def _seq_end_masks(tm, cols, seq_len, tiles_per_seq):
    if tiles_per_seq > 1:
        return None
    seq_row = lax.broadcasted_iota(jnp.int32, (tm, cols), 0) % seq_len
    return seq_row != 0, seq_row != seq_len - 1


def _token_conv3(z, w, tm, halo, masks):
    rows = z.shape[0]
    mid = slice(halo, halo + tm)
    dn = pltpu.roll(z, 1, 0)[mid]
    up = pltpu.roll(z, rows - 1, 0)[mid]
    if masks is not None:
        dn = jnp.where(masks[0], dn, 0.0)
        up = jnp.where(masks[1], up, 0.0)
    return dn * w[0:1] + z[mid] * w[1:2] + up * w[2:3]


def _adaln_kernel(cond_ref, w_ref, b_ref, o_ref):
    a = jax.nn.silu(cond_ref[...]).astype(BF16)
    o_ref[0] = _dot(a, w_ref[0].astype(BF16)) + b_ref[0]


def _adaln(cond, ada_w, ada_b):
    n_out = 6 * D_MODEL
    return pl.pallas_call(
        _adaln_kernel,
        grid=(DEPTH, n_out // ADA_TN),
        in_specs=[pl.BlockSpec((COND_ROWS, D_MODEL), lambda l, j: (0, 0)),
                  pl.BlockSpec((1, D_MODEL, ADA_TN), lambda l, j: (l, 0, j)),
                  pl.BlockSpec((1, 1, ADA_TN), lambda l, j: (l, 0, j))],
        out_specs=pl.BlockSpec((1, COND_ROWS, ADA_TN), lambda l, j: (l, 0, j)),
        out_shape=jax.ShapeDtypeStruct((DEPTH, COND_ROWS, n_out), F32),
        compiler_params=_params(2),
        name="adaln",
    )(cond, ada_w, ada_b.reshape(DEPTH, 1, n_out))


def _head_rms(z, gain):
    n = z.shape[1]
    w = min(n, MXU_WIDTH)
    r = lax.broadcasted_iota(jnp.int32, (w, w), 0) // HEAD_DIM
    c = lax.broadcasted_iota(jnp.int32, (w, w), 1) // HEAD_DIM
    ones = (r == c).astype(BF16)
    sq = (z * z).astype(BF16)
    ss = jnp.concatenate([_dot(sq[:, k:k + w], ones) for k in range(0, n, w)], axis=1)
    return z * lax.rsqrt(ss * (1.0 / HEAD_DIM) + EPS) * gain


def _rope(z, cos, sin_lo, sin_hi):
    outs = []
    for k in range(0, z.shape[1], LANES):
        blk = z[:, k:k + LANES]
        outs.append(blk * cos + pltpu.roll(blk, LANES - 16, 1) * sin_lo + pltpu.roll(blk, 16, 1) * sin_hi)
    return jnp.concatenate(outs, axis=1)


def _in_even_kernel(*refs, rope, tiles_per_seq, seq_len):
    halo = _halo_rows(tiles_per_seq)
    n_x = 3 if halo else 1
    x_refs, refs = refs[:n_x], refs[n_x:]
    mod_ref, g_ref, w_ref, qg_ref, kg_ref, cw_ref = refs[:6]
    refs = refs[6:]
    if rope:
        cos_ref, slo_ref, shi_ref = refs[:3]
        refs = refs[3:]
    q_ref, k_ref, v_ref, sc_ref = refs[:4]
    h_scr = refs[4] if halo else None
    tm = q_ref.shape[0]
    h_all = _modulated_rows(x_refs, h_scr, g_ref[...], _mod_chunk(mod_ref, 0), _mod_chunk(mod_ref, 1),
                            tiles_per_seq)
    h = h_scr[halo:halo + tm, :] if halo else h_all
    c0 = 0
    c1 = c0 + ATTN_WIDTH
    c2 = c1 + 2 * KV_WIDTH
    c3 = c2 + CONV_WIDTH
    c4 = c3 + CONV_WIDTH
    c5 = c4 + CONV_WIDTH
    q = _head_rms(_dot(h, w_ref[:, c0:c1]), qg_ref[...] * (HEAD_DIM ** -0.5 * LOG2_E))
    zkv = _dot(h, w_ref[:, c1:c2])
    k = _head_rms(zkv[:, :KV_WIDTH], kg_ref[...])
    if rope:
        cos, slo, shi = cos_ref[...], slo_ref[...], shi_ref[...]
        q = _rope(q, cos, slo, shi)
        k = _rope(k, cos, slo, shi)
    q_ref[...] = q
    k_ref[...] = k
    v_ref[...] = zkv[:, KV_WIDTH:]
    ch = _dot(h_all, w_ref[:, c3:c4]) * _dot(h_all, w_ref[:, c4:c5])
    conv = _token_conv3(ch, cw_ref[...], tm, halo, _seq_end_masks(tm, CONV_WIDTH, seq_len, tiles_per_seq))
    sc_ref[...] = (_dot(h, w_ref[:, c2:c3]) * conv).astype(BF16)


def _in_even(x, mod_l, mod_row, g, w_bf, qg, kg, conv_w, rope_tabs, tm, seq_len):
    T = x.shape[0]
    n_in = w_bf.shape[1]
    tiles_per_seq = max(seq_len // tm, 1)
    assert tm % seq_len == 0 or seq_len % tm == 0
    tok = lambda w: pl.BlockSpec((tm, w), lambda i: (i, 0))
    x_specs = _token_specs(T, tm, tiles_per_seq)
    in_specs = x_specs + [_mod_spec(mod_row), _resident((1, D_MODEL)), _resident((D_MODEL, n_in)),
                          _resident((1, ATTN_WIDTH)), _resident((1, KV_WIDTH)), _resident((3, CONV_WIDTH))]
    args = [x] * len(x_specs) + [mod_l, g, w_bf, qg, kg, conv_w]
    if rope_tabs is not None:
        in_specs += [pl.BlockSpec((tm, LANES), lambda i: (i % tiles_per_seq, 0))] * 3
        args += list(rope_tabs)
    outs = ((ATTN_WIDTH, F32), (KV_WIDTH, F32), (KV_WIDTH, F32), (CONV_WIDTH, BF16))
    scratch = [pltpu.VMEM((tm + 2 * TOKEN_HALO, D_MODEL), BF16)] if tiles_per_seq > 1 else []
    return pl.pallas_call(
        functools.partial(_in_even_kernel, rope=rope_tabs is not None, tiles_per_seq=tiles_per_seq,
                          seq_len=seq_len),
        grid=(T // tm,),
        in_specs=in_specs,
        out_specs=[tok(w) for w, _ in outs],
        out_shape=[jax.ShapeDtypeStruct((T, w), dt) for w, dt in outs],
        scratch_shapes=scratch,
        compiler_params=_params(1),
        name="in_even",
    )(*args)


def _stack_heads(q):
    lane_lo = lax.broadcasted_iota(jnp.int32, (BLOCK, LANES), 1) < HEAD_DIM
    heads = []
    for pair in range(N_HEADS // 2):
        kv = (2 * pair) // GQA_GROUP
        qp = q[:, pair * LANES:(pair + 1) * LANES]
        qr = pltpu.roll(qp, HEAD_DIM, 1)
        for half in range(2):
            src = qp if half == kv else qr
            heads.append(jnp.where(lane_lo, src, 0.0) if kv == 0 else jnp.where(lane_lo, 0.0, src))
    return jnp.concatenate(heads, axis=0).astype(BF16)


def _transposed_values(v_blocks):
    vt = jnp.concatenate([v[j:j + BLOCK].T for v in v_blocks for j in range(0, v.shape[0], BLOCK)], axis=1)
    return jnp.concatenate([vt.astype(BF16), jnp.ones((SUM_ROWS, vt.shape[1]), BF16)], axis=0)


def _attend(q, kcat, vt, cap, sink_row):
    st = lax.dot_general(kcat, _stack_heads(q), (((1,), (1,)), ((), ())), preferred_element_type=F32)
    if cap is not None:
        n_band = cap.shape[0]
        capped = jnp.minimum(st[:n_band], jnp.concatenate([cap] * N_HEADS, axis=1))
        st = jnp.concatenate([capped, st[n_band:]], axis=0)
    m = jnp.maximum(jnp.max(st, axis=0, keepdims=True), sink_row)
    pt = jnp.exp2(st - m).astype(BF16)
    ot = _dot(vt, pt)
    denom = ot[KV_WIDTH:KV_WIDTH + 1] + jnp.exp2(sink_row - m)
    ot = ot[:KV_WIDTH] / denom
    pairs = []
    for pair in range(N_HEADS // 2):
        kv = (2 * pair) // GQA_GROUP
        dims = slice(kv * HEAD_DIM, (kv + 1) * HEAD_DIM)
        both = [ot[dims, (2 * pair + half) * BLOCK:(2 * pair + half + 1) * BLOCK] for half in range(2)]
        pairs.append(jnp.concatenate(both, axis=0).T)
    return jnp.concatenate(pairs, axis=1)


def _project_out(attn_scr, sc_ref, x_ref, mod_ref, w_ref, o_ref):
    mix = _dot(attn_scr[...], w_ref[:ATTN_WIDTH, :]) + _dot(sc_ref[...], w_ref[ATTN_WIDTH:, :])
    o_ref[...] = x_ref[...] + _mod_chunk(mod_ref, 2) * mix


def _attn_ctx_kernel(sink_ref, q_ref, k_ref, v_ref, sc_ref, x_ref, mod_ref, w_ref, o_ref, attn_scr):
    kcat = k_ref[...].astype(BF16)
    vt = _transposed_values([v_ref[...]])
    sink_row = sink_ref[...] * LOG2_E
    for j in range(0, SEQ, BLOCK):
        attn_scr[j:j + BLOCK, :] = _attend(q_ref[j:j + BLOCK, :], kcat, vt, None, sink_row).astype(BF16)
    _project_out(attn_scr, sc_ref, x_ref, mod_ref, w_ref, o_ref)


def _attn_ctx(q, k, v, sc, x, mod_l, sink_row, w_out_bf):
    T = q.shape[0]
    tok = lambda w: pl.BlockSpec((SEQ, w), lambda i: (i, 0))
    return pl.pallas_call(
        _attn_ctx_kernel,
        grid=(T // SEQ,),
        in_specs=[_resident((1, N_HEADS * BLOCK)), tok(ATTN_WIDTH), tok(KV_WIDTH), tok(KV_WIDTH),
                  tok(CONV_WIDTH), tok(D_MODEL), _mod_spec(lambda i: CTX_ROW),
                  _resident((ATTN_WIDTH + CONV_WIDTH, D_MODEL))],
        out_specs=tok(D_MODEL),
        out_shape=jax.ShapeDtypeStruct((T, D_MODEL), F32),
        scratch_shapes=[pltpu.VMEM((SEQ, ATTN_WIDTH), BF16)],
        compiler_params=_params(1),
        name="attn_ctx",
    )(sink_row, q, k, v, sc, x, mod_l, w_out_bf)


def _band_cap(has_prev, has_next):
    c = lax.broadcasted_iota(jnp.int32, (3 * BLOCK, BLOCK), 0)
    r = lax.broadcasted_iota(jnp.int32, (3 * BLOCK, BLOCK), 1)
    first_prev = r + jnp.where(has_prev, 0, BLOCK)
    last_next = r + 2 * BLOCK - jnp.where(has_next, 0, BLOCK)
    masked = ((c < BLOCK) & (c < first_prev)) | ((c >= 2 * BLOCK) & (c > last_next))
    return jnp.where(masked, NEG_INF, SCORE_CAP)


def _attn_win_kernel(sink_ref, q_ref, kp_ref, kc_ref, kn_ref, vp_ref, vc_ref, vn_ref, ck_ref, cv_ref,
                     sc_ref, x_ref, mod_ref, w_ref, o_ref, attn_scr, *, n_steps):
    i = pl.program_id(1)
    kc, vc = kc_ref[...], vc_ref[...]
    inner = range(0, ATTN_Q_BLOCKS * BLOCK, BLOCK)
    k_blocks = [kp_ref[...]] + [kc[j:j + BLOCK] for j in inner] + [kn_ref[...]]
    v_blocks = [vp_ref[...]] + [vc[j:j + BLOCK] for j in inner] + [vn_ref[...]]
    sink_row = sink_ref[...] * LOG2_E
    for j in range(ATTN_Q_BLOCKS):
        kcat = jnp.concatenate(k_blocks[j:j + 3] + [ck_ref[0]], axis=0).astype(BF16)
        vt = _transposed_values(v_blocks[j:j + 3] + [cv_ref[0]])
        cap = _band_cap(i > 0 if j == 0 else True, i < n_steps - 1 if j == ATTN_Q_BLOCKS - 1 else True)
        rows = slice(j * BLOCK, (j + 1) * BLOCK)
        attn_scr[rows, :] = _attend(q_ref[rows, :], kcat, vt, cap, sink_row).astype(BF16)
    _project_out(attn_scr, sc_ref, x_ref, mod_ref, w_ref, o_ref)


def _attn_win(q, k, v, ck, cv, sc, x, mod_l, sink_row, w_out_bf):
    T = q.shape[0]
    tq = ATTN_Q_BLOCKS * BLOCK
    n_steps = DEC_SEQ // tq
    nb = DEC_SEQ // BLOCK
    cur = lambda w: pl.BlockSpec((tq, w), lambda b, i: (b * n_steps + i, 0))
    prev = pl.BlockSpec((BLOCK, KV_WIDTH), lambda b, i: (b * nb + jnp.maximum(ATTN_Q_BLOCKS * i - 1, 0), 0))
    nxt = pl.BlockSpec((BLOCK, KV_WIDTH),
                       lambda b, i: (b * nb + jnp.minimum(ATTN_Q_BLOCKS * (i + 1), nb - 1), 0))
    ctx = pl.BlockSpec((1, PAST_LEN, KV_WIDTH), lambda b, i: (b, 0, 0))
    return pl.pallas_call(
        functools.partial(_attn_win_kernel, n_steps=n_steps),
        grid=(T // DEC_SEQ, n_steps),
        in_specs=[_resident((1, N_HEADS * BLOCK)), cur(ATTN_WIDTH),
                  prev, cur(KV_WIDTH), nxt, prev, cur(KV_WIDTH), nxt, ctx, ctx,
                  cur(CONV_WIDTH), cur(D_MODEL), _mod_spec(lambda b, i: b),
                  _resident((ATTN_WIDTH + CONV_WIDTH, D_MODEL))],
        out_specs=cur(D_MODEL),
        out_shape=jax.ShapeDtypeStruct((T, D_MODEL), F32),
        scratch_shapes=[pltpu.VMEM((tq, ATTN_WIDTH), BF16)],
        compiler_params=_params(2),
        name="attn_win",
    )(sink_row, q, k, k, k, v, v, v, ck, cv, sc, x, mod_l, w_out_bf)


def _gelu_tanh(x):
    k = 0.7978845608028654
    half = 0.5 * x
    return half + half * jnp.tanh(x * (k + (k * 0.044715) * (x * x)))


def _gmlp_kernel(x_ref, mod_ref, g_ref, win_ref, vg_ref, ws_ref, bs_ref, wout_ref, o_ref, gated_scr):
    x = x_ref[...]
    n_chunks = x.shape[0] // CHUNK
    h = _modulate(x, g_ref[...], _mod_chunk(mod_ref, 0), _mod_chunk(mod_ref, 1)).astype(BF16)
    u = _gelu_tanh(_dot(h, win_ref[:, :GMLP_WIDTH]))
    v = _gelu_tanh(_dot(h, win_ref[:, GMLP_WIDTH:]))
    ms = jnp.mean(v * v, axis=-1, keepdims=True)
    v = (v * lax.rsqrt(ms + EPS) * vg_ref[...]).astype(BF16)
    for grp in range(GMLP_GROUPS):
        lanes = slice(grp * GMLP_GROUP_DIM, (grp + 1) * GMLP_GROUP_DIM)
        rhs = jnp.concatenate([v[n * CHUNK:(n + 1) * CHUNK, lanes] for n in range(n_chunks)], axis=1)
        s = _dot(ws_ref[grp], rhs)
        for n in range(n_chunks):
            rows = slice(n * CHUNK, (n + 1) * CHUNK)
            s_n = s[:, n * GMLP_GROUP_DIM:(n + 1) * GMLP_GROUP_DIM] + bs_ref[grp]
            gated_scr[rows, lanes] = (u[rows, lanes] * s_n).astype(BF16)
    o_ref[...] = x + _mod_chunk(mod_ref, 2) * _dot(gated_scr[...], wout_ref[...])


def _gmlp(x, mod_l, mod_row, g, win_bf, vg, ws_bf, bs_full, wout_bf, tm):
    T = x.shape[0]
    tok = pl.BlockSpec((tm, D_MODEL), lambda i: (i, 0))
    return pl.pallas_call(
        _gmlp_kernel,
        grid=(T // tm,),
        in_specs=[tok, _mod_spec(mod_row),
                  _resident((1, D_MODEL)), _resident((D_MODEL, 2 * GMLP_WIDTH)), _resident((1, GMLP_WIDTH)),
                  _resident((GMLP_GROUPS, CHUNK, CHUNK)), _resident((GMLP_GROUPS, CHUNK, GMLP_GROUP_DIM)),
                  _resident((GMLP_WIDTH, D_MODEL))],
        out_specs=tok,
        out_shape=jax.ShapeDtypeStruct((T, D_MODEL), F32),
        scratch_shapes=[pltpu.VMEM((tm, GMLP_WIDTH), BF16)],
        compiler_params=_params(1),
        name="gmlp",
    )(x, mod_l, g, win_bf, vg, ws_bf, bs_full, wout_bf)


def _ffn_kernel(*refs, tiles_per_seq, seq_len):
    halo = _halo_rows(tiles_per_seq)
    n_x = 3 if halo else 1
    x_refs, refs = refs[:n_x], refs[n_x:]
    mod_ref, g_ref, wup_ref, cw_ref, wdn_ref, o_ref, act_scr = refs[:7]
    h_scr = refs[7] if halo else None
    tm = o_ref.shape[0]
    h = _modulated_rows(x_refs, h_scr, g_ref[...], _mod_chunk(mod_ref, 3), _mod_chunk(mod_ref, 4), tiles_per_seq)
    masks = _seq_end_masks(tm, FFN_COLS, seq_len, tiles_per_seq)
    for j in range(0, D_FF, FFN_COLS):
        gate_cols = slice(j, j + FFN_COLS)
        val_cols = slice(D_FF + j, D_FF + j + FFN_COLS)
        zg = _token_conv3(_dot(h, wup_ref[0, :, gate_cols]), cw_ref[0, :, gate_cols], tm, halo, masks)
        zv = _token_conv3(_dot(h, wup_ref[0, :, val_cols]), cw_ref[0, :, val_cols], tm, halo, masks)
        act_scr[:, gate_cols] = (jax.nn.silu(zg) * zv).astype(BF16)
    x = x_refs[n_x // 2][...]
    o_ref[...] = x + _mod_chunk(mod_ref, 5) * _dot(act_scr[...], wdn_ref[0])


def _ffn(x, mod_l, mod_row, g, wup_bf, conv_w, wdn_bf, layer, tm, seq_len):
    T = x.shape[0]
    tiles_per_seq = max(seq_len // tm, 1)
    assert tm % seq_len == 0 or seq_len % tm == 0
    x_specs = _token_specs(T, tm, tiles_per_seq)
    in_specs = x_specs + [_mod_spec(mod_row), _resident((1, D_MODEL)), _layer_slab(wup_bf.shape, layer),
                          _layer_slab(conv_w.shape, layer), _layer_slab(wdn_bf.shape, layer)]
    scratch = [pltpu.VMEM((tm, D_FF), BF16)]
    if tiles_per_seq > 1:
        scratch.append(pltpu.VMEM((tm + 2 * TOKEN_HALO, D_MODEL), BF16))
    return pl.pallas_call(
        functools.partial(_ffn_kernel, tiles_per_seq=tiles_per_seq, seq_len=seq_len),
        grid=(T // tm,),
        in_specs=in_specs,
        out_specs=pl.BlockSpec((tm, D_MODEL), lambda i: (i, 0)),
        out_shape=jax.ShapeDtypeStruct((T, D_MODEL), F32),
        scratch_shapes=scratch,
        compiler_params=_params(1),
        name="conv_ffn",
    )(*([x] * len(x_specs) + [mod_l, g, wup_bf, conv_w, wdn_bf]))


def _rope_tables(n_tokens):
    t = jnp.arange(n_tokens)
    n_freq = HEAD_DIM // 4
    inv = ROPE_BASE ** (-jnp.arange(n_freq, dtype=F32) / n_freq)
    row_ang = (t // GRID_W).astype(F32)[:, None] * inv
    col_ang = (t % GRID_W).astype(F32)[:, None] * inv
    ang = jnp.concatenate([row_ang, row_ang, col_ang, col_ang], axis=1)
    ang = jnp.tile(ang, (1, LANES // HEAD_DIM))
    first = (jnp.arange(LANES) % (2 * n_freq)) < n_freq
    cos, sin = jnp.cos(ang), jnp.sin(ang)
    return cos, jnp.where(first, -sin, 0.0), jnp.where(first, 0.0, sin)


def kernel(x_prompt, x_sample, cache_k, cache_v, c, c_ctx, ada_w, ada_b, norm_mix_g, norm_ffn_g, w_in_even,
           q_norm_g, k_norm_g, sink_logit, short_conv_w, w_out_even, w_in_odd, gmlp_norm_g, w_spatial,
           b_spatial, w_out_odd, w_up, ffn_conv_w, w_down):
    n_p, n_s = BATCH * SEQ, DEC_BATCH * DEC_SEQ
    xp = x_prompt.reshape(n_p, D_MODEL)
    xs = x_sample.reshape(n_s, D_MODEL)
    cond = jnp.concatenate([c, c_ctx[None, :], jnp.zeros((COND_ROWS - DEC_BATCH - 1, D_MODEL), F32)], axis=0)
    mod = _adaln(cond, ada_w, ada_b)
    rope_tabs = _rope_tables(DEC_SEQ)
    w_in_even_bf, w_out_even_bf = _to_bf16(w_in_even), _to_bf16(w_out_even)
    w_in_odd_bf, w_out_odd_bf = _to_bf16(w_in_odd), _to_bf16(w_out_odd)
    w_spatial_bf = _to_bf16(w_spatial.reshape(N_ODD, GMLP_GROUPS * CHUNK, CHUNK))
    w_up_bf, w_down_bf = _to_bf16(w_up), _to_bf16(w_down)

    tm_p, tm_s = 1024, 1024
    row_p = lambda i: CTX_ROW
    row_s = lambda i: i // (DEC_SEQ // tm_s)
    row_vec = lambda a: a.reshape(1, -1)
    new_k, new_v = [], []
    for l in range(DEPTH):
        mod_l = mod[l].reshape(COND_ROWS, 1, 6 * D_MODEL)
        g_mix = row_vec(norm_mix_g[l])
        if l % 2 == 0:
            e = l // 2
            w_in, w_out = w_in_even_bf[e], w_out_even_bf[e]
            qg = row_vec(jnp.tile(q_norm_g[e], N_HEADS))
            kg = row_vec(jnp.tile(k_norm_g[e], N_KV_HEADS))
            sink = jnp.repeat(sink_logit[e], BLOCK).reshape(1, N_HEADS * BLOCK)
            cw = short_conv_w[e]
            qp, kp, vp, scp = _in_even(xp, mod_l, row_p, g_mix, w_in, qg, kg, cw, None, tm_p, SEQ)
            new_k.append(kp.reshape(BATCH, SEQ, N_KV_HEADS, HEAD_DIM))
            new_v.append(vp.reshape(BATCH, SEQ, N_KV_HEADS, HEAD_DIM))
            xp = _attn_ctx(qp, kp, vp, scp, xp, mod_l, sink, w_out)
            qs, ks, vs, scs = _in_even(xs, mod_l, row_s, g_mix, w_in, qg, kg, cw, rope_tabs, tm_s, DEC_SEQ)
            ck = cache_k[:, e].reshape(DEC_BATCH, PAST_LEN, KV_WIDTH)
            cv = cache_v[:, e].reshape(DEC_BATCH, PAST_LEN, KV_WIDTH)
            xs = _attn_win(qs, ks, vs, ck, cv, scs, xs, mod_l, sink, w_out)
        else:
            o = l // 2
            w_in, w_out = w_in_odd_bf[o], w_out_odd_bf[o]
            vg = row_vec(gmlp_norm_g[o])
            ws = w_spatial_bf[o].reshape(GMLP_GROUPS, CHUNK, CHUNK)
            bs_full = jnp.broadcast_to(b_spatial[o][:, :, None], (GMLP_GROUPS, CHUNK, GMLP_GROUP_DIM))
            xp = _gmlp(xp, mod_l, row_p, g_mix, w_in, vg, ws, bs_full, w_out, tm_p)
            xs = _gmlp(xs, mod_l, row_s, g_mix, w_in, vg, ws, bs_full, w_out, tm_s)
        g_ffn = row_vec(norm_ffn_g[l])
        xp = _ffn(xp, mod_l, row_p, g_ffn, w_up_bf, ffn_conv_w, w_down_bf, l, tm_p, SEQ)
        xs = _ffn(xs, mod_l, row_s, g_ffn, w_up_bf, ffn_conv_w, w_down_bf, l, tm_s, DEC_SEQ)
    return (xp.reshape(BATCH, SEQ, D_MODEL), xs.reshape(DEC_BATCH, DEC_SEQ, D_MODEL),
            jnp.stack(new_k, axis=1), jnp.stack(new_v, axis=1))
```

```python
import functools

import jax
import jax.numpy as jnp
from jax import lax
from jax.experimental import pallas as pl
from jax.experimental.pallas import tpu as pltpu

F32 = jnp.float32
BF16 = jnp.bfloat16

D_MODEL = 1024
BATCH = 16
SEQ = 256
DEPTH = 2
DEC_BATCH = 4
DEC_SEQ = 2048
PAST_LEN = 512
GRID_W = 64
N_HEADS = 8
N_KV_HEADS = 2
HEAD_DIM = 64
GQA_GROUP = N_HEADS // N_KV_HEADS
ATTN_WIDTH = N_HEADS * HEAD_DIM
KV_WIDTH = N_KV_HEADS * HEAD_DIM
WINDOW = 128
BLOCK = 128
ROPE_BASE = 10000.0
CONV_WIDTH = 512
CHUNK = 128
GMLP_WIDTH = 1024
GMLP_GROUPS = 8
GMLP_GROUP_DIM = GMLP_WIDTH // GMLP_GROUPS
D_FF = 2816
EPS = 1e-6
NEG_INF = -1e30
N_EVEN = (DEPTH + 1) // 2
N_ODD = DEPTH // 2
LOG2_E = 1.4426950408889634

LANES = 128
SUBLANES_F32 = 8
SUBLANES_BF16 = 16
MXU_WIDTH = 256
VMEM_LIMIT_BYTES = 56 * 1024 * 1024

COND_ROWS = 8
CTX_ROW = DEC_BATCH
TOKEN_HALO = SUBLANES_BF16
FFN_COLS = MXU_WIDTH
CAST_BLOCK_BYTES = 3 * 1024 * 1024
ADA_TN = 1536
ATTN_Q_BLOCKS = 4
SUM_ROWS = SUBLANES_BF16
SCORE_CAP = 3.0e38


def _params(n_axes):
    return pltpu.CompilerParams(dimension_semantics=("parallel",) * n_axes,
                                vmem_limit_bytes=VMEM_LIMIT_BYTES)


def _resident(shape):
    zeros = (0,) * len(shape)
    return pl.BlockSpec(shape, lambda *_: zeros, pipeline_mode=pl.Buffered(1))


def _layer_slab(shape, layer):
    return pl.BlockSpec((1,) + tuple(shape[1:]), lambda *_: (layer, 0, 0), pipeline_mode=pl.Buffered(1))


def _mod_spec(mod_row):
    return pl.BlockSpec((1, 1, 6 * D_MODEL), lambda *idx: (mod_row(*idx), 0, 0))


def _dot(a, b):
    return jnp.dot(a, b, preferred_element_type=F32)


def _cast_kernel(x_ref, o_ref):
    o_ref[...] = x_ref[...].astype(BF16)


def _to_bf16(w):
    n, rows, cols = w.shape
    fits = [r for r in range(SUBLANES_BF16, rows + 1, SUBLANES_BF16)
            if rows % r == 0 and r * cols * 4 <= CAST_BLOCK_BYTES]
    tr = max(fits)
    spec = pl.BlockSpec((1, tr, cols), lambda i, j: (i, j, 0))
    return pl.pallas_call(
        _cast_kernel, grid=(n, rows // tr), in_specs=[spec], out_specs=spec,
        out_shape=jax.ShapeDtypeStruct(w.shape, BF16), compiler_params=_params(2), name="to_bf16",
    )(w)


def _modulate(x, g, shift, scale):
    ms = jnp.mean(x * x, axis=-1, keepdims=True)
    return (x * lax.rsqrt(ms + EPS) * g) * (1.0 + scale) + shift


def _mod_chunk(mod_ref, k):
    return mod_ref[0, :, k * D_MODEL:(k + 1) * D_MODEL]


def _halo_rows(tiles_per_seq):
    return TOKEN_HALO if tiles_per_seq > 1 else 0


def _token_specs(T, tm, tiles_per_seq):
    tok = pl.BlockSpec((tm, D_MODEL), lambda i: (i, 0))
    if tiles_per_seq == 1:
        return [tok]
    per_tile = tm // TOKEN_HALO
    last = T // TOKEN_HALO - 1
    prev = pl.BlockSpec((TOKEN_HALO, D_MODEL), lambda i: (jnp.maximum(i * per_tile - 1, 0), 0))
    nxt = pl.BlockSpec((TOKEN_HALO, D_MODEL), lambda i: (jnp.minimum((i + 1) * per_tile, last), 0))
    return [prev, tok, nxt]


def _modulated_rows(x_refs, h_scr, g, shift, scale, tiles_per_seq):
    if tiles_per_seq == 1:
        return _modulate(x_refs[0][...], g, shift, scale).astype(BF16)
    xp_ref, x_ref, xn_ref = x_refs
    tm = x_ref.shape[0]
    pos = pl.program_id(0) % tiles_per_seq
    hp = jnp.where(pos > 0, _modulate(xp_ref[...], g, shift, scale), 0.0)
    hn = jnp.where(pos < tiles_per_seq - 1, _modulate(xn_ref[...], g, shift, scale), 0.0)
    h_scr[:TOKEN_HALO, :] = hp.astype(BF16)
    h_scr[TOKEN_HALO:TOKEN_HALO + tm, :] = _modulate(x_ref[...], g, shift, scale).astype(BF16)
    h_scr[TOKEN_HALO + tm:, :] = hn.astype(BF16)
    return h_scr[...]


def _seq_end_masks(tm, cols, seq_len, tiles_per_seq):
    if tiles_per_seq > 1:
        return None
    seq_row = lax.broadcasted_iota(jnp.int32, (tm, cols), 0) % seq_len
    return seq_row != 0, seq_row != seq_len - 1


def _token_conv3(z, w, tm, halo, masks):
    rows = z.shape[0]
    mid = slice(halo, halo + tm)
    dn = pltpu.roll(z, 1, 0)[mid]
    up = pltpu.roll(z, rows - 1, 0)[mid]
    if masks is not None:
        dn = jnp.where(masks[0], dn, 0.0)
        up = jnp.where(masks[1], up, 0.0)
    return dn * w[0:1] + z[mid] * w[1:2] + up * w[2:3]


def _adaln_kernel(cond_ref, w_ref, b_ref, o_ref):
    a = jax.nn.silu(cond_ref[...]).astype(BF16)
    o_ref[0] = _dot(a, w_ref[0].astype(BF16)) + b_ref[0]


def _adaln(cond, ada_w, ada_b):
    n_out = 6 * D_MODEL
    return pl.pallas_call(
        _adaln_kernel,
        grid=(DEPTH, n_out // ADA_TN),
        in_specs=[pl.BlockSpec((COND_ROWS, D_MODEL), lambda l, j: (0, 0)),
                  pl.BlockSpec((1, D_MODEL, ADA_TN), lambda l, j: (l, 0, j)),
                  pl.BlockSpec((1, 1, ADA_TN), lambda l, j: (l, 0, j))],
        out_specs=pl.BlockSpec((1, COND_ROWS, ADA_TN), lambda l, j: (l, 0, j)),
        out_shape=jax.ShapeDtypeStruct((DEPTH, COND_ROWS, n_out), F32),
        compiler_params=_params(2),
        name="adaln",
    )(cond, ada_w, ada_b.reshape(DEPTH, 1, n_out))


def _head_rms(z, gain):
    n = z.shape[1]
    w = min(n, MXU_WIDTH)
    r = lax.broadcasted_iota(jnp.int32, (w, w), 0) // HEAD_DIM
    c = lax.broadcasted_iota(jnp.int32, (w, w), 1) // HEAD_DIM
    ones = (r == c).astype(BF16)
    sq = (z * z).astype(BF16)
    ss = jnp.concatenate([_dot(sq[:, k:k + w], ones) for k in range(0, n, w)], axis=1)
    return z * lax.rsqrt(ss * (1.0 / HEAD_DIM) + EPS) * gain


def _rope(z, cos, sin_lo, sin_hi):
    outs = []
    for k in range(0, z.shape[1], LANES):
        blk = z[:, k:k + LANES]
        outs.append(blk * cos + pltpu.roll(blk, LANES - 16, 1) * sin_lo + pltpu.roll(blk, 16, 1) * sin_hi)
    return jnp.concatenate(outs, axis=1)


def _in_even_kernel(*refs, rope, tiles_per_seq, seq_len):
    halo = _halo_rows(tiles_per_seq)
    n_x = 3 if halo else 1
    x_refs, refs = refs[:n_x], refs[n_x:]
    mod_ref, g_ref, w_ref, qg_ref, kg_ref, cw_ref = refs[:6]
    refs = refs[6:]
    if rope:
        cos_ref, slo_ref, shi_ref = refs[:3]
        refs = refs[3:]
    q_ref, k_ref, v_ref, sc_ref = refs[:4]
    h_scr = refs[4] if halo else None
    tm = q_ref.shape[0]
    h_all = _modulated_rows(x_refs, h_scr, g_ref[...], _mod_chunk(mod_ref, 0), _mod_chunk(mod_ref, 1),
                            tiles_per_seq)
    h = h_scr[halo:halo + tm, :] if halo else h_all
    c0 = 0
    c1 = c0 + ATTN_WIDTH
    c2 = c1 + 2 * KV_WIDTH
    c3 = c2 + CONV_WIDTH
    c4 = c3 + CONV_WIDTH
    c5 = c4 + CONV_WIDTH
    q = _head_rms(_dot(h, w_ref[:, c0:c1]), qg_ref[...] * (HEAD_DIM ** -0.5 * LOG2_E))
    zkv = _dot(h, w_ref[:, c1:c2])
    k = _head_rms(zkv[:, :KV_WIDTH], kg_ref[...])
    if rope:
        cos, slo, shi = cos_ref[...], slo_ref[...], shi_ref[...]
        q = _rope(q, cos, slo, shi)
        k = _rope(k, cos, slo, shi)
    q_ref[...] = q
    k_ref[...] = k
    v_ref[...] = zkv[:, KV_WIDTH:]
    ch = _dot(h_all, w_ref[:, c3:c4]) * _dot(h_all, w_ref[:, c4:c5])
    conv = _token_conv3(ch, cw_ref[...], tm, halo, _seq_end_masks(tm, CONV_WIDTH, seq_len, tiles_per_seq))
    sc_ref[...] = (_dot(h, w_ref[:, c2:c3]) * conv).astype(BF16)


def _in_even(x, mod_l, mod_row, g, w_bf, qg, kg, conv_w, rope_tabs, tm, seq_len):
    T = x.shape[0]
    n_in = w_bf.shape[1]
    tiles_per_seq = max(seq_len // tm, 1)
    assert tm % seq_len == 0 or seq_len % tm == 0
    tok = lambda w: pl.BlockSpec((tm, w), lambda i: (i, 0))
    x_specs = _token_specs(T, tm, tiles_per_seq)
    in_specs = x_specs + [_mod_spec(mod_row), _resident((1, D_MODEL)), _resident((D_MODEL, n_in)),
                          _resident((1, ATTN_WIDTH)), _resident((1, KV_WIDTH)), _resident((3, CONV_WIDTH))]
    args = [x] * len(x_specs) + [mod_l, g, w_bf, qg, kg, conv_w]
    if rope_tabs is not None:
        in_specs += [pl.BlockSpec((tm, LANES), lambda i: (i % tiles_per_seq, 0))] * 3
        args += list(rope_tabs)
    outs = ((ATTN_WIDTH, F32), (KV_WIDTH, F32), (KV_WIDTH, F32), (CONV_WIDTH, BF16))
    scratch = [pltpu.VMEM((tm + 2 * TOKEN_HALO, D_MODEL), BF16)] if tiles_per_seq > 1 else []
    return pl.pallas_call(
        functools.partial(_in_even_kernel, rope=rope_tabs is not None, tiles_per_seq=tiles_per_seq,
                          seq_len=seq_len),
        grid=(T // tm,),
        in_specs=in_specs,
        out_specs=[tok(w) for w, _ in outs],
        out_shape=[jax.ShapeDtypeStruct((T, w), dt) for w, dt in outs],
        scratch_shapes=scratch,
        compiler_params=_params(1),
        name="in_even",
    )(*args)


def _stack_heads(q):
    lane_lo = lax.broadcasted_iota(jnp.int32, (BLOCK, LANES), 1) < HEAD_DIM
    heads = []
    for pair in range(N_HEADS // 2):
        kv = (2 * pair) // GQA_GROUP
        qp = q[:, pair * LANES:(pair + 1) * LANES]
        qr = pltpu.roll(qp, HEAD_DIM, 1)
        for half in range(2):
            src = qp if half == kv else qr
            heads.append(jnp.where(lane_lo, src, 0.0) if kv == 0 else jnp.where(lane_lo, 0.0, src))
    return jnp.concatenate(heads, axis=0).astype(BF16)


def _transposed_values(v_blocks):
    vt = jnp.concatenate([v[j:j + BLOCK].T for v in v_blocks for j in range(0, v.shape[0], BLOCK)], axis=1)
    return jnp.concatenate([vt.astype(BF16), jnp.ones((SUM_ROWS, vt.shape[1]), BF16)], axis=0)


def _scores(q, kcat, cap):
    st = lax.dot_general(kcat, _stack_heads(q), (((1,), (1,)), ((), ())), preferred_element_type=F32)
    if cap is None:
        return st
    n_band = cap.shape[0]
    capped = jnp.minimum(st[:n_band], jnp.concatenate([cap] * N_HEADS, axis=1))
    return jnp.concatenate([capped, st[n_band:]], axis=0)


def _weighted_values(st, vt, sink_row):
    m = jnp.maximum(jnp.max(st, axis=0, keepdims=True), sink_row)
    pt = jnp.exp2(st - m).astype(BF16)
    ot = _dot(vt, pt)
    denom = ot[KV_WIDTH:KV_WIDTH + 1] + jnp.exp2(sink_row - m)
    ot = ot[:KV_WIDTH] / denom
    pairs = []
    for pair in range(N_HEADS // 2):
        kv = (2 * pair) // GQA_GROUP
        dims = slice(kv * HEAD_DIM, (kv + 1) * HEAD_DIM)
        both = [ot[dims, (2 * pair + half) * BLOCK:(2 * pair + half + 1) * BLOCK] for half in range(2)]
        pairs.append(jnp.concatenate(both, axis=0).T)
    return jnp.concatenate(pairs, axis=1)


def _attend_blocks(n_blocks, scores_of, values_of, sink_row, attn_scr):
    st = scores_of(0)
    for j in range(n_blocks):
        st_next = scores_of(j + 1) if j + 1 < n_blocks else None
        attn_scr[j * BLOCK:(j + 1) * BLOCK, :] = _weighted_values(st, values_of(j), sink_row).astype(BF16)
        st = st_next


def _project_out(attn_scr, sc_ref, x_ref, mod_ref, w_ref, o_ref):
    mix = _dot(attn_scr[...], w_ref[:ATTN_WIDTH, :]) + _dot(sc_ref[...], w_ref[ATTN_WIDTH:, :])
    o_ref[...] = x_ref[...] + _mod_chunk(mod_ref, 2) * mix


def _attn_ctx_kernel(sink_ref, q_ref, k_ref, v_ref, sc_ref, x_ref, mod_ref, w_ref, o_ref, attn_scr):
    kcat = k_ref[...].astype(BF16)
    vt = _transposed_values([v_ref[...]])
    sink_row = sink_ref[...] * LOG2_E
    _attend_blocks(SEQ // BLOCK, lambda j: _scores(q_ref[j * BLOCK:(j + 1) * BLOCK, :], kcat, None),
                   lambda j: vt, sink_row, attn_scr)
    _project_out(attn_scr, sc_ref, x_ref, mod_ref, w_ref, o_ref)


def _attn_ctx(q, k, v, sc, x, mod_l, sink_row, w_out_bf):
    T = q.shape[0]
    tok = lambda w: pl.BlockSpec((SEQ, w), lambda i: (i, 0))
    return pl.pallas_call(
        _attn_ctx_kernel,
        grid=(T // SEQ,),
        in_specs=[_resident((1, N_HEADS * BLOCK)), tok(ATTN_WIDTH), tok(KV_WIDTH), tok(KV_WIDTH),
                  tok(CONV_WIDTH), tok(D_MODEL), _mod_spec(lambda i: CTX_ROW),
                  _resident((ATTN_WIDTH + CONV_WIDTH, D_MODEL))],
        out_specs=tok(D_MODEL),
        out_shape=jax.ShapeDtypeStruct((T, D_MODEL), F32),
        scratch_shapes=[pltpu.VMEM((SEQ, ATTN_WIDTH), BF16)],
        compiler_params=_params(1),
        name="attn_ctx",
    )(sink_row, q, k, v, sc, x, mod_l, w_out_bf)


def _band_cap(has_prev, has_next):
    c = lax.broadcasted_iota(jnp.int32, (3 * BLOCK, BLOCK), 0)
    r = lax.broadcasted_iota(jnp.int32, (3 * BLOCK, BLOCK), 1)
    first_prev = r + jnp.where(has_prev, 0, BLOCK)
    last_next = r + 2 * BLOCK - jnp.where(has_next, 0, BLOCK)
    masked = ((c < BLOCK) & (c < first_prev)) | ((c >= 2 * BLOCK) & (c > last_next))
    return jnp.where(masked, NEG_INF, SCORE_CAP)


def _attn_win_kernel(sink_ref, q_ref, kp_ref, kc_ref, kn_ref, vp_ref, vc_ref, vn_ref, ck_ref, cv_ref,
                     sc_ref, x_ref, mod_ref, w_ref, o_ref, attn_scr, *, n_steps):
    i = pl.program_id(1)
    kc, vc = kc_ref[...], vc_ref[...]
    inner = range(0, ATTN_Q_BLOCKS * BLOCK, BLOCK)
    k_blocks = [kp_ref[...]] + [kc[j:j + BLOCK] for j in inner] + [kn_ref[...]]
    v_blocks = [vp_ref[...]] + [vc[j:j + BLOCK] for j in inner] + [vn_ref[...]]
    sink_row = sink_ref[...] * LOG2_E

    def scores_of(j):
        kcat = jnp.concatenate(k_blocks[j:j + 3] + [ck_ref[0]], axis=0).astype(BF16)
        cap = _band_cap(i > 0 if j == 0 else True, i < n_steps - 1 if j == ATTN_Q_BLOCKS - 1 else True)
        return _scores(q_ref[j * BLOCK:(j + 1) * BLOCK, :], kcat, cap)

    _attend_blocks(ATTN_Q_BLOCKS, scores_of, lambda j: _transposed_values(v_blocks[j:j + 3] + [cv_ref[0]]),
                   sink_row, attn_scr)
    _project_out(attn_scr, sc_ref, x_ref, mod_ref, w_ref, o_ref)


def _attn_win(q, k, v, ck, cv, sc, x, mod_l, sink_row, w_out_bf):
    T = q.shape[0]
    tq = ATTN_Q_BLOCKS * BLOCK
    n_steps = DEC_SEQ // tq
    nb = DEC_SEQ // BLOCK
    cur = lambda w: pl.BlockSpec((tq, w), lambda b, i: (b * n_steps + i, 0))
    prev = pl.BlockSpec((BLOCK, KV_WIDTH), lambda b, i: (b * nb + jnp.maximum(ATTN_Q_BLOCKS * i - 1, 0), 0))
    nxt = pl.BlockSpec((BLOCK, KV_WIDTH),
                       lambda b, i: (b * nb + jnp.minimum(ATTN_Q_BLOCKS * (i + 1), nb - 1), 0))
    ctx = pl.BlockSpec((1, PAST_LEN, KV_WIDTH), lambda b, i: (b, 0, 0))
    return pl.pallas_call(
        functools.partial(_attn_win_kernel, n_steps=n_steps),
        grid=(T // DEC_SEQ, n_steps),
        in_specs=[_resident((1, N_HEADS * BLOCK)), cur(ATTN_WIDTH),
                  prev, cur(KV_WIDTH), nxt, prev, cur(KV_WIDTH), nxt, ctx, ctx,
                  cur(CONV_WIDTH), cur(D_MODEL), _mod_spec(lambda b, i: b),
                  _resident((ATTN_WIDTH + CONV_WIDTH, D_MODEL))],
        out_specs=cur(D_MODEL),
        out_shape=jax.ShapeDtypeStruct((T, D_MODEL), F32),
        scratch_shapes=[pltpu.VMEM((tq, ATTN_WIDTH), BF16)],
        compiler_params=_params(2),
        name="attn_win",
    )(sink_row, q, k, k, k, v, v, v, ck, cv, sc, x, mod_l, w_out_bf)


def _gelu_tanh(x):
    k = 0.7978845608028654
    half = 0.5 * x
    return half + half * jnp.tanh(x * (k + (k * 0.044715) * (x * x)))


def _gmlp_kernel(x_ref, mod_ref, g_ref, win_ref, vg_ref, ws_ref, bs_ref, wout_ref, o_ref, gated_scr):
    x = x_ref[...]
    n_chunks = x.shape[0] // CHUNK
    h = _modulate(x, g_ref[...], _mod_chunk(mod_ref, 0), _mod_chunk(mod_ref, 1)).astype(BF16)
    u = _gelu_tanh(_dot(h, win_ref[:, :GMLP_WIDTH]))
    v = _gelu_tanh(_dot(h, win_ref[:, GMLP_WIDTH:]))
    ms = jnp.mean(v * v, axis=-1, keepdims=True)
    v = (v * lax.rsqrt(ms + EPS) * vg_ref[...]).astype(BF16)
    for grp in range(GMLP_GROUPS):
        lanes = slice(grp * GMLP_GROUP_DIM, (grp + 1) * GMLP_GROUP_DIM)
        rhs = jnp.concatenate([v[n * CHUNK:(n + 1) * CHUNK, lanes] for n in range(n_chunks)], axis=1)
        s = _dot(ws_ref[grp], rhs)
        for n in range(n_chunks):
            rows = slice(n * CHUNK, (n + 1) * CHUNK)
            s_n = s[:, n * GMLP_GROUP_DIM:(n + 1) * GMLP_GROUP_DIM] + bs_ref[grp]
            gated_scr[rows, lanes] = (u[rows, lanes] * s_n).astype(BF16)
    o_ref[...] = x + _mod_chunk(mod_ref, 2) * _dot(gated_scr[...], wout_ref[...])


def _gmlp(x, mod_l, mod_row, g, win_bf, vg, ws_bf, bs_full, wout_bf, tm):
    T = x.shape[0]
    tok = pl.BlockSpec((tm, D_MODEL), lambda i: (i, 0))
    return pl.pallas_call(
        _gmlp_kernel,
        grid=(T // tm,),
        in_specs=[tok, _mod_spec(mod_row),
                  _resident((1, D_MODEL)), _resident((D_MODEL, 2 * GMLP_WIDTH)), _resident((1, GMLP_WIDTH)),
                  _resident((GMLP_GROUPS, CHUNK, CHUNK)), _resident((GMLP_GROUPS, CHUNK, GMLP_GROUP_DIM)),
                  _resident((GMLP_WIDTH, D_MODEL))],
        out_specs=tok,
        out_shape=jax.ShapeDtypeStruct((T, D_MODEL), F32),
        scratch_shapes=[pltpu.VMEM((tm, GMLP_WIDTH), BF16)],
        compiler_params=_params(1),
        name="gmlp",
    )(x, mod_l, g, win_bf, vg, ws_bf, bs_full, wout_bf)


def _ffn_kernel(*refs, tiles_per_seq, seq_len):
    halo = _halo_rows(tiles_per_seq)
    n_x = 3 if halo else 1
    x_refs, refs = refs[:n_x], refs[n_x:]
    mod_ref, g_ref, wup_ref, cw_ref, wdn_ref, o_ref, act_scr = refs[:7]
    h_scr = refs[7] if halo else None
    tm = o_ref.shape[0]
    h = _modulated_rows(x_refs, h_scr, g_ref[...], _mod_chunk(mod_ref, 3), _mod_chunk(mod_ref, 4), tiles_per_seq)
    masks = _seq_end_masks(tm, FFN_COLS, seq_len, tiles_per_seq)
    for j in range(0, D_FF, FFN_COLS):
        gate_cols = slice(j, j + FFN_COLS)
        val_cols = slice(D_FF + j, D_FF + j + FFN_COLS)
        zg = _token_conv3(_dot(h, wup_ref[0, :, gate_cols]), cw_ref[0, :, gate_cols], tm, halo, masks)
        zv = _token_conv3(_dot(h, wup_ref[0, :, val_cols]), cw_ref[0, :, val_cols], tm, halo, masks)
        act_scr[:, gate_cols] = (jax.nn.silu(zg) * zv).astype(BF16)
    x = x_refs[n_x // 2][...]
    o_ref[...] = x + _mod_chunk(mod_ref, 5) * _dot(act_scr[...], wdn_ref[0])


def _ffn(x, mod_l, mod_row, g, wup_bf, conv_w, wdn_bf, layer, tm, seq_len):
    T = x.shape[0]
    tiles_per_seq = max(seq_len // tm, 1)
    assert tm % seq_len == 0 or seq_len % tm == 0
    x_specs = _token_specs(T, tm, tiles_per_seq)
    in_specs = x_specs + [_mod_spec(mod_row), _resident((1, D_MODEL)), _layer_slab(wup_bf.shape, layer),
                          _layer_slab(conv_w.shape, layer), _layer_slab(wdn_bf.shape, layer)]
    scratch = [pltpu.VMEM((tm, D_FF), BF16)]
    if tiles_per_seq > 1:
        scratch.append(pltpu.VMEM((tm + 2 * TOKEN_HALO, D_MODEL), BF16))
    return pl.pallas_call(
        functools.partial(_ffn_kernel, tiles_per_seq=tiles_per_seq, seq_len=seq_len),
        grid=(T // tm,),
        in_specs=in_specs,
        out_specs=pl.BlockSpec((tm, D_MODEL), lambda i: (i, 0)),
        out_shape=jax.ShapeDtypeStruct((T, D_MODEL), F32),
        scratch_shapes=scratch,
        compiler_params=_params(1),
        name="conv_ffn",
    )(*([x] * len(x_specs) + [mod_l, g, wup_bf, conv_w, wdn_bf]))


def _rope_tables(n_tokens):
    t = jnp.arange(n_tokens)
    n_freq = HEAD_DIM // 4
    inv = ROPE_BASE ** (-jnp.arange(n_freq, dtype=F32) / n_freq)
    row_ang = (t // GRID_W).astype(F32)[:, None] * inv
    col_ang = (t % GRID_W).astype(F32)[:, None] * inv
    ang = jnp.concatenate([row_ang, row_ang, col_ang, col_ang], axis=1)
    ang = jnp.tile(ang, (1, LANES // HEAD_DIM))
    first = (jnp.arange(LANES) % (2 * n_freq)) < n_freq
    cos, sin = jnp.cos(ang), jnp.sin(ang)
    return cos, jnp.where(first, -sin, 0.0), jnp.where(first, 0.0, sin)


def kernel(x_prompt, x_sample, cache_k, cache_v, c, c_ctx, ada_w, ada_b, norm_mix_g, norm_ffn_g, w_in_even,
           q_norm_g, k_norm_g, sink_logit, short_conv_w, w_out_even, w_in_odd, gmlp_norm_g, w_spatial,
           b_spatial, w_out_odd, w_up, ffn_conv_w, w_down):
    n_p, n_s = BATCH * SEQ, DEC_BATCH * DEC_SEQ
    xp = x_prompt.reshape(n_p, D_MODEL)
    xs = x_sample.reshape(n_s, D_MODEL)
    cond = jnp.concatenate([c, c_ctx[None, :], jnp.zeros((COND_ROWS - DEC_BATCH - 1, D_MODEL), F32)], axis=0)
    mod = _adaln(cond, ada_w, ada_b)
    rope_tabs = _rope_tables(DEC_SEQ)
    w_in_even_bf, w_out_even_bf = _to_bf16(w_in_even), _to_bf16(w_out_even)
    w_in_odd_bf, w_out_odd_bf = _to_bf16(w_in_odd), _to_bf16(w_out_odd)
    w_spatial_bf = _to_bf16(w_spatial.reshape(N_ODD, GMLP_GROUPS * CHUNK, CHUNK))
    w_up_bf, w_down_bf = _to_bf16(w_up), _to_bf16(w_down)

    tm_p, tm_s = 1024, 1024
    row_p = lambda i: CTX_ROW
    row_s = lambda i: i // (DEC_SEQ // tm_s)
    row_vec = lambda a: a.reshape(1, -1)
    new_k, new_v = [], []
    for l in range(DEPTH):
        mod_l = mod[l].reshape(COND_ROWS, 1, 6 * D_MODEL)
        g_mix = row_vec(norm_mix_g[l])
        if l % 2 == 0:
            e = l // 2
            w_in, w_out = w_in_even_bf[e], w_out_even_bf[e]
            qg = row_vec(jnp.tile(q_norm_g[e], N_HEADS))
            kg = row_vec(jnp.tile(k_norm_g[e], N_KV_HEADS))
            sink = jnp.repeat(sink_logit[e], BLOCK).reshape(1, N_HEADS * BLOCK)
            cw = short_conv_w[e]
            qp, kp, vp, scp = _in_even(xp, mod_l, row_p, g_mix, w_in, qg, kg, cw, None, tm_p, SEQ)
            new_k.append(kp.reshape(BATCH, SEQ, N_KV_HEADS, HEAD_DIM))
            new_v.append(vp.reshape(BATCH, SEQ, N_KV_HEADS, HEAD_DIM))
            xp = _attn_ctx(qp, kp, vp, scp, xp, mod_l, sink, w_out)
            qs, ks, vs, scs = _in_even(xs, mod_l, row_s, g_mix, w_in, qg, kg, cw, rope_tabs, tm_s, DEC_SEQ)
            ck = cache_k[:, e].reshape(DEC_BATCH, PAST_LEN, KV_WIDTH)
            cv = cache_v[:, e].reshape(DEC_BATCH, PAST_LEN, KV_WIDTH)
            xs = _attn_win(qs, ks, vs, ck, cv, scs, xs, mod_l, sink, w_out)
        else:
            o = l // 2
            w_in, w_out = w_in_odd_bf[o], w_out_odd_bf[o]
            vg = row_vec(gmlp_norm_g[o])
            ws = w_spatial_bf[o].reshape(GMLP_GROUPS, CHUNK, CHUNK)
            bs_full = jnp.broadcast_to(b_spatial[o][:, :, None], (GMLP_GROUPS, CHUNK, GMLP_GROUP_DIM))
            xp = _gmlp(xp, mod_l, row_p, g_mix, w_in, vg, ws, bs_full, w_out, tm_p)
            xs = _gmlp(xs, mod_l, row_s, g_mix, w_in, vg, ws, bs_full, w_out, tm_s)
        g_ffn = row_vec(norm_ffn_g[l])
        xp = _ffn(xp, mod_l, row_p, g_ffn, w_up_bf, ffn_conv_w, w_down_bf, l, tm_p, SEQ)
        xs = _ffn(xs, mod_l, row_s, g_ffn, w_up_bf, ffn_conv_w, w_down_bf, l, tm_s, DEC_SEQ)
    return (xp.reshape(BATCH, SEQ, D_MODEL), xs.reshape(DEC_BATCH, DEC_SEQ, D_MODEL),
            jnp.stack(new_k, axis=1), jnp.stack(new_v, axis=1))
```

```python
import functools

import jax
import jax.numpy as jnp
from jax import lax
from jax.experimental import pallas as pl
from jax.experimental.pallas import tpu as pltpu

F32 = jnp.float32
BF16 = jnp.bfloat16

D_MODEL = 1024
BATCH = 16
SEQ = 256
DEPTH = 2
DEC_BATCH = 4
DEC_SEQ = 2048
PAST_LEN = 512
GRID_W = 64
N_HEADS = 8
N_KV_HEADS = 2
HEAD_DIM = 64
GQA_GROUP = N_HEADS // N_KV_HEADS
ATTN_WIDTH = N_HEADS * HEAD_DIM
KV_WIDTH = N_KV_HEADS * HEAD_DIM
WINDOW = 128
BLOCK = 128
ROPE_BASE = 10000.0
CONV_WIDTH = 512
CHUNK = 128
GMLP_WIDTH = 1024
GMLP_GROUPS = 8
GMLP_GROUP_DIM = GMLP_WIDTH // GMLP_GROUPS
D_FF = 2816
EPS = 1e-6
NEG_INF = -1e30
N_EVEN = (DEPTH + 1) // 2
N_ODD = DEPTH // 2
LOG2_E = 1.4426950408889634

LANES = 128
SUBLANES_F32 = 8
SUBLANES_BF16 = 16
MXU_WIDTH = 256
VMEM_LIMIT_BYTES = 56 * 1024 * 1024

COND_ROWS = 8
CTX_ROW = DEC_BATCH
TOKEN_HALO = SUBLANES_BF16
FFN_COLS = MXU_WIDTH
LEAD_PARTS = 4
CAST_BLOCK_BYTES = 3 * 1024 * 1024
ADA_TN = 1536
ATTN_Q_BLOCKS = 4
SUM_ROWS = SUBLANES_BF16
SCORE_CAP = 3.0e38


def _params(n_axes):
    return pltpu.CompilerParams(dimension_semantics=("parallel",) * n_axes,
                                vmem_limit_bytes=VMEM_LIMIT_BYTES)


def _resident(shape):
    zeros = (0,) * len(shape)
    return pl.BlockSpec(shape, lambda *_: zeros, pipeline_mode=pl.Buffered(1))


def _layer_slab(shape, layer):
    return pl.BlockSpec((1,) + tuple(shape[1:]), lambda *_: (layer, 0, 0), pipeline_mode=pl.Buffered(1))


def _mod_spec(mod_row):
    return pl.BlockSpec((1, 1, 6 * D_MODEL), lambda *idx: (mod_row(*idx), 0, 0))


def _dot(a, b):
    return jnp.dot(a, b, preferred_element_type=F32)


def _cast_kernel(x_ref, o_ref):
    o_ref[...] = x_ref[...].astype(BF16)


def _to_bf16(w):
    n, rows, cols = w.shape
    fits = [r for r in range(SUBLANES_BF16, rows + 1, SUBLANES_BF16)
            if rows % r == 0 and r * cols * 4 <= CAST_BLOCK_BYTES]
    tr = max(fits)
    spec = pl.BlockSpec((1, tr, cols), lambda i, j: (i, j, 0))
    return pl.pallas_call(
        _cast_kernel, grid=(n, rows // tr), in_specs=[spec], out_specs=spec,
        out_shape=jax.ShapeDtypeStruct(w.shape, BF16), compiler_params=_params(2), name="to_bf16",
    )(w)


def _modulate(x, g, shift, scale):
    ms = jnp.mean(x * x, axis=-1, keepdims=True)
    return (x * lax.rsqrt(ms + EPS) * g) * (1.0 + scale) + shift


def _mod_chunk(mod_ref, k):
    return mod_ref[0, :, k * D_MODEL:(k + 1) * D_MODEL]


def _halo_rows(tiles_per_seq):
    return TOKEN_HALO if tiles_per_seq > 1 else 0


def _token_specs(T, tm, tiles_per_seq):
    tok = pl.BlockSpec((tm, D_MODEL), lambda i: (i, 0))
    if tiles_per_seq == 1:
        return [tok]
    per_tile = tm // TOKEN_HALO
    last = T // TOKEN_HALO - 1
    prev = pl.BlockSpec((TOKEN_HALO, D_MODEL), lambda i: (jnp.maximum(i * per_tile - 1, 0), 0))
    nxt = pl.BlockSpec((TOKEN_HALO, D_MODEL), lambda i: (jnp.minimum((i + 1) * per_tile, last), 0))
    return [prev, tok, nxt]


def _modulated_parts(x_refs, h_scr, g, shift, scale, tiles_per_seq):
    halo = _halo_rows(tiles_per_seq)
    x_ref = x_refs[len(x_refs) // 2]
    tm = x_ref.shape[0]
    step = tm // LEAD_PARTS
    parts = []
    for p in range(LEAD_PARTS):
        lo, hi = halo + p * step, halo + (p + 1) * step
        piece = _modulate(x_ref[p * step:(p + 1) * step, :], g, shift, scale).astype(BF16)
        if halo and p == 0:
            pos = pl.program_id(0) % tiles_per_seq
            edge = jnp.where(pos > 0, _modulate(x_refs[0][...], g, shift, scale), 0.0).astype(BF16)
            piece, lo = jnp.concatenate([edge, piece], axis=0), 0
        if halo and p == LEAD_PARTS - 1:
            pos = pl.program_id(0) % tiles_per_seq
            edge = jnp.where(pos < tiles_per_seq - 1, _modulate(x_refs[2][...], g, shift, scale), 0.0).astype(BF16)
            piece, hi = jnp.concatenate([piece, edge], axis=0), tm + 2 * halo
        h_scr[lo:hi, :] = piece
        parts.append(piece)
    return parts


def _dot_rows(parts, w):
    return jnp.concatenate([_dot(p, w) for p in parts], axis=0)


def _seq_end_masks(tm, cols, seq_len, tiles_per_seq):
    if tiles_per_seq > 1:
        return None
    seq_row = lax.broadcasted_iota(jnp.int32, (tm, cols), 0) % seq_len
    return seq_row != 0, seq_row != seq_len - 1


def _token_conv3(z, w, tm, halo, masks):
    rows = z.shape[0]
    mid = slice(halo, halo + tm)
    dn = pltpu.roll(z, 1, 0)[mid]
    up = pltpu.roll(z, rows - 1, 0)[mid]
    if masks is not None:
        dn = jnp.where(masks[0], dn, 0.0)
        up = jnp.where(masks[1], up, 0.0)
    return dn * w[0:1] + z[mid] * w[1:2] + up * w[2:3]


def _adaln_kernel(cond_ref, w_ref, b_ref, o_ref):
    a = jax.nn.silu(cond_ref[...]).astype(BF16)
    o_ref[0] = _dot(a, w_ref[0].astype(BF16)) + b_ref[0]


def _adaln(cond, ada_w, ada_b):
    n_out = 6 * D_MODEL
    return pl.pallas_call(
        _adaln_kernel,
        grid=(DEPTH, n_out // ADA_TN),
        in_specs=[pl.BlockSpec((COND_ROWS, D_MODEL), lambda l, j: (0, 0)),
                  pl.BlockSpec((1, D_MODEL, ADA_TN), lambda l, j: (l, 0, j)),
                  pl.BlockSpec((1, 1, ADA_TN), lambda l, j: (l, 0, j))],
        out_specs=pl.BlockSpec((1, COND_ROWS, ADA_TN), lambda l, j: (l, 0, j)),
        out_shape=jax.ShapeDtypeStruct((DEPTH, COND_ROWS, n_out), F32),
        compiler_params=_params(2),
        name="adaln",
    )(cond, ada_w, ada_b.reshape(DEPTH, 1, n_out))


def _head_rms(z, gain):
    n = z.shape[1]
    w = min(n, MXU_WIDTH)
    r = lax.broadcasted_iota(jnp.int32, (w, w), 0) // HEAD_DIM
    c = lax.broadcasted_iota(jnp.int32, (w, w), 1) // HEAD_DIM
    ones = (r == c).astype(BF16)
    sq = (z * z).astype(BF16)
    ss = jnp.concatenate([_dot(sq[:, k:k + w], ones) for k in range(0, n, w)], axis=1)
    return z * lax.rsqrt(ss * (1.0 / HEAD_DIM) + EPS) * gain


def _rope(z, cos, sin_lo, sin_hi):
    outs = []
    for k in range(0, z.shape[1], LANES):
        blk = z[:, k:k + LANES]
        outs.append(blk * cos + pltpu.roll(blk, LANES - 16, 1) * sin_lo + pltpu.roll(blk, 16, 1) * sin_hi)
    return jnp.concatenate(outs, axis=1)


def _in_even_kernel(*refs, rope, tiles_per_seq, seq_len):
    halo = _halo_rows(tiles_per_seq)
    n_x = 3 if halo else 1
    x_refs, refs = refs[:n_x], refs[n_x:]
    mod_ref, g_ref, w_ref, qg_ref, kg_ref, cw_ref = refs[:6]
    refs = refs[6:]
    if rope:
        cos_ref, slo_ref, shi_ref = refs[:3]
        refs = refs[3:]
    q_ref, k_ref, v_ref, sc_ref, h_scr = refs
    tm = q_ref.shape[0]
    parts = _modulated_parts(x_refs, h_scr, g_ref[...], _mod_chunk(mod_ref, 0), _mod_chunk(mod_ref, 1),
                             tiles_per_seq)
    c0 = 0
    c1 = c0 + ATTN_WIDTH
    c2 = c1 + 2 * KV_WIDTH
    c3 = c2 + CONV_WIDTH
    c4 = c3 + CONV_WIDTH
    c5 = c4 + CONV_WIDTH
    zq = _dot_rows(parts, w_ref[:, c0:c1])[halo:halo + tm]
    h_all = h_scr[...]
    h = h_scr[halo:halo + tm, :]
    q = _head_rms(zq, qg_ref[...] * (HEAD_DIM ** -0.5 * LOG2_E))
    zkv = _dot(h, w_ref[:, c1:c2])
    k = _head_rms(zkv[:, :KV_WIDTH], kg_ref[...])
    if rope:
        cos, slo, shi = cos_ref[...], slo_ref[...], shi_ref[...]
        q = _rope(q, cos, slo, shi)
        k = _rope(k, cos, slo, shi)
    q_ref[...] = q
    k_ref[...] = k
    v_ref[...] = zkv[:, KV_WIDTH:]
    ch = _dot(h_all, w_ref[:, c3:c4]) * _dot(h_all, w_ref[:, c4:c5])
    conv = _token_conv3(ch, cw_ref[...], tm, halo, _seq_end_masks(tm, CONV_WIDTH, seq_len, tiles_per_seq))
    sc_ref[...] = (_dot(h, w_ref[:, c2:c3]) * conv).astype(BF16)


def _in_even(x, mod_l, mod_row, g, w_bf, qg, kg, conv_w, rope_tabs, tm, seq_len):
    T = x.shape[0]
    n_in = w_bf.shape[1]
    tiles_per_seq = max(seq_len // tm, 1)
    assert tm % seq_len == 0 or seq_len % tm == 0
    tok = lambda w: pl.BlockSpec((tm, w), lambda i: (i, 0))
    x_specs = _token_specs(T, tm, tiles_per_seq)
    in_specs = x_specs + [_mod_spec(mod_row), _resident((1, D_MODEL)), _resident((D_MODEL, n_in)),
                          _resident((1, ATTN_WIDTH)), _resident((1, KV_WIDTH)), _resident((3, CONV_WIDTH))]
    args = [x] * len(x_specs) + [mod_l, g, w_bf, qg, kg, conv_w]
    if rope_tabs is not None:
        in_specs += [pl.BlockSpec((tm, LANES), lambda i: (i % tiles_per_seq, 0))] * 3
        args += list(rope_tabs)
    outs = ((ATTN_WIDTH, F32), (KV_WIDTH, F32), (KV_WIDTH, F32), (CONV_WIDTH, BF16))
    scratch = [pltpu.VMEM((tm + 2 * _halo_rows(tiles_per_seq), D_MODEL), BF16)]
    return pl.pallas_call(
        functools.partial(_in_even_kernel, rope=rope_tabs is not None, tiles_per_seq=tiles_per_seq,
                          seq_len=seq_len),
        grid=(T // tm,),
        in_specs=in_specs,
        out_specs=[tok(w) for w, _ in outs],
        out_shape=[jax.ShapeDtypeStruct((T, w), dt) for w, dt in outs],
        scratch_shapes=scratch,
        compiler_params=_params(1),
        name="in_even",
    )(*args)


def _stack_heads(q):
    lane_lo = lax.broadcasted_iota(jnp.int32, (BLOCK, LANES), 1) < HEAD_DIM
    heads = []
    for pair in range(N_HEADS // 2):
        kv = (2 * pair) // GQA_GROUP
        qp = q[:, pair * LANES:(pair + 1) * LANES]
        qr = pltpu.roll(qp, HEAD_DIM, 1)
        for half in range(2):
            src = qp if half == kv else qr
            heads.append(jnp.where(lane_lo, src, 0.0) if kv == 0 else jnp.where(lane_lo, 0.0, src))
    return jnp.concatenate(heads, axis=0).astype(BF16)


def _transposed_values(v_blocks):
    vt = jnp.concatenate([v[j:j + BLOCK].T for v in v_blocks for j in range(0, v.shape[0], BLOCK)], axis=1)
    return jnp.concatenate([vt.astype(BF16), jnp.ones((SUM_ROWS, vt.shape[1]), BF16)], axis=0)


def _scores(q, kcat, cap):
    st = lax.dot_general(kcat, _stack_heads(q), (((1,), (1,)), ((), ())), preferred_element_type=F32)
    if cap is None:
        return st
    n_band = cap.shape[0]
    capped = jnp.minimum(st[:n_band], jnp.concatenate([cap] * N_HEADS, axis=1))
    return jnp.concatenate([capped, st[n_band:]], axis=0)


def _weighted_values(st, vt, sink_row):
    m = jnp.maximum(jnp.max(st, axis=0, keepdims=True), sink_row)
    pt = jnp.exp2(st - m).astype(BF16)
    ot = _dot(vt, pt)
    denom = ot[KV_WIDTH:KV_WIDTH + 1] + jnp.exp2(sink_row - m)
    ot = ot[:KV_WIDTH] / denom
    pairs = []
    for pair in range(N_HEADS // 2):
        kv = (2 * pair) // GQA_GROUP
        dims = slice(kv * HEAD_DIM, (kv + 1) * HEAD_DIM)
        both = [ot[dims, (2 * pair + half) * BLOCK:(2 * pair + half + 1) * BLOCK] for half in range(2)]
        pairs.append(jnp.concatenate(both, axis=0).T)
    return jnp.concatenate(pairs, axis=1)


def _attend_blocks(n_blocks, scores_of, values_of, sink_row, attn_scr):
    st = scores_of(0)
    for j in range(n_blocks):
        st_next = scores_of(j + 1) if j + 1 < n_blocks else None
        attn_scr[j * BLOCK:(j + 1) * BLOCK, :] = _weighted_values(st, values_of(j), sink_row).astype(BF16)
        st = st_next


def _project_out(attn_scr, sc_ref, x_ref, mod_ref, w_ref, o_ref):
    mix = _dot(attn_scr[...], w_ref[:ATTN_WIDTH, :]) + _dot(sc_ref[...], w_ref[ATTN_WIDTH:, :])
    o_ref[...] = x_ref[...] + _mod_chunk(mod_ref, 2) * mix


def _attn_ctx_kernel(sink_ref, q_ref, k_ref, v_ref, sc_ref, x_ref, mod_ref, w_ref, o_ref, attn_scr):
    kcat = k_ref[...].astype(BF16)
    vt = _transposed_values([v_ref[...]])
    sink_row = sink_ref[...] * LOG2_E
    _attend_blocks(SEQ // BLOCK, lambda j: _scores(q_ref[j * BLOCK:(j + 1) * BLOCK, :], kcat, None),
                   lambda j: vt, sink_row, attn_scr)
    _project_out(attn_scr, sc_ref, x_ref, mod_ref, w_ref, o_ref)


def _attn_ctx(q, k, v, sc, x, mod_l, sink_row, w_out_bf):
    T = q.shape[0]
    tok = lambda w: pl.BlockSpec((SEQ, w), lambda i: (i, 0))
    return pl.pallas_call(
        _attn_ctx_kernel,
        grid=(T // SEQ,),
        in_specs=[_resident((1, N_HEADS * BLOCK)), tok(ATTN_WIDTH), tok(KV_WIDTH), tok(KV_WIDTH),
                  tok(CONV_WIDTH), tok(D_MODEL), _mod_spec(lambda i: CTX_ROW),
                  _resident((ATTN_WIDTH + CONV_WIDTH, D_MODEL))],
        out_specs=tok(D_MODEL),
        out_shape=jax.ShapeDtypeStruct((T, D_MODEL), F32),
        scratch_shapes=[pltpu.VMEM((SEQ, ATTN_WIDTH), BF16)],
        compiler_params=_params(1),
        name="attn_ctx",
    )(sink_row, q, k, v, sc, x, mod_l, w_out_bf)


def _band_cap(has_prev, has_next):
    c = lax.broadcasted_iota(jnp.int32, (3 * BLOCK, BLOCK), 0)
    r = lax.broadcasted_iota(jnp.int32, (3 * BLOCK, BLOCK), 1)
    first_prev = r + jnp.where(has_prev, 0, BLOCK)
    last_next = r + 2 * BLOCK - jnp.where(has_next, 0, BLOCK)
    masked = ((c < BLOCK) & (c < first_prev)) | ((c >= 2 * BLOCK) & (c > last_next))
    return jnp.where(masked, NEG_INF, SCORE_CAP)


def _attn_win_kernel(sink_ref, q_ref, kp_ref, kc_ref, kn_ref, vp_ref, vc_ref, vn_ref, ck_ref, cv_ref,
                     sc_ref, x_ref, mod_ref, w_ref, o_ref, attn_scr, *, n_steps):
    i = pl.program_id(1)
    kc, vc = kc_ref[...], vc_ref[...]
    inner = range(0, ATTN_Q_BLOCKS * BLOCK, BLOCK)
    k_blocks = [kp_ref[...]] + [kc[j:j + BLOCK] for j in inner] + [kn_ref[...]]
    v_blocks = [vp_ref[...]] + [vc[j:j + BLOCK] for j in inner] + [vn_ref[...]]
    sink_row = sink_ref[...] * LOG2_E

    def scores_of(j):
        kcat = jnp.concatenate(k_blocks[j:j + 3] + [ck_ref[0]], axis=0).astype(BF16)
        cap = _band_cap(i > 0 if j == 0 else True, i < n_steps - 1 if j == ATTN_Q_BLOCKS - 1 else True)
        return _scores(q_ref[j * BLOCK:(j + 1) * BLOCK, :], kcat, cap)

    _attend_blocks(ATTN_Q_BLOCKS, scores_of, lambda j: _transposed_values(v_blocks[j:j + 3] + [cv_ref[0]]),
                   sink_row, attn_scr)
    _project_out(attn_scr, sc_ref, x_ref, mod_ref, w_ref, o_ref)


def _attn_win(q, k, v, ck, cv, sc, x, mod_l, sink_row, w_out_bf):
    T = q.shape[0]
    tq = ATTN_Q_BLOCKS * BLOCK
    n_steps = DEC_SEQ // tq
    nb = DEC_SEQ // BLOCK
    cur = lambda w: pl.BlockSpec((tq, w), lambda b, i: (b * n_steps + i, 0))
    prev = pl.BlockSpec((BLOCK, KV_WIDTH), lambda b, i: (b * nb + jnp.maximum(ATTN_Q_BLOCKS * i - 1, 0), 0))
    nxt = pl.BlockSpec((BLOCK, KV_WIDTH),
                       lambda b, i: (b * nb + jnp.minimum(ATTN_Q_BLOCKS * (i + 1), nb - 1), 0))
    ctx = pl.BlockSpec((1, PAST_LEN, KV_WIDTH), lambda b, i: (b, 0, 0))
    return pl.pallas_call(
        functools.partial(_attn_win_kernel, n_steps=n_steps),
        grid=(T // DEC_SEQ, n_steps),
        in_specs=[_resident((1, N_HEADS * BLOCK)), cur(ATTN_WIDTH),
                  prev, cur(KV_WIDTH), nxt, prev, cur(KV_WIDTH), nxt, ctx, ctx,
                  cur(CONV_WIDTH), cur(D_MODEL), _mod_spec(lambda b, i: b),
                  _resident((ATTN_WIDTH + CONV_WIDTH, D_MODEL))],
        out_specs=cur(D_MODEL),
        out_shape=jax.ShapeDtypeStruct((T, D_MODEL), F32),
        scratch_shapes=[pltpu.VMEM((tq, ATTN_WIDTH), BF16)],
        compiler_params=_params(2),
        name="attn_win",
    )(sink_row, q, k, k, k, v, v, v, ck, cv, sc, x, mod_l, w_out_bf)


def _gelu_tanh(x):
    k = 0.7978845608028654
    half = 0.5 * x
    return half + half * jnp.tanh(x * (k + (k * 0.044715) * (x * x)))


def _gmlp_kernel(x_ref, mod_ref, g_ref, win_ref, vg_ref, ws_ref, bs_ref, wout_ref, o_ref, gated_scr, h_scr):
    n_chunks = x_ref.shape[0] // CHUNK
    parts = _modulated_parts([x_ref], h_scr, g_ref[...], _mod_chunk(mod_ref, 0), _mod_chunk(mod_ref, 1), 1)
    u = _gelu_tanh(_dot_rows(parts, win_ref[:, :GMLP_WIDTH]))
    v = _gelu_tanh(_dot(h_scr[...], win_ref[:, GMLP_WIDTH:]))
    ms = jnp.mean(v * v, axis=-1, keepdims=True)
    v = (v * lax.rsqrt(ms + EPS) * vg_ref[...]).astype(BF16)
    for grp in range(GMLP_GROUPS):
        lanes = slice(grp * GMLP_GROUP_DIM, (grp + 1) * GMLP_GROUP_DIM)
        rhs = jnp.concatenate([v[n * CHUNK:(n + 1) * CHUNK, lanes] for n in range(n_chunks)], axis=1)
        s = _dot(ws_ref[grp], rhs)
        for n in range(n_chunks):
            rows = slice(n * CHUNK, (n + 1) * CHUNK)
            s_n = s[:, n * GMLP_GROUP_DIM:(n + 1) * GMLP_GROUP_DIM] + bs_ref[grp]
            gated_scr[rows, lanes] = (u[rows, lanes] * s_n).astype(BF16)
    o_ref[...] = x_ref[...] + _mod_chunk(mod_ref, 2) * _dot(gated_scr[...], wout_ref[...])


def _gmlp(x, mod_l, mod_row, g, win_bf, vg, ws_bf, bs_full, wout_bf, tm):
    T = x.shape[0]
    tok = pl.BlockSpec((tm, D_MODEL), lambda i: (i, 0))
    return pl.pallas_call(
        _gmlp_kernel,
        grid=(T // tm,),
        in_specs=[tok, _mod_spec(mod_row),
                  _resident((1, D_MODEL)), _resident((D_MODEL, 2 * GMLP_WIDTH)), _resident((1, GMLP_WIDTH)),
                  _resident((GMLP_GROUPS, CHUNK, CHUNK)), _resident((GMLP_GROUPS, CHUNK, GMLP_GROUP_DIM)),
                  _resident((GMLP_WIDTH, D_MODEL))],
        out_specs=tok,
        out_shape=jax.ShapeDtypeStruct((T, D_MODEL), F32),
        scratch_shapes=[pltpu.VMEM((tm, GMLP_WIDTH), BF16), pltpu.VMEM((tm, D_MODEL), BF16)],
        compiler_params=_params(1),
        name="gmlp",
    )(x, mod_l, g, win_bf, vg, ws_bf, bs_full, wout_bf)


def _ffn_kernel(*refs, tiles_per_seq, seq_len):
    halo = _halo_rows(tiles_per_seq)
    n_x = 3 if halo else 1
    x_refs, refs = refs[:n_x], refs[n_x:]
    mod_ref, g_ref, wup_ref, cw_ref, wdn_ref, o_ref, act_scr, h_scr = refs
    tm = o_ref.shape[0]
    parts = _modulated_parts(x_refs, h_scr, g_ref[...], _mod_chunk(mod_ref, 3), _mod_chunk(mod_ref, 4),
                             tiles_per_seq)
    masks = _seq_end_masks(tm, FFN_COLS, seq_len, tiles_per_seq)
    for j in range(0, D_FF, FFN_COLS):
        gate_cols = slice(j, j + FFN_COLS)
        val_cols = slice(D_FF + j, D_FF + j + FFN_COLS)
        if j == 0:
            up = lambda cols: _dot_rows(parts, wup_ref[0, :, cols])
        else:
            h = h_scr[...]
            up = lambda cols: _dot(h, wup_ref[0, :, cols])
        zg = _token_conv3(up(gate_cols), cw_ref[0, :, gate_cols], tm, halo, masks)
        zv = _token_conv3(up(val_cols), cw_ref[0, :, val_cols], tm, halo, masks)
        act_scr[:, gate_cols] = (jax.nn.silu(zg) * zv).astype(BF16)
    x = x_refs[n_x // 2][...]
    o_ref[...] = x + _mod_chunk(mod_ref, 5) * _dot(act_scr[...], wdn_ref[0])


def _ffn(x, mod_l, mod_row, g, wup_bf, conv_w, wdn_bf, layer, tm, seq_len):
    T = x.shape[0]
    tiles_per_seq = max(seq_len // tm, 1)
    assert tm % seq_len == 0 or seq_len % tm == 0
    x_specs = _token_specs(T, tm, tiles_per_seq)
    in_specs = x_specs + [_mod_spec(mod_row), _resident((1, D_MODEL)), _layer_slab(wup_bf.shape, layer),
                          _layer_slab(conv_w.shape, layer), _layer_slab(wdn_bf.shape, layer)]
    scratch = [pltpu.VMEM((tm, D_FF), BF16), pltpu.VMEM((tm + 2 * _halo_rows(tiles_per_seq), D_MODEL), BF16)]
    return pl.pallas_call(
        functools.partial(_ffn_kernel, tiles_per_seq=tiles_per_seq, seq_len=seq_len),
        grid=(T // tm,),
        in_specs=in_specs,
        out_specs=pl.BlockSpec((tm, D_MODEL), lambda i: (i, 0)),
        out_shape=jax.ShapeDtypeStruct((T, D_MODEL), F32),
        scratch_shapes=scratch,
        compiler_params=_params(1),
        name="conv_ffn",
    )(*([x] * len(x_specs) + [mod_l, g, wup_bf, conv_w, wdn_bf]))


def _rope_tables(n_tokens):
    t = jnp.arange(n_tokens)
    n_freq = HEAD_DIM // 4
    inv = ROPE_BASE ** (-jnp.arange(n_freq, dtype=F32) / n_freq)
    row_ang = (t // GRID_W).astype(F32)[:, None] * inv
    col_ang = (t % GRID_W).astype(F32)[:, None] * inv
    ang = jnp.concatenate([row_ang, row_ang, col_ang, col_ang], axis=1)
    ang = jnp.tile(ang, (1, LANES // HEAD_DIM))
    first = (jnp.arange(LANES) % (2 * n_freq)) < n_freq
    cos, sin = jnp.cos(ang), jnp.sin(ang)
    return cos, jnp.where(first, -sin, 0.0), jnp.where(first, 0.0, sin)


def kernel(x_prompt, x_sample, cache_k, cache_v, c, c_ctx, ada_w, ada_b, norm_mix_g, norm_ffn_g, w_in_even,
           q_norm_g, k_norm_g, sink_logit, short_conv_w, w_out_even, w_in_odd, gmlp_norm_g, w_spatial,
           b_spatial, w_out_odd, w_up, ffn_conv_w, w_down):
    n_p, n_s = BATCH * SEQ, DEC_BATCH * DEC_SEQ
    xp = x_prompt.reshape(n_p, D_MODEL)
    xs = x_sample.reshape(n_s, D_MODEL)
    cond = jnp.concatenate([c, c_ctx[None, :], jnp.zeros((COND_ROWS - DEC_BATCH - 1, D_MODEL), F32)], axis=0)
    mod = _adaln(cond, ada_w, ada_b)
    rope_tabs = _rope_tables(DEC_SEQ)
    w_in_even_bf, w_out_even_bf = _to_bf16(w_in_even), _to_bf16(w_out_even)
    w_in_odd_bf, w_out_odd_bf = _to_bf16(w_in_odd), _to_bf16(w_out_odd)
    w_spatial_bf = _to_bf16(w_spatial.reshape(N_ODD, GMLP_GROUPS * CHUNK, CHUNK))
    w_up_bf, w_down_bf = _to_bf16(w_up), _to_bf16(w_down)

    tm_p, tm_s = 1024, 1024
    row_p = lambda i: CTX_ROW
    row_s = lambda i: i // (DEC_SEQ // tm_s)
    row_vec = lambda a: a.reshape(1, -1)
    new_k, new_v = [], []
    for l in range(DEPTH):
        mod_l = mod[l].reshape(COND_ROWS, 1, 6 * D_MODEL)
        g_mix = row_vec(norm_mix_g[l])
        if l % 2 == 0:
            e = l // 2
            w_in, w_out = w_in_even_bf[e], w_out_even_bf[e]
            qg = row_vec(jnp.tile(q_norm_g[e], N_HEADS))
            kg = row_vec(jnp.tile(k_norm_g[e], N_KV_HEADS))
            sink = jnp.repeat(sink_logit[e], BLOCK).reshape(1, N_HEADS * BLOCK)
            cw = short_conv_w[e]
            qp, kp, vp, scp = _in_even(xp, mod_l, row_p, g_mix, w_in, qg, kg, cw, None, tm_p, SEQ)
            new_k.append(kp.reshape(BATCH, SEQ, N_KV_HEADS, HEAD_DIM))
            new_v.append(vp.reshape(BATCH, SEQ, N_KV_HEADS, HEAD_DIM))
            xp = _attn_ctx(qp, kp, vp, scp, xp, mod_l, sink, w_out)
            qs, ks, vs, scs = _in_even(xs, mod_l, row_s, g_mix, w_in, qg, kg, cw, rope_tabs, tm_s, DEC_SEQ)
            ck = cache_k[:, e].reshape(DEC_BATCH, PAST_LEN, KV_WIDTH)
            cv = cache_v[:, e].reshape(DEC_BATCH, PAST_LEN, KV_WIDTH)
            xs = _attn_win(qs, ks, vs, ck, cv, scs, xs, mod_l, sink, w_out)
        else:
            o = l // 2
            w_in, w_out = w_in_odd_bf[o], w_out_odd_bf[o]
            vg = row_vec(gmlp_norm_g[o])
            ws = w_spatial_bf[o].reshape(GMLP_GROUPS, CHUNK, CHUNK)
            bs_full = jnp.broadcast_to(b_spatial[o][:, :, None], (GMLP_GROUPS, CHUNK, GMLP_GROUP_DIM))
            xp = _gmlp(xp, mod_l, row_p, g_mix, w_in, vg, ws, bs_full, w_out, tm_p)
            xs = _gmlp(xs, mod_l, row_s, g_mix, w_in, vg, ws, bs_full, w_out, tm_s)
        g_ffn = row_vec(norm_ffn_g[l])
        xp = _ffn(xp, mod_l, row_p, g_ffn, w_up_bf, ffn_conv_w, w_down_bf, l, tm_p, SEQ)
        xs = _ffn(xs, mod_l, row_s, g_ffn, w_up_bf, ffn_conv_w, w_down_bf, l, tm_s, DEC_SEQ)
    per_layer = lambda parts: parts[0][:, None] if len(parts) == 1 else jnp.stack(parts, axis=1)
    return (xp.reshape(BATCH, SEQ, D_MODEL), xs.reshape(DEC_BATCH, DEC_SEQ, D_MODEL),
            per_layer(new_k), per_layer(new_v))
```

```python
import functools

import jax
import jax.numpy as jnp
import numpy as np
from jax import lax
from jax.experimental import pallas as pl
from jax.experimental.pallas import tpu as pltpu

F32 = jnp.float32
BF16 = jnp.bfloat16

D_MODEL = 1024
BATCH = 16
SEQ = 256
DEPTH = 2
DEC_BATCH = 4
DEC_SEQ = 2048
PAST_LEN = 512
GRID_W = 64
N_HEADS = 8
N_KV_HEADS = 2
HEAD_DIM = 64
GQA_GROUP = N_HEADS // N_KV_HEADS
ATTN_WIDTH = N_HEADS * HEAD_DIM
KV_WIDTH = N_KV_HEADS * HEAD_DIM
WINDOW = 128
BLOCK = 128
ROPE_BASE = 10000.0
CONV_WIDTH = 512
CHUNK = 128
GMLP_WIDTH = 1024
GMLP_GROUPS = 8
GMLP_GROUP_DIM = GMLP_WIDTH // GMLP_GROUPS
D_FF = 2816
EPS = 1e-6
NEG_INF = -1e30
N_EVEN = (DEPTH + 1) // 2
N_ODD = DEPTH // 2
LOG2_E = 1.4426950408889634

LANES = 128
SUBLANES_F32 = 8
SUBLANES_BF16 = 16
MXU_WIDTH = 256
VMEM_LIMIT_BYTES = 56 * 1024 * 1024

COND_ROWS = 8
CTX_ROW = DEC_BATCH
TOKEN_HALO = SUBLANES_BF16
FFN_COLS = MXU_WIDTH
LEAD_PARTS = 4
CAST_BLOCK_BYTES = 3 * 1024 * 1024
ADA_TN = 1536
ATTN_Q_BLOCKS = 4
ATTN_CTX_SEQS = 2
SUM_ROWS = SUBLANES_BF16
SCORE_CAP = 3.0e38


def _params(n_axes):
    return pltpu.CompilerParams(dimension_semantics=("parallel",) * n_axes,
                                vmem_limit_bytes=VMEM_LIMIT_BYTES)


def _resident(shape):
    zeros = (0,) * len(shape)
    return pl.BlockSpec(shape, lambda *_: zeros, pipeline_mode=pl.Buffered(1))


def _layer_slab(shape, layer):
    return pl.BlockSpec((1,) + tuple(shape[1:]), lambda *_: (layer, 0, 0), pipeline_mode=pl.Buffered(1))


def _mod_spec(mod_row):
    return pl.BlockSpec((1, 1, 6 * D_MODEL), lambda *idx: (mod_row(*idx), 0, 0))


def _dot(a, b):
    return jnp.dot(a, b, preferred_element_type=F32)


def _cast_kernel(x_ref, o_ref):
    o_ref[...] = x_ref[...].astype(BF16)


def _to_bf16(w):
    n, rows, cols = w.shape
    fits = [r for r in range(SUBLANES_BF16, rows + 1, SUBLANES_BF16)
            if rows % r == 0 and r * cols * 4 <= CAST_BLOCK_BYTES]
    tr = max(fits)
    spec = pl.BlockSpec((1, tr, cols), lambda i, j: (i, j, 0))
    return pl.pallas_call(
        _cast_kernel, grid=(n, rows // tr), in_specs=[spec], out_specs=spec,
        out_shape=jax.ShapeDtypeStruct(w.shape, BF16), compiler_params=_params(2), name="to_bf16",
    )(w)


def _modulate(x, g, shift, scale):
    ms = jnp.mean(x * x, axis=-1, keepdims=True)
    return (x * lax.rsqrt(ms + EPS) * g) * (1.0 + scale) + shift


def _mod_chunk(mod_ref, k):
    return mod_ref[0, :, k * D_MODEL:(k + 1) * D_MODEL]


def _halo_rows(tiles_per_seq):
    return TOKEN_HALO if tiles_per_seq > 1 else 0


def _token_specs(T, tm, tiles_per_seq):
    tok = pl.BlockSpec((tm, D_MODEL), lambda i: (i, 0))
    if tiles_per_seq == 1:
        return [tok]
    per_tile = tm // TOKEN_HALO
    last = T // TOKEN_HALO - 1
    prev = pl.BlockSpec((TOKEN_HALO, D_MODEL), lambda i: (jnp.maximum(i * per_tile - 1, 0), 0))
    nxt = pl.BlockSpec((TOKEN_HALO, D_MODEL), lambda i: (jnp.minimum((i + 1) * per_tile, last), 0))
    return [prev, tok, nxt]


def _modulated_parts(x_refs, h_scr, g, shift, scale, tiles_per_seq):
    halo = _halo_rows(tiles_per_seq)
    x_ref = x_refs[len(x_refs) // 2]
    tm = x_ref.shape[0]
    step = tm // LEAD_PARTS
    parts = []
    for p in range(LEAD_PARTS):
        lo, hi = halo + p * step, halo + (p + 1) * step
        piece = _modulate(x_ref[p * step:(p + 1) * step, :], g, shift, scale).astype(BF16)
        if halo and p == 0:
            pos = pl.program_id(0) % tiles_per_seq
            edge = jnp.where(pos > 0, _modulate(x_refs[0][...], g, shift, scale), 0.0).astype(BF16)
            piece, lo = jnp.concatenate([edge, piece], axis=0), 0
        if halo and p == LEAD_PARTS - 1:
            pos = pl.program_id(0) % tiles_per_seq
            edge = jnp.where(pos < tiles_per_seq - 1, _modulate(x_refs[2][...], g, shift, scale), 0.0).astype(BF16)
            piece, hi = jnp.concatenate([piece, edge], axis=0), tm + 2 * halo
        h_scr[lo:hi, :] = piece
        parts.append(piece)
    return parts


def _dot_rows(parts, w):
    return jnp.concatenate([_dot(p, w) for p in parts], axis=0)


def _seq_end_masks(tm, cols, seq_len, tiles_per_seq):
    if tiles_per_seq > 1:
        return None
    seq_row = lax.broadcasted_iota(jnp.int32, (tm, cols), 0) % seq_len
    return seq_row != 0, seq_row != seq_len - 1


def _token_conv3(z, w, tm, halo, masks):
    rows = z.shape[0]
    mid = slice(halo, halo + tm)
    dn = pltpu.roll(z, 1, 0)[mid]
    up = pltpu.roll(z, rows - 1, 0)[mid]
    if masks is not None:
        dn = jnp.where(masks[0], dn, 0.0)
        up = jnp.where(masks[1], up, 0.0)
    return dn * w[0:1] + z[mid] * w[1:2] + up * w[2:3]


def _adaln_kernel(cond_ref, w_ref, b_ref, o_ref):
    a = jax.nn.silu(cond_ref[...]).astype(BF16)
    rows = _dot(a, w_ref[0].astype(BF16)) + b_ref[0]
    for r in range(COND_ROWS):
        o_ref[r] = rows[r:r + 1, :]


def _adaln(cond, ada_w, ada_b):
    n_out = 6 * D_MODEL
    return pl.pallas_call(
        _adaln_kernel,
        grid=(DEPTH, n_out // ADA_TN),
        in_specs=[pl.BlockSpec((COND_ROWS, D_MODEL), lambda l, j: (0, 0)),
                  pl.BlockSpec((1, D_MODEL, ADA_TN), lambda l, j: (l, 0, j)),
                  pl.BlockSpec((1, 1, ADA_TN), lambda l, j: (l, 0, j))],
        out_specs=pl.BlockSpec((COND_ROWS, 1, ADA_TN), lambda l, j: (l, 0, j)),
        out_shape=jax.ShapeDtypeStruct((DEPTH * COND_ROWS, 1, n_out), F32),
        compiler_params=_params(2),
        name="adaln",
    )(cond, ada_w, ada_b.reshape(DEPTH, 1, n_out))


def _head_rms(z, gain):
    n = z.shape[1]
    w = min(n, MXU_WIDTH)
    r = lax.broadcasted_iota(jnp.int32, (w, w), 0) // HEAD_DIM
    c = lax.broadcasted_iota(jnp.int32, (w, w), 1) // HEAD_DIM
    ones = (r == c).astype(BF16)
    sq = (z * z).astype(BF16)
    ss = jnp.concatenate([_dot(sq[:, k:k + w], ones) for k in range(0, n, w)], axis=1)
    return z * lax.rsqrt(ss * (1.0 / HEAD_DIM) + EPS) * gain


def _rope(z, cos, sin_lo, sin_hi):
    outs = []
    for k in range(0, z.shape[1], LANES):
        blk = z[:, k:k + LANES]
        outs.append(blk * cos + pltpu.roll(blk, LANES - 16, 1) * sin_lo + pltpu.roll(blk, 16, 1) * sin_hi)
    return jnp.concatenate(outs, axis=1)


def _in_even_kernel(*refs, rope, tiles_per_seq, seq_len):
    halo = _halo_rows(tiles_per_seq)
    n_x = 3 if halo else 1
    x_refs, refs = refs[:n_x], refs[n_x:]
    mod_ref, g_ref, w_ref, qg_ref, kg_ref, cw_ref = refs[:6]
    refs = refs[6:]
    if rope:
        cos_ref, slo_ref, shi_ref = refs[:3]
        refs = refs[3:]
    q_ref, k_ref, v_ref, sc_ref, h_scr = refs
    tm = q_ref.shape[0]
    parts = _modulated_parts(x_refs, h_scr, g_ref[...], _mod_chunk(mod_ref, 0), _mod_chunk(mod_ref, 1),
                             tiles_per_seq)
    c0 = 0
    c1 = c0 + ATTN_WIDTH
    c2 = c1 + 2 * KV_WIDTH
    c3 = c2 + CONV_WIDTH
    c4 = c3 + CONV_WIDTH
    c5 = c4 + CONV_WIDTH
    zq = _dot_rows(parts, w_ref[:, c0:c1])[halo:halo + tm]
    h_all = h_scr[...]
    h = h_scr[halo:halo + tm, :]
    q = _head_rms(zq, qg_ref[...] * (HEAD_DIM ** -0.5 * LOG2_E))
    zkv = _dot(h, w_ref[:, c1:c2])
    k = _head_rms(zkv[:, :KV_WIDTH], kg_ref[...])
    if rope:
        cos, slo, shi = cos_ref[...], slo_ref[...], shi_ref[...]
        q = _rope(q, cos, slo, shi)
        k = _rope(k, cos, slo, shi)
    q_ref[...] = q
    k_ref[...] = k
    v_ref[...] = zkv[:, KV_WIDTH:]
    ch = _dot(h_all, w_ref[:, c3:c4]) * _dot(h_all, w_ref[:, c4:c5])
    conv = _token_conv3(ch, cw_ref[...], tm, halo, _seq_end_masks(tm, CONV_WIDTH, seq_len, tiles_per_seq))
    sc_ref[...] = (_dot(h, w_ref[:, c2:c3]) * conv).astype(BF16)


def _in_even(x, mod_l, mod_row, g, w_bf, qg, kg, conv_w, rope_tabs, tm, seq_len):
    T = x.shape[0]
    n_in = w_bf.shape[1]
    tiles_per_seq = max(seq_len // tm, 1)
    assert tm % seq_len == 0 or seq_len % tm == 0
    tok = lambda w: pl.BlockSpec((tm, w), lambda i: (i, 0))
    x_specs = _token_specs(T, tm, tiles_per_seq)
    in_specs = x_specs + [_mod_spec(mod_row), _resident((1, D_MODEL)), _resident((D_MODEL, n_in)),
                          _resident((1, ATTN_WIDTH)), _resident((1, KV_WIDTH)), _resident((3, CONV_WIDTH))]
    args = [x] * len(x_specs) + [mod_l, g, w_bf, qg, kg, conv_w]
    if rope_tabs is not None:
        in_specs += [pl.BlockSpec((tm, LANES), lambda i: (i % tiles_per_seq, 0))] * 3
        args += list(rope_tabs)
    outs = ((ATTN_WIDTH, F32), (KV_WIDTH, F32), (KV_WIDTH, F32), (CONV_WIDTH, BF16))
    scratch = [pltpu.VMEM((tm + 2 * _halo_rows(tiles_per_seq), D_MODEL), BF16)]
    return pl.pallas_call(
        functools.partial(_in_even_kernel, rope=rope_tabs is not None, tiles_per_seq=tiles_per_seq,
                          seq_len=seq_len),
        grid=(T // tm,),
        in_specs=in_specs,
        out_specs=[tok(w) for w, _ in outs],
        out_shape=[jax.ShapeDtypeStruct((T, w), dt) for w, dt in outs],
        scratch_shapes=scratch,
        compiler_params=_params(1),
        name="in_even",
    )(*args)


def _stack_heads(q):
    lane_lo = lax.broadcasted_iota(jnp.int32, (BLOCK, LANES), 1) < HEAD_DIM
    heads = []
    for pair in range(N_HEADS // 2):
        kv = (2 * pair) // GQA_GROUP
        qp = q[:, pair * LANES:(pair + 1) * LANES]
        qr = pltpu.roll(qp, HEAD_DIM, 1)
        for half in range(2):
            src = qp if half == kv else qr
            heads.append(jnp.where(lane_lo, src, 0.0) if kv == 0 else jnp.where(lane_lo, 0.0, src))
    return jnp.concatenate(heads, axis=0).astype(BF16)


def _transposed_values(v_blocks):
    vt = jnp.concatenate([v[j:j + BLOCK].T for v in v_blocks for j in range(0, v.shape[0], BLOCK)], axis=1)
    return jnp.concatenate([vt.astype(BF16), jnp.ones((SUM_ROWS, vt.shape[1]), BF16)], axis=0)


def _scores(q, kcat, cap):
    st = lax.dot_general(kcat, _stack_heads(q), (((1,), (1,)), ((), ())), preferred_element_type=F32)
    if cap is None:
        return st
    n_band = cap.shape[0]
    capped = jnp.minimum(st[:n_band], jnp.concatenate([cap] * N_HEADS, axis=1))
    return jnp.concatenate([capped, st[n_band:]], axis=0)


def _weighted_values(st, vt, sink_row):
    m = jnp.maximum(jnp.max(st, axis=0, keepdims=True), sink_row)
    pt = jnp.exp2(st - m).astype(BF16)
    ot = _dot(vt, pt)
    denom = ot[KV_WIDTH:KV_WIDTH + 1] + jnp.exp2(sink_row - m)
    ot = ot[:KV_WIDTH] / denom
    pairs = []
    for pair in range(N_HEADS // 2):
        kv = (2 * pair) // GQA_GROUP
        dims = slice(kv * HEAD_DIM, (kv + 1) * HEAD_DIM)
        both = [ot[dims, (2 * pair + half) * BLOCK:(2 * pair + half + 1) * BLOCK] for half in range(2)]
        pairs.append(jnp.concatenate(both, axis=0).T)
    return jnp.concatenate(pairs, axis=1)


def _attend_blocks(n_blocks, scores_of, values_of, sink_row, attn_scr):
    st = scores_of(0)
    for j in range(n_blocks):
        st_next = scores_of(j + 1) if j + 1 < n_blocks else None
        attn_scr[j * BLOCK:(j + 1) * BLOCK, :] = _weighted_values(st, values_of(j), sink_row).astype(BF16)
        st = st_next


def _project_out(attn_scr, sc_ref, x_ref, mod_ref, w_ref, o_ref):
    mix = _dot(attn_scr[...], w_ref[:ATTN_WIDTH, :]) + _dot(sc_ref[...], w_ref[ATTN_WIDTH:, :])
    o_ref[...] = x_ref[...] + _mod_chunk(mod_ref, 2) * mix


def _attn_ctx_kernel(sink_ref, q_ref, k_ref, v_ref, sc_ref, x_ref, mod_ref, w_ref, o_ref, attn_scr):
    per_seq = SEQ // BLOCK
    kcats = [k_ref[s * SEQ:(s + 1) * SEQ, :].astype(BF16) for s in range(ATTN_CTX_SEQS)]
    vts = [_transposed_values([v_ref[s * SEQ:(s + 1) * SEQ, :]]) for s in range(ATTN_CTX_SEQS)]
    sink_row = sink_ref[...] * LOG2_E
    _attend_blocks(ATTN_CTX_SEQS * per_seq,
                   lambda j: _scores(q_ref[j * BLOCK:(j + 1) * BLOCK, :], kcats[j // per_seq], None),
                   lambda j: vts[j // per_seq], sink_row, attn_scr)
    _project_out(attn_scr, sc_ref, x_ref, mod_ref, w_ref, o_ref)


def _attn_ctx(q, k, v, sc, x, mod_l, mod_base, sink_row, w_out_bf):
    T = q.shape[0]
    tq = ATTN_CTX_SEQS * SEQ
    tok = lambda w: pl.BlockSpec((tq, w), lambda i: (i, 0))
    return pl.pallas_call(
        _attn_ctx_kernel,
        grid=(T // tq,),
        in_specs=[_resident((1, N_HEADS * BLOCK)), tok(ATTN_WIDTH), tok(KV_WIDTH), tok(KV_WIDTH),
                  tok(CONV_WIDTH), tok(D_MODEL), _mod_spec(lambda i: mod_base + CTX_ROW),
                  _resident((ATTN_WIDTH + CONV_WIDTH, D_MODEL))],
        out_specs=tok(D_MODEL),
        out_shape=jax.ShapeDtypeStruct((T, D_MODEL), F32),
        scratch_shapes=[pltpu.VMEM((tq, ATTN_WIDTH), BF16)],
        compiler_params=_params(1),
        name="attn_ctx",
    )(sink_row, q, k, v, sc, x, mod_l, w_out_bf)


def _band_cap(has_prev, has_next):
    c = lax.broadcasted_iota(jnp.int32, (3 * BLOCK, BLOCK), 0)
    r = lax.broadcasted_iota(jnp.int32, (3 * BLOCK, BLOCK), 1)
    first_prev = r + jnp.where(has_prev, 0, BLOCK)
    last_next = r + 2 * BLOCK - jnp.where(has_next, 0, BLOCK)
    masked = ((c < BLOCK) & (c < first_prev)) | ((c >= 2 * BLOCK) & (c > last_next))
    return jnp.where(masked, NEG_INF, SCORE_CAP)


def _attn_win_kernel(sink_ref, q_ref, kp_ref, kc_ref, kn_ref, vp_ref, vc_ref, vn_ref, ck_ref, cv_ref,
                     sc_ref, x_ref, mod_ref, w_ref, o_ref, attn_scr, *, n_steps):
    i = pl.program_id(1)
    kc, vc = kc_ref[...], vc_ref[...]
    inner = range(0, ATTN_Q_BLOCKS * BLOCK, BLOCK)
    k_blocks = [kp_ref[...]] + [kc[j:j + BLOCK] for j in inner] + [kn_ref[...]]
    v_blocks = [vp_ref[...]] + [vc[j:j + BLOCK] for j in inner] + [vn_ref[...]]
    sink_row = sink_ref[...] * LOG2_E

    def scores_of(j):
        kcat = jnp.concatenate(k_blocks[j:j + 3] + [ck_ref[0]], axis=0).astype(BF16)
        cap = _band_cap(i > 0 if j == 0 else True, i < n_steps - 1 if j == ATTN_Q_BLOCKS - 1 else True)
        return _scores(q_ref[j * BLOCK:(j + 1) * BLOCK, :], kcat, cap)

    _attend_blocks(ATTN_Q_BLOCKS, scores_of, lambda j: _transposed_values(v_blocks[j:j + 3] + [cv_ref[0]]),
                   sink_row, attn_scr)
    _project_out(attn_scr, sc_ref, x_ref, mod_ref, w_ref, o_ref)


def _attn_win(q, k, v, ck, cv, sc, x, mod_l, mod_base, sink_row, w_out_bf):
    T = q.shape[0]
    tq = ATTN_Q_BLOCKS * BLOCK
    n_steps = DEC_SEQ // tq
    nb = DEC_SEQ // BLOCK
    cur = lambda w: pl.BlockSpec((tq, w), lambda b, i: (b * n_steps + i, 0))
    prev = pl.BlockSpec((BLOCK, KV_WIDTH), lambda b, i: (b * nb + jnp.maximum(ATTN_Q_BLOCKS * i - 1, 0), 0))
    nxt = pl.BlockSpec((BLOCK, KV_WIDTH),
                       lambda b, i: (b * nb + jnp.minimum(ATTN_Q_BLOCKS * (i + 1), nb - 1), 0))
    ctx = pl.BlockSpec((1, PAST_LEN, KV_WIDTH), lambda b, i: (b, 0, 0))
    return pl.pallas_call(
        functools.partial(_attn_win_kernel, n_steps=n_steps),
        grid=(T // DEC_SEQ, n_steps),
        in_specs=[_resident((1, N_HEADS * BLOCK)), cur(ATTN_WIDTH),
                  prev, cur(KV_WIDTH), nxt, prev, cur(KV_WIDTH), nxt, ctx, ctx,
                  cur(CONV_WIDTH), cur(D_MODEL), _mod_spec(lambda b, i: mod_base + b),
                  _resident((ATTN_WIDTH + CONV_WIDTH, D_MODEL))],
        out_specs=cur(D_MODEL),
        out_shape=jax.ShapeDtypeStruct((T, D_MODEL), F32),
        scratch_shapes=[pltpu.VMEM((tq, ATTN_WIDTH), BF16)],
        compiler_params=_params(2),
        name="attn_win",
    )(sink_row, q, k, k, k, v, v, v, ck, cv, sc, x, mod_l, w_out_bf)


def _gelu_tanh(x):
    k = 0.7978845608028654
    half = 0.5 * x
    return half + half * jnp.tanh(x * (k + (k * 0.044715) * (x * x)))


def _gmlp_kernel(x_ref, mod_ref, g_ref, win_ref, vg_ref, ws_ref, bs_ref, wout_ref, o_ref, gated_scr, h_scr):
    n_chunks = x_ref.shape[0] // CHUNK
    parts = _modulated_parts([x_ref], h_scr, g_ref[...], _mod_chunk(mod_ref, 0), _mod_chunk(mod_ref, 1), 1)
    u = _gelu_tanh(_dot_rows(parts, win_ref[:, :GMLP_WIDTH]))
    v = _gelu_tanh(_dot(h_scr[...], win_ref[:, GMLP_WIDTH:]))
    ms = jnp.mean(v * v, axis=-1, keepdims=True)
    v = (v * lax.rsqrt(ms + EPS) * vg_ref[...]).astype(BF16)
    for grp in range(GMLP_GROUPS):
        lanes = slice(grp * GMLP_GROUP_DIM, (grp + 1) * GMLP_GROUP_DIM)
        rhs = jnp.concatenate([v[n * CHUNK:(n + 1) * CHUNK, lanes] for n in range(n_chunks)], axis=1)
        s = _dot(ws_ref[grp], rhs)
        for n in range(n_chunks):
            rows = slice(n * CHUNK, (n + 1) * CHUNK)
            s_n = s[:, n * GMLP_GROUP_DIM:(n + 1) * GMLP_GROUP_DIM] + bs_ref[grp]
            gated_scr[rows, lanes] = (u[rows, lanes] * s_n).astype(BF16)
    o_ref[...] = x_ref[...] + _mod_chunk(mod_ref, 2) * _dot(gated_scr[...], wout_ref[...])


def _gmlp(x, mod_l, mod_row, g, win_bf, vg, ws_bf, bs_full, wout_bf, tm):
    T = x.shape[0]
    tok = pl.BlockSpec((tm, D_MODEL), lambda i: (i, 0))
    return pl.pallas_call(
        _gmlp_kernel,
        grid=(T // tm,),
        in_specs=[tok, _mod_spec(mod_row),
                  _resident((1, D_MODEL)), _resident((D_MODEL, 2 * GMLP_WIDTH)), _resident((1, GMLP_WIDTH)),
                  _resident((GMLP_GROUPS, CHUNK, CHUNK)), _resident((GMLP_GROUPS, CHUNK, GMLP_GROUP_DIM)),
                  _resident((GMLP_WIDTH, D_MODEL))],
        out_specs=tok,
        out_shape=jax.ShapeDtypeStruct((T, D_MODEL), F32),
        scratch_shapes=[pltpu.VMEM((tm, GMLP_WIDTH), BF16), pltpu.VMEM((tm, D_MODEL), BF16)],
        compiler_params=_params(1),
        name="gmlp",
    )(x, mod_l, g, win_bf, vg, ws_bf, bs_full, wout_bf)


def _ffn_kernel(*refs, tiles_per_seq, seq_len):
    halo = _halo_rows(tiles_per_seq)
    n_x = 3 if halo else 1
    x_refs, refs = refs[:n_x], refs[n_x:]
    mod_ref, g_ref, wup_ref, cw_ref, wdn_ref, o_ref, act_scr, h_scr = refs
    tm = o_ref.shape[0]
    parts = _modulated_parts(x_refs, h_scr, g_ref[...], _mod_chunk(mod_ref, 3), _mod_chunk(mod_ref, 4),
                             tiles_per_seq)
    masks = _seq_end_masks(tm, FFN_COLS, seq_len, tiles_per_seq)
    for j in range(0, D_FF, FFN_COLS):
        gate_cols = slice(j, j + FFN_COLS)
        val_cols = slice(D_FF + j, D_FF + j + FFN_COLS)
        if j == 0:
            up = lambda cols: _dot_rows(parts, wup_ref[0, :, cols])
        else:
            h = h_scr[...]
            up = lambda cols: _dot(h, wup_ref[0, :, cols])
        zg = _token_conv3(up(gate_cols), cw_ref[0, :, gate_cols], tm, halo, masks)
        zv = _token_conv3(up(val_cols), cw_ref[0, :, val_cols], tm, halo, masks)
        act_scr[:, gate_cols] = (jax.nn.silu(zg) * zv).astype(BF16)
    x = x_refs[n_x // 2][...]
    o_ref[...] = x + _mod_chunk(mod_ref, 5) * _dot(act_scr[...], wdn_ref[0])


def _ffn(x, mod_l, mod_row, g, wup_bf, conv_w, wdn_bf, layer, tm, seq_len):
    T = x.shape[0]
    tiles_per_seq = max(seq_len // tm, 1)
    assert tm % seq_len == 0 or seq_len % tm == 0
    x_specs = _token_specs(T, tm, tiles_per_seq)
    in_specs = x_specs + [_mod_spec(mod_row), _resident((1, D_MODEL)), _layer_slab(wup_bf.shape, layer),
                          _layer_slab(conv_w.shape, layer), _layer_slab(wdn_bf.shape, layer)]
    scratch = [pltpu.VMEM((tm, D_FF), BF16), pltpu.VMEM((tm + 2 * _halo_rows(tiles_per_seq), D_MODEL), BF16)]
    return pl.pallas_call(
        functools.partial(_ffn_kernel, tiles_per_seq=tiles_per_seq, seq_len=seq_len),
        grid=(T // tm,),
        in_specs=in_specs,
        out_specs=pl.BlockSpec((tm, D_MODEL), lambda i: (i, 0)),
        out_shape=jax.ShapeDtypeStruct((T, D_MODEL), F32),
        scratch_shapes=scratch,
        compiler_params=_params(1),
        name="conv_ffn",
    )(*([x] * len(x_specs) + [mod_l, g, wup_bf, conv_w, wdn_bf]))


def _rope_tables(n_tokens):
    t = np.arange(n_tokens)
    n_freq = HEAD_DIM // 4
    inv = (ROPE_BASE ** (-np.arange(n_freq, dtype=np.float32) / n_freq)).astype(np.float32)
    row_ang = (t // GRID_W).astype(np.float32)[:, None] * inv
    col_ang = (t % GRID_W).astype(np.float32)[:, None] * inv
    ang = np.concatenate([row_ang, row_ang, col_ang, col_ang], axis=1)
    ang = np.tile(ang, (1, LANES // HEAD_DIM)).astype(np.float64)
    first = (np.arange(LANES) % (2 * n_freq)) < n_freq
    cos, sin = np.cos(ang), np.sin(ang)
    tables = (cos, np.where(first, -sin, 0.0), np.where(first, 0.0, sin))
    return tuple(jnp.asarray(tab.astype(np.float32)) for tab in tables)


def kernel(x_prompt, x_sample, cache_k, cache_v, c, c_ctx, ada_w, ada_b, norm_mix_g, norm_ffn_g, w_in_even,
           q_norm_g, k_norm_g, sink_logit, short_conv_w, w_out_even, w_in_odd, gmlp_norm_g, w_spatial,
           b_spatial, w_out_odd, w_up, ffn_conv_w, w_down):
    n_p, n_s = BATCH * SEQ, DEC_BATCH * DEC_SEQ
    xp = x_prompt.reshape(n_p, D_MODEL)
    xs = x_sample.reshape(n_s, D_MODEL)
    cond = jnp.concatenate([c, c_ctx[None, :], jnp.zeros((COND_ROWS - DEC_BATCH - 1, D_MODEL), F32)], axis=0)
    mod_l = _adaln(cond, ada_w, ada_b)
    rope_tabs = _rope_tables(DEC_SEQ)
    w_in_even_bf, w_out_even_bf = _to_bf16(w_in_even), _to_bf16(w_out_even)
    w_in_odd_bf, w_out_odd_bf = _to_bf16(w_in_odd), _to_bf16(w_out_odd)
    w_spatial_bf = _to_bf16(w_spatial.reshape(N_ODD, GMLP_GROUPS * CHUNK, CHUNK))
    w_up_bf, w_down_bf = _to_bf16(w_up), _to_bf16(w_down)

    tm_p, tm_s = 1024, 1024
    row_vec = lambda a: a.reshape(1, -1)
    new_k, new_v = [], []
    for l in range(DEPTH):
        base = l * COND_ROWS
        row_p = lambda i, base=base: base + CTX_ROW
        row_s = lambda i, base=base: base + i // (DEC_SEQ // tm_s)
        g_mix = row_vec(norm_mix_g[l])
        if l % 2 == 0:
            e = l // 2
            w_in, w_out = w_in_even_bf[e], w_out_even_bf[e]
            qg = row_vec(jnp.tile(q_norm_g[e], N_HEADS))
            kg = row_vec(jnp.tile(k_norm_g[e], N_KV_HEADS))
            sink = jnp.repeat(sink_logit[e], BLOCK).reshape(1, N_HEADS * BLOCK)
            cw = short_conv_w[e]
            qp, kp, vp, scp = _in_even(xp, mod_l, row_p, g_mix, w_in, qg, kg, cw, None, tm_p, SEQ)
            new_k.append(kp.reshape(BATCH, SEQ, N_KV_HEADS, HEAD_DIM))
            new_v.append(vp.reshape(BATCH, SEQ, N_KV_HEADS, HEAD_DIM))
            xp = _attn_ctx(qp, kp, vp, scp, xp, mod_l, base, sink, w_out)
            qs, ks, vs, scs = _in_even(xs, mod_l, row_s, g_mix, w_in, qg, kg, cw, rope_tabs, tm_s, DEC_SEQ)
            ck = cache_k[:, e].reshape(DEC_BATCH, PAST_LEN, KV_WIDTH)
            cv = cache_v[:, e].reshape(DEC_BATCH, PAST_LEN, KV_WIDTH)
            xs = _attn_win(qs, ks, vs, ck, cv, scs, xs, mod_l, base, sink, w_out)
        else:
            o = l // 2
            w_in, w_out = w_in_odd_bf[o], w_out_odd_bf[o]
            vg = row_vec(gmlp_norm_g[o])
            ws = w_spatial_bf[o].reshape(GMLP_GROUPS, CHUNK, CHUNK)
            bs_full = jnp.broadcast_to(b_spatial[o][:, :, None], (GMLP_GROUPS, CHUNK, GMLP_GROUP_DIM))
            xp = _gmlp(xp, mod_l, row_p, g_mix, w_in, vg, ws, bs_full, w_out, tm_p)
            xs = _gmlp(xs, mod_l, row_s, g_mix, w_in, vg, ws, bs_full, w_out, tm_s)
        g_ffn = row_vec(norm_ffn_g[l])
        xp = _ffn(xp, mod_l, row_p, g_ffn, w_up_bf, ffn_conv_w, w_down_bf, l, tm_p, SEQ)
        xs = _ffn(xs, mod_l, row_s, g_ffn, w_up_bf, ffn_conv_w, w_down_bf, l, tm_s, DEC_SEQ)
    per_layer = lambda parts: parts[0][:, None] if len(parts) == 1 else jnp.stack(parts, axis=1)
    return (xp.reshape(BATCH, SEQ, D_MODEL), xs.reshape(DEC_BATCH, DEC_SEQ, D_MODEL),
            per_layer(new_k), per_layer(new_v))
```

```python
import functools

import jax
import jax.numpy as jnp
import numpy as np
from jax import lax
from jax.experimental import pallas as pl
from jax.experimental.pallas import tpu as pltpu

F32 = jnp.float32
BF16 = jnp.bfloat16

D_MODEL = 1024
BATCH = 16
SEQ = 256
DEPTH = 2
DEC_BATCH = 4
DEC_SEQ = 2048
PAST_LEN = 512
GRID_W = 64
N_HEADS = 8
N_KV_HEADS = 2
HEAD_DIM = 64
GQA_GROUP = N_HEADS // N_KV_HEADS
ATTN_WIDTH = N_HEADS * HEAD_DIM
KV_WIDTH = N_KV_HEADS * HEAD_DIM
WINDOW = 128
BLOCK = 128
ROPE_BASE = 10000.0
CONV_WIDTH = 512
CHUNK = 128
GMLP_WIDTH = 1024
GMLP_GROUPS = 8
GMLP_GROUP_DIM = GMLP_WIDTH // GMLP_GROUPS
D_FF = 2816
EPS = 1e-6
NEG_INF = -1e30
N_EVEN = (DEPTH + 1) // 2
N_ODD = DEPTH // 2
LOG2_E = 1.4426950408889634

LANES = 128
SUBLANES_F32 = 8
SUBLANES_BF16 = 16
MXU_WIDTH = 256
VMEM_LIMIT_BYTES = 56 * 1024 * 1024

COND_ROWS = 8
CTX_ROW = DEC_BATCH
TOKEN_HALO = SUBLANES_BF16
FFN_COLS = MXU_WIDTH
LEAD_PARTS = 4
CAST_BLOCK_BYTES = 3 * 1024 * 1024
ADA_TN = 1536
ATTN_Q_BLOCKS = 4
ATTN_CTX_SEQS = 2
SUM_ROWS = SUBLANES_BF16
SCORE_CAP = 3.0e38


def _params(n_axes):
    return pltpu.CompilerParams(dimension_semantics=("parallel",) * n_axes,
                                vmem_limit_bytes=VMEM_LIMIT_BYTES)


def _resident(shape):
    zeros = (0,) * len(shape)
    return pl.BlockSpec(shape, lambda *_: zeros, pipeline_mode=pl.Buffered(1))


def _layer_slab(shape, layer):
    return pl.BlockSpec((1,) + tuple(shape[1:]), lambda *_: (layer, 0, 0), pipeline_mode=pl.Buffered(1))


def _mod_spec(mod_row):
    return pl.BlockSpec((1, 1, 6 * D_MODEL), lambda *idx: (mod_row(*idx), 0, 0))


def _dot(a, b):
    return jnp.dot(a, b, preferred_element_type=F32)


def _cast_kernel(x_ref, o_ref):
    o_ref[...] = x_ref[...].astype(BF16)


def _to_bf16(w):
    n, rows, cols = w.shape
    fits = [r for r in range(SUBLANES_BF16, rows + 1, SUBLANES_BF16)
            if rows % r == 0 and r * cols * 4 <= CAST_BLOCK_BYTES]
    tr = max(fits)
    spec = pl.BlockSpec((1, tr, cols), lambda i, j: (i, j, 0))
    return pl.pallas_call(
        _cast_kernel, grid=(n, rows // tr), in_specs=[spec], out_specs=spec,
        out_shape=jax.ShapeDtypeStruct(w.shape, BF16), compiler_params=_params(2), name="to_bf16",
    )(w)


def _side_cast_specs(side, n_steps, step_of):
    w, layer = side
    _, rows, cols = w.shape
    tr = rows // n_steps
    assert tr * n_steps == rows and tr % SUBLANES_BF16 == 0
    return (pl.BlockSpec((1, tr, cols), lambda *idx: (layer, step_of(*idx), 0)),
            pl.BlockSpec((tr, cols), lambda *idx: (step_of(*idx), 0)),
            jax.ShapeDtypeStruct((rows, cols), BF16))


def _split_refs(rest, n_out, side_cast):
    if not side_cast:
        return rest[:n_out], rest[n_out:]
    side_in, outs, side_out, scratch = rest[0], rest[1:1 + n_out], rest[1 + n_out], rest[2 + n_out:]
    side_out[...] = side_in[0].astype(BF16)
    return outs, scratch


def _modulate(x, g, shift, scale):
    ms = jnp.mean(x * x, axis=-1, keepdims=True)
    return (x * lax.rsqrt(ms + EPS) * g) * (1.0 + scale) + shift


def _mod_chunk(mod_ref, k):
    return mod_ref[0, :, k * D_MODEL:(k + 1) * D_MODEL]


def _halo_rows(tiles_per_seq):
    return TOKEN_HALO if tiles_per_seq > 1 else 0


def _token_specs(T, tm, tiles_per_seq):
    tok = pl.BlockSpec((tm, D_MODEL), lambda i: (i, 0))
    if tiles_per_seq == 1:
        return [tok]
    per_tile = tm // TOKEN_HALO
    last = T // TOKEN_HALO - 1
    prev = pl.BlockSpec((TOKEN_HALO, D_MODEL), lambda i: (jnp.maximum(i * per_tile - 1, 0), 0))
    nxt = pl.BlockSpec((TOKEN_HALO, D_MODEL), lambda i: (jnp.minimum((i + 1) * per_tile, last), 0))
    return [prev, tok, nxt]


def _modulated_parts(x_refs, h_scr, g, shift, scale, tiles_per_seq):
    halo = _halo_rows(tiles_per_seq)
    x_ref = x_refs[len(x_refs) // 2]
    tm = x_ref.shape[0]
    step = tm // LEAD_PARTS
    parts = []
    for p in range(LEAD_PARTS):
        lo, hi = halo + p * step, halo + (p + 1) * step
        piece = _modulate(x_ref[p * step:(p + 1) * step, :], g, shift, scale).astype(BF16)
        if halo and p == 0:
            pos = pl.program_id(0) % tiles_per_seq
            edge = jnp.where(pos > 0, _modulate(x_refs[0][...], g, shift, scale), 0.0).astype(BF16)
            piece, lo = jnp.concatenate([edge, piece], axis=0), 0
        if halo and p == LEAD_PARTS - 1:
            pos = pl.program_id(0) % tiles_per_seq
            edge = jnp.where(pos < tiles_per_seq - 1, _modulate(x_refs[2][...], g, shift, scale), 0.0).astype(BF16)
            piece, hi = jnp.concatenate([piece, edge], axis=0), tm + 2 * halo
        h_scr[lo:hi, :] = piece
        parts.append(piece)
    return parts


def _dot_rows(parts, w):
    return jnp.concatenate([_dot(p, w) for p in parts], axis=0)


def _seq_end_masks(tm, cols, seq_len, tiles_per_seq):
    if tiles_per_seq > 1:
        return None
    seq_row = lax.broadcasted_iota(jnp.int32, (tm, cols), 0) % seq_len
    return seq_row != 0, seq_row != seq_len - 1


def _token_conv3(z, w, tm, halo, masks):
    rows = z.shape[0]
    mid = slice(halo, halo + tm)
    dn = pltpu.roll(z, 1, 0)[mid]
    up = pltpu.roll(z, rows - 1, 0)[mid]
    if masks is not None:
        dn = jnp.where(masks[0], dn, 0.0)
        up = jnp.where(masks[1], up, 0.0)
    return dn * w[0:1] + z[mid] * w[1:2] + up * w[2:3]


def _adaln_kernel(cond_ref, w_ref, b_ref, o_ref):
    a = jax.nn.silu(cond_ref[...]).astype(BF16)
    rows = _dot(a, w_ref[0].astype(BF16)) + b_ref[0]
    for r in range(COND_ROWS):
        o_ref[r] = rows[r:r + 1, :]


def _adaln(cond, ada_w, ada_b):
    n_out = 6 * D_MODEL
    return pl.pallas_call(
        _adaln_kernel,
        grid=(DEPTH, n_out // ADA_TN),
        in_specs=[pl.BlockSpec((COND_ROWS, D_MODEL), lambda l, j: (0, 0)),
                  pl.BlockSpec((1, D_MODEL, ADA_TN), lambda l, j: (l, 0, j)),
                  pl.BlockSpec((1, 1, ADA_TN), lambda l, j: (l, 0, j))],
        out_specs=pl.BlockSpec((COND_ROWS, 1, ADA_TN), lambda l, j: (l, 0, j)),
        out_shape=jax.ShapeDtypeStruct((DEPTH * COND_ROWS, 1, n_out), F32),
        compiler_params=_params(2),
        name="adaln",
    )(cond, ada_w, ada_b.reshape(DEPTH, 1, n_out))


def _head_rms(z, gain):
    n = z.shape[1]
    w = min(n, MXU_WIDTH)
    r = lax.broadcasted_iota(jnp.int32, (w, w), 0) // HEAD_DIM
    c = lax.broadcasted_iota(jnp.int32, (w, w), 1) // HEAD_DIM
    ones = (r == c).astype(BF16)
    sq = (z * z).astype(BF16)
    ss = jnp.concatenate([_dot(sq[:, k:k + w], ones) for k in range(0, n, w)], axis=1)
    return z * lax.rsqrt(ss * (1.0 / HEAD_DIM) + EPS) * gain


def _rope(z, cos, sin_lo, sin_hi):
    outs = []
    for k in range(0, z.shape[1], LANES):
        blk = z[:, k:k + LANES]
        outs.append(blk * cos + pltpu.roll(blk, LANES - 16, 1) * sin_lo + pltpu.roll(blk, 16, 1) * sin_hi)
    return jnp.concatenate(outs, axis=1)


def _in_even_kernel(*refs, rope, tiles_per_seq, seq_len, side_cast):
    halo = _halo_rows(tiles_per_seq)
    n_x = 3 if halo else 1
    x_refs, refs = refs[:n_x], refs[n_x:]
    mod_ref, g_ref, w_ref, qg_ref, kg_ref, cw_ref = refs[:6]
    refs = refs[6:]
    if rope:
        cos_ref, slo_ref, shi_ref = refs[:3]
        refs = refs[3:]
    (q_ref, k_ref, v_ref, sc_ref), (h_scr,) = _split_refs(refs, 4, side_cast)
    tm = q_ref.shape[0]
    parts = _modulated_parts(x_refs, h_scr, g_ref[...], _mod_chunk(mod_ref, 0), _mod_chunk(mod_ref, 1),
                             tiles_per_seq)
    c0 = 0
    c1 = c0 + ATTN_WIDTH
    c2 = c1 + 2 * KV_WIDTH
    c3 = c2 + CONV_WIDTH
    c4 = c3 + CONV_WIDTH
    c5 = c4 + CONV_WIDTH
    zq = _dot_rows(parts, w_ref[:, c0:c1])[halo:halo + tm]
    h_all = h_scr[...]
    h = h_scr[halo:halo + tm, :]
    q = _head_rms(zq, qg_ref[...] * (HEAD_DIM ** -0.5 * LOG2_E))
    zkv = _dot(h, w_ref[:, c1:c2])
    k = _head_rms(zkv[:, :KV_WIDTH], kg_ref[...])
    if rope:
        cos, slo, shi = cos_ref[...], slo_ref[...], shi_ref[...]
        q = _rope(q, cos, slo, shi)
        k = _rope(k, cos, slo, shi)
    q_ref[...] = q
    k_ref[...] = k
    v_ref[...] = zkv[:, KV_WIDTH:]
    ch = _dot(h_all, w_ref[:, c3:c4]) * _dot(h_all, w_ref[:, c4:c5])
    conv = _token_conv3(ch, cw_ref[...], tm, halo, _seq_end_masks(tm, CONV_WIDTH, seq_len, tiles_per_seq))
    sc_ref[...] = (_dot(h, w_ref[:, c2:c3]) * conv).astype(BF16)


def _in_even(x, mod_l, mod_row, g, w_bf, qg, kg, conv_w, rope_tabs, tm, seq_len, side=None):
    T = x.shape[0]
    n_in = w_bf.shape[1]
    tiles_per_seq = max(seq_len // tm, 1)
    assert tm % seq_len == 0 or seq_len % tm == 0
    tok = lambda w: pl.BlockSpec((tm, w), lambda i: (i, 0))
    x_specs = _token_specs(T, tm, tiles_per_seq)
    in_specs = x_specs + [_mod_spec(mod_row), _resident((1, D_MODEL)), _resident((D_MODEL, n_in)),
                          _resident((1, ATTN_WIDTH)), _resident((1, KV_WIDTH)), _resident((3, CONV_WIDTH))]
    args = [x] * len(x_specs) + [mod_l, g, w_bf, qg, kg, conv_w]
    if rope_tabs is not None:
        in_specs += [pl.BlockSpec((tm, LANES), lambda i: (i % tiles_per_seq, 0))] * 3
        args += list(rope_tabs)
    outs = ((ATTN_WIDTH, F32), (KV_WIDTH, F32), (KV_WIDTH, F32), (CONV_WIDTH, BF16))
    out_specs = [tok(w) for w, _ in outs]
    out_shape = [jax.ShapeDtypeStruct((T, w), dt) for w, dt in outs]
    if side is not None:
        side_in, side_out, side_shape = _side_cast_specs(side, T // tm, lambda i: i)
        in_specs, args = in_specs + [side_in], args + [side[0]]
        out_specs, out_shape = out_specs + [side_out], out_shape + [side_shape]
    scratch = [pltpu.VMEM((tm + 2 * _halo_rows(tiles_per_seq), D_MODEL), BF16)]
    return pl.pallas_call(
        functools.partial(_in_even_kernel, rope=rope_tabs is not None, tiles_per_seq=tiles_per_seq,
                          seq_len=seq_len, side_cast=side is not None),
        grid=(T // tm,),
        in_specs=in_specs,
        out_specs=out_specs,
        out_shape=out_shape,
        scratch_shapes=scratch,
        compiler_params=_params(1),
        name="in_even",
    )(*args)


def _stack_heads(q):
    lane_lo = lax.broadcasted_iota(jnp.int32, (BLOCK, LANES), 1) < HEAD_DIM
    heads = []
    for pair in range(N_HEADS // 2):
        kv = (2 * pair) // GQA_GROUP
        qp = q[:, pair * LANES:(pair + 1) * LANES]
        qr = pltpu.roll(qp, HEAD_DIM, 1)
        for half in range(2):
            src = qp if half == kv else qr
            heads.append(jnp.where(lane_lo, src, 0.0) if kv == 0 else jnp.where(lane_lo, 0.0, src))
    return jnp.concatenate(heads, axis=0).astype(BF16)


def _transposed_values(v_blocks):
    vt = jnp.concatenate([v[j:j + BLOCK].T for v in v_blocks for j in range(0, v.shape[0], BLOCK)], axis=1)
    return jnp.concatenate([vt.astype(BF16), jnp.ones((SUM_ROWS, vt.shape[1]), BF16)], axis=0)


def _scores(q, kcat, cap):
    st = lax.dot_general(kcat, _stack_heads(q), (((1,), (1,)), ((), ())), preferred_element_type=F32)
    if cap is None:
        return st
    n_band = cap.shape[0]
    capped = jnp.minimum(st[:n_band], jnp.concatenate([cap] * N_HEADS, axis=1))
    return jnp.concatenate([capped, st[n_band:]], axis=0)


def _weighted_values(st, vt, sink_row):
    m = jnp.maximum(jnp.max(st, axis=0, keepdims=True), sink_row)
    pt = jnp.exp2(st - m).astype(BF16)
    ot = _dot(vt, pt)
    denom = ot[KV_WIDTH:KV_WIDTH + 1] + jnp.exp2(sink_row - m)
    ot = ot[:KV_WIDTH] / denom
    pairs = []
    for pair in range(N_HEADS // 2):
        kv = (2 * pair) // GQA_GROUP
        dims = slice(kv * HEAD_DIM, (kv + 1) * HEAD_DIM)
        both = [ot[dims, (2 * pair + half) * BLOCK:(2 * pair + half + 1) * BLOCK] for half in range(2)]
        pairs.append(jnp.concatenate(both, axis=0).T)
    return jnp.concatenate(pairs, axis=1)


def _attend_blocks(n_blocks, scores_of, values_of, sink_row, attn_scr):
    st = scores_of(0)
    for j in range(n_blocks):
        st_next = scores_of(j + 1) if j + 1 < n_blocks else None
        attn_scr[j * BLOCK:(j + 1) * BLOCK, :] = _weighted_values(st, values_of(j), sink_row).astype(BF16)
        st = st_next


def _project_out(attn_scr, sc_ref, x_ref, mod_ref, w_ref, o_ref):
    mix = _dot(attn_scr[...], w_ref[:ATTN_WIDTH, :]) + _dot(sc_ref[...], w_ref[ATTN_WIDTH:, :])
    o_ref[...] = x_ref[...] + _mod_chunk(mod_ref, 2) * mix


def _attn_ctx_kernel(sink_ref, q_ref, k_ref, v_ref, sc_ref, x_ref, mod_ref, w_ref, o_ref, attn_scr):
    per_seq = SEQ // BLOCK
    kcats = [k_ref[s * SEQ:(s + 1) * SEQ, :].astype(BF16) for s in range(ATTN_CTX_SEQS)]
    vts = [_transposed_values([v_ref[s * SEQ:(s + 1) * SEQ, :]]) for s in range(ATTN_CTX_SEQS)]
    sink_row = sink_ref[...] * LOG2_E
    _attend_blocks(ATTN_CTX_SEQS * per_seq,
                   lambda j: _scores(q_ref[j * BLOCK:(j + 1) * BLOCK, :], kcats[j // per_seq], None),
                   lambda j: vts[j // per_seq], sink_row, attn_scr)
    _project_out(attn_scr, sc_ref, x_ref, mod_ref, w_ref, o_ref)


def _attn_ctx(q, k, v, sc, x, mod_l, mod_base, sink_row, w_out_bf):
    T = q.shape[0]
    tq = ATTN_CTX_SEQS * SEQ
    tok = lambda w: pl.BlockSpec((tq, w), lambda i: (i, 0))
    return pl.pallas_call(
        _attn_ctx_kernel,
        grid=(T // tq,),
        in_specs=[_resident((1, N_HEADS * BLOCK)), tok(ATTN_WIDTH), tok(KV_WIDTH), tok(KV_WIDTH),
                  tok(CONV_WIDTH), tok(D_MODEL), _mod_spec(lambda i: mod_base + CTX_ROW),
                  _resident((ATTN_WIDTH + CONV_WIDTH, D_MODEL))],
        out_specs=tok(D_MODEL),
        out_shape=jax.ShapeDtypeStruct((T, D_MODEL), F32),
        scratch_shapes=[pltpu.VMEM((tq, ATTN_WIDTH), BF16)],
        compiler_params=_params(1),
        name="attn_ctx",
    )(sink_row, q, k, v, sc, x, mod_l, w_out_bf)


def _band_cap(has_prev, has_next):
    c = lax.broadcasted_iota(jnp.int32, (3 * BLOCK, BLOCK), 0)
    r = lax.broadcasted_iota(jnp.int32, (3 * BLOCK, BLOCK), 1)
    first_prev = r + jnp.where(has_prev, 0, BLOCK)
    last_next = r + 2 * BLOCK - jnp.where(has_next, 0, BLOCK)
    masked = ((c < BLOCK) & (c < first_prev)) | ((c >= 2 * BLOCK) & (c > last_next))
    return jnp.where(masked, NEG_INF, SCORE_CAP)


def _attn_win_kernel(sink_ref, q_ref, kp_ref, kc_ref, kn_ref, vp_ref, vc_ref, vn_ref, ck_ref, cv_ref,
                     sc_ref, x_ref, mod_ref, w_ref, *rest, n_steps, side_cast):
    (o_ref,), (attn_scr,) = _split_refs(rest, 1, side_cast)
    i = pl.program_id(1)
    kc, vc = kc_ref[...], vc_ref[...]
    inner = range(0, ATTN_Q_BLOCKS * BLOCK, BLOCK)
    k_blocks = [kp_ref[...]] + [kc[j:j + BLOCK] for j in inner] + [kn_ref[...]]
    v_blocks = [vp_ref[...]] + [vc[j:j + BLOCK] for j in inner] + [vn_ref[...]]
    sink_row = sink_ref[...] * LOG2_E

    def scores_of(j):
        kcat = jnp.concatenate(k_blocks[j:j + 3] + [ck_ref[0]], axis=0).astype(BF16)
        cap = _band_cap(i > 0 if j == 0 else True, i < n_steps - 1 if j == ATTN_Q_BLOCKS - 1 else True)
        return _scores(q_ref[j * BLOCK:(j + 1) * BLOCK, :], kcat, cap)

    _attend_blocks(ATTN_Q_BLOCKS, scores_of, lambda j: _transposed_values(v_blocks[j:j + 3] + [cv_ref[0]]),
                   sink_row, attn_scr)
    _project_out(attn_scr, sc_ref, x_ref, mod_ref, w_ref, o_ref)


def _attn_win(q, k, v, ck, cv, sc, x, mod_l, mod_base, sink_row, w_out_bf, side=None):
    T = q.shape[0]
    tq = ATTN_Q_BLOCKS * BLOCK
    n_steps = DEC_SEQ // tq
    nb = DEC_SEQ // BLOCK
    cur = lambda w: pl.BlockSpec((tq, w), lambda b, i: (b * n_steps + i, 0))
    prev = pl.BlockSpec((BLOCK, KV_WIDTH), lambda b, i: (b * nb + jnp.maximum(ATTN_Q_BLOCKS * i - 1, 0), 0))
    nxt = pl.BlockSpec((BLOCK, KV_WIDTH),
                       lambda b, i: (b * nb + jnp.minimum(ATTN_Q_BLOCKS * (i + 1), nb - 1), 0))
    ctx = pl.BlockSpec((1, PAST_LEN, KV_WIDTH), lambda b, i: (b, 0, 0))
    in_specs = [_resident((1, N_HEADS * BLOCK)), cur(ATTN_WIDTH),
                prev, cur(KV_WIDTH), nxt, prev, cur(KV_WIDTH), nxt, ctx, ctx,
                cur(CONV_WIDTH), cur(D_MODEL), _mod_spec(lambda b, i: mod_base + b),
                _resident((ATTN_WIDTH + CONV_WIDTH, D_MODEL))]
    args = [sink_row, q, k, k, k, v, v, v, ck, cv, sc, x, mod_l, w_out_bf]
    out_specs, out_shape = [cur(D_MODEL)], [jax.ShapeDtypeStruct((T, D_MODEL), F32)]
    if side is not None:
        side_in, side_out, side_shape = _side_cast_specs(side, (T // DEC_SEQ) * n_steps, lambda b, i: b * n_steps + i)
        in_specs, args = in_specs + [side_in], args + [side[0]]
        out_specs, out_shape = out_specs + [side_out], out_shape + [side_shape]
    return pl.pallas_call(
        functools.partial(_attn_win_kernel, n_steps=n_steps, side_cast=side is not None),
        grid=(T // DEC_SEQ, n_steps),
        in_specs=in_specs,
        out_specs=out_specs,
        out_shape=out_shape,
        scratch_shapes=[pltpu.VMEM((tq, ATTN_WIDTH), BF16)],
        compiler_params=_params(2),
        name="attn_win",
    )(*args)


def _gelu_tanh(x):
    k = 0.7978845608028654
    half = 0.5 * x
    return half + half * jnp.tanh(x * (k + (k * 0.044715) * (x * x)))


def _gmlp_kernel(x_ref, mod_ref, g_ref, win_ref, vg_ref, ws_ref, bs_ref, wout_ref, *rest, side_cast):
    (o_ref,), (gated_scr, h_scr) = _split_refs(rest, 1, side_cast)
    n_chunks = x_ref.shape[0] // CHUNK
    parts = _modulated_parts([x_ref], h_scr, g_ref[...], _mod_chunk(mod_ref, 0), _mod_chunk(mod_ref, 1), 1)
    u = _gelu_tanh(_dot_rows(parts, win_ref[:, :GMLP_WIDTH]))
    v = _gelu_tanh(_dot(h_scr[...], win_ref[:, GMLP_WIDTH:]))
    ms = jnp.mean(v * v, axis=-1, keepdims=True)
    v = (v * lax.rsqrt(ms + EPS) * vg_ref[...]).astype(BF16)
    for grp in range(GMLP_GROUPS):
        lanes = slice(grp * GMLP_GROUP_DIM, (grp + 1) * GMLP_GROUP_DIM)
        rhs = jnp.concatenate([v[n * CHUNK:(n + 1) * CHUNK, lanes] for n in range(n_chunks)], axis=1)
        s = _dot(ws_ref[grp], rhs)
        for n in range(n_chunks):
            rows = slice(n * CHUNK, (n + 1) * CHUNK)
            s_n = s[:, n * GMLP_GROUP_DIM:(n + 1) * GMLP_GROUP_DIM] + bs_ref[grp]
            gated_scr[rows, lanes] = (u[rows, lanes] * s_n).astype(BF16)
    o_ref[...] = x_ref[...] + _mod_chunk(mod_ref, 2) * _dot(gated_scr[...], wout_ref[...])


def _gmlp(x, mod_l, mod_row, g, win_bf, vg, ws_bf, bs_full, wout_bf, tm, side=None):
    T = x.shape[0]
    tok = pl.BlockSpec((tm, D_MODEL), lambda i: (i, 0))
    in_specs = [tok, _mod_spec(mod_row),
                _resident((1, D_MODEL)), _resident((D_MODEL, 2 * GMLP_WIDTH)), _resident((1, GMLP_WIDTH)),
                _resident((GMLP_GROUPS, CHUNK, CHUNK)), _resident((GMLP_GROUPS, CHUNK, GMLP_GROUP_DIM)),
                _resident((GMLP_WIDTH, D_MODEL))]
    args = [x, mod_l, g, win_bf, vg, ws_bf, bs_full, wout_bf]
    out_specs, out_shape = [tok], [jax.ShapeDtypeStruct((T, D_MODEL), F32)]
    if side is not None:
        side_in, side_out, side_shape = _side_cast_specs(side, T // tm, lambda i: i)
        in_specs, args = in_specs + [side_in], args + [side[0]]
        out_specs, out_shape = out_specs + [side_out], out_shape + [side_shape]
    return pl.pallas_call(
        functools.partial(_gmlp_kernel, side_cast=side is not None),
        grid=(T // tm,),
        in_specs=in_specs,
        out_specs=out_specs,
        out_shape=out_shape,
        scratch_shapes=[pltpu.VMEM((tm, GMLP_WIDTH), BF16), pltpu.VMEM((tm, D_MODEL), BF16)],
        compiler_params=_params(1),
        name="gmlp",
    )(*args)


def _ffn_kernel(*refs, tiles_per_seq, seq_len):
    halo = _halo_rows(tiles_per_seq)
    n_x = 3 if halo else 1
    x_refs, refs = refs[:n_x], refs[n_x:]
    mod_ref, g_ref, wup_ref, cw_ref, wdn_ref, o_ref, act_scr, h_scr = refs
    tm = o_ref.shape[0]
    parts = _modulated_parts(x_refs, h_scr, g_ref[...], _mod_chunk(mod_ref, 3), _mod_chunk(mod_ref, 4),
                             tiles_per_seq)
    masks = _seq_end_masks(tm, FFN_COLS, seq_len, tiles_per_seq)
    for j in range(0, D_FF, FFN_COLS):
        gate_cols = slice(j, j + FFN_COLS)
        val_cols = slice(D_FF + j, D_FF + j + FFN_COLS)
        if j == 0:
            up = lambda cols: _dot_rows(parts, wup_ref[:, cols])
        else:
            h = h_scr[...]
            up = lambda cols: _dot(h, wup_ref[:, cols])
        zg = _token_conv3(up(gate_cols), cw_ref[0, :, gate_cols], tm, halo, masks)
        zv = _token_conv3(up(val_cols), cw_ref[0, :, val_cols], tm, halo, masks)
        act_scr[:, gate_cols] = (jax.nn.silu(zg) * zv).astype(BF16)
    x = x_refs[n_x // 2][...]
    o_ref[...] = x + _mod_chunk(mod_ref, 5) * _dot(act_scr[...], wdn_ref[...])


def _ffn(x, mod_l, mod_row, g, wup_bf, conv_w, wdn_bf, layer, tm, seq_len):
    T = x.shape[0]
    tiles_per_seq = max(seq_len // tm, 1)
    assert tm % seq_len == 0 or seq_len % tm == 0
    x_specs = _token_specs(T, tm, tiles_per_seq)
    in_specs = x_specs + [_mod_spec(mod_row), _resident((1, D_MODEL)), _resident(wup_bf.shape),
                          _layer_slab(conv_w.shape, layer), _resident(wdn_bf.shape)]
    scratch = [pltpu.VMEM((tm, D_FF), BF16), pltpu.VMEM((tm + 2 * _halo_rows(tiles_per_seq), D_MODEL), BF16)]
    return pl.pallas_call(
        functools.partial(_ffn_kernel, tiles_per_seq=tiles_per_seq, seq_len=seq_len),
        grid=(T // tm,),
        in_specs=in_specs,
        out_specs=pl.BlockSpec((tm, D_MODEL), lambda i: (i, 0)),
        out_shape=jax.ShapeDtypeStruct((T, D_MODEL), F32),
        scratch_shapes=scratch,
        compiler_params=_params(1),
        name="conv_ffn",
    )(*([x] * len(x_specs) + [mod_l, g, wup_bf, conv_w, wdn_bf]))


def _rope_tables(n_tokens):
    t = np.arange(n_tokens)
    n_freq = HEAD_DIM // 4
    inv = (ROPE_BASE ** (-np.arange(n_freq, dtype=np.float32) / n_freq)).astype(np.float32)
    row_ang = (t // GRID_W).astype(np.float32)[:, None] * inv
    col_ang = (t % GRID_W).astype(np.float32)[:, None] * inv
    ang = np.concatenate([row_ang, row_ang, col_ang, col_ang], axis=1)
    ang = np.tile(ang, (1, LANES // HEAD_DIM)).astype(np.float64)
    first = (np.arange(LANES) % (2 * n_freq)) < n_freq
    cos, sin = np.cos(ang), np.sin(ang)
    tables = (cos, np.where(first, -sin, 0.0), np.where(first, 0.0, sin))
    return tuple(jnp.asarray(tab.astype(np.float32)) for tab in tables)


def kernel(x_prompt, x_sample, cache_k, cache_v, c, c_ctx, ada_w, ada_b, norm_mix_g, norm_ffn_g, w_in_even,
           q_norm_g, k_norm_g, sink_logit, short_conv_w, w_out_even, w_in_odd, gmlp_norm_g, w_spatial,
           b_spatial, w_out_odd, w_up, ffn_conv_w, w_down):
    n_p, n_s = BATCH * SEQ, DEC_BATCH * DEC_SEQ
    xp = x_prompt.reshape(n_p, D_MODEL)
    xs = x_sample.reshape(n_s, D_MODEL)
    cond = jnp.concatenate([c, c_ctx[None, :], jnp.zeros((COND_ROWS - DEC_BATCH - 1, D_MODEL), F32)], axis=0)
    mod_l = _adaln(cond, ada_w, ada_b)
    rope_tabs = _rope_tables(DEC_SEQ)
    w_in_even_bf, w_out_even_bf = _to_bf16(w_in_even), _to_bf16(w_out_even)
    w_in_odd_bf, w_out_odd_bf = _to_bf16(w_in_odd), _to_bf16(w_out_odd)
    w_spatial_bf = _to_bf16(w_spatial.reshape(N_ODD, GMLP_GROUPS * CHUNK, CHUNK))
    w_up_bf, w_down_bf = [None] * DEPTH, [None] * DEPTH

    tm_p, tm_s = 1024, 1024
    row_vec = lambda a: a.reshape(1, -1)
    new_k, new_v = [], []
    for l in range(DEPTH):
        base = l * COND_ROWS
        row_p = lambda i, base=base: base + CTX_ROW
        row_s = lambda i, base=base: base + i // (DEC_SEQ // tm_s)
        g_mix = row_vec(norm_mix_g[l])
        if l % 2 == 0:
            e = l // 2
            w_in, w_out = w_in_even_bf[e], w_out_even_bf[e]
            qg = row_vec(jnp.tile(q_norm_g[e], N_HEADS))
            kg = row_vec(jnp.tile(k_norm_g[e], N_KV_HEADS))
            sink = jnp.repeat(sink_logit[e], BLOCK).reshape(1, N_HEADS * BLOCK)
            cw = short_conv_w[e]
            qp, kp, vp, scp = _in_even(xp, mod_l, row_p, g_mix, w_in, qg, kg, cw, None, tm_p, SEQ)
            new_k.append(kp.reshape(BATCH, SEQ, N_KV_HEADS, HEAD_DIM))
            new_v.append(vp.reshape(BATCH, SEQ, N_KV_HEADS, HEAD_DIM))
            xp = _attn_ctx(qp, kp, vp, scp, xp, mod_l, base, sink, w_out)
            qs, ks, vs, scs, w_down_bf[l] = _in_even(xs, mod_l, row_s, g_mix, w_in, qg, kg, cw, rope_tabs, tm_s,
                                                     DEC_SEQ, side=(w_down, l))
            ck = cache_k[:, e].reshape(DEC_BATCH, PAST_LEN, KV_WIDTH)
            cv = cache_v[:, e].reshape(DEC_BATCH, PAST_LEN, KV_WIDTH)
            xs, w_up_bf[l] = _attn_win(qs, ks, vs, ck, cv, scs, xs, mod_l, base, sink, w_out, side=(w_up, l))
        else:
            o = l // 2
            w_in, w_out = w_in_odd_bf[o], w_out_odd_bf[o]
            vg = row_vec(gmlp_norm_g[o])
            ws = w_spatial_bf[o].reshape(GMLP_GROUPS, CHUNK, CHUNK)
            bs_full = jnp.broadcast_to(b_spatial[o][:, :, None], (GMLP_GROUPS, CHUNK, GMLP_GROUP_DIM))
            xp, w_down_bf[l] = _gmlp(xp, mod_l, row_p, g_mix, w_in, vg, ws, bs_full, w_out, tm_p, side=(w_down, l))
            xs, w_up_bf[l] = _gmlp(xs, mod_l, row_s, g_mix, w_in, vg, ws, bs_full, w_out, tm_s, side=(w_up, l))
        g_ffn = row_vec(norm_ffn_g[l])
        xp = _ffn(xp, mod_l, row_p, g_ffn, w_up_bf[l], ffn_conv_w, w_down_bf[l], l, tm_p, SEQ)
        xs = _ffn(xs, mod_l, row_s, g_ffn, w_up_bf[l], ffn_conv_w, w_down_bf[l], l, tm_s, DEC_SEQ)
    per_layer = lambda parts: parts[0][:, None] if len(parts) == 1 else jnp.stack(parts, axis=1)
    return (xp.reshape(BATCH, SEQ, D_MODEL), xs.reshape(DEC_BATCH, DEC_SEQ, D_MODEL),
            per_layer(new_k), per_layer(new_v))
```

```python
import functools

import jax
import jax.numpy as jnp
import numpy as np
from jax import lax
from jax.experimental import pallas as pl
from jax.experimental.pallas import tpu as pltpu

F32 = jnp.float32
BF16 = jnp.bfloat16

D_MODEL = 1024
BATCH = 16
SEQ = 256
DEPTH = 2
DEC_BATCH = 4
DEC_SEQ = 2048
PAST_LEN = 512
GRID_W = 64
N_HEADS = 8
N_KV_HEADS = 2
HEAD_DIM = 64
GQA_GROUP = N_HEADS // N_KV_HEADS
ATTN_WIDTH = N_HEADS * HEAD_DIM
KV_WIDTH = N_KV_HEADS * HEAD_DIM
WINDOW = 128
BLOCK = 128
ROPE_BASE = 10000.0
CONV_WIDTH = 512
CHUNK = 128
GMLP_WIDTH = 1024
GMLP_GROUPS = 8
GMLP_GROUP_DIM = GMLP_WIDTH // GMLP_GROUPS
D_FF = 2816
EPS = 1e-6
NEG_INF = -1e30
N_EVEN = (DEPTH + 1) // 2
N_ODD = DEPTH // 2
LOG2_E = 1.4426950408889634

LANES = 128
SUBLANES_F32 = 8
SUBLANES_BF16 = 16
MXU_WIDTH = 256
VMEM_LIMIT_BYTES = 56 * 1024 * 1024

COND_ROWS = 8
CTX_ROW = DEC_BATCH
TOKEN_HALO = SUBLANES_BF16
FFN_COLS = MXU_WIDTH
LEAD_PARTS = 4
CAST_BLOCK_BYTES = 3 * 1024 * 1024
ADA_TN = 1536
ATTN_Q_BLOCKS = 4
ATTN_CTX_SEQS = 2
SUM_ROWS = SUBLANES_BF16
SCORE_CAP = 3.0e38


def _params(n_axes):
    return pltpu.CompilerParams(dimension_semantics=("parallel",) * n_axes,
                                vmem_limit_bytes=VMEM_LIMIT_BYTES)


def _resident(shape):
    zeros = (0,) * len(shape)
    return pl.BlockSpec(shape, lambda *_: zeros, pipeline_mode=pl.Buffered(1))


def _layer_slab(shape, layer):
    return pl.BlockSpec((1,) + tuple(shape[1:]), lambda *_: (layer, 0, 0), pipeline_mode=pl.Buffered(1))


def _mod_spec(mod_row):
    return pl.BlockSpec((1, 1, 6 * D_MODEL), lambda *idx: (mod_row(*idx), 0, 0))


def _dot(a, b):
    return jnp.dot(a, b, preferred_element_type=F32)


def _cast_kernel(x_ref, o_ref):
    o_ref[...] = x_ref[...].astype(BF16)


def _to_bf16(w):
    n, rows, cols = w.shape
    fits = [r for r in range(SUBLANES_BF16, rows + 1, SUBLANES_BF16)
            if rows % r == 0 and r * cols * 4 <= CAST_BLOCK_BYTES]
    tr = max(fits)
    spec = pl.BlockSpec((1, tr, cols), lambda i, j: (i, j, 0))
    return pl.pallas_call(
        _cast_kernel, grid=(n, rows // tr), in_specs=[spec], out_specs=spec,
        out_shape=jax.ShapeDtypeStruct(w.shape, BF16), compiler_params=_params(2), name="to_bf16",
    )(w)


def _side_cast_specs(side, n_steps, step_of):
    w, layer = side
    _, rows, cols = w.shape
    tr = rows // n_steps
    assert tr * n_steps == rows and tr % SUBLANES_BF16 == 0
    return (pl.BlockSpec((1, tr, cols), lambda *idx: (layer, step_of(*idx), 0)),
            pl.BlockSpec((tr, cols), lambda *idx: (step_of(*idx), 0)),
            jax.ShapeDtypeStruct((rows, cols), BF16))


def _split_refs(rest, n_out, side_cast):
    if not side_cast:
        return rest[:n_out], rest[n_out:]
    side_in, outs, side_out, scratch = rest[0], rest[1:1 + n_out], rest[1 + n_out], rest[2 + n_out:]
    side_out[...] = side_in[0].astype(BF16)
    return outs, scratch


def _host_call(kernel, *, grid, in_specs, args, out_specs, out_shape, scratch_shapes, name, side, step_of):
    in_specs, args, out_specs, out_shape = list(in_specs), list(args), list(out_specs), list(out_shape)
    if side is not None:
        n_steps = 1
        for extent in grid:
            n_steps *= extent
        side_in, side_out, side_shape = _side_cast_specs(side, n_steps, step_of)
        in_specs.append(side_in)
        args.append(side[0])
        out_specs.append(side_out)
        out_shape.append(side_shape)
    return pl.pallas_call(
        functools.partial(kernel, side_cast=side is not None),
        grid=grid, in_specs=in_specs, out_specs=out_specs, out_shape=out_shape, scratch_shapes=scratch_shapes,
        compiler_params=_params(len(grid)), name=name,
    )(*args)


def _modulate(x, g, shift, scale):
    ms = jnp.mean(x * x, axis=-1, keepdims=True)
    return (x * lax.rsqrt(ms + EPS) * g) * (1.0 + scale) + shift


def _mod_chunk(mod_ref, k):
    return mod_ref[0, :, k * D_MODEL:(k + 1) * D_MODEL]


def _halo_rows(tiles_per_seq):
    return TOKEN_HALO if tiles_per_seq > 1 else 0


def _token_specs(T, tm, tiles_per_seq):
    tok = pl.BlockSpec((tm, D_MODEL), lambda i: (i, 0))
    if tiles_per_seq == 1:
        return [tok]
    per_tile = tm // TOKEN_HALO
    last = T // TOKEN_HALO - 1
    prev = pl.BlockSpec((TOKEN_HALO, D_MODEL), lambda i: (jnp.maximum(i * per_tile - 1, 0), 0))
    nxt = pl.BlockSpec((TOKEN_HALO, D_MODEL), lambda i: (jnp.minimum((i + 1) * per_tile, last), 0))
    return [prev, tok, nxt]


def _modulated_parts(x_refs, h_scr, g, shift, scale, tiles_per_seq):
    halo = _halo_rows(tiles_per_seq)
    x_ref = x_refs[len(x_refs) // 2]
    tm = x_ref.shape[0]
    step = tm // LEAD_PARTS
    parts = []
    for p in range(LEAD_PARTS):
        lo, hi = halo + p * step, halo + (p + 1) * step
        piece = _modulate(x_ref[p * step:(p + 1) * step, :], g, shift, scale).astype(BF16)
        if halo and p == 0:
            pos = pl.program_id(0) % tiles_per_seq
            edge = jnp.where(pos > 0, _modulate(x_refs[0][...], g, shift, scale), 0.0).astype(BF16)
            piece, lo = jnp.concatenate([edge, piece], axis=0), 0
        if halo and p == LEAD_PARTS - 1:
            pos = pl.program_id(0) % tiles_per_seq
            edge = jnp.where(pos < tiles_per_seq - 1, _modulate(x_refs[2][...], g, shift, scale), 0.0).astype(BF16)
            piece, hi = jnp.concatenate([piece, edge], axis=0), tm + 2 * halo
        h_scr[lo:hi, :] = piece
        parts.append(piece)
    return parts


def _dot_rows(parts, w):
    return jnp.concatenate([_dot(p, w) for p in parts], axis=0)


def _seq_end_masks(tm, cols, seq_len, tiles_per_seq):
    if tiles_per_seq > 1:
        return None
    seq_row = lax.broadcasted_iota(jnp.int32, (tm, cols), 0) % seq_len
    return seq_row != 0, seq_row != seq_len - 1


def _token_conv3(z, w, tm, halo, masks):
    rows = z.shape[0]
    mid = slice(halo, halo + tm)
    dn = pltpu.roll(z, 1, 0)[mid]
    up = pltpu.roll(z, rows - 1, 0)[mid]
    if masks is not None:
        dn = jnp.where(masks[0], dn, 0.0)
        up = jnp.where(masks[1], up, 0.0)
    return dn * w[0:1] + z[mid] * w[1:2] + up * w[2:3]


def _adaln_kernel(cond_ref, w_ref, b_ref, *rest, side_cast):
    (o_ref,), _ = _split_refs(rest, 1, side_cast)
    a = jax.nn.silu(cond_ref[...]).astype(BF16)
    rows = _dot(a, w_ref[0].astype(BF16)) + b_ref[0]
    for r in range(COND_ROWS):
        o_ref[r] = rows[r:r + 1, :]


def _adaln(cond, ada_w, ada_b, side=None):
    n_out = 6 * D_MODEL
    n_col = n_out // ADA_TN
    return _host_call(
        _adaln_kernel,
        grid=(DEPTH, n_col),
        in_specs=[pl.BlockSpec((COND_ROWS, D_MODEL), lambda l, j: (0, 0)),
                  pl.BlockSpec((1, D_MODEL, ADA_TN), lambda l, j: (l, 0, j)),
                  pl.BlockSpec((1, 1, ADA_TN), lambda l, j: (l, 0, j))],
        args=[cond, ada_w, ada_b.reshape(DEPTH, 1, n_out)],
        out_specs=[pl.BlockSpec((COND_ROWS, 1, ADA_TN), lambda l, j: (l, 0, j))],
        out_shape=[jax.ShapeDtypeStruct((DEPTH * COND_ROWS, 1, n_out), F32)],
        scratch_shapes=[], name="adaln", side=side, step_of=lambda l, j: l * n_col + j)


def _head_rms(z, gain):
    n = z.shape[1]
    w = min(n, MXU_WIDTH)
    r = lax.broadcasted_iota(jnp.int32, (w, w), 0) // HEAD_DIM
    c = lax.broadcasted_iota(jnp.int32, (w, w), 1) // HEAD_DIM
    ones = (r == c).astype(BF16)
    sq = (z * z).astype(BF16)
    ss = jnp.concatenate([_dot(sq[:, k:k + w], ones) for k in range(0, n, w)], axis=1)
    return z * lax.rsqrt(ss * (1.0 / HEAD_DIM) + EPS) * gain


def _rope(z, cos, sin_lo, sin_hi):
    outs = []
    for k in range(0, z.shape[1], LANES):
        blk = z[:, k:k + LANES]
        outs.append(blk * cos + pltpu.roll(blk, LANES - 16, 1) * sin_lo + pltpu.roll(blk, 16, 1) * sin_hi)
    return jnp.concatenate(outs, axis=1)


def _in_even_kernel(*refs, rope, tiles_per_seq, seq_len, side_cast):
    halo = _halo_rows(tiles_per_seq)
    n_x = 3 if halo else 1
    x_refs, refs = refs[:n_x], refs[n_x:]
    mod_ref, g_ref, w_ref, qg_ref, kg_ref, cw_ref = refs[:6]
    refs = refs[6:]
    if rope:
        cos_ref, slo_ref, shi_ref = refs[:3]
        refs = refs[3:]
    (q_ref, k_ref, v_ref, sc_ref), (h_scr,) = _split_refs(refs, 4, side_cast)
    tm = q_ref.shape[0]
    parts = _modulated_parts(x_refs, h_scr, g_ref[...], _mod_chunk(mod_ref, 0), _mod_chunk(mod_ref, 1),
                             tiles_per_seq)
    c0 = 0
    c1 = c0 + ATTN_WIDTH
    c2 = c1 + 2 * KV_WIDTH
    c3 = c2 + CONV_WIDTH
    c4 = c3 + CONV_WIDTH
    c5 = c4 + CONV_WIDTH
    zq = _dot_rows(parts, w_ref[:, c0:c1])[halo:halo + tm]
    h_all = h_scr[...]
    h = h_scr[halo:halo + tm, :]
    q = _head_rms(zq, qg_ref[...] * (HEAD_DIM ** -0.5 * LOG2_E))
    zkv = _dot(h, w_ref[:, c1:c2])
    k = _head_rms(zkv[:, :KV_WIDTH], kg_ref[...])
    if rope:
        cos, slo, shi = cos_ref[...], slo_ref[...], shi_ref[...]
        q = _rope(q, cos, slo, shi)
        k = _rope(k, cos, slo, shi)
    q_ref[...] = q
    k_ref[...] = k
    v_ref[...] = zkv[:, KV_WIDTH:]
    ch = _dot(h_all, w_ref[:, c3:c4]) * _dot(h_all, w_ref[:, c4:c5])
    conv = _token_conv3(ch, cw_ref[...], tm, halo, _seq_end_masks(tm, CONV_WIDTH, seq_len, tiles_per_seq))
    sc_ref[...] = (_dot(h, w_ref[:, c2:c3]) * conv).astype(BF16)


def _in_even(x, mod_l, mod_row, g, w_bf, qg, kg, conv_w, rope_tabs, tm, seq_len, side=None):
    T = x.shape[0]
    n_in = w_bf.shape[1]
    tiles_per_seq = max(seq_len // tm, 1)
    assert tm % seq_len == 0 or seq_len % tm == 0
    tok = lambda w: pl.BlockSpec((tm, w), lambda i: (i, 0))
    x_specs = _token_specs(T, tm, tiles_per_seq)
    in_specs = x_specs + [_mod_spec(mod_row), _resident((1, D_MODEL)), _resident((D_MODEL, n_in)),
                          _resident((1, ATTN_WIDTH)), _resident((1, KV_WIDTH)), _resident((3, CONV_WIDTH))]
    args = [x] * len(x_specs) + [mod_l, g, w_bf, qg, kg, conv_w]
    if rope_tabs is not None:
        in_specs += [pl.BlockSpec((tm, LANES), lambda i: (i % tiles_per_seq, 0))] * 3
        args += list(rope_tabs)
    outs = ((ATTN_WIDTH, F32), (KV_WIDTH, F32), (KV_WIDTH, F32), (CONV_WIDTH, BF16))
    scratch = [pltpu.VMEM((tm + 2 * _halo_rows(tiles_per_seq), D_MODEL), BF16)]
    return _host_call(
        functools.partial(_in_even_kernel, rope=rope_tabs is not None, tiles_per_seq=tiles_per_seq, seq_len=seq_len),
        grid=(T // tm,), in_specs=in_specs, args=args,
        out_specs=[tok(w) for w, _ in outs], out_shape=[jax.ShapeDtypeStruct((T, w), dt) for w, dt in outs],
        scratch_shapes=scratch, name="in_even", side=side, step_of=lambda i: i)


def _stack_heads(q):
    lane_lo = lax.broadcasted_iota(jnp.int32, (BLOCK, LANES), 1) < HEAD_DIM
    heads = []
    for pair in range(N_HEADS // 2):
        kv = (2 * pair) // GQA_GROUP
        qp = q[:, pair * LANES:(pair + 1) * LANES]
        qr = pltpu.roll(qp, HEAD_DIM, 1)
        for half in range(2):
            src = qp if half == kv else qr
            heads.append(jnp.where(lane_lo, src, 0.0) if kv == 0 else jnp.where(lane_lo, 0.0, src))
    return jnp.concatenate(heads, axis=0).astype(BF16)


def _transposed_values(v_blocks):
    vt = jnp.concatenate([v[j:j + BLOCK].T for v in v_blocks for j in range(0, v.shape[0], BLOCK)], axis=1)
    return jnp.concatenate([vt.astype(BF16), jnp.ones((SUM_ROWS, vt.shape[1]), BF16)], axis=0)


def _scores(q, kcat, cap):
    st = lax.dot_general(kcat, _stack_heads(q), (((1,), (1,)), ((), ())), preferred_element_type=F32)
    if cap is None:
        return st
    n_band = cap.shape[0]
    capped = jnp.minimum(st[:n_band], jnp.concatenate([cap] * N_HEADS, axis=1))
    return jnp.concatenate([capped, st[n_band:]], axis=0)


def _weighted_values(st, vt, sink_row):
    m = jnp.maximum(jnp.max(st, axis=0, keepdims=True), sink_row)
    pt = jnp.exp2(st - m).astype(BF16)
    ot = _dot(vt, pt)
    denom = ot[KV_WIDTH:KV_WIDTH + 1] + jnp.exp2(sink_row - m)
    ot = ot[:KV_WIDTH] / denom
    pairs = []
    for pair in range(N_HEADS // 2):
        kv = (2 * pair) // GQA_GROUP
        dims = slice(kv * HEAD_DIM, (kv + 1) * HEAD_DIM)
        both = [ot[dims, (2 * pair + half) * BLOCK:(2 * pair + half + 1) * BLOCK] for half in range(2)]
        pairs.append(jnp.concatenate(both, axis=0).T)
    return jnp.concatenate(pairs, axis=1)


def _attend_blocks(n_blocks, scores_of, values_of, sink_row, attn_scr):
    st = scores_of(0)
    for j in range(n_blocks):
        st_next = scores_of(j + 1) if j + 1 < n_blocks else None
        attn_scr[j * BLOCK:(j + 1) * BLOCK, :] = _weighted_values(st, values_of(j), sink_row).astype(BF16)
        st = st_next


def _project_out(attn_scr, sc_ref, x_ref, mod_ref, w_ref, o_ref):
    mix = _dot(attn_scr[...], w_ref[:ATTN_WIDTH, :]) + _dot(sc_ref[...], w_ref[ATTN_WIDTH:, :])
    o_ref[...] = x_ref[...] + _mod_chunk(mod_ref, 2) * mix


def _attn_ctx_kernel(sink_ref, q_ref, k_ref, v_ref, sc_ref, x_ref, mod_ref, w_ref, *rest, side_cast):
    (o_ref,), (attn_scr,) = _split_refs(rest, 1, side_cast)
    per_seq = SEQ // BLOCK
    kcats = [k_ref[s * SEQ:(s + 1) * SEQ, :].astype(BF16) for s in range(ATTN_CTX_SEQS)]
    vts = [_transposed_values([v_ref[s * SEQ:(s + 1) * SEQ, :]]) for s in range(ATTN_CTX_SEQS)]
    sink_row = sink_ref[...] * LOG2_E
    _attend_blocks(ATTN_CTX_SEQS * per_seq,
                   lambda j: _scores(q_ref[j * BLOCK:(j + 1) * BLOCK, :], kcats[j // per_seq], None),
                   lambda j: vts[j // per_seq], sink_row, attn_scr)
    _project_out(attn_scr, sc_ref, x_ref, mod_ref, w_ref, o_ref)


def _attn_ctx(q, k, v, sc, x, mod_l, mod_base, sink_row, w_out_bf, side=None):
    T = q.shape[0]
    tq = ATTN_CTX_SEQS * SEQ
    tok = lambda w: pl.BlockSpec((tq, w), lambda i: (i, 0))
    return _host_call(
        _attn_ctx_kernel,
        grid=(T // tq,),
        in_specs=[_resident((1, N_HEADS * BLOCK)), tok(ATTN_WIDTH), tok(KV_WIDTH), tok(KV_WIDTH),
                  tok(CONV_WIDTH), tok(D_MODEL), _mod_spec(lambda i: mod_base + CTX_ROW),
                  _resident((ATTN_WIDTH + CONV_WIDTH, D_MODEL))],
        args=[sink_row, q, k, v, sc, x, mod_l, w_out_bf],
        out_specs=[tok(D_MODEL)], out_shape=[jax.ShapeDtypeStruct((T, D_MODEL), F32)],
        scratch_shapes=[pltpu.VMEM((tq, ATTN_WIDTH), BF16)], name="attn_ctx", side=side, step_of=lambda i: i)


def _band_cap(has_prev, has_next):
    c = lax.broadcasted_iota(jnp.int32, (3 * BLOCK, BLOCK), 0)
    r = lax.broadcasted_iota(jnp.int32, (3 * BLOCK, BLOCK), 1)
    first_prev = r + jnp.where(has_prev, 0, BLOCK)
    last_next = r + 2 * BLOCK - jnp.where(has_next, 0, BLOCK)
    masked = ((c < BLOCK) & (c < first_prev)) | ((c >= 2 * BLOCK) & (c > last_next))
    return jnp.where(masked, NEG_INF, SCORE_CAP)


def _attn_win_kernel(sink_ref, q_ref, kp_ref, kc_ref, kn_ref, vp_ref, vc_ref, vn_ref, ck_ref, cv_ref,
                     sc_ref, x_ref, mod_ref, w_ref, *rest, n_steps, side_cast):
    (o_ref,), (attn_scr,) = _split_refs(rest, 1, side_cast)
    i = pl.program_id(1)
    kc, vc = kc_ref[...], vc_ref[...]
    inner = range(0, ATTN_Q_BLOCKS * BLOCK, BLOCK)
    k_blocks = [kp_ref[...]] + [kc[j:j + BLOCK] for j in inner] + [kn_ref[...]]
    v_blocks = [vp_ref[...]] + [vc[j:j + BLOCK] for j in inner] + [vn_ref[...]]
    sink_row = sink_ref[...] * LOG2_E

    def scores_of(j):
        kcat = jnp.concatenate(k_blocks[j:j + 3] + [ck_ref[0]], axis=0).astype(BF16)
        cap = _band_cap(i > 0 if j == 0 else True, i < n_steps - 1 if j == ATTN_Q_BLOCKS - 1 else True)
        return _scores(q_ref[j * BLOCK:(j + 1) * BLOCK, :], kcat, cap)

    _attend_blocks(ATTN_Q_BLOCKS, scores_of, lambda j: _transposed_values(v_blocks[j:j + 3] + [cv_ref[0]]),
                   sink_row, attn_scr)
    _project_out(attn_scr, sc_ref, x_ref, mod_ref, w_ref, o_ref)


def _attn_win(q, k, v, ck, cv, sc, x, mod_l, mod_base, sink_row, w_out_bf, side=None):
    T = q.shape[0]
    tq = ATTN_Q_BLOCKS * BLOCK
    n_steps = DEC_SEQ // tq
    nb = DEC_SEQ // BLOCK
    cur = lambda w: pl.BlockSpec((tq, w), lambda b, i: (b * n_steps + i, 0))
    prev = pl.BlockSpec((BLOCK, KV_WIDTH), lambda b, i: (b * nb + jnp.maximum(ATTN_Q_BLOCKS * i - 1, 0), 0))
    nxt = pl.BlockSpec((BLOCK, KV_WIDTH),
                       lambda b, i: (b * nb + jnp.minimum(ATTN_Q_BLOCKS * (i + 1), nb - 1), 0))
    ctx = pl.BlockSpec((1, PAST_LEN, KV_WIDTH), lambda b, i: (b, 0, 0))
    in_specs = [_resident((1, N_HEADS * BLOCK)), cur(ATTN_WIDTH),
                prev, cur(KV_WIDTH), nxt, prev, cur(KV_WIDTH), nxt, ctx, ctx,
                cur(CONV_WIDTH), cur(D_MODEL), _mod_spec(lambda b, i: mod_base + b),
                _resident((ATTN_WIDTH + CONV_WIDTH, D_MODEL))]
    args = [sink_row, q, k, k, k, v, v, v, ck, cv, sc, x, mod_l, w_out_bf]
    return _host_call(
        functools.partial(_attn_win_kernel, n_steps=n_steps),
        grid=(T // DEC_SEQ, n_steps), in_specs=in_specs, args=args,
        out_specs=[cur(D_MODEL)], out_shape=[jax.ShapeDtypeStruct((T, D_MODEL), F32)],
        scratch_shapes=[pltpu.VMEM((tq, ATTN_WIDTH), BF16)], name="attn_win", side=side,
        step_of=lambda b, i: b * n_steps + i)


def _gelu_tanh(x):
    k = 0.7978845608028654
    half = 0.5 * x
    return half + half * jnp.tanh(x * (k + (k * 0.044715) * (x * x)))


def _gmlp_kernel(x_ref, mod_ref, g_ref, win_ref, vg_ref, ws_ref, bs_ref, wout_ref, *rest, side_cast):
    (o_ref,), (gated_scr, h_scr) = _split_refs(rest, 1, side_cast)
    n_chunks = x_ref.shape[0] // CHUNK
    parts = _modulated_parts([x_ref], h_scr, g_ref[...], _mod_chunk(mod_ref, 0), _mod_chunk(mod_ref, 1), 1)
    u = _gelu_tanh(_dot_rows(parts, win_ref[:, :GMLP_WIDTH]))
    v = _gelu_tanh(_dot(h_scr[...], win_ref[:, GMLP_WIDTH:]))
    ms = jnp.mean(v * v, axis=-1, keepdims=True)
    v = (v * lax.rsqrt(ms + EPS) * vg_ref[...]).astype(BF16)
    for grp in range(GMLP_GROUPS):
        lanes = slice(grp * GMLP_GROUP_DIM, (grp + 1) * GMLP_GROUP_DIM)
        rhs = jnp.concatenate([v[n * CHUNK:(n + 1) * CHUNK, lanes] for n in range(n_chunks)], axis=1)
        s = _dot(ws_ref[grp], rhs)
        for n in range(n_chunks):
            rows = slice(n * CHUNK, (n + 1) * CHUNK)
            s_n = s[:, n * GMLP_GROUP_DIM:(n + 1) * GMLP_GROUP_DIM] + bs_ref[grp]
            gated_scr[rows, lanes] = (u[rows, lanes] * s_n).astype(BF16)
    o_ref[...] = x_ref[...] + _mod_chunk(mod_ref, 2) * _dot(gated_scr[...], wout_ref[...])


def _gmlp(x, mod_l, mod_row, g, win_bf, vg, ws_bf, bs_full, wout_bf, tm, side=None):
    T = x.shape[0]
    tok = pl.BlockSpec((tm, D_MODEL), lambda i: (i, 0))
    in_specs = [tok, _mod_spec(mod_row),
                _resident((1, D_MODEL)), _resident((D_MODEL, 2 * GMLP_WIDTH)), _resident((1, GMLP_WIDTH)),
                _resident((GMLP_GROUPS, CHUNK, CHUNK)), _resident((GMLP_GROUPS, CHUNK, GMLP_GROUP_DIM)),
                _resident((GMLP_WIDTH, D_MODEL))]
    args = [x, mod_l, g, win_bf, vg, ws_bf, bs_full, wout_bf]
    return _host_call(
        _gmlp_kernel, grid=(T // tm,), in_specs=in_specs, args=args,
        out_specs=[tok], out_shape=[jax.ShapeDtypeStruct((T, D_MODEL), F32)],
        scratch_shapes=[pltpu.VMEM((tm, GMLP_WIDTH), BF16), pltpu.VMEM((tm, D_MODEL), BF16)],
        name="gmlp", side=side, step_of=lambda i: i)


def _ffn_kernel(*refs, tiles_per_seq, seq_len, side_cast):
    halo = _halo_rows(tiles_per_seq)
    n_x = 3 if halo else 1
    x_refs, refs = refs[:n_x], refs[n_x:]
    mod_ref, g_ref, wup_ref, cw_ref, wdn_ref = refs[:5]
    (o_ref,), (act_scr, h_scr) = _split_refs(refs[5:], 1, side_cast)
    tm = o_ref.shape[0]
    parts = _modulated_parts(x_refs, h_scr, g_ref[...], _mod_chunk(mod_ref, 3), _mod_chunk(mod_ref, 4),
                             tiles_per_seq)
    masks = _seq_end_masks(tm, FFN_COLS, seq_len, tiles_per_seq)
    for j in range(0, D_FF, FFN_COLS):
        gate_cols = slice(j, j + FFN_COLS)
        val_cols = slice(D_FF + j, D_FF + j + FFN_COLS)
        if j == 0:
            up = lambda cols: _dot_rows(parts, wup_ref[:, cols])
        else:
            h = h_scr[...]
            up = lambda cols: _dot(h, wup_ref[:, cols])
        zg = _token_conv3(up(gate_cols), cw_ref[0, :, gate_cols], tm, halo, masks)
        zv = _token_conv3(up(val_cols), cw_ref[0, :, val_cols], tm, halo, masks)
        act_scr[:, gate_cols] = (jax.nn.silu(zg) * zv).astype(BF16)
    x = x_refs[n_x // 2][...]
    o_ref[...] = x + _mod_chunk(mod_ref, 5) * _dot(act_scr[...], wdn_ref[...])


def _ffn(x, mod_l, mod_row, g, wup_bf, conv_w, wdn_bf, layer, tm, seq_len, side=None):
    T = x.shape[0]
    tiles_per_seq = max(seq_len // tm, 1)
    assert tm % seq_len == 0 or seq_len % tm == 0
    x_specs = _token_specs(T, tm, tiles_per_seq)
    in_specs = x_specs + [_mod_spec(mod_row), _resident((1, D_MODEL)), _resident(wup_bf.shape),
                          _layer_slab(conv_w.shape, layer), _resident(wdn_bf.shape)]
    scratch = [pltpu.VMEM((tm, D_FF), BF16), pltpu.VMEM((tm + 2 * _halo_rows(tiles_per_seq), D_MODEL), BF16)]
    return _host_call(
        functools.partial(_ffn_kernel, tiles_per_seq=tiles_per_seq, seq_len=seq_len),
        grid=(T // tm,), in_specs=in_specs, args=[x] * len(x_specs) + [mod_l, g, wup_bf, conv_w, wdn_bf],
        out_specs=[pl.BlockSpec((tm, D_MODEL), lambda i: (i, 0))], out_shape=[jax.ShapeDtypeStruct((T, D_MODEL), F32)],
        scratch_shapes=scratch, name="conv_ffn", side=side, step_of=lambda i: i)


def _rope_tables(n_tokens):
    t = np.arange(n_tokens)
    n_freq = HEAD_DIM // 4
    inv = (ROPE_BASE ** (-np.arange(n_freq, dtype=np.float32) / n_freq)).astype(np.float32)
    row_ang = (t // GRID_W).astype(np.float32)[:, None] * inv
    col_ang = (t % GRID_W).astype(np.float32)[:, None] * inv
    ang = np.concatenate([row_ang, row_ang, col_ang, col_ang], axis=1)
    ang = np.tile(ang, (1, LANES // HEAD_DIM)).astype(np.float64)
    first = (np.arange(LANES) % (2 * n_freq)) < n_freq
    cos, sin = np.cos(ang), np.sin(ang)
    tables = (cos, np.where(first, -sin, 0.0), np.where(first, 0.0, sin))
    return tuple(jnp.asarray(tab.astype(np.float32)) for tab in tables)


class _Bf16Weights:
    def __init__(self, **stacks):
        self.stacks, self.ready = stacks, {}

    def side(self, name, slab):
        if slab < self.stacks[name].shape[0] and (name, slab) not in self.ready:
            return self.stacks[name], slab
        return None

    def collect(self, name, slab, side, outs):
        if side is None:
            return outs
        self.ready[(name, slab)] = outs[-1]
        return outs[:-1]

    def get(self, name, slab):
        if (name, slab) not in self.ready:
            self.ready[(name, slab)] = _to_bf16(self.stacks[name][slab:slab + 1])[0]
        return self.ready[(name, slab)]


def kernel(x_prompt, x_sample, cache_k, cache_v, c, c_ctx, ada_w, ada_b, norm_mix_g, norm_ffn_g, w_in_even,
           q_norm_g, k_norm_g, sink_logit, short_conv_w, w_out_even, w_in_odd, gmlp_norm_g, w_spatial,
           b_spatial, w_out_odd, w_up, ffn_conv_w, w_down):
    n_p, n_s = BATCH * SEQ, DEC_BATCH * DEC_SEQ
    xp = x_prompt.reshape(n_p, D_MODEL)
    xs = x_sample.reshape(n_s, D_MODEL)
    cond = jnp.concatenate([c, c_ctx[None, :], jnp.zeros((COND_ROWS - DEC_BATCH - 1, D_MODEL), F32)], axis=0)
    bf = _Bf16Weights(in_even=w_in_even, out_even=w_out_even, in_odd=w_in_odd, out_odd=w_out_odd,
                      spatial=w_spatial.reshape(N_ODD, GMLP_GROUPS * CHUNK, CHUNK), up=w_up, down=w_down)

    def hosted(name, slab, call):
        side = bf.side(name, slab)
        return bf.collect(name, slab, side, call(side))

    mod_l, = hosted("in_even", 0, lambda side: _adaln(cond, ada_w, ada_b, side))
    rope_tabs = _rope_tables(DEC_SEQ)

    tm_p, tm_s = 1024, 1024
    row_vec = lambda a: a.reshape(1, -1)
    new_k, new_v = [], []
    for l in range(DEPTH):
        base = l * COND_ROWS
        row_p = lambda i, base=base: base + CTX_ROW
        row_s = lambda i, base=base: base + i // (DEC_SEQ // tm_s)
        g_mix = row_vec(norm_mix_g[l])
        nxt = (l + 1) // 2
        if l % 2 == 0:
            e = l // 2
            w_in = bf.get("in_even", e)
            qg = row_vec(jnp.tile(q_norm_g[e], N_HEADS))
            kg = row_vec(jnp.tile(k_norm_g[e], N_KV_HEADS))
            sink = jnp.repeat(sink_logit[e], BLOCK).reshape(1, N_HEADS * BLOCK)
            cw = short_conv_w[e]
            qp, kp, vp, scp = hosted("out_even", e, lambda side: _in_even(
                xp, mod_l, row_p, g_mix, w_in, qg, kg, cw, None, tm_p, SEQ, side))
            new_k.append(kp.reshape(BATCH, SEQ, N_KV_HEADS, HEAD_DIM))
            new_v.append(vp.reshape(BATCH, SEQ, N_KV_HEADS, HEAD_DIM))
            w_out = bf.get("out_even", e)
            xp, = hosted("spatial", nxt, lambda side: _attn_ctx(qp, kp, vp, scp, xp, mod_l, base, sink, w_out, side))
            qs, ks, vs, scs = hosted("down", l, lambda side: _in_even(
                xs, mod_l, row_s, g_mix, w_in, qg, kg, cw, rope_tabs, tm_s, DEC_SEQ, side))
            ck = cache_k[:, e].reshape(DEC_BATCH, PAST_LEN, KV_WIDTH)
            cv = cache_v[:, e].reshape(DEC_BATCH, PAST_LEN, KV_WIDTH)
            xs, = hosted("up", l, lambda side: _attn_win(qs, ks, vs, ck, cv, scs, xs, mod_l, base, sink, w_out, side))
            ffn_hosts = ("in_odd", "out_odd")
        else:
            o = l // 2
            w_in, w_out = bf.get("in_odd", o), bf.get("out_odd", o)
            vg = row_vec(gmlp_norm_g[o])
            ws = bf.get("spatial", o).reshape(GMLP_GROUPS, CHUNK, CHUNK)
            bs_full = jnp.broadcast_to(b_spatial[o][:, :, None], (GMLP_GROUPS, CHUNK, GMLP_GROUP_DIM))
            xp, = hosted("down", l, lambda side: _gmlp(xp, mod_l, row_p, g_mix, w_in, vg, ws, bs_full, w_out, tm_p, side))
            xs, = hosted("up", l, lambda side: _gmlp(xs, mod_l, row_s, g_mix, w_in, vg, ws, bs_full, w_out, tm_s, side))
            ffn_hosts = ("in_even", "out_even")
        g_ffn = row_vec(norm_ffn_g[l])
        w_up_l, w_down_l = bf.get("up", l), bf.get("down", l)
        xp, = hosted(ffn_hosts[0], nxt, lambda side: _ffn(
            xp, mod_l, row_p, g_ffn, w_up_l, ffn_conv_w, w_down_l, l, tm_p, SEQ, side))
        xs, = hosted(ffn_hosts[1], nxt, lambda side: _ffn(
            xs, mod_l, row_s, g_ffn, w_up_l, ffn_conv_w, w_down_l, l, tm_s, DEC_SEQ, side))
    per_layer = lambda parts: parts[0][:, None] if len(parts) == 1 else jnp.stack(parts, axis=1)
    return (xp.reshape(BATCH, SEQ, D_MODEL), xs.reshape(DEC_BATCH, DEC_SEQ, D_MODEL),
            per_layer(new_k), per_layer(new_v))
```

```python
import functools

import jax
import jax.numpy as jnp
import numpy as np
from jax import lax
from jax.experimental import pallas as pl
from jax.experimental.pallas import tpu as pltpu

F32 = jnp.float32
BF16 = jnp.bfloat16

D_MODEL = 1024
BATCH = 16
SEQ = 256
DEPTH = 2
DEC_BATCH = 4
DEC_SEQ = 2048
PAST_LEN = 512
GRID_W = 64
N_HEADS = 8
N_KV_HEADS = 2
HEAD_DIM = 64
GQA_GROUP = N_HEADS // N_KV_HEADS
ATTN_WIDTH = N_HEADS * HEAD_DIM
KV_WIDTH = N_KV_HEADS * HEAD_DIM
WINDOW = 128
BLOCK = 128
ROPE_BASE = 10000.0
CONV_WIDTH = 512
CHUNK = 128
GMLP_WIDTH = 1024
GMLP_GROUPS = 8
GMLP_GROUP_DIM = GMLP_WIDTH // GMLP_GROUPS
D_FF = 2816
EPS = 1e-6
NEG_INF = -1e30
N_EVEN = (DEPTH + 1) // 2
N_ODD = DEPTH // 2
LOG2_E = 1.4426950408889634

LANES = 128
SUBLANES_F32 = 8
SUBLANES_BF16 = 16
MXU_WIDTH = 256
VMEM_LIMIT_BYTES = 56 * 1024 * 1024

COND_ROWS = 8
CTX_ROW = DEC_BATCH
TOKEN_HALO = SUBLANES_BF16
FFN_COLS = MXU_WIDTH
LEAD_PARTS = 4
CAST_BLOCK_BYTES = 3 * 1024 * 1024
ADA_TN = 1536
ATTN_Q_BLOCKS = 4
ATTN_CTX_SEQS = 2
SUM_ROWS = SUBLANES_BF16
SCORE_CAP = 3.0e38


def _params(n_axes):
    return pltpu.CompilerParams(dimension_semantics=("parallel",) * n_axes,
                                vmem_limit_bytes=VMEM_LIMIT_BYTES)


def _resident(shape):
    zeros = (0,) * len(shape)
    return pl.BlockSpec(shape, lambda *_: zeros, pipeline_mode=pl.Buffered(1))


def _layer_slab(shape, layer):
    return pl.BlockSpec((1,) + tuple(shape[1:]), lambda *_: (layer, 0, 0), pipeline_mode=pl.Buffered(1))


def _mod_spec(mod_row):
    return pl.BlockSpec((1, 1, 6 * D_MODEL), lambda *idx: (mod_row(*idx), 0, 0))


def _dot(a, b):
    return jnp.dot(a, b, preferred_element_type=F32)


def _cast_kernel(x_ref, o_ref):
    o_ref[...] = x_ref[...].astype(BF16)


def _to_bf16(w):
    n, rows, cols = w.shape
    fits = [r for r in range(SUBLANES_BF16, rows + 1, SUBLANES_BF16)
            if rows % r == 0 and r * cols * 4 <= CAST_BLOCK_BYTES]
    tr = max(fits)
    spec = pl.BlockSpec((1, tr, cols), lambda i, j: (i, j, 0))
    return pl.pallas_call(
        _cast_kernel, grid=(n, rows // tr), in_specs=[spec], out_specs=spec,
        out_shape=jax.ShapeDtypeStruct(w.shape, BF16), compiler_params=_params(2), name="to_bf16",
    )(w)


def _side_cast_specs(side, n_steps, step_of):
    w, layer = side
    _, rows, cols = w.shape
    tr = rows // n_steps
    assert tr * n_steps == rows and tr % SUBLANES_BF16 == 0
    return (pl.BlockSpec((1, tr, cols), lambda *idx: (layer, step_of(*idx), 0)),
            pl.BlockSpec((tr, cols), lambda *idx: (step_of(*idx), 0)),
            jax.ShapeDtypeStruct((rows, cols), BF16))


def _split_refs(rest, n_out, side_cast):
    n = side_cast
    side_ins, outs, side_outs, scratch = rest[:n], rest[n:n + n_out], rest[n + n_out:2 * n + n_out], rest[2 * n + n_out:]
    for side_in, side_out in zip(side_ins, side_outs):
        side_out[...] = side_in[0].astype(BF16)
    return outs, scratch


def _host_call(kernel, *, grid, in_specs, args, out_specs, out_shape, scratch_shapes, name, side, step_of):
    in_specs, args, out_specs, out_shape = list(in_specs), list(args), list(out_specs), list(out_shape)
    n_steps = 1
    for extent in grid:
        n_steps *= extent
    sides = [_side_cast_specs(one, n_steps, step_of) for one in side]
    in_specs += [spec for spec, _, _ in sides]
    args += [stack for stack, _ in side]
    out_specs += [spec for _, spec, _ in sides]
    out_shape += [shape for _, _, shape in sides]
    return pl.pallas_call(
        functools.partial(kernel, side_cast=len(side)),
        grid=grid, in_specs=in_specs, out_specs=out_specs, out_shape=out_shape, scratch_shapes=scratch_shapes,
        compiler_params=_params(len(grid)), name=name,
    )(*args)


def _modulate(x, g, shift, scale):
    ms = jnp.mean(x * x, axis=-1, keepdims=True)
    return (x * lax.rsqrt(ms + EPS) * g) * (1.0 + scale) + shift


def _mod_chunk(mod_ref, k):
    return mod_ref[0, :, k * D_MODEL:(k + 1) * D_MODEL]


def _halo_rows(tiles_per_seq):
    return TOKEN_HALO if tiles_per_seq > 1 else 0


def _token_specs(T, tm, tiles_per_seq):
    tok = pl.BlockSpec((tm, D_MODEL), lambda i: (i, 0))
    if tiles_per_seq == 1:
        return [tok]
    per_tile = tm // TOKEN_HALO
    last = T // TOKEN_HALO - 1
    prev = pl.BlockSpec((TOKEN_HALO, D_MODEL), lambda i: (jnp.maximum(i * per_tile - 1, 0), 0))
    nxt = pl.BlockSpec((TOKEN_HALO, D_MODEL), lambda i: (jnp.minimum((i + 1) * per_tile, last), 0))
    return [prev, tok, nxt]


def _modulated_parts(x_refs, h_scr, g, shift, scale, tiles_per_seq):
    halo = _halo_rows(tiles_per_seq)
    x_ref = x_refs[len(x_refs) // 2]
    tm = x_ref.shape[0]
    step = tm // LEAD_PARTS
    parts = []
    for p in range(LEAD_PARTS):
        lo, hi = halo + p * step, halo + (p + 1) * step
        piece = _modulate(x_ref[p * step:(p + 1) * step, :], g, shift, scale).astype(BF16)
        if halo and p == 0:
            pos = pl.program_id(0) % tiles_per_seq
            edge = jnp.where(pos > 0, _modulate(x_refs[0][...], g, shift, scale), 0.0).astype(BF16)
            piece, lo = jnp.concatenate([edge, piece], axis=0), 0
        if halo and p == LEAD_PARTS - 1:
            pos = pl.program_id(0) % tiles_per_seq
            edge = jnp.where(pos < tiles_per_seq - 1, _modulate(x_refs[2][...], g, shift, scale), 0.0).astype(BF16)
            piece, hi = jnp.concatenate([piece, edge], axis=0), tm + 2 * halo
        h_scr[lo:hi, :] = piece
        parts.append(piece)
    return parts


def _dot_rows(parts, w):
    return jnp.concatenate([_dot(p, w) for p in parts], axis=0)


def _seq_end_masks(tm, cols, seq_len, tiles_per_seq):
    if tiles_per_seq > 1:
        return None
    seq_row = lax.broadcasted_iota(jnp.int32, (tm, cols), 0) % seq_len
    return seq_row != 0, seq_row != seq_len - 1


def _token_conv3(z, w, tm, halo, masks):
    rows = z.shape[0]
    mid = slice(halo, halo + tm)
    dn = pltpu.roll(z, 1, 0)[mid]
    up = pltpu.roll(z, rows - 1, 0)[mid]
    if masks is not None:
        dn = jnp.where(masks[0], dn, 0.0)
        up = jnp.where(masks[1], up, 0.0)
    return dn * w[0:1] + z[mid] * w[1:2] + up * w[2:3]


def _adaln_kernel(cond_ref, w_ref, b_ref, *rest, side_cast):
    (o_ref,), _ = _split_refs(rest, 1, side_cast)
    a = jax.nn.silu(cond_ref[...]).astype(BF16)
    rows = _dot(a, w_ref[0].astype(BF16)) + b_ref[0]
    for r in range(COND_ROWS):
        o_ref[r] = rows[r:r + 1, :]


def _adaln(cond, ada_w, ada_b, side=()):
    n_out = 6 * D_MODEL
    n_col = n_out // ADA_TN
    return _host_call(
        _adaln_kernel,
        grid=(DEPTH, n_col),
        in_specs=[pl.BlockSpec((COND_ROWS, D_MODEL), lambda l, j: (0, 0)),
                  pl.BlockSpec((1, D_MODEL, ADA_TN), lambda l, j: (l, 0, j)),
                  pl.BlockSpec((1, 1, ADA_TN), lambda l, j: (l, 0, j))],
        args=[cond, ada_w, ada_b.reshape(DEPTH, 1, n_out)],
        out_specs=[pl.BlockSpec((COND_ROWS, 1, ADA_TN), lambda l, j: (l, 0, j))],
        out_shape=[jax.ShapeDtypeStruct((DEPTH * COND_ROWS, 1, n_out), F32)],
        scratch_shapes=[], name="adaln", side=side, step_of=lambda l, j: l * n_col + j)


def _head_rms(z, gain):
    n = z.shape[1]
    w = min(n, MXU_WIDTH)
    r = lax.broadcasted_iota(jnp.int32, (w, w), 0) // HEAD_DIM
    c = lax.broadcasted_iota(jnp.int32, (w, w), 1) // HEAD_DIM
    ones = (r == c).astype(BF16)
    sq = (z * z).astype(BF16)
    ss = jnp.concatenate([_dot(sq[:, k:k + w], ones) for k in range(0, n, w)], axis=1)
    return z * lax.rsqrt(ss * (1.0 / HEAD_DIM) + EPS) * gain


def _rope(z, cos, sin_lo, sin_hi):
    outs = []
    for k in range(0, z.shape[1], LANES):
        blk = z[:, k:k + LANES]
        outs.append(blk * cos + pltpu.roll(blk, LANES - 16, 1) * sin_lo + pltpu.roll(blk, 16, 1) * sin_hi)
    return jnp.concatenate(outs, axis=1)


def _in_even_kernel(*refs, rope, tiles_per_seq, seq_len, side_cast):
    halo = _halo_rows(tiles_per_seq)
    n_x = 3 if halo else 1
    x_refs, refs = refs[:n_x], refs[n_x:]
    mod_ref, g_ref, w_ref, qg_ref, kg_ref, cw_ref = refs[:6]
    refs = refs[6:]
    if rope:
        cos_ref, slo_ref, shi_ref = refs[:3]
        refs = refs[3:]
    (q_ref, k_ref, v_ref, sc_ref), (h_scr,) = _split_refs(refs, 4, side_cast)
    tm = q_ref.shape[0]
    parts = _modulated_parts(x_refs, h_scr, g_ref[...], _mod_chunk(mod_ref, 0), _mod_chunk(mod_ref, 1),
                             tiles_per_seq)
    c0 = 0
    c1 = c0 + ATTN_WIDTH
    c2 = c1 + 2 * KV_WIDTH
    c3 = c2 + CONV_WIDTH
    c4 = c3 + CONV_WIDTH
    c5 = c4 + CONV_WIDTH
    zq = _dot_rows(parts, w_ref[:, c0:c1])[halo:halo + tm]
    h_all = h_scr[...]
    h = h_scr[halo:halo + tm, :]
    q = _head_rms(zq, qg_ref[...] * (HEAD_DIM ** -0.5 * LOG2_E))
    zkv = _dot(h, w_ref[:, c1:c2])
    k = _head_rms(zkv[:, :KV_WIDTH], kg_ref[...])
    if rope:
        cos, slo, shi = cos_ref[...], slo_ref[...], shi_ref[...]
        q = _rope(q, cos, slo, shi)
        k = _rope(k, cos, slo, shi)
    q_ref[...] = q
    k_ref[...] = k
    v_ref[...] = zkv[:, KV_WIDTH:]
    ch = _dot(h_all, w_ref[:, c3:c4]) * _dot(h_all, w_ref[:, c4:c5])
    conv = _token_conv3(ch, cw_ref[...], tm, halo, _seq_end_masks(tm, CONV_WIDTH, seq_len, tiles_per_seq))
    sc_ref[...] = (_dot(h, w_ref[:, c2:c3]) * conv).astype(BF16)


def _in_even(x, mod_l, mod_row, g, w_bf, qg, kg, conv_w, rope_tabs, tm, seq_len, side=()):
    T = x.shape[0]
    n_in = w_bf.shape[1]
    tiles_per_seq = max(seq_len // tm, 1)
    assert tm % seq_len == 0 or seq_len % tm == 0
    tok = lambda w: pl.BlockSpec((tm, w), lambda i: (i, 0))
    x_specs = _token_specs(T, tm, tiles_per_seq)
    in_specs = x_specs + [_mod_spec(mod_row), _resident((1, D_MODEL)), _resident((D_MODEL, n_in)),
                          _resident((1, ATTN_WIDTH)), _resident((1, KV_WIDTH)), _resident((3, CONV_WIDTH))]
    args = [x] * len(x_specs) + [mod_l, g, w_bf, qg, kg, conv_w]
    if rope_tabs is not None:
        in_specs += [pl.BlockSpec((tm, LANES), lambda i: (i % tiles_per_seq, 0))] * 3
        args += list(rope_tabs)
    outs = ((ATTN_WIDTH, F32), (KV_WIDTH, F32), (KV_WIDTH, F32), (CONV_WIDTH, BF16))
    scratch = [pltpu.VMEM((tm + 2 * _halo_rows(tiles_per_seq), D_MODEL), BF16)]
    return _host_call(
        functools.partial(_in_even_kernel, rope=rope_tabs is not None, tiles_per_seq=tiles_per_seq, seq_len=seq_len),
        grid=(T // tm,), in_specs=in_specs, args=args,
        out_specs=[tok(w) for w, _ in outs], out_shape=[jax.ShapeDtypeStruct((T, w), dt) for w, dt in outs],
        scratch_shapes=scratch, name="in_even", side=side, step_of=lambda i: i)


def _stack_heads(q):
    lane_lo = lax.broadcasted_iota(jnp.int32, (BLOCK, LANES), 1) < HEAD_DIM
    heads = []
    for pair in range(N_HEADS // 2):
        kv = (2 * pair) // GQA_GROUP
        qp = q[:, pair * LANES:(pair + 1) * LANES]
        qr = pltpu.roll(qp, HEAD_DIM, 1)
        for half in range(2):
            src = qp if half == kv else qr
            heads.append(jnp.where(lane_lo, src, 0.0) if kv == 0 else jnp.where(lane_lo, 0.0, src))
    return jnp.concatenate(heads, axis=0).astype(BF16)


def _transposed_values(v_blocks):
    vt = jnp.concatenate([v[j:j + BLOCK].T for v in v_blocks for j in range(0, v.shape[0], BLOCK)], axis=1)
    return jnp.concatenate([vt.astype(BF16), jnp.ones((SUM_ROWS, vt.shape[1]), BF16)], axis=0)


def _scores(q, kcat, cap):
    st = lax.dot_general(kcat, _stack_heads(q), (((1,), (1,)), ((), ())), preferred_element_type=F32)
    if cap is None:
        return st
    n_band = cap.shape[0]
    capped = jnp.minimum(st[:n_band], jnp.concatenate([cap] * N_HEADS, axis=1))
    return jnp.concatenate([capped, st[n_band:]], axis=0)


def _weighted_values(st, vt, sink_row):
    m = jnp.maximum(jnp.max(st, axis=0, keepdims=True), sink_row)
    pt = jnp.exp2(st - m).astype(BF16)
    ot = _dot(vt, pt)
    denom = ot[KV_WIDTH:KV_WIDTH + 1] + jnp.exp2(sink_row - m)
    ot = ot[:KV_WIDTH] / denom
    pairs = []
    for pair in range(N_HEADS // 2):
        kv = (2 * pair) // GQA_GROUP
        dims = slice(kv * HEAD_DIM, (kv + 1) * HEAD_DIM)
        both = [ot[dims, (2 * pair + half) * BLOCK:(2 * pair + half + 1) * BLOCK] for half in range(2)]
        pairs.append(jnp.concatenate(both, axis=0).T)
    return jnp.concatenate(pairs, axis=1)


def _attend_blocks(n_blocks, scores_of, values_of, sink_row, attn_scr):
    st = scores_of(0)
    for j in range(n_blocks):
        st_next = scores_of(j + 1) if j + 1 < n_blocks else None
        attn_scr[j * BLOCK:(j + 1) * BLOCK, :] = _weighted_values(st, values_of(j), sink_row).astype(BF16)
        st = st_next


def _project_out(attn_scr, sc_ref, x_ref, mod_ref, w_ref, o_ref):
    mix = _dot(attn_scr[...], w_ref[:ATTN_WIDTH, :]) + _dot(sc_ref[...], w_ref[ATTN_WIDTH:, :])
    o_ref[...] = x_ref[...] + _mod_chunk(mod_ref, 2) * mix


def _attn_ctx_kernel(sink_ref, q_ref, k_ref, v_ref, sc_ref, x_ref, mod_ref, w_ref, *rest, side_cast):
    (o_ref,), (attn_scr,) = _split_refs(rest, 1, side_cast)
    per_seq = SEQ // BLOCK
    kcats = [k_ref[s * SEQ:(s + 1) * SEQ, :].astype(BF16) for s in range(ATTN_CTX_SEQS)]
    vts = [_transposed_values([v_ref[s * SEQ:(s + 1) * SEQ, :]]) for s in range(ATTN_CTX_SEQS)]
    sink_row = sink_ref[...] * LOG2_E
    _attend_blocks(ATTN_CTX_SEQS * per_seq,
                   lambda j: _scores(q_ref[j * BLOCK:(j + 1) * BLOCK, :], kcats[j // per_seq], None),
                   lambda j: vts[j // per_seq], sink_row, attn_scr)
    _project_out(attn_scr, sc_ref, x_ref, mod_ref, w_ref, o_ref)


def _attn_ctx(q, k, v, sc, x, mod_l, mod_base, sink_row, w_out_bf, side=()):
    T = q.shape[0]
    tq = ATTN_CTX_SEQS * SEQ
    tok = lambda w: pl.BlockSpec((tq, w), lambda i: (i, 0))
    return _host_call(
        _attn_ctx_kernel,
        grid=(T // tq,),
        in_specs=[_resident((1, N_HEADS * BLOCK)), tok(ATTN_WIDTH), tok(KV_WIDTH), tok(KV_WIDTH),
                  tok(CONV_WIDTH), tok(D_MODEL), _mod_spec(lambda i: mod_base + CTX_ROW),
                  _resident((ATTN_WIDTH + CONV_WIDTH, D_MODEL))],
        args=[sink_row, q, k, v, sc, x, mod_l, w_out_bf],
        out_specs=[tok(D_MODEL)], out_shape=[jax.ShapeDtypeStruct((T, D_MODEL), F32)],
        scratch_shapes=[pltpu.VMEM((tq, ATTN_WIDTH), BF16)], name="attn_ctx", side=side, step_of=lambda i: i)


def _band_cap(has_prev, has_next):
    c = lax.broadcasted_iota(jnp.int32, (3 * BLOCK, BLOCK), 0)
    r = lax.broadcasted_iota(jnp.int32, (3 * BLOCK, BLOCK), 1)
    first_prev = r + jnp.where(has_prev, 0, BLOCK)
    last_next = r + 2 * BLOCK - jnp.where(has_next, 0, BLOCK)
    masked = ((c < BLOCK) & (c < first_prev)) | ((c >= 2 * BLOCK) & (c > last_next))
    return jnp.where(masked, NEG_INF, SCORE_CAP)


def _attn_win_kernel(sink_ref, q_ref, kp_ref, kc_ref, kn_ref, vp_ref, vc_ref, vn_ref, ck_ref, cv_ref,
                     sc_ref, x_ref, mod_ref, w_ref, *rest, n_steps, side_cast):
    (o_ref,), (attn_scr,) = _split_refs(rest, 1, side_cast)
    i = pl.program_id(1)
    kc, vc = kc_ref[...], vc_ref[...]
    inner = range(0, ATTN_Q_BLOCKS * BLOCK, BLOCK)
    k_blocks = [kp_ref[...]] + [kc[j:j + BLOCK] for j in inner] + [kn_ref[...]]
    v_blocks = [vp_ref[...]] + [vc[j:j + BLOCK] for j in inner] + [vn_ref[...]]
    sink_row = sink_ref[...] * LOG2_E

    def scores_of(j):
        kcat = jnp.concatenate(k_blocks[j:j + 3] + [ck_ref[0]], axis=0).astype(BF16)
        cap = _band_cap(i > 0 if j == 0 else True, i < n_steps - 1 if j == ATTN_Q_BLOCKS - 1 else True)
        return _scores(q_ref[j * BLOCK:(j + 1) * BLOCK, :], kcat, cap)

    _attend_blocks(ATTN_Q_BLOCKS, scores_of, lambda j: _transposed_values(v_blocks[j:j + 3] + [cv_ref[0]]),
                   sink_row, attn_scr)
    _project_out(attn_scr, sc_ref, x_ref, mod_ref, w_ref, o_ref)


def _attn_win(q, k, v, ck, cv, sc, x, mod_l, mod_base, sink_row, w_out_bf, side=()):
    T = q.shape[0]
    tq = ATTN_Q_BLOCKS * BLOCK
    n_steps = DEC_SEQ // tq
    nb = DEC_SEQ // BLOCK
    cur = lambda w: pl.BlockSpec((tq, w), lambda b, i: (b * n_steps + i, 0))
    prev = pl.BlockSpec((BLOCK, KV_WIDTH), lambda b, i: (b * nb + jnp.maximum(ATTN_Q_BLOCKS * i - 1, 0), 0))
    nxt = pl.BlockSpec((BLOCK, KV_WIDTH),
                       lambda b, i: (b * nb + jnp.minimum(ATTN_Q_BLOCKS * (i + 1), nb - 1), 0))
    ctx = pl.BlockSpec((1, PAST_LEN, KV_WIDTH), lambda b, i: (b, 0, 0))
    in_specs = [_resident((1, N_HEADS * BLOCK)), cur(ATTN_WIDTH),
                prev, cur(KV_WIDTH), nxt, prev, cur(KV_WIDTH), nxt, ctx, ctx,
                cur(CONV_WIDTH), cur(D_MODEL), _mod_spec(lambda b, i: mod_base + b),
                _resident((ATTN_WIDTH + CONV_WIDTH, D_MODEL))]
    args = [sink_row, q, k, k, k, v, v, v, ck, cv, sc, x, mod_l, w_out_bf]
    return _host_call(
        functools.partial(_attn_win_kernel, n_steps=n_steps),
        grid=(T // DEC_SEQ, n_steps), in_specs=in_specs, args=args,
        out_specs=[cur(D_MODEL)], out_shape=[jax.ShapeDtypeStruct((T, D_MODEL), F32)],
        scratch_shapes=[pltpu.VMEM((tq, ATTN_WIDTH), BF16)], name="attn_win", side=side,
        step_of=lambda b, i: b * n_steps + i)


def _gelu_tanh(x):
    k = 0.7978845608028654
    half = 0.5 * x
    return half + half * jnp.tanh(x * (k + (k * 0.044715) * (x * x)))


def _gmlp_kernel(x_ref, mod_ref, g_ref, win_ref, vg_ref, ws_ref, bs_ref, wout_ref, *rest, side_cast):
    (o_ref,), (gated_scr, h_scr) = _split_refs(rest, 1, side_cast)
    n_chunks = x_ref.shape[0] // CHUNK
    parts = _modulated_parts([x_ref], h_scr, g_ref[...], _mod_chunk(mod_ref, 0), _mod_chunk(mod_ref, 1), 1)
    u = _gelu_tanh(_dot_rows(parts, win_ref[:, :GMLP_WIDTH]))
    v = _gelu_tanh(_dot(h_scr[...], win_ref[:, GMLP_WIDTH:]))
    ms = jnp.mean(v * v, axis=-1, keepdims=True)
    v = (v * lax.rsqrt(ms + EPS) * vg_ref[...]).astype(BF16)
    for grp in range(GMLP_GROUPS):
        lanes = slice(grp * GMLP_GROUP_DIM, (grp + 1) * GMLP_GROUP_DIM)
        rhs = jnp.concatenate([v[n * CHUNK:(n + 1) * CHUNK, lanes] for n in range(n_chunks)], axis=1)
        s = _dot(ws_ref[grp], rhs)
        for n in range(n_chunks):
            rows = slice(n * CHUNK, (n + 1) * CHUNK)
            s_n = s[:, n * GMLP_GROUP_DIM:(n + 1) * GMLP_GROUP_DIM] + bs_ref[grp]
            gated_scr[rows, lanes] = (u[rows, lanes] * s_n).astype(BF16)
    o_ref[...] = x_ref[...] + _mod_chunk(mod_ref, 2) * _dot(gated_scr[...], wout_ref[...])


def _gmlp(x, mod_l, mod_row, g, win_bf, vg, ws_bf, bs_full, wout_bf, tm, side=()):
    T = x.shape[0]
    tok = pl.BlockSpec((tm, D_MODEL), lambda i: (i, 0))
    in_specs = [tok, _mod_spec(mod_row),
                _resident((1, D_MODEL)), _resident((D_MODEL, 2 * GMLP_WIDTH)), _resident((1, GMLP_WIDTH)),
                _resident((GMLP_GROUPS, CHUNK, CHUNK)), _resident((GMLP_GROUPS, CHUNK, GMLP_GROUP_DIM)),
                _resident((GMLP_WIDTH, D_MODEL))]
    args = [x, mod_l, g, win_bf, vg, ws_bf, bs_full, wout_bf]
    return _host_call(
        _gmlp_kernel, grid=(T // tm,), in_specs=in_specs, args=args,
        out_specs=[tok], out_shape=[jax.ShapeDtypeStruct((T, D_MODEL), F32)],
        scratch_shapes=[pltpu.VMEM((tm, GMLP_WIDTH), BF16), pltpu.VMEM((tm, D_MODEL), BF16)],
        name="gmlp", side=side, step_of=lambda i: i)


def _ffn_kernel(*refs, tiles_per_seq, seq_len, side_cast):
    halo = _halo_rows(tiles_per_seq)
    n_x = 3 if halo else 1
    x_refs, refs = refs[:n_x], refs[n_x:]
    mod_ref, g_ref, wup_ref, cw_ref, wdn_ref = refs[:5]
    (o_ref,), (act_scr, h_scr) = _split_refs(refs[5:], 1, side_cast)
    tm = o_ref.shape[0]
    parts = _modulated_parts(x_refs, h_scr, g_ref[...], _mod_chunk(mod_ref, 3), _mod_chunk(mod_ref, 4),
                             tiles_per_seq)
    masks = _seq_end_masks(tm, FFN_COLS, seq_len, tiles_per_seq)
    for j in range(0, D_FF, FFN_COLS):
        gate_cols = slice(j, j + FFN_COLS)
        val_cols = slice(D_FF + j, D_FF + j + FFN_COLS)
        if j == 0:
            up = lambda cols: _dot_rows(parts, wup_ref[:, cols])
        else:
            h = h_scr[...]
            up = lambda cols: _dot(h, wup_ref[:, cols])
        zg = _token_conv3(up(gate_cols), cw_ref[0, :, gate_cols], tm, halo, masks)
        zv = _token_conv3(up(val_cols), cw_ref[0, :, val_cols], tm, halo, masks)
        act_scr[:, gate_cols] = (jax.nn.silu(zg) * zv).astype(BF16)
    x = x_refs[n_x // 2][...]
    o_ref[...] = x + _mod_chunk(mod_ref, 5) * _dot(act_scr[...], wdn_ref[...])


def _ffn(x, mod_l, mod_row, g, wup_bf, conv_w, wdn_bf, layer, tm, seq_len, side=()):
    T = x.shape[0]
    tiles_per_seq = max(seq_len // tm, 1)
    assert tm % seq_len == 0 or seq_len % tm == 0
    x_specs = _token_specs(T, tm, tiles_per_seq)
    in_specs = x_specs + [_mod_spec(mod_row), _resident((1, D_MODEL)), _resident(wup_bf.shape),
                          _layer_slab(conv_w.shape, layer), _resident(wdn_bf.shape)]
    scratch = [pltpu.VMEM((tm, D_FF), BF16), pltpu.VMEM((tm + 2 * _halo_rows(tiles_per_seq), D_MODEL), BF16)]
    return _host_call(
        functools.partial(_ffn_kernel, tiles_per_seq=tiles_per_seq, seq_len=seq_len),
        grid=(T // tm,), in_specs=in_specs, args=[x] * len(x_specs) + [mod_l, g, wup_bf, conv_w, wdn_bf],
        out_specs=[pl.BlockSpec((tm, D_MODEL), lambda i: (i, 0))], out_shape=[jax.ShapeDtypeStruct((T, D_MODEL), F32)],
        scratch_shapes=scratch, name="conv_ffn", side=side, step_of=lambda i: i)


def _rope_tables(n_tokens):
    t = np.arange(n_tokens)
    n_freq = HEAD_DIM // 4
    inv = (ROPE_BASE ** (-np.arange(n_freq, dtype=np.float32) / n_freq)).astype(np.float32)
    row_ang = (t // GRID_W).astype(np.float32)[:, None] * inv
    col_ang = (t % GRID_W).astype(np.float32)[:, None] * inv
    ang = np.concatenate([row_ang, row_ang, col_ang, col_ang], axis=1)
    ang = np.tile(ang, (1, LANES // HEAD_DIM)).astype(np.float64)
    first = (np.arange(LANES) % (2 * n_freq)) < n_freq
    cos, sin = np.cos(ang), np.sin(ang)
    tables = (cos, np.where(first, -sin, 0.0), np.where(first, 0.0, sin))
    return tuple(jnp.asarray(tab.astype(np.float32)) for tab in tables)


class _Bf16Weights:
    def __init__(self, **stacks):
        self.stacks, self.ready = stacks, {}

    def hosted(self, keys, call):
        due = [(name, slab) for name, slab in keys
               if slab < self.stacks[name].shape[0] and (name, slab) not in self.ready]
        outs = call([(self.stacks[name], slab) for name, slab in due])
        for key, cast in zip(due, outs[len(outs) - len(due):]):
            self.ready[key] = cast
        return outs[:len(outs) - len(due)]

    def get(self, name, slab):
        if (name, slab) not in self.ready:
            self.ready[(name, slab)] = _to_bf16(self.stacks[name][slab:slab + 1])[0]
        return self.ready[(name, slab)]


def kernel(x_prompt, x_sample, cache_k, cache_v, c, c_ctx, ada_w, ada_b, norm_mix_g, norm_ffn_g, w_in_even,
           q_norm_g, k_norm_g, sink_logit, short_conv_w, w_out_even, w_in_odd, gmlp_norm_g, w_spatial,
           b_spatial, w_out_odd, w_up, ffn_conv_w, w_down):
    n_p, n_s = BATCH * SEQ, DEC_BATCH * DEC_SEQ
    xp = x_prompt.reshape(n_p, D_MODEL)
    xs = x_sample.reshape(n_s, D_MODEL)
    cond = jnp.concatenate([c, c_ctx[None, :], jnp.zeros((COND_ROWS - DEC_BATCH - 1, D_MODEL), F32)], axis=0)
    bf = _Bf16Weights(in_even=w_in_even, out_even=w_out_even, in_odd=w_in_odd, out_odd=w_out_odd,
                      spatial=w_spatial.reshape(N_ODD, GMLP_GROUPS * CHUNK, CHUNK), up=w_up, down=w_down)

    mod_l, = bf.hosted([("in_even", 0)], lambda side: _adaln(cond, ada_w, ada_b, side))
    rope_tabs = _rope_tables(DEC_SEQ)

    tm_p, tm_s = 1024, 1024
    row_vec = lambda a: a.reshape(1, -1)
    new_k, new_v = [], []
    for l in range(DEPTH):
        base = l * COND_ROWS
        row_p = lambda i, base=base: base + CTX_ROW
        row_s = lambda i, base=base: base + i // (DEC_SEQ // tm_s)
        g_mix = row_vec(norm_mix_g[l])
        nxt = (l + 1) // 2
        if l % 2 == 0:
            e = l // 2
            w_in = bf.get("in_even", e)
            qg = row_vec(jnp.tile(q_norm_g[e], N_HEADS))
            kg = row_vec(jnp.tile(k_norm_g[e], N_KV_HEADS))
            sink = jnp.repeat(sink_logit[e], BLOCK).reshape(1, N_HEADS * BLOCK)
            cw = short_conv_w[e]
            qp, kp, vp, scp = bf.hosted([("out_even", e)], lambda side: _in_even(
                xp, mod_l, row_p, g_mix, w_in, qg, kg, cw, None, tm_p, SEQ, side))
            new_k.append(kp.reshape(BATCH, SEQ, N_KV_HEADS, HEAD_DIM))
            new_v.append(vp.reshape(BATCH, SEQ, N_KV_HEADS, HEAD_DIM))
            w_out = bf.get("out_even", e)
            odd_next = [("in_odd", nxt), ("out_odd", nxt), ("spatial", nxt)]
            xp, = bf.hosted(odd_next, lambda side: _attn_ctx(qp, kp, vp, scp, xp, mod_l, base, sink, w_out, side))
            qs, ks, vs, scs = bf.hosted([("down", l)], lambda side: _in_even(
                xs, mod_l, row_s, g_mix, w_in, qg, kg, cw, rope_tabs, tm_s, DEC_SEQ, side))
            ck = cache_k[:, e].reshape(DEC_BATCH, PAST_LEN, KV_WIDTH)
            cv = cache_v[:, e].reshape(DEC_BATCH, PAST_LEN, KV_WIDTH)
            xs, = bf.hosted([("up", l)], lambda side: _attn_win(
                qs, ks, vs, ck, cv, scs, xs, mod_l, base, sink, w_out, side))
        else:
            o = l // 2
            w_in, w_out = bf.get("in_odd", o), bf.get("out_odd", o)
            vg = row_vec(gmlp_norm_g[o])
            ws = bf.get("spatial", o).reshape(GMLP_GROUPS, CHUNK, CHUNK)
            bs_full = jnp.broadcast_to(b_spatial[o][:, :, None], (GMLP_GROUPS, CHUNK, GMLP_GROUP_DIM))
            xp, = bf.hosted([("down", l)], lambda side: _gmlp(
                xp, mod_l, row_p, g_mix, w_in, vg, ws, bs_full, w_out, tm_p, side))
            xs, = bf.hosted([("up", l)], lambda side: _gmlp(
                xs, mod_l, row_s, g_mix, w_in, vg, ws, bs_full, w_out, tm_s, side))
        g_ffn = row_vec(norm_ffn_g[l])
        w_up_l, w_down_l = bf.get("up", l), bf.get("down", l)
        xp, = _ffn(xp, mod_l, row_p, g_ffn, w_up_l, ffn_conv_w, w_down_l, l, tm_p, SEQ)
        xs, = _ffn(xs, mod_l, row_s, g_ffn, w_up_l, ffn_conv_w, w_down_l, l, tm_s, DEC_SEQ)
    per_layer = lambda parts: parts[0][:, None] if len(parts) == 1 else jnp.stack(parts, axis=1)
    return (xp.reshape(BATCH, SEQ, D_MODEL), xs.reshape(DEC_BATCH, DEC_SEQ, D_MODEL),
            per_layer(new_k), per_layer(new_v))
```

```python
import functools

import jax
import jax.numpy as jnp
import numpy as np
from jax import lax
from jax.experimental import pallas as pl
from jax.experimental.pallas import tpu as pltpu

F32 = jnp.float32
BF16 = jnp.bfloat16

D_MODEL = 1024
BATCH = 16
SEQ = 256
DEPTH = 2
DEC_BATCH = 4
DEC_SEQ = 2048
PAST_LEN = 512
GRID_W = 64
N_HEADS = 8
N_KV_HEADS = 2
HEAD_DIM = 64
GQA_GROUP = N_HEADS // N_KV_HEADS
ATTN_WIDTH = N_HEADS * HEAD_DIM
KV_WIDTH = N_KV_HEADS * HEAD_DIM
WINDOW = 128
BLOCK = 128
ROPE_BASE = 10000.0
CONV_WIDTH = 512
CHUNK = 128
GMLP_WIDTH = 1024
GMLP_GROUPS = 8
GMLP_GROUP_DIM = GMLP_WIDTH // GMLP_GROUPS
D_FF = 2816
EPS = 1e-6
NEG_INF = -1e30
N_EVEN = (DEPTH + 1) // 2
N_ODD = DEPTH // 2
LOG2_E = 1.4426950408889634

LANES = 128
SUBLANES_F32 = 8
SUBLANES_BF16 = 16
MXU_WIDTH = 256
VMEM_LIMIT_BYTES = 56 * 1024 * 1024

COND_ROWS = 8
CTX_ROW = DEC_BATCH
TOKEN_HALO = SUBLANES_BF16
FFN_COLS = MXU_WIDTH
LEAD_PARTS = 4
CAST_BLOCK_BYTES = 3 * 1024 * 1024
ADA_TN = 1536
ATTN_Q_BLOCKS = 4
ATTN_CTX_SEQS = 2
SUM_ROWS = SUBLANES_BF16
SCORE_CAP = 3.0e38


def _params(n_axes):
    return pltpu.CompilerParams(dimension_semantics=("parallel",) * n_axes,
                                vmem_limit_bytes=VMEM_LIMIT_BYTES)


def _resident(shape):
    zeros = (0,) * len(shape)
    return pl.BlockSpec(shape, lambda *_: zeros, pipeline_mode=pl.Buffered(1))


def _layer_slab(shape, layer):
    return pl.BlockSpec((1,) + tuple(shape[1:]), lambda *_: (layer, 0, 0), pipeline_mode=pl.Buffered(1))


def _mod_spec(mod_row):
    return pl.BlockSpec((1, 1, 6 * D_MODEL), lambda *idx: (mod_row(*idx), 0, 0))


def _dot(a, b):
    return jnp.dot(a, b, preferred_element_type=F32)


def _cast_kernel(x_ref, o_ref):
    o_ref[...] = x_ref[...].astype(BF16)


def _to_bf16(w):
    n, rows, cols = w.shape
    fits = [r for r in range(SUBLANES_BF16, rows + 1, SUBLANES_BF16)
            if rows % r == 0 and r * cols * 4 <= CAST_BLOCK_BYTES]
    tr = max(fits)
    spec = pl.BlockSpec((1, tr, cols), lambda i, j: (i, j, 0))
    return pl.pallas_call(
        _cast_kernel, grid=(n, rows // tr), in_specs=[spec], out_specs=spec,
        out_shape=jax.ShapeDtypeStruct(w.shape, BF16), compiler_params=_params(2), name="to_bf16",
    )(w)


def _side_cast_specs(side, n_steps, step_of):
    w, layer = side
    _, rows, cols = w.shape
    tr = rows // n_steps
    assert tr * n_steps == rows and tr % SUBLANES_BF16 == 0
    return (pl.BlockSpec((1, tr, cols), lambda *idx: (layer, step_of(*idx), 0)),
            pl.BlockSpec((tr, cols), lambda *idx: (step_of(*idx), 0)),
            jax.ShapeDtypeStruct((rows, cols), BF16))


def _split_refs(rest, n_out, side_cast):
    n = side_cast
    side_ins, outs, side_outs, scratch = rest[:n], rest[n:n + n_out], rest[n + n_out:2 * n + n_out], rest[2 * n + n_out:]
    for side_in, side_out in zip(side_ins, side_outs):
        side_out[...] = side_in[0].astype(BF16)
    return outs, scratch


def _host_call(kernel, *, grid, in_specs, args, out_specs, out_shape, scratch_shapes, name, side, step_of):
    in_specs, args, out_specs, out_shape = list(in_specs), list(args), list(out_specs), list(out_shape)
    n_steps = 1
    for extent in grid:
        n_steps *= extent
    sides = [_side_cast_specs(one, n_steps, step_of) for one in side]
    in_specs += [spec for spec, _, _ in sides]
    args += [stack for stack, _ in side]
    out_specs += [spec for _, spec, _ in sides]
    out_shape += [shape for _, _, shape in sides]
    return pl.pallas_call(
        functools.partial(kernel, side_cast=len(side)),
        grid=grid, in_specs=in_specs, out_specs=out_specs, out_shape=out_shape, scratch_shapes=scratch_shapes,
        compiler_params=_params(len(grid)), name=name,
    )(*args)


def _modulate(x, g, shift, scale):
    ms = jnp.mean(x * x, axis=-1, keepdims=True)
    return (x * lax.rsqrt(ms + EPS) * g) * (1.0 + scale) + shift


def _mod_chunk(mod_ref, k):
    return mod_ref[0, :, k * D_MODEL:(k + 1) * D_MODEL]


def _halo_rows(tiles_per_seq):
    return TOKEN_HALO if tiles_per_seq > 1 else 0


def _token_specs(T, tm, tiles_per_seq):
    tok = pl.BlockSpec((tm, D_MODEL), lambda i: (i, 0))
    if tiles_per_seq == 1:
        return [tok]
    per_tile = tm // TOKEN_HALO
    last = T // TOKEN_HALO - 1
    prev = pl.BlockSpec((TOKEN_HALO, D_MODEL), lambda i: (jnp.maximum(i * per_tile - 1, 0), 0))
    nxt = pl.BlockSpec((TOKEN_HALO, D_MODEL), lambda i: (jnp.minimum((i + 1) * per_tile, last), 0))
    return [prev, tok, nxt]


def _modulated_parts(x_refs, h_scr, g, shift, scale, tiles_per_seq):
    halo = _halo_rows(tiles_per_seq)
    x_ref = x_refs[len(x_refs) // 2]
    tm = x_ref.shape[0]
    step = tm // LEAD_PARTS
    parts = []
    for p in range(LEAD_PARTS):
        lo, hi = halo + p * step, halo + (p + 1) * step
        piece = _modulate(x_ref[p * step:(p + 1) * step, :], g, shift, scale).astype(BF16)
        if halo and p == 0:
            pos = pl.program_id(0) % tiles_per_seq
            edge = jnp.where(pos > 0, _modulate(x_refs[0][...], g, shift, scale), 0.0).astype(BF16)
            piece, lo = jnp.concatenate([edge, piece], axis=0), 0
        if halo and p == LEAD_PARTS - 1:
            pos = pl.program_id(0) % tiles_per_seq
            edge = jnp.where(pos < tiles_per_seq - 1, _modulate(x_refs[2][...], g, shift, scale), 0.0).astype(BF16)
            piece, hi = jnp.concatenate([piece, edge], axis=0), tm + 2 * halo
        h_scr[lo:hi, :] = piece
        parts.append(piece)
    return parts


def _dot_rows(parts, w):
    return jnp.concatenate([_dot(p, w) for p in parts], axis=0)


def _seq_end_masks(tm, cols, seq_len, tiles_per_seq):
    if tiles_per_seq > 1:
        return None
    seq_row = lax.broadcasted_iota(jnp.int32, (tm, cols), 0) % seq_len
    return seq_row != 0, seq_row != seq_len - 1


def _token_conv3(z, w, tm, halo, masks):
    rows = z.shape[0]
    mid = slice(halo, halo + tm)
    dn = pltpu.roll(z, 1, 0)[mid]
    up = pltpu.roll(z, rows - 1, 0)[mid]
    if masks is not None:
        dn = jnp.where(masks[0], dn, 0.0)
        up = jnp.where(masks[1], up, 0.0)
    return dn * w[0:1] + z[mid] * w[1:2] + up * w[2:3]


def _adaln_kernel(cond_ref, w_ref, b_ref, *rest, side_cast):
    (o_ref,), _ = _split_refs(rest, 1, side_cast)
    a = jax.nn.silu(cond_ref[...]).astype(BF16)
    rows = _dot(a, w_ref[0].astype(BF16)) + b_ref[0]
    for r in range(COND_ROWS):
        o_ref[r] = rows[r:r + 1, :]


def _adaln(cond, ada_w, ada_b, side=()):
    n_out = 6 * D_MODEL
    n_col = n_out // ADA_TN
    return _host_call(
        _adaln_kernel,
        grid=(DEPTH, n_col),
        in_specs=[pl.BlockSpec((COND_ROWS, D_MODEL), lambda l, j: (0, 0)),
                  pl.BlockSpec((1, D_MODEL, ADA_TN), lambda l, j: (l, 0, j)),
                  pl.BlockSpec((1, 1, ADA_TN), lambda l, j: (l, 0, j))],
        args=[cond, ada_w, ada_b.reshape(DEPTH, 1, n_out)],
        out_specs=[pl.BlockSpec((COND_ROWS, 1, ADA_TN), lambda l, j: (l, 0, j))],
        out_shape=[jax.ShapeDtypeStruct((DEPTH * COND_ROWS, 1, n_out), F32)],
        scratch_shapes=[], name="adaln", side=side, step_of=lambda l, j: l * n_col + j)


def _head_rms(z, gain):
    n = z.shape[1]
    w = min(n, MXU_WIDTH)
    r = lax.broadcasted_iota(jnp.int32, (w, w), 0) // HEAD_DIM
    c = lax.broadcasted_iota(jnp.int32, (w, w), 1) // HEAD_DIM
    ones = (r == c).astype(BF16)
    sq = (z * z).astype(BF16)
    ss = jnp.concatenate([_dot(sq[:, k:k + w], ones) for k in range(0, n, w)], axis=1)
    return z * lax.rsqrt(ss * (1.0 / HEAD_DIM) + EPS) * gain


def _rope(z, cos, sin_lo, sin_hi):
    outs = []
    for k in range(0, z.shape[1], LANES):
        blk = z[:, k:k + LANES]
        outs.append(blk * cos + pltpu.roll(blk, LANES - 16, 1) * sin_lo + pltpu.roll(blk, 16, 1) * sin_hi)
    return jnp.concatenate(outs, axis=1)


def _in_even_kernel(*refs, rope, tiles_per_seq, seq_len, cache_out, side_cast):
    halo = _halo_rows(tiles_per_seq)
    n_x = 3 if halo else 1
    x_refs, refs = refs[:n_x], refs[n_x:]
    mod_ref, g_ref, w_ref, qg_ref, kg_ref, cw_ref = refs[:6]
    refs = refs[6:]
    if rope:
        cos_ref, slo_ref, shi_ref = refs[:3]
        refs = refs[3:]
    outs, (h_scr,) = _split_refs(refs, 6 if cache_out else 4, side_cast)
    q_ref, k_ref, v_ref, sc_ref = outs[:4]
    tm = q_ref.shape[0]
    parts = _modulated_parts(x_refs, h_scr, g_ref[...], _mod_chunk(mod_ref, 0), _mod_chunk(mod_ref, 1),
                             tiles_per_seq)
    c0 = 0
    c1 = c0 + ATTN_WIDTH
    c2 = c1 + 2 * KV_WIDTH
    c3 = c2 + CONV_WIDTH
    c4 = c3 + CONV_WIDTH
    c5 = c4 + CONV_WIDTH
    zq = _dot_rows(parts, w_ref[:, c0:c1])[halo:halo + tm]
    h_all = h_scr[...]
    h = h_scr[halo:halo + tm, :]
    q = _head_rms(zq, qg_ref[...] * (HEAD_DIM ** -0.5 * LOG2_E))
    zkv = _dot(h, w_ref[:, c1:c2])
    k = _head_rms(zkv[:, :KV_WIDTH], kg_ref[...])
    if rope:
        cos, slo, shi = cos_ref[...], slo_ref[...], shi_ref[...]
        q = _rope(q, cos, slo, shi)
        k = _rope(k, cos, slo, shi)
    v = zkv[:, KV_WIDTH:]
    q_ref[...] = q
    k_ref[...] = k
    v_ref[...] = v
    if cache_out:
        for cache_ref, rows in zip(outs[4:], (k, v)):
            per_seq = rows.reshape(tm // seq_len, seq_len, KV_WIDTH)
            for head in range(N_KV_HEADS):
                cache_ref[:, 0, :, head, :] = per_seq[:, :, head * HEAD_DIM:(head + 1) * HEAD_DIM]
    ch = _dot(h_all, w_ref[:, c3:c4]) * _dot(h_all, w_ref[:, c4:c5])
    conv = _token_conv3(ch, cw_ref[...], tm, halo, _seq_end_masks(tm, CONV_WIDTH, seq_len, tiles_per_seq))
    sc_ref[...] = (_dot(h, w_ref[:, c2:c3]) * conv).astype(BF16)


def _in_even(x, mod_l, mod_row, g, w_bf, qg, kg, conv_w, rope_tabs, tm, seq_len, side=(), cache_out=False):
    T = x.shape[0]
    n_in = w_bf.shape[1]
    tiles_per_seq = max(seq_len // tm, 1)
    assert tm % seq_len == 0 or seq_len % tm == 0
    tok = lambda w: pl.BlockSpec((tm, w), lambda i: (i, 0))
    x_specs = _token_specs(T, tm, tiles_per_seq)
    in_specs = x_specs + [_mod_spec(mod_row), _resident((1, D_MODEL)), _resident((D_MODEL, n_in)),
                          _resident((1, ATTN_WIDTH)), _resident((1, KV_WIDTH)), _resident((3, CONV_WIDTH))]
    args = [x] * len(x_specs) + [mod_l, g, w_bf, qg, kg, conv_w]
    if rope_tabs is not None:
        in_specs += [pl.BlockSpec((tm, LANES), lambda i: (i % tiles_per_seq, 0))] * 3
        args += list(rope_tabs)
    outs = ((ATTN_WIDTH, F32), (KV_WIDTH, F32), (KV_WIDTH, F32), (CONV_WIDTH, BF16))
    out_specs = [tok(w) for w, _ in outs]
    out_shape = [jax.ShapeDtypeStruct((T, w), dt) for w, dt in outs]
    if cache_out:
        n_seq = tm // seq_len
        cache_block = (n_seq, 1, seq_len, N_KV_HEADS, HEAD_DIM)
        out_specs += [pl.BlockSpec(cache_block, lambda i: (i, 0, 0, 0, 0))] * 2
        out_shape += [jax.ShapeDtypeStruct((T // seq_len,) + cache_block[1:], F32)] * 2
    scratch = [pltpu.VMEM((tm + 2 * _halo_rows(tiles_per_seq), D_MODEL), BF16)]
    return _host_call(
        functools.partial(_in_even_kernel, rope=rope_tabs is not None, tiles_per_seq=tiles_per_seq, seq_len=seq_len,
                          cache_out=cache_out),
        grid=(T // tm,), in_specs=in_specs, args=args, out_specs=out_specs, out_shape=out_shape,
        scratch_shapes=scratch, name="in_even", side=side, step_of=lambda i: i)


def _stack_heads(q):
    lane_lo = lax.broadcasted_iota(jnp.int32, (BLOCK, LANES), 1) < HEAD_DIM
    heads = []
    for pair in range(N_HEADS // 2):
        kv = (2 * pair) // GQA_GROUP
        qp = q[:, pair * LANES:(pair + 1) * LANES]
        qr = pltpu.roll(qp, HEAD_DIM, 1)
        for half in range(2):
            src = qp if half == kv else qr
            heads.append(jnp.where(lane_lo, src, 0.0) if kv == 0 else jnp.where(lane_lo, 0.0, src))
    return jnp.concatenate(heads, axis=0).astype(BF16)


def _transposed_values(v_blocks):
    vt = jnp.concatenate([v[j:j + BLOCK].T for v in v_blocks for j in range(0, v.shape[0], BLOCK)], axis=1)
    return jnp.concatenate([vt.astype(BF16), jnp.ones((SUM_ROWS, vt.shape[1]), BF16)], axis=0)


def _scores(q, kcat, cap):
    st = lax.dot_general(kcat, _stack_heads(q), (((1,), (1,)), ((), ())), preferred_element_type=F32)
    if cap is None:
        return st
    n_band = cap.shape[0]
    capped = jnp.minimum(st[:n_band], jnp.concatenate([cap] * N_HEADS, axis=1))
    return jnp.concatenate([capped, st[n_band:]], axis=0)


def _weighted_values(st, vt, sink_row):
    m = jnp.maximum(jnp.max(st, axis=0, keepdims=True), sink_row)
    pt = jnp.exp2(st - m).astype(BF16)
    ot = _dot(vt, pt)
    denom = ot[KV_WIDTH:KV_WIDTH + 1] + jnp.exp2(sink_row - m)
    ot = ot[:KV_WIDTH] / denom
    pairs = []
    for pair in range(N_HEADS // 2):
        kv = (2 * pair) // GQA_GROUP
        dims = slice(kv * HEAD_DIM, (kv + 1) * HEAD_DIM)
        both = [ot[dims, (2 * pair + half) * BLOCK:(2 * pair + half + 1) * BLOCK] for half in range(2)]
        pairs.append(jnp.concatenate(both, axis=0).T)
    return jnp.concatenate(pairs, axis=1)


def _attend_blocks(n_blocks, scores_of, values_of, sink_row, attn_scr):
    st = scores_of(0)
    for j in range(n_blocks):
        st_next = scores_of(j + 1) if j + 1 < n_blocks else None
        attn_scr[j * BLOCK:(j + 1) * BLOCK, :] = _weighted_values(st, values_of(j), sink_row).astype(BF16)
        st = st_next


def _project_out(attn_scr, sc_ref, x_ref, mod_ref, w_ref, o_ref):
    mix = _dot(attn_scr[...], w_ref[:ATTN_WIDTH, :]) + _dot(sc_ref[...], w_ref[ATTN_WIDTH:, :])
    o_ref[...] = x_ref[...] + _mod_chunk(mod_ref, 2) * mix


def _attn_ctx_kernel(sink_ref, q_ref, k_ref, v_ref, sc_ref, x_ref, mod_ref, w_ref, *rest, side_cast):
    (o_ref,), (attn_scr,) = _split_refs(rest, 1, side_cast)
    per_seq = SEQ // BLOCK
    kcats = [k_ref[s * SEQ:(s + 1) * SEQ, :].astype(BF16) for s in range(ATTN_CTX_SEQS)]
    vts = [_transposed_values([v_ref[s * SEQ:(s + 1) * SEQ, :]]) for s in range(ATTN_CTX_SEQS)]
    sink_row = sink_ref[...] * LOG2_E
    _attend_blocks(ATTN_CTX_SEQS * per_seq,
                   lambda j: _scores(q_ref[j * BLOCK:(j + 1) * BLOCK, :], kcats[j // per_seq], None),
                   lambda j: vts[j // per_seq], sink_row, attn_scr)
    _project_out(attn_scr, sc_ref, x_ref, mod_ref, w_ref, o_ref)


def _attn_ctx(q, k, v, sc, x, mod_l, mod_base, sink_row, w_out_bf, side=()):
    T = q.shape[0]
    tq = ATTN_CTX_SEQS * SEQ
    tok = lambda w: pl.BlockSpec((tq, w), lambda i: (i, 0))
    return _host_call(
        _attn_ctx_kernel,
        grid=(T // tq,),
        in_specs=[_resident((1, N_HEADS * BLOCK)), tok(ATTN_WIDTH), tok(KV_WIDTH), tok(KV_WIDTH),
                  tok(CONV_WIDTH), tok(D_MODEL), _mod_spec(lambda i: mod_base + CTX_ROW),
                  _resident((ATTN_WIDTH + CONV_WIDTH, D_MODEL))],
        args=[sink_row, q, k, v, sc, x, mod_l, w_out_bf],
        out_specs=[tok(D_MODEL)], out_shape=[jax.ShapeDtypeStruct((T, D_MODEL), F32)],
        scratch_shapes=[pltpu.VMEM((tq, ATTN_WIDTH), BF16)], name="attn_ctx", side=side, step_of=lambda i: i)


def _band_cap(has_prev, has_next):
    c = lax.broadcasted_iota(jnp.int32, (3 * BLOCK, BLOCK), 0)
    r = lax.broadcasted_iota(jnp.int32, (3 * BLOCK, BLOCK), 1)
    first_prev = r + jnp.where(has_prev, 0, BLOCK)
    last_next = r + 2 * BLOCK - jnp.where(has_next, 0, BLOCK)
    masked = ((c < BLOCK) & (c < first_prev)) | ((c >= 2 * BLOCK) & (c > last_next))
    return jnp.where(masked, NEG_INF, SCORE_CAP)


def _attn_win_kernel(sink_ref, q_ref, kp_ref, kc_ref, kn_ref, vp_ref, vc_ref, vn_ref, ck_ref, cv_ref,
                     sc_ref, x_ref, mod_ref, w_ref, *rest, n_steps, side_cast):
    (o_ref,), (attn_scr,) = _split_refs(rest, 1, side_cast)
    i = pl.program_id(1)
    kc, vc = kc_ref[...], vc_ref[...]
    inner = range(0, ATTN_Q_BLOCKS * BLOCK, BLOCK)
    k_blocks = [kp_ref[...]] + [kc[j:j + BLOCK] for j in inner] + [kn_ref[...]]
    v_blocks = [vp_ref[...]] + [vc[j:j + BLOCK] for j in inner] + [vn_ref[...]]
    sink_row = sink_ref[...] * LOG2_E

    def scores_of(j):
        kcat = jnp.concatenate(k_blocks[j:j + 3] + [ck_ref[0]], axis=0).astype(BF16)
        cap = _band_cap(i > 0 if j == 0 else True, i < n_steps - 1 if j == ATTN_Q_BLOCKS - 1 else True)
        return _scores(q_ref[j * BLOCK:(j + 1) * BLOCK, :], kcat, cap)

    _attend_blocks(ATTN_Q_BLOCKS, scores_of, lambda j: _transposed_values(v_blocks[j:j + 3] + [cv_ref[0]]),
                   sink_row, attn_scr)
    _project_out(attn_scr, sc_ref, x_ref, mod_ref, w_ref, o_ref)


def _attn_win(q, k, v, ck, cv, sc, x, mod_l, mod_base, sink_row, w_out_bf, side=()):
    T = q.shape[0]
    tq = ATTN_Q_BLOCKS * BLOCK
    n_steps = DEC_SEQ // tq
    nb = DEC_SEQ // BLOCK
    cur = lambda w: pl.BlockSpec((tq, w), lambda b, i: (b * n_steps + i, 0))
    prev = pl.BlockSpec((BLOCK, KV_WIDTH), lambda b, i: (b * nb + jnp.maximum(ATTN_Q_BLOCKS * i - 1, 0), 0))
    nxt = pl.BlockSpec((BLOCK, KV_WIDTH),
                       lambda b, i: (b * nb + jnp.minimum(ATTN_Q_BLOCKS * (i + 1), nb - 1), 0))
    ctx = pl.BlockSpec((1, PAST_LEN, KV_WIDTH), lambda b, i: (b, 0, 0))
    in_specs = [_resident((1, N_HEADS * BLOCK)), cur(ATTN_WIDTH),
                prev, cur(KV_WIDTH), nxt, prev, cur(KV_WIDTH), nxt, ctx, ctx,
                cur(CONV_WIDTH), cur(D_MODEL), _mod_spec(lambda b, i: mod_base + b),
                _resident((ATTN_WIDTH + CONV_WIDTH, D_MODEL))]
    args = [sink_row, q, k, k, k, v, v, v, ck, cv, sc, x, mod_l, w_out_bf]
    return _host_call(
        functools.partial(_attn_win_kernel, n_steps=n_steps),
        grid=(T // DEC_SEQ, n_steps), in_specs=in_specs, args=args,
        out_specs=[cur(D_MODEL)], out_shape=[jax.ShapeDtypeStruct((T, D_MODEL), F32)],
        scratch_shapes=[pltpu.VMEM((tq, ATTN_WIDTH), BF16)], name="attn_win", side=side,
        step_of=lambda b, i: b * n_steps + i)


def _gelu_tanh(x):
    k = 0.7978845608028654
    half = 0.5 * x
    return half + half * jnp.tanh(x * (k + (k * 0.044715) * (x * x)))


def _gmlp_kernel(x_ref, mod_ref, g_ref, win_ref, vg_ref, ws_ref, bs_ref, wout_ref, *rest, side_cast):
    (o_ref,), (gated_scr, h_scr) = _split_refs(rest, 1, side_cast)
    n_chunks = x_ref.shape[0] // CHUNK
    parts = _modulated_parts([x_ref], h_scr, g_ref[...], _mod_chunk(mod_ref, 0), _mod_chunk(mod_ref, 1), 1)
    u = _gelu_tanh(_dot_rows(parts, win_ref[:, :GMLP_WIDTH]))
    v = _gelu_tanh(_dot(h_scr[...], win_ref[:, GMLP_WIDTH:]))
    ms = jnp.mean(v * v, axis=-1, keepdims=True)
    v = (v * lax.rsqrt(ms + EPS) * vg_ref[...]).astype(BF16)
    for grp in range(GMLP_GROUPS):
        lanes = slice(grp * GMLP_GROUP_DIM, (grp + 1) * GMLP_GROUP_DIM)
        rhs = jnp.concatenate([v[n * CHUNK:(n + 1) * CHUNK, lanes] for n in range(n_chunks)], axis=1)
        s = _dot(ws_ref[grp], rhs)
        for n in range(n_chunks):
            rows = slice(n * CHUNK, (n + 1) * CHUNK)
            s_n = s[:, n * GMLP_GROUP_DIM:(n + 1) * GMLP_GROUP_DIM] + bs_ref[grp]
            gated_scr[rows, lanes] = (u[rows, lanes] * s_n).astype(BF16)
    o_ref[...] = x_ref[...] + _mod_chunk(mod_ref, 2) * _dot(gated_scr[...], wout_ref[...])


def _gmlp(x, mod_l, mod_row, g, win_bf, vg, ws_bf, bs_full, wout_bf, tm, side=()):
    T = x.shape[0]
    tok = pl.BlockSpec((tm, D_MODEL), lambda i: (i, 0))
    in_specs = [tok, _mod_spec(mod_row),
                _resident((1, D_MODEL)), _resident((D_MODEL, 2 * GMLP_WIDTH)), _resident((1, GMLP_WIDTH)),
                _resident((GMLP_GROUPS, CHUNK, CHUNK)), _resident((GMLP_GROUPS, CHUNK, GMLP_GROUP_DIM)),
                _resident((GMLP_WIDTH, D_MODEL))]
    args = [x, mod_l, g, win_bf, vg, ws_bf, bs_full, wout_bf]
    return _host_call(
        _gmlp_kernel, grid=(T // tm,), in_specs=in_specs, args=args,
        out_specs=[tok], out_shape=[jax.ShapeDtypeStruct((T, D_MODEL), F32)],
        scratch_shapes=[pltpu.VMEM((tm, GMLP_WIDTH), BF16), pltpu.VMEM((tm, D_MODEL), BF16)],
        name="gmlp", side=side, step_of=lambda i: i)


def _ffn_kernel(*refs, tiles_per_seq, seq_len, side_cast):
    halo = _halo_rows(tiles_per_seq)
    n_x = 3 if halo else 1
    x_refs, refs = refs[:n_x], refs[n_x:]
    mod_ref, g_ref, wup_ref, cw_ref, wdn_ref = refs[:5]
    (o_ref,), (act_scr, h_scr) = _split_refs(refs[5:], 1, side_cast)
    tm = o_ref.shape[0]
    parts = _modulated_parts(x_refs, h_scr, g_ref[...], _mod_chunk(mod_ref, 3), _mod_chunk(mod_ref, 4),
                             tiles_per_seq)
    masks = _seq_end_masks(tm, FFN_COLS, seq_len, tiles_per_seq)
    for j in range(0, D_FF, FFN_COLS):
        gate_cols = slice(j, j + FFN_COLS)
        val_cols = slice(D_FF + j, D_FF + j + FFN_COLS)
        if j == 0:
            up = lambda cols: _dot_rows(parts, wup_ref[:, cols])
        else:
            h = h_scr[...]
            up = lambda cols: _dot(h, wup_ref[:, cols])
        zg = _token_conv3(up(gate_cols), cw_ref[0, :, gate_cols], tm, halo, masks)
        zv = _token_conv3(up(val_cols), cw_ref[0, :, val_cols], tm, halo, masks)
        act_scr[:, gate_cols] = (jax.nn.silu(zg) * zv).astype(BF16)
    x = x_refs[n_x // 2][...]
    o_ref[...] = x + _mod_chunk(mod_ref, 5) * _dot(act_scr[...], wdn_ref[...])


def _ffn(x, mod_l, mod_row, g, wup_bf, conv_w, wdn_bf, layer, tm, seq_len, side=()):
    T = x.shape[0]
    tiles_per_seq = max(seq_len // tm, 1)
    assert tm % seq_len == 0 or seq_len % tm == 0
    x_specs = _token_specs(T, tm, tiles_per_seq)
    in_specs = x_specs + [_mod_spec(mod_row), _resident((1, D_MODEL)), _resident(wup_bf.shape),
                          _layer_slab(conv_w.shape, layer), _resident(wdn_bf.shape)]
    scratch = [pltpu.VMEM((tm, D_FF), BF16), pltpu.VMEM((tm + 2 * _halo_rows(tiles_per_seq), D_MODEL), BF16)]
    return _host_call(
        functools.partial(_ffn_kernel, tiles_per_seq=tiles_per_seq, seq_len=seq_len),
        grid=(T // tm,), in_specs=in_specs, args=[x] * len(x_specs) + [mod_l, g, wup_bf, conv_w, wdn_bf],
        out_specs=[pl.BlockSpec((tm, D_MODEL), lambda i: (i, 0))], out_shape=[jax.ShapeDtypeStruct((T, D_MODEL), F32)],
        scratch_shapes=scratch, name="conv_ffn", side=side, step_of=lambda i: i)


def _rope_tables(n_tokens):
    t = np.arange(n_tokens)
    n_freq = HEAD_DIM // 4
    inv = (ROPE_BASE ** (-np.arange(n_freq, dtype=np.float32) / n_freq)).astype(np.float32)
    row_ang = (t // GRID_W).astype(np.float32)[:, None] * inv
    col_ang = (t % GRID_W).astype(np.float32)[:, None] * inv
    ang = np.concatenate([row_ang, row_ang, col_ang, col_ang], axis=1)
    ang = np.tile(ang, (1, LANES // HEAD_DIM)).astype(np.float64)
    first = (np.arange(LANES) % (2 * n_freq)) < n_freq
    cos, sin = np.cos(ang), np.sin(ang)
    tables = (cos, np.where(first, -sin, 0.0), np.where(first, 0.0, sin))
    return tuple(jnp.asarray(tab.astype(np.float32)) for tab in tables)


class _Bf16Weights:
    def __init__(self, **stacks):
        self.stacks, self.ready = stacks, {}

    def hosted(self, keys, call):
        due = [(name, slab) for name, slab in keys
               if slab < self.stacks[name].shape[0] and (name, slab) not in self.ready]
        outs = call([(self.stacks[name], slab) for name, slab in due])
        for key, cast in zip(due, outs[len(outs) - len(due):]):
            self.ready[key] = cast
        return outs[:len(outs) - len(due)]

    def get(self, name, slab):
        if (name, slab) not in self.ready:
            self.ready[(name, slab)] = _to_bf16(self.stacks[name][slab:slab + 1])[0]
        return self.ready[(name, slab)]


def kernel(x_prompt, x_sample, cache_k, cache_v, c, c_ctx, ada_w, ada_b, norm_mix_g, norm_ffn_g, w_in_even,
           q_norm_g, k_norm_g, sink_logit, short_conv_w, w_out_even, w_in_odd, gmlp_norm_g, w_spatial,
           b_spatial, w_out_odd, w_up, ffn_conv_w, w_down):
    n_p, n_s = BATCH * SEQ, DEC_BATCH * DEC_SEQ
    xp = x_prompt.reshape(n_p, D_MODEL)
    xs = x_sample.reshape(n_s, D_MODEL)
    cond = jnp.concatenate([c, c_ctx[None, :], jnp.zeros((COND_ROWS - DEC_BATCH - 1, D_MODEL), F32)], axis=0)
    bf = _Bf16Weights(in_even=w_in_even, out_even=w_out_even, in_odd=w_in_odd, out_odd=w_out_odd,
                      spatial=w_spatial.reshape(N_ODD, GMLP_GROUPS * CHUNK, CHUNK), up=w_up, down=w_down)

    mod_l, = bf.hosted([("in_even", 0)], lambda side: _adaln(cond, ada_w, ada_b, side))
    rope_tabs = _rope_tables(DEC_SEQ)

    tm_p, tm_s = 1024, 1024
    row_vec = lambda a: a.reshape(1, -1)
    new_k, new_v = [], []
    for l in range(DEPTH):
        base = l * COND_ROWS
        row_p = lambda i, base=base: base + CTX_ROW
        row_s = lambda i, base=base: base + i // (DEC_SEQ // tm_s)
        g_mix = row_vec(norm_mix_g[l])
        nxt = (l + 1) // 2
        if l % 2 == 0:
            e = l // 2
            w_in = bf.get("in_even", e)
            qg = row_vec(jnp.tile(q_norm_g[e], N_HEADS))
            kg = row_vec(jnp.tile(k_norm_g[e], N_KV_HEADS))
            sink = jnp.repeat(sink_logit[e], BLOCK).reshape(1, N_HEADS * BLOCK)
            cw = short_conv_w[e]
            qp, kp, vp, scp, k_layer, v_layer = bf.hosted([("out_even", e)], lambda side: _in_even(
                xp, mod_l, row_p, g_mix, w_in, qg, kg, cw, None, tm_p, SEQ, side, cache_out=True))
            new_k.append(k_layer)
            new_v.append(v_layer)
            w_out = bf.get("out_even", e)
            odd_next = [("in_odd", nxt), ("out_odd", nxt), ("spatial", nxt)]
            xp, = bf.hosted(odd_next, lambda side: _attn_ctx(qp, kp, vp, scp, xp, mod_l, base, sink, w_out, side))
            qs, ks, vs, scs = bf.hosted([("down", l), ("down", l + 1)], lambda side: _in_even(
                xs, mod_l, row_s, g_mix, w_in, qg, kg, cw, rope_tabs, tm_s, DEC_SEQ, side))
            ck = cache_k[:, e].reshape(DEC_BATCH, PAST_LEN, KV_WIDTH)
            cv = cache_v[:, e].reshape(DEC_BATCH, PAST_LEN, KV_WIDTH)
            xs, = bf.hosted([("up", l), ("up", l + 1)], lambda side: _attn_win(
                qs, ks, vs, ck, cv, scs, xs, mod_l, base, sink, w_out, side))
        else:
            o = l // 2
            w_in, w_out = bf.get("in_odd", o), bf.get("out_odd", o)
            vg = row_vec(gmlp_norm_g[o])
            ws = bf.get("spatial", o).reshape(GMLP_GROUPS, CHUNK, CHUNK)
            bs_full = jnp.broadcast_to(b_spatial[o][:, :, None], (GMLP_GROUPS, CHUNK, GMLP_GROUP_DIM))
            xp, = bf.hosted([("down", l)], lambda side: _gmlp(
                xp, mod_l, row_p, g_mix, w_in, vg, ws, bs_full, w_out, tm_p, side))
            xs, = bf.hosted([("up", l)], lambda side: _gmlp(
                xs, mod_l, row_s, g_mix, w_in, vg, ws, bs_full, w_out, tm_s, side))
        g_ffn = row_vec(norm_ffn_g[l])
        w_up_l, w_down_l = bf.get("up", l), bf.get("down", l)
        xp, = _ffn(xp, mod_l, row_p, g_ffn, w_up_l, ffn_conv_w, w_down_l, l, tm_p, SEQ)
        xs, = _ffn(xs, mod_l, row_s, g_ffn, w_up_l, ffn_conv_w, w_down_l, l, tm_s, DEC_SEQ)
    per_layer = lambda parts: parts[0] if len(parts) == 1 else jnp.concatenate(parts, axis=1)
    return (xp.reshape(BATCH, SEQ, D_MODEL), xs.reshape(DEC_BATCH, DEC_SEQ, D_MODEL),
            per_layer(new_k), per_layer(new_v))
```

```python
import functools

import jax
import jax.numpy as jnp
import numpy as np
from jax import lax
from jax.experimental import pallas as pl
from jax.experimental.pallas import tpu as pltpu

F32 = jnp.float32
BF16 = jnp.bfloat16

D_MODEL = 1024
BATCH = 16
SEQ = 256
DEPTH = 2
DEC_BATCH = 4
DEC_SEQ = 2048
PAST_LEN = 512
GRID_W = 64
N_HEADS = 8
N_KV_HEADS = 2
HEAD_DIM = 64
GQA_GROUP = N_HEADS // N_KV_HEADS
ATTN_WIDTH = N_HEADS * HEAD_DIM
KV_WIDTH = N_KV_HEADS * HEAD_DIM
WINDOW = 128
BLOCK = 128
ROPE_BASE = 10000.0
CONV_WIDTH = 512
CHUNK = 128
GMLP_WIDTH = 1024
GMLP_GROUPS = 8
GMLP_GROUP_DIM = GMLP_WIDTH // GMLP_GROUPS
D_FF = 2816
EPS = 1e-6
NEG_INF = -1e30
N_EVEN = (DEPTH + 1) // 2
N_ODD = DEPTH // 2
LOG2_E = 1.4426950408889634

LANES = 128
SUBLANES_F32 = 8
SUBLANES_BF16 = 16
MXU_WIDTH = 256
VMEM_LIMIT_BYTES = 56 * 1024 * 1024

COND_ROWS = 8
CTX_ROW = DEC_BATCH
TOKEN_HALO = SUBLANES_BF16
FFN_COLS = MXU_WIDTH
LEAD_PARTS = 4
CAST_BLOCK_BYTES = 3 * 1024 * 1024
ADA_TN = 1536
ATTN_Q_BLOCKS = 4
ATTN_CTX_SEQS = 2
SUM_ROWS = SUBLANES_BF16
SCORE_CAP = 3.0e38


def _params(n_axes):
    return pltpu.CompilerParams(dimension_semantics=("parallel",) * n_axes,
                                vmem_limit_bytes=VMEM_LIMIT_BYTES)


def _resident(shape):
    zeros = (0,) * len(shape)
    return pl.BlockSpec(shape, lambda *_: zeros, pipeline_mode=pl.Buffered(1))


def _layer_slab(shape, layer):
    return pl.BlockSpec((1,) + tuple(shape[1:]), lambda *_: (layer, 0, 0), pipeline_mode=pl.Buffered(1))


def _mod_spec(mod_row):
    return pl.BlockSpec((1, 1, 6 * D_MODEL), lambda *idx: (mod_row(*idx), 0, 0))


def _dot(a, b):
    return jnp.dot(a, b, preferred_element_type=F32)


def _cast_kernel(x_ref, o_ref):
    o_ref[...] = x_ref[...].astype(BF16)


def _to_bf16(w):
    n, rows, cols = w.shape
    fits = [r for r in range(SUBLANES_BF16, rows + 1, SUBLANES_BF16)
            if rows % r == 0 and r * cols * 4 <= CAST_BLOCK_BYTES]
    tr = max(fits)
    spec = pl.BlockSpec((1, tr, cols), lambda i, j: (i, j, 0))
    return pl.pallas_call(
        _cast_kernel, grid=(n, rows // tr), in_specs=[spec], out_specs=spec,
        out_shape=jax.ShapeDtypeStruct(w.shape, BF16), compiler_params=_params(2), name="to_bf16",
    )(w)


def _side_cast_specs(side, n_steps, step_of):
    w, layer = side
    _, rows, cols = w.shape
    tr = rows // n_steps
    assert tr * n_steps == rows and tr % SUBLANES_BF16 == 0
    return (pl.BlockSpec((1, tr, cols), lambda *idx: (layer, step_of(*idx), 0)),
            pl.BlockSpec((tr, cols), lambda *idx: (step_of(*idx), 0)),
            jax.ShapeDtypeStruct((rows, cols), BF16))


def _split_refs(rest, n_out, side_cast):
    n = side_cast
    side_ins, outs, side_outs, scratch = rest[:n], rest[n:n + n_out], rest[n + n_out:2 * n + n_out], rest[2 * n + n_out:]
    for side_in, side_out in zip(side_ins, side_outs):
        side_out[...] = side_in[0].astype(BF16)
    return outs, scratch


def _host_call(kernel, *, grid, in_specs, args, out_specs, out_shape, scratch_shapes, name, side, step_of):
    in_specs, args, out_specs, out_shape = list(in_specs), list(args), list(out_specs), list(out_shape)
    n_steps = 1
    for extent in grid:
        n_steps *= extent
    sides = [_side_cast_specs(one, n_steps, step_of) for one in side]
    in_specs += [spec for spec, _, _ in sides]
    args += [stack for stack, _ in side]
    out_specs += [spec for _, spec, _ in sides]
    out_shape += [shape for _, _, shape in sides]
    return pl.pallas_call(
        functools.partial(kernel, side_cast=len(side)),
        grid=grid, in_specs=in_specs, out_specs=out_specs, out_shape=out_shape, scratch_shapes=scratch_shapes,
        compiler_params=_params(len(grid)), name=name,
    )(*args)


def _modulate(x, g, shift, scale):
    ms = jnp.mean(x * x, axis=-1, keepdims=True)
    return (x * lax.rsqrt(ms + EPS) * g) * (1.0 + scale) + shift


def _mod_chunk(mod_ref, k):
    return mod_ref[0, :, k * D_MODEL:(k + 1) * D_MODEL]


def _halo_rows(tiles_per_seq):
    return TOKEN_HALO if tiles_per_seq > 1 else 0


def _token_specs(T, tm, tiles_per_seq):
    tok = pl.BlockSpec((tm, D_MODEL), lambda i: (i, 0))
    if tiles_per_seq == 1:
        return [tok]
    per_tile = tm // TOKEN_HALO
    last = T // TOKEN_HALO - 1
    prev = pl.BlockSpec((TOKEN_HALO, D_MODEL), lambda i: (jnp.maximum(i * per_tile - 1, 0), 0))
    nxt = pl.BlockSpec((TOKEN_HALO, D_MODEL), lambda i: (jnp.minimum((i + 1) * per_tile, last), 0))
    return [prev, tok, nxt]


def _modulated_parts(x_refs, h_scr, g, shift, scale, tiles_per_seq):
    halo = _halo_rows(tiles_per_seq)
    x_ref = x_refs[len(x_refs) // 2]
    tm = x_ref.shape[0]
    step = tm // LEAD_PARTS
    parts = []
    for p in range(LEAD_PARTS):
        lo, hi = halo + p * step, halo + (p + 1) * step
        piece = _modulate(x_ref[p * step:(p + 1) * step, :], g, shift, scale).astype(BF16)
        if halo and p == 0:
            pos = pl.program_id(0) % tiles_per_seq
            edge = jnp.where(pos > 0, _modulate(x_refs[0][...], g, shift, scale), 0.0).astype(BF16)
            piece, lo = jnp.concatenate([edge, piece], axis=0), 0
        if halo and p == LEAD_PARTS - 1:
            pos = pl.program_id(0) % tiles_per_seq
            edge = jnp.where(pos < tiles_per_seq - 1, _modulate(x_refs[2][...], g, shift, scale), 0.0).astype(BF16)
            piece, hi = jnp.concatenate([piece, edge], axis=0), tm + 2 * halo
        h_scr[lo:hi, :] = piece
        parts.append(piece)
    return parts


def _dot_rows(parts, w):
    return jnp.concatenate([_dot(p, w) for p in parts], axis=0)


def _seq_end_masks(tm, cols, seq_len, tiles_per_seq):
    if tiles_per_seq > 1:
        return None
    seq_row = lax.broadcasted_iota(jnp.int32, (tm, cols), 0) % seq_len
    return seq_row != 0, seq_row != seq_len - 1


def _token_conv3(z, w, tm, halo, masks):
    rows = z.shape[0]
    mid = slice(halo, halo + tm)
    dn = pltpu.roll(z, 1, 0)[mid]
    up = pltpu.roll(z, rows - 1, 0)[mid]
    if masks is not None:
        dn = jnp.where(masks[0], dn, 0.0)
        up = jnp.where(masks[1], up, 0.0)
    return dn * w[0:1] + z[mid] * w[1:2] + up * w[2:3]


def _adaln_kernel(cond_ref, w_ref, b_ref, *rest, side_cast):
    (o_ref,), _ = _split_refs(rest, 1, side_cast)
    a = jax.nn.silu(cond_ref[...]).astype(BF16)
    rows = _dot(a, w_ref[0].astype(BF16)) + b_ref[0]
    for r in range(COND_ROWS):
        o_ref[r] = rows[r:r + 1, :]


def _adaln(cond, ada_w, ada_b, side=()):
    n_out = 6 * D_MODEL
    n_col = n_out // ADA_TN
    return _host_call(
        _adaln_kernel,
        grid=(DEPTH, n_col),
        in_specs=[pl.BlockSpec((COND_ROWS, D_MODEL), lambda l, j: (0, 0)),
                  pl.BlockSpec((1, D_MODEL, ADA_TN), lambda l, j: (l, 0, j)),
                  pl.BlockSpec((1, 1, ADA_TN), lambda l, j: (l, 0, j))],
        args=[cond, ada_w, ada_b.reshape(DEPTH, 1, n_out)],
        out_specs=[pl.BlockSpec((COND_ROWS, 1, ADA_TN), lambda l, j: (l, 0, j))],
        out_shape=[jax.ShapeDtypeStruct((DEPTH * COND_ROWS, 1, n_out), F32)],
        scratch_shapes=[], name="adaln", side=side, step_of=lambda l, j: l * n_col + j)


def _head_rms(z, gain):
    n = z.shape[1]
    w = min(n, MXU_WIDTH)
    r = lax.broadcasted_iota(jnp.int32, (w, w), 0) // HEAD_DIM
    c = lax.broadcasted_iota(jnp.int32, (w, w), 1) // HEAD_DIM
    ones = (r == c).astype(BF16)
    sq = (z * z).astype(BF16)
    ss = jnp.concatenate([_dot(sq[:, k:k + w], ones) for k in range(0, n, w)], axis=1)
    return z * lax.rsqrt(ss * (1.0 / HEAD_DIM) + EPS) * gain


def _rope(z, cos, sin_lo, sin_hi):
    outs = []
    for k in range(0, z.shape[1], LANES):
        blk = z[:, k:k + LANES]
        outs.append(blk * cos + pltpu.roll(blk, LANES - 16, 1) * sin_lo + pltpu.roll(blk, 16, 1) * sin_hi)
    return jnp.concatenate(outs, axis=1)


def _in_even_kernel(*refs, rope, tiles_per_seq, seq_len, cache_out, side_cast):
    halo = _halo_rows(tiles_per_seq)
    n_x = 3 if halo else 1
    x_refs, refs = refs[:n_x], refs[n_x:]
    mod_ref, g_ref, w_ref, qg_ref, kg_ref, cw_ref = refs[:6]
    refs = refs[6:]
    if rope:
        cos_ref, slo_ref, shi_ref = refs[:3]
        refs = refs[3:]
    outs, (h_scr,) = _split_refs(refs, 6 if cache_out else 4, side_cast)
    q_ref, k_ref, v_ref, sc_ref = outs[:4]
    tm = q_ref.shape[0]
    parts = _modulated_parts(x_refs, h_scr, g_ref[...], _mod_chunk(mod_ref, 0), _mod_chunk(mod_ref, 1),
                             tiles_per_seq)
    c0 = 0
    c1 = c0 + ATTN_WIDTH
    c2 = c1 + 2 * KV_WIDTH
    c3 = c2 + CONV_WIDTH
    c4 = c3 + CONV_WIDTH
    c5 = c4 + CONV_WIDTH
    zq = _dot_rows(parts, w_ref[:, c0:c1])[halo:halo + tm]
    h_all = h_scr[...]
    h = h_scr[halo:halo + tm, :]
    q = _head_rms(zq, qg_ref[...] * (HEAD_DIM ** -0.5 * LOG2_E))
    zkv = _dot(h, w_ref[:, c1:c2])
    k = _head_rms(zkv[:, :KV_WIDTH], kg_ref[...])
    if rope:
        cos, slo, shi = cos_ref[...], slo_ref[...], shi_ref[...]
        q = _rope(q, cos, slo, shi)
        k = _rope(k, cos, slo, shi)
    v = zkv[:, KV_WIDTH:]
    q_ref[...] = q
    k_ref[...] = k
    v_ref[...] = v
    if cache_out:
        for cache_ref, rows in zip(outs[4:], (k, v)):
            for s in range(tm // seq_len):
                seq_t = jnp.concatenate([rows[j:j + BLOCK, :].T for j in range(s * seq_len, (s + 1) * seq_len, BLOCK)],
                                        axis=1)
                cache_ref[s, 0] = seq_t.reshape(N_KV_HEADS, HEAD_DIM, seq_len)
    ch = _dot(h_all, w_ref[:, c3:c4]) * _dot(h_all, w_ref[:, c4:c5])
    conv = _token_conv3(ch, cw_ref[...], tm, halo, _seq_end_masks(tm, CONV_WIDTH, seq_len, tiles_per_seq))
    sc_ref[...] = (_dot(h, w_ref[:, c2:c3]) * conv).astype(BF16)


def _in_even(x, mod_l, mod_row, g, w_bf, qg, kg, conv_w, rope_tabs, tm, seq_len, side=(), cache_out=False):
    T = x.shape[0]
    n_in = w_bf.shape[1]
    tiles_per_seq = max(seq_len // tm, 1)
    assert tm % seq_len == 0 or seq_len % tm == 0
    tok = lambda w: pl.BlockSpec((tm, w), lambda i: (i, 0))
    x_specs = _token_specs(T, tm, tiles_per_seq)
    in_specs = x_specs + [_mod_spec(mod_row), _resident((1, D_MODEL)), _resident((D_MODEL, n_in)),
                          _resident((1, ATTN_WIDTH)), _resident((1, KV_WIDTH)), _resident((3, CONV_WIDTH))]
    args = [x] * len(x_specs) + [mod_l, g, w_bf, qg, kg, conv_w]
    if rope_tabs is not None:
        in_specs += [pl.BlockSpec((tm, LANES), lambda i: (i % tiles_per_seq, 0))] * 3
        args += list(rope_tabs)
    outs = ((ATTN_WIDTH, F32), (KV_WIDTH, F32), (KV_WIDTH, F32), (CONV_WIDTH, BF16))
    out_specs = [tok(w) for w, _ in outs]
    out_shape = [jax.ShapeDtypeStruct((T, w), dt) for w, dt in outs]
    if cache_out:
        n_seq = tm // seq_len
        cache_block = (n_seq, 1, N_KV_HEADS, HEAD_DIM, seq_len)
        out_specs += [pl.BlockSpec(cache_block, lambda i: (i, 0, 0, 0, 0))] * 2
        out_shape += [jax.ShapeDtypeStruct((T // seq_len,) + cache_block[1:], F32)] * 2
    scratch = [pltpu.VMEM((tm + 2 * _halo_rows(tiles_per_seq), D_MODEL), BF16)]
    return _host_call(
        functools.partial(_in_even_kernel, rope=rope_tabs is not None, tiles_per_seq=tiles_per_seq, seq_len=seq_len,
                          cache_out=cache_out),
        grid=(T // tm,), in_specs=in_specs, args=args, out_specs=out_specs, out_shape=out_shape,
        scratch_shapes=scratch, name="in_even", side=side, step_of=lambda i: i)


def _stack_heads(q):
    lane_lo = lax.broadcasted_iota(jnp.int32, (BLOCK, LANES), 1) < HEAD_DIM
    heads = []
    for pair in range(N_HEADS // 2):
        kv = (2 * pair) // GQA_GROUP
        qp = q[:, pair * LANES:(pair + 1) * LANES]
        qr = pltpu.roll(qp, HEAD_DIM, 1)
        for half in range(2):
            src = qp if half == kv else qr
            heads.append(jnp.where(lane_lo, src, 0.0) if kv == 0 else jnp.where(lane_lo, 0.0, src))
    return jnp.concatenate(heads, axis=0).astype(BF16)


def _transposed_values(v_blocks, vt_tail=None):
    cols = [v[j:j + BLOCK].T for v in v_blocks for j in range(0, v.shape[0], BLOCK)]
    vt = jnp.concatenate(cols + ([] if vt_tail is None else [vt_tail]), axis=1)
    return jnp.concatenate([vt.astype(BF16), jnp.ones((SUM_ROWS, vt.shape[1]), BF16)], axis=0)


def _scores(q, kcat, cap):
    st = lax.dot_general(kcat, _stack_heads(q), (((1,), (1,)), ((), ())), preferred_element_type=F32)
    if cap is None:
        return st
    n_band = cap.shape[0]
    capped = jnp.minimum(st[:n_band], jnp.concatenate([cap] * N_HEADS, axis=1))
    return jnp.concatenate([capped, st[n_band:]], axis=0)


def _weighted_values(st, vt, sink_row):
    m = jnp.maximum(jnp.max(st, axis=0, keepdims=True), sink_row)
    pt = jnp.exp2(st - m).astype(BF16)
    ot = _dot(vt, pt)
    denom = ot[KV_WIDTH:KV_WIDTH + 1] + jnp.exp2(sink_row - m)
    ot = ot[:KV_WIDTH] / denom
    pairs = []
    for pair in range(N_HEADS // 2):
        kv = (2 * pair) // GQA_GROUP
        dims = slice(kv * HEAD_DIM, (kv + 1) * HEAD_DIM)
        both = [ot[dims, (2 * pair + half) * BLOCK:(2 * pair + half + 1) * BLOCK] for half in range(2)]
        pairs.append(jnp.concatenate(both, axis=0).T)
    return jnp.concatenate(pairs, axis=1)


def _attend_blocks(n_blocks, scores_of, values_of, sink_row, attn_scr):
    st = scores_of(0)
    for j in range(n_blocks):
        st_next = scores_of(j + 1) if j + 1 < n_blocks else None
        attn_scr[j * BLOCK:(j + 1) * BLOCK, :] = _weighted_values(st, values_of(j), sink_row).astype(BF16)
        st = st_next


def _project_out(attn_scr, sc_ref, x_ref, mod_ref, w_ref, o_ref):
    mix = _dot(attn_scr[...], w_ref[:ATTN_WIDTH, :]) + _dot(sc_ref[...], w_ref[ATTN_WIDTH:, :])
    o_ref[...] = x_ref[...] + _mod_chunk(mod_ref, 2) * mix


def _attn_ctx_kernel(sink_ref, q_ref, k_ref, v_ref, sc_ref, x_ref, mod_ref, w_ref, *rest, side_cast):
    (o_ref,), (attn_scr,) = _split_refs(rest, 1, side_cast)
    per_seq = SEQ // BLOCK
    kcats = [k_ref[s * SEQ:(s + 1) * SEQ, :].astype(BF16) for s in range(ATTN_CTX_SEQS)]
    vts = [_transposed_values([v_ref[s * SEQ:(s + 1) * SEQ, :]]) for s in range(ATTN_CTX_SEQS)]
    sink_row = sink_ref[...] * LOG2_E
    _attend_blocks(ATTN_CTX_SEQS * per_seq,
                   lambda j: _scores(q_ref[j * BLOCK:(j + 1) * BLOCK, :], kcats[j // per_seq], None),
                   lambda j: vts[j // per_seq], sink_row, attn_scr)
    _project_out(attn_scr, sc_ref, x_ref, mod_ref, w_ref, o_ref)


def _attn_ctx(q, k, v, sc, x, mod_l, mod_base, sink_row, w_out_bf, side=()):
    T = q.shape[0]
    tq = ATTN_CTX_SEQS * SEQ
    tok = lambda w: pl.BlockSpec((tq, w), lambda i: (i, 0))
    return _host_call(
        _attn_ctx_kernel,
        grid=(T // tq,),
        in_specs=[_resident((1, N_HEADS * BLOCK)), tok(ATTN_WIDTH), tok(KV_WIDTH), tok(KV_WIDTH),
                  tok(CONV_WIDTH), tok(D_MODEL), _mod_spec(lambda i: mod_base + CTX_ROW),
                  _resident((ATTN_WIDTH + CONV_WIDTH, D_MODEL))],
        args=[sink_row, q, k, v, sc, x, mod_l, w_out_bf],
        out_specs=[tok(D_MODEL)], out_shape=[jax.ShapeDtypeStruct((T, D_MODEL), F32)],
        scratch_shapes=[pltpu.VMEM((tq, ATTN_WIDTH), BF16)], name="attn_ctx", side=side, step_of=lambda i: i)


def _band_cap(has_prev, has_next):
    c = lax.broadcasted_iota(jnp.int32, (3 * BLOCK, BLOCK), 0)
    r = lax.broadcasted_iota(jnp.int32, (3 * BLOCK, BLOCK), 1)
    first_prev = r + jnp.where(has_prev, 0, BLOCK)
    last_next = r + 2 * BLOCK - jnp.where(has_next, 0, BLOCK)
    masked = ((c < BLOCK) & (c < first_prev)) | ((c >= 2 * BLOCK) & (c > last_next))
    return jnp.where(masked, NEG_INF, SCORE_CAP)


def _attn_win_kernel(sink_ref, q_ref, kp_ref, kc_ref, kn_ref, vp_ref, vc_ref, vn_ref, ckt_ref, cvt_ref,
                     sc_ref, x_ref, mod_ref, w_ref, *rest, n_steps, side_cast):
    (o_ref,), (attn_scr,) = _split_refs(rest, 1, side_cast)
    i = pl.program_id(1)
    kc, vc = kc_ref[...], vc_ref[...]
    inner = range(0, ATTN_Q_BLOCKS * BLOCK, BLOCK)
    k_blocks = [kp_ref[...]] + [kc[j:j + BLOCK] for j in inner] + [kn_ref[...]]
    v_blocks = [vp_ref[...]] + [vc[j:j + BLOCK] for j in inner] + [vn_ref[...]]
    sink_row = sink_ref[...] * LOG2_E
    ckt = ckt_ref[0]
    ck = jnp.concatenate([ckt[:, j:j + BLOCK].T for j in range(0, PAST_LEN, BLOCK)], axis=0)

    def scores_of(j):
        kcat = jnp.concatenate(k_blocks[j:j + 3] + [ck], axis=0).astype(BF16)
        cap = _band_cap(i > 0 if j == 0 else True, i < n_steps - 1 if j == ATTN_Q_BLOCKS - 1 else True)
        return _scores(q_ref[j * BLOCK:(j + 1) * BLOCK, :], kcat, cap)

    _attend_blocks(ATTN_Q_BLOCKS, scores_of, lambda j: _transposed_values(v_blocks[j:j + 3], cvt_ref[0]),
                   sink_row, attn_scr)
    _project_out(attn_scr, sc_ref, x_ref, mod_ref, w_ref, o_ref)


def _attn_win(q, k, v, ck, cv, sc, x, mod_l, mod_base, sink_row, w_out_bf, side=()):
    T = q.shape[0]
    tq = ATTN_Q_BLOCKS * BLOCK
    n_steps = DEC_SEQ // tq
    nb = DEC_SEQ // BLOCK
    cur = lambda w: pl.BlockSpec((tq, w), lambda b, i: (b * n_steps + i, 0))
    prev = pl.BlockSpec((BLOCK, KV_WIDTH), lambda b, i: (b * nb + jnp.maximum(ATTN_Q_BLOCKS * i - 1, 0), 0))
    nxt = pl.BlockSpec((BLOCK, KV_WIDTH),
                       lambda b, i: (b * nb + jnp.minimum(ATTN_Q_BLOCKS * (i + 1), nb - 1), 0))
    ctx = pl.BlockSpec((1, KV_WIDTH, PAST_LEN), lambda b, i: (b, 0, 0))
    in_specs = [_resident((1, N_HEADS * BLOCK)), cur(ATTN_WIDTH),
                prev, cur(KV_WIDTH), nxt, prev, cur(KV_WIDTH), nxt, ctx, ctx,
                cur(CONV_WIDTH), cur(D_MODEL), _mod_spec(lambda b, i: mod_base + b),
                _resident((ATTN_WIDTH + CONV_WIDTH, D_MODEL))]
    args = [sink_row, q, k, k, k, v, v, v, ck, cv, sc, x, mod_l, w_out_bf]
    return _host_call(
        functools.partial(_attn_win_kernel, n_steps=n_steps),
        grid=(T // DEC_SEQ, n_steps), in_specs=in_specs, args=args,
        out_specs=[cur(D_MODEL)], out_shape=[jax.ShapeDtypeStruct((T, D_MODEL), F32)],
        scratch_shapes=[pltpu.VMEM((tq, ATTN_WIDTH), BF16)], name="attn_win", side=side,
        step_of=lambda b, i: b * n_steps + i)


def _gelu_tanh(x):
    k = 0.7978845608028654
    half = 0.5 * x
    return half + half * jnp.tanh(x * (k + (k * 0.044715) * (x * x)))


def _gmlp_kernel(x_ref, mod_ref, g_ref, win_ref, vg_ref, ws_ref, bs_ref, wout_ref, *rest, side_cast):
    (o_ref,), (gated_scr, h_scr) = _split_refs(rest, 1, side_cast)
    n_chunks = x_ref.shape[0] // CHUNK
    parts = _modulated_parts([x_ref], h_scr, g_ref[...], _mod_chunk(mod_ref, 0), _mod_chunk(mod_ref, 1), 1)
    u = _gelu_tanh(_dot_rows(parts, win_ref[:, :GMLP_WIDTH]))
    v = _gelu_tanh(_dot(h_scr[...], win_ref[:, GMLP_WIDTH:]))
    ms = jnp.mean(v * v, axis=-1, keepdims=True)
    v = (v * lax.rsqrt(ms + EPS) * vg_ref[...]).astype(BF16)
    for grp in range(GMLP_GROUPS):
        lanes = slice(grp * GMLP_GROUP_DIM, (grp + 1) * GMLP_GROUP_DIM)
        rhs = jnp.concatenate([v[n * CHUNK:(n + 1) * CHUNK, lanes] for n in range(n_chunks)], axis=1)
        s = _dot(ws_ref[grp], rhs)
        for n in range(n_chunks):
            rows = slice(n * CHUNK, (n + 1) * CHUNK)
            s_n = s[:, n * GMLP_GROUP_DIM:(n + 1) * GMLP_GROUP_DIM] + bs_ref[grp]
            gated_scr[rows, lanes] = (u[rows, lanes] * s_n).astype(BF16)
    o_ref[...] = x_ref[...] + _mod_chunk(mod_ref, 2) * _dot(gated_scr[...], wout_ref[...])


def _gmlp(x, mod_l, mod_row, g, win_bf, vg, ws_bf, bs_full, wout_bf, tm, side=()):
    T = x.shape[0]
    tok = pl.BlockSpec((tm, D_MODEL), lambda i: (i, 0))
    in_specs = [tok, _mod_spec(mod_row),
                _resident((1, D_MODEL)), _resident((D_MODEL, 2 * GMLP_WIDTH)), _resident((1, GMLP_WIDTH)),
                _resident((GMLP_GROUPS, CHUNK, CHUNK)), _resident((GMLP_GROUPS, CHUNK, GMLP_GROUP_DIM)),
                _resident((GMLP_WIDTH, D_MODEL))]
    args = [x, mod_l, g, win_bf, vg, ws_bf, bs_full, wout_bf]
    return _host_call(
        _gmlp_kernel, grid=(T // tm,), in_specs=in_specs, args=args,
        out_specs=[tok], out_shape=[jax.ShapeDtypeStruct((T, D_MODEL), F32)],
        scratch_shapes=[pltpu.VMEM((tm, GMLP_WIDTH), BF16), pltpu.VMEM((tm, D_MODEL), BF16)],
        name="gmlp", side=side, step_of=lambda i: i)


def _ffn_kernel(*refs, tiles_per_seq, seq_len, side_cast):
    halo = _halo_rows(tiles_per_seq)
    n_x = 3 if halo else 1
    x_refs, refs = refs[:n_x], refs[n_x:]
    mod_ref, g_ref, wup_ref, cw_ref, wdn_ref = refs[:5]
    (o_ref,), (act_scr, h_scr) = _split_refs(refs[5:], 1, side_cast)
    tm = o_ref.shape[0]
    parts = _modulated_parts(x_refs, h_scr, g_ref[...], _mod_chunk(mod_ref, 3), _mod_chunk(mod_ref, 4),
                             tiles_per_seq)
    masks = _seq_end_masks(tm, FFN_COLS, seq_len, tiles_per_seq)
    for j in range(0, D_FF, FFN_COLS):
        gate_cols = slice(j, j + FFN_COLS)
        val_cols = slice(D_FF + j, D_FF + j + FFN_COLS)
        if j == 0:
            up = lambda cols: _dot_rows(parts, wup_ref[:, cols])
        else:
            h = h_scr[...]
            up = lambda cols: _dot(h, wup_ref[:, cols])
        zg = _token_conv3(up(gate_cols), cw_ref[0, :, gate_cols], tm, halo, masks)
        zv = _token_conv3(up(val_cols), cw_ref[0, :, val_cols], tm, halo, masks)
        act_scr[:, gate_cols] = (jax.nn.silu(zg) * zv).astype(BF16)
    x = x_refs[n_x // 2][...]
    o_ref[...] = x + _mod_chunk(mod_ref, 5) * _dot(act_scr[...], wdn_ref[...])


def _ffn(x, mod_l, mod_row, g, wup_bf, conv_w, wdn_bf, layer, tm, seq_len, side=()):
    T = x.shape[0]
    tiles_per_seq = max(seq_len // tm, 1)
    assert tm % seq_len == 0 or seq_len % tm == 0
    x_specs = _token_specs(T, tm, tiles_per_seq)
    in_specs = x_specs + [_mod_spec(mod_row), _resident((1, D_MODEL)), _resident(wup_bf.shape),
                          _layer_slab(conv_w.shape, layer), _resident(wdn_bf.shape)]
    scratch = [pltpu.VMEM((tm, D_FF), BF16), pltpu.VMEM((tm + 2 * _halo_rows(tiles_per_seq), D_MODEL), BF16)]
    return _host_call(
        functools.partial(_ffn_kernel, tiles_per_seq=tiles_per_seq, seq_len=seq_len),
        grid=(T // tm,), in_specs=in_specs, args=[x] * len(x_specs) + [mod_l, g, wup_bf, conv_w, wdn_bf],
        out_specs=[pl.BlockSpec((tm, D_MODEL), lambda i: (i, 0))], out_shape=[jax.ShapeDtypeStruct((T, D_MODEL), F32)],
        scratch_shapes=scratch, name="conv_ffn", side=side, step_of=lambda i: i)


def _rope_tables(n_tokens):
    t = np.arange(n_tokens)
    n_freq = HEAD_DIM // 4
    inv = (ROPE_BASE ** (-np.arange(n_freq, dtype=np.float32) / n_freq)).astype(np.float32)
    row_ang = (t // GRID_W).astype(np.float32)[:, None] * inv
    col_ang = (t % GRID_W).astype(np.float32)[:, None] * inv
    ang = np.concatenate([row_ang, row_ang, col_ang, col_ang], axis=1)
    ang = np.tile(ang, (1, LANES // HEAD_DIM)).astype(np.float64)
    first = (np.arange(LANES) % (2 * n_freq)) < n_freq
    cos, sin = np.cos(ang), np.sin(ang)
    tables = (cos, np.where(first, -sin, 0.0), np.where(first, 0.0, sin))
    return tuple(jnp.asarray(tab.astype(np.float32)) for tab in tables)


class _Bf16Weights:
    def __init__(self, **stacks):
        self.stacks, self.ready = stacks, {}

    def hosted(self, keys, call):
        due = [(name, slab) for name, slab in keys
               if slab < self.stacks[name].shape[0] and (name, slab) not in self.ready]
        outs = call([(self.stacks[name], slab) for name, slab in due])
        for key, cast in zip(due, outs[len(outs) - len(due):]):
            self.ready[key] = cast
        return outs[:len(outs) - len(due)]

    def get(self, name, slab):
        if (name, slab) not in self.ready:
            self.ready[(name, slab)] = _to_bf16(self.stacks[name][slab:slab + 1])[0]
        return self.ready[(name, slab)]


def kernel(x_prompt, x_sample, cache_k, cache_v, c, c_ctx, ada_w, ada_b, norm_mix_g, norm_ffn_g, w_in_even,
           q_norm_g, k_norm_g, sink_logit, short_conv_w, w_out_even, w_in_odd, gmlp_norm_g, w_spatial,
           b_spatial, w_out_odd, w_up, ffn_conv_w, w_down):
    n_p, n_s = BATCH * SEQ, DEC_BATCH * DEC_SEQ
    xp = x_prompt.reshape(n_p, D_MODEL)
    xs = x_sample.reshape(n_s, D_MODEL)
    cond = jnp.concatenate([c, c_ctx[None, :], jnp.zeros((COND_ROWS - DEC_BATCH - 1, D_MODEL), F32)], axis=0)
    bf = _Bf16Weights(in_even=w_in_even, out_even=w_out_even, in_odd=w_in_odd, out_odd=w_out_odd,
                      spatial=w_spatial.reshape(N_ODD, GMLP_GROUPS * CHUNK, CHUNK), up=w_up, down=w_down)

    mod_l, = bf.hosted([("in_even", 0)], lambda side: _adaln(cond, ada_w, ada_b, side))
    rope_tabs = _rope_tables(DEC_SEQ)

    tm_p, tm_s = 1024, 1024
    row_vec = lambda a: a.reshape(1, -1)
    new_k, new_v = [], []
    for l in range(DEPTH):
        base = l * COND_ROWS
        row_p = lambda i, base=base: base + CTX_ROW
        row_s = lambda i, base=base: base + i // (DEC_SEQ // tm_s)
        g_mix = row_vec(norm_mix_g[l])
        nxt = (l + 1) // 2
        if l % 2 == 0:
            e = l // 2
            w_in = bf.get("in_even", e)
            qg = row_vec(jnp.tile(q_norm_g[e], N_HEADS))
            kg = row_vec(jnp.tile(k_norm_g[e], N_KV_HEADS))
            sink = jnp.repeat(sink_logit[e], BLOCK).reshape(1, N_HEADS * BLOCK)
            cw = short_conv_w[e]
            qp, kp, vp, scp, k_layer, v_layer = bf.hosted([("out_even", e)], lambda side: _in_even(
                xp, mod_l, row_p, g_mix, w_in, qg, kg, cw, None, tm_p, SEQ, side, cache_out=True))
            new_k.append(k_layer)
            new_v.append(v_layer)
            w_out = bf.get("out_even", e)
            odd_next = [("in_odd", nxt), ("out_odd", nxt), ("spatial", nxt)]
            xp, = bf.hosted(odd_next, lambda side: _attn_ctx(qp, kp, vp, scp, xp, mod_l, base, sink, w_out, side))
            qs, ks, vs, scs = bf.hosted([("down", l), ("down", l + 1)], lambda side: _in_even(
                xs, mod_l, row_s, g_mix, w_in, qg, kg, cw, rope_tabs, tm_s, DEC_SEQ, side))
            ck = jnp.transpose(cache_k[:, e], (0, 2, 3, 1)).reshape(DEC_BATCH, KV_WIDTH, PAST_LEN)
            cv = jnp.transpose(cache_v[:, e], (0, 2, 3, 1)).reshape(DEC_BATCH, KV_WIDTH, PAST_LEN)
            xs, = bf.hosted([("up", l), ("up", l + 1)], lambda side: _attn_win(
                qs, ks, vs, ck, cv, scs, xs, mod_l, base, sink, w_out, side))
        else:
            o = l // 2
            w_in, w_out = bf.get("in_odd", o), bf.get("out_odd", o)
            vg = row_vec(gmlp_norm_g[o])
            ws = bf.get("spatial", o).reshape(GMLP_GROUPS, CHUNK, CHUNK)
            bs_full = jnp.broadcast_to(b_spatial[o][:, :, None], (GMLP_GROUPS, CHUNK, GMLP_GROUP_DIM))
            xp, = bf.hosted([("down", l)], lambda side: _gmlp(
                xp, mod_l, row_p, g_mix, w_in, vg, ws, bs_full, w_out, tm_p, side))
            xs, = bf.hosted([("up", l)], lambda side: _gmlp(
                xs, mod_l, row_s, g_mix, w_in, vg, ws, bs_full, w_out, tm_s, side))
        g_ffn = row_vec(norm_ffn_g[l])
        w_up_l, w_down_l = bf.get("up", l), bf.get("down", l)
        xp, = _ffn(xp, mod_l, row_p, g_ffn, w_up_l, ffn_conv_w, w_down_l, l, tm_p, SEQ)
        xs, = _ffn(xs, mod_l, row_s, g_ffn, w_up_l, ffn_conv_w, w_down_l, l, tm_s, DEC_SEQ)
    per_layer = lambda parts: jnp.transpose(parts[0] if len(parts) == 1 else jnp.concatenate(parts, axis=1),
                                            (0, 1, 4, 2, 3))
    return (xp.reshape(BATCH, SEQ, D_MODEL), xs.reshape(DEC_BATCH, DEC_SEQ, D_MODEL),
            per_layer(new_k), per_layer(new_v))
```

```python
import functools

import jax
import jax.numpy as jnp
import numpy as np
from jax import lax
from jax.experimental import pallas as pl
from jax.experimental.pallas import tpu as pltpu

F32 = jnp.float32
BF16 = jnp.bfloat16

D_MODEL = 1024
BATCH = 16
SEQ = 256
DEPTH = 2
DEC_BATCH = 4
DEC_SEQ = 2048
PAST_LEN = 512
GRID_W = 64
N_HEADS = 8
N_KV_HEADS = 2
HEAD_DIM = 64
GQA_GROUP = N_HEADS // N_KV_HEADS
ATTN_WIDTH = N_HEADS * HEAD_DIM
KV_WIDTH = N_KV_HEADS * HEAD_DIM
WINDOW = 128
BLOCK = 128
ROPE_BASE = 10000.0
CONV_WIDTH = 512
CHUNK = 128
GMLP_WIDTH = 1024
GMLP_GROUPS = 8
GMLP_GROUP_DIM = GMLP_WIDTH // GMLP_GROUPS
D_FF = 2816
EPS = 1e-6
NEG_INF = -1e30
N_EVEN = (DEPTH + 1) // 2
N_ODD = DEPTH // 2
LOG2_E = 1.4426950408889634

LANES = 128
SUBLANES_F32 = 8
SUBLANES_BF16 = 16
MXU_WIDTH = 256
VMEM_LIMIT_BYTES = 56 * 1024 * 1024

COND_ROWS = 8
CTX_ROW = DEC_BATCH
TOKEN_HALO = SUBLANES_BF16
FFN_COLS = 2 * MXU_WIDTH
LEAD_PARTS = 4
CAST_BLOCK_BYTES = 3 * 1024 * 1024
ADA_TN = 1536
ATTN_Q_BLOCKS = 4
ATTN_CTX_SEQS = 2
SUM_ROWS = SUBLANES_BF16
SCORE_CAP = 3.0e38


def _params(n_axes):
    return pltpu.CompilerParams(dimension_semantics=("parallel",) * n_axes,
                                vmem_limit_bytes=VMEM_LIMIT_BYTES)


def _resident(shape):
    zeros = (0,) * len(shape)
    return pl.BlockSpec(shape, lambda *_: zeros, pipeline_mode=pl.Buffered(1))


def _layer_slab(shape, layer):
    return pl.BlockSpec((1,) + tuple(shape[1:]), lambda *_: (layer, 0, 0), pipeline_mode=pl.Buffered(1))


def _mod_spec(mod_row):
    return pl.BlockSpec((1, 1, 6 * D_MODEL), lambda *idx: (mod_row(*idx), 0, 0))


def _dot(a, b):
    return jnp.dot(a, b, preferred_element_type=F32)


def _cast_kernel(x_ref, o_ref):
    o_ref[...] = x_ref[...].astype(BF16)


def _to_bf16(w):
    n, rows, cols = w.shape
    fits = [r for r in range(SUBLANES_BF16, rows + 1, SUBLANES_BF16)
            if rows % r == 0 and r * cols * 4 <= CAST_BLOCK_BYTES]
    tr = max(fits)
    spec = pl.BlockSpec((1, tr, cols), lambda i, j: (i, j, 0))
    return pl.pallas_call(
        _cast_kernel, grid=(n, rows // tr), in_specs=[spec], out_specs=spec,
        out_shape=jax.ShapeDtypeStruct(w.shape, BF16), compiler_params=_params(2), name="to_bf16",
    )(w)


def _side_cast_specs(side, n_steps, step_of):
    w, layer = side
    _, rows, cols = w.shape
    tr = rows // n_steps
    assert tr * n_steps == rows and tr % SUBLANES_BF16 == 0
    return (pl.BlockSpec((1, tr, cols), lambda *idx: (layer, step_of(*idx), 0)),
            pl.BlockSpec((tr, cols), lambda *idx: (step_of(*idx), 0)),
            jax.ShapeDtypeStruct((rows, cols), BF16))


def _split_refs(rest, n_out, side_cast):
    n = side_cast
    side_ins, outs, side_outs, scratch = rest[:n], rest[n:n + n_out], rest[n + n_out:2 * n + n_out], rest[2 * n + n_out:]
    for side_in, side_out in zip(side_ins, side_outs):
        side_out[...] = side_in[0].astype(BF16)
    return outs, scratch


def _host_call(kernel, *, grid, in_specs, args, out_specs, out_shape, scratch_shapes, name, side, step_of):
    in_specs, args, out_specs, out_shape = list(in_specs), list(args), list(out_specs), list(out_shape)
    n_steps = 1
    for extent in grid:
        n_steps *= extent
    sides = [_side_cast_specs(one, n_steps, step_of) for one in side]
    in_specs += [spec for spec, _, _ in sides]
    args += [stack for stack, _ in side]
    out_specs += [spec for _, spec, _ in sides]
    out_shape += [shape for _, _, shape in sides]
    return pl.pallas_call(
        functools.partial(kernel, side_cast=len(side)),
        grid=grid, in_specs=in_specs, out_specs=out_specs, out_shape=out_shape, scratch_shapes=scratch_shapes,
        compiler_params=_params(len(grid)), name=name,
    )(*args)


def _modulate(x, g, shift, scale):
    ms = jnp.mean(x * x, axis=-1, keepdims=True)
    return (x * lax.rsqrt(ms + EPS) * g) * (1.0 + scale) + shift


def _mod_chunk(mod_ref, k):
    return mod_ref[0, :, k * D_MODEL:(k + 1) * D_MODEL]


def _halo_rows(tiles_per_seq):
    return TOKEN_HALO if tiles_per_seq > 1 else 0


def _token_specs(T, tm, tiles_per_seq):
    tok = pl.BlockSpec((tm, D_MODEL), lambda i: (i, 0))
    if tiles_per_seq == 1:
        return [tok]
    per_tile = tm // TOKEN_HALO
    last = T // TOKEN_HALO - 1
    prev = pl.BlockSpec((TOKEN_HALO, D_MODEL), lambda i: (jnp.maximum(i * per_tile - 1, 0), 0))
    nxt = pl.BlockSpec((TOKEN_HALO, D_MODEL), lambda i: (jnp.minimum((i + 1) * per_tile, last), 0))
    return [prev, tok, nxt]


def _modulated_parts(x_refs, h_scr, g, shift, scale, tiles_per_seq):
    halo = _halo_rows(tiles_per_seq)
    x_ref = x_refs[len(x_refs) // 2]
    tm = x_ref.shape[0]
    step = tm // LEAD_PARTS
    parts = []
    for p in range(LEAD_PARTS):
        lo, hi = halo + p * step, halo + (p + 1) * step
        piece = _modulate(x_ref[p * step:(p + 1) * step, :], g, shift, scale).astype(BF16)
        if halo and p == 0:
            pos = pl.program_id(0) % tiles_per_seq
            edge = jnp.where(pos > 0, _modulate(x_refs[0][...], g, shift, scale), 0.0).astype(BF16)
            piece, lo = jnp.concatenate([edge, piece], axis=0), 0
        if halo and p == LEAD_PARTS - 1:
            pos = pl.program_id(0) % tiles_per_seq
            edge = jnp.where(pos < tiles_per_seq - 1, _modulate(x_refs[2][...], g, shift, scale), 0.0).astype(BF16)
            piece, hi = jnp.concatenate([piece, edge], axis=0), tm + 2 * halo
        h_scr[lo:hi, :] = piece
        parts.append(piece)
    return parts


def _dot_rows(parts, w):
    return jnp.concatenate([_dot(p, w) for p in parts], axis=0)


def _seq_end_masks(tm, cols, seq_len, tiles_per_seq):
    if tiles_per_seq > 1:
        return None
    seq_row = lax.broadcasted_iota(jnp.int32, (tm, cols), 0) % seq_len
    return seq_row != 0, seq_row != seq_len - 1


def _token_conv3(z, w, tm, halo, masks):
    rows = z.shape[0]
    mid = slice(halo, halo + tm)
    dn = pltpu.roll(z, 1, 0)[mid]
    up = pltpu.roll(z, rows - 1, 0)[mid]
    if masks is not None:
        dn = jnp.where(masks[0], dn, 0.0)
        up = jnp.where(masks[1], up, 0.0)
    return dn * w[0:1] + z[mid] * w[1:2] + up * w[2:3]


def _adaln_kernel(cond_ref, w_ref, b_ref, *rest, side_cast):
    (o_ref,), _ = _split_refs(rest, 1, side_cast)
    a = jax.nn.silu(cond_ref[...]).astype(BF16)
    rows = _dot(a, w_ref[0].astype(BF16)) + b_ref[0]
    for r in range(COND_ROWS):
        o_ref[r] = rows[r:r + 1, :]


def _adaln(cond, ada_w, ada_b, side=()):
    n_out = 6 * D_MODEL
    n_col = n_out // ADA_TN
    return _host_call(
        _adaln_kernel,
        grid=(DEPTH, n_col),
        in_specs=[pl.BlockSpec((COND_ROWS, D_MODEL), lambda l, j: (0, 0)),
                  pl.BlockSpec((1, D_MODEL, ADA_TN), lambda l, j: (l, 0, j)),
                  pl.BlockSpec((1, 1, ADA_TN), lambda l, j: (l, 0, j))],
        args=[cond, ada_w, ada_b.reshape(DEPTH, 1, n_out)],
        out_specs=[pl.BlockSpec((COND_ROWS, 1, ADA_TN), lambda l, j: (l, 0, j))],
        out_shape=[jax.ShapeDtypeStruct((DEPTH * COND_ROWS, 1, n_out), F32)],
        scratch_shapes=[], name="adaln", side=side, step_of=lambda l, j: l * n_col + j)


def _head_rms(z, gain):
    n = z.shape[1]
    w = min(n, MXU_WIDTH)
    r = lax.broadcasted_iota(jnp.int32, (w, w), 0) // HEAD_DIM
    c = lax.broadcasted_iota(jnp.int32, (w, w), 1) // HEAD_DIM
    ones = (r == c).astype(BF16)
    sq = (z * z).astype(BF16)
    ss = jnp.concatenate([_dot(sq[:, k:k + w], ones) for k in range(0, n, w)], axis=1)
    return z * lax.rsqrt(ss * (1.0 / HEAD_DIM) + EPS) * gain


def _rope(z, cos, sin_lo, sin_hi):
    outs = []
    for k in range(0, z.shape[1], LANES):
        blk = z[:, k:k + LANES]
        outs.append(blk * cos + pltpu.roll(blk, LANES - 16, 1) * sin_lo + pltpu.roll(blk, 16, 1) * sin_hi)
    return jnp.concatenate(outs, axis=1)


def _in_even_kernel(*refs, rope, tiles_per_seq, seq_len, cache_out, side_cast):
    halo = _halo_rows(tiles_per_seq)
    n_x = 3 if halo else 1
    x_refs, refs = refs[:n_x], refs[n_x:]
    mod_ref, g_ref, w_ref, qg_ref, kg_ref, cw_ref = refs[:6]
    refs = refs[6:]
    if rope:
        cos_ref, slo_ref, shi_ref = refs[:3]
        refs = refs[3:]
    outs, (h_scr,) = _split_refs(refs, 6 if cache_out else 4, side_cast)
    q_ref, k_ref, v_ref, sc_ref = outs[:4]
    tm = q_ref.shape[0]
    parts = _modulated_parts(x_refs, h_scr, g_ref[...], _mod_chunk(mod_ref, 0), _mod_chunk(mod_ref, 1),
                             tiles_per_seq)
    c0 = 0
    c1 = c0 + ATTN_WIDTH
    c2 = c1 + 2 * KV_WIDTH
    c3 = c2 + CONV_WIDTH
    c4 = c3 + CONV_WIDTH
    c5 = c4 + CONV_WIDTH
    zq = _dot_rows(parts, w_ref[:, c0:c1])[halo:halo + tm]
    h_all = h_scr[...]
    h = h_scr[halo:halo + tm, :]
    q = _head_rms(zq, qg_ref[...] * (HEAD_DIM ** -0.5 * LOG2_E))
    zkv = _dot(h, w_ref[:, c1:c2])
    k = _head_rms(zkv[:, :KV_WIDTH], kg_ref[...])
    if rope:
        cos, slo, shi = cos_ref[...], slo_ref[...], shi_ref[...]
        q = _rope(q, cos, slo, shi)
        k = _rope(k, cos, slo, shi)
    v = zkv[:, KV_WIDTH:]
    q_ref[...] = q
    k_ref[...] = k
    v_ref[...] = v
    if cache_out:
        for cache_ref, rows in zip(outs[4:], (k, v)):
            for s in range(tm // seq_len):
                seq_t = jnp.concatenate([rows[j:j + BLOCK, :].T for j in range(s * seq_len, (s + 1) * seq_len, BLOCK)],
                                        axis=1)
                cache_ref[s, 0] = seq_t.reshape(N_KV_HEADS, HEAD_DIM, seq_len)
    ch = _dot(h_all, w_ref[:, c3:c4]) * _dot(h_all, w_ref[:, c4:c5])
    conv = _token_conv3(ch, cw_ref[...], tm, halo, _seq_end_masks(tm, CONV_WIDTH, seq_len, tiles_per_seq))
    sc_ref[...] = (_dot(h, w_ref[:, c2:c3]) * conv).astype(BF16)


def _in_even(x, mod_l, mod_row, g, w_bf, qg, kg, conv_w, rope_tabs, tm, seq_len, side=(), cache_out=False):
    T = x.shape[0]
    n_in = w_bf.shape[1]
    tiles_per_seq = max(seq_len // tm, 1)
    assert tm % seq_len == 0 or seq_len % tm == 0
    tok = lambda w: pl.BlockSpec((tm, w), lambda i: (i, 0))
    x_specs = _token_specs(T, tm, tiles_per_seq)
    in_specs = x_specs + [_mod_spec(mod_row), _resident((1, D_MODEL)), _resident((D_MODEL, n_in)),
                          _resident((1, ATTN_WIDTH)), _resident((1, KV_WIDTH)), _resident((3, CONV_WIDTH))]
    args = [x] * len(x_specs) + [mod_l, g, w_bf, qg, kg, conv_w]
    if rope_tabs is not None:
        in_specs += [pl.BlockSpec((tm, LANES), lambda i: (i % tiles_per_seq, 0))] * 3
        args += list(rope_tabs)
    outs = ((ATTN_WIDTH, F32), (KV_WIDTH, F32), (KV_WIDTH, F32), (CONV_WIDTH, BF16))
    out_specs = [tok(w) for w, _ in outs]
    out_shape = [jax.ShapeDtypeStruct((T, w), dt) for w, dt in outs]
    if cache_out:
        n_seq = tm // seq_len
        cache_block = (n_seq, 1, N_KV_HEADS, HEAD_DIM, seq_len)
        out_specs += [pl.BlockSpec(cache_block, lambda i: (i, 0, 0, 0, 0))] * 2
        out_shape += [jax.ShapeDtypeStruct((T // seq_len,) + cache_block[1:], F32)] * 2
    scratch = [pltpu.VMEM((tm + 2 * _halo_rows(tiles_per_seq), D_MODEL), BF16)]
    return _host_call(
        functools.partial(_in_even_kernel, rope=rope_tabs is not None, tiles_per_seq=tiles_per_seq, seq_len=seq_len,
                          cache_out=cache_out),
        grid=(T // tm,), in_specs=in_specs, args=args, out_specs=out_specs, out_shape=out_shape,
        scratch_shapes=scratch, name="in_even", side=side, step_of=lambda i: i)


def _stack_heads(q):
    lane_lo = lax.broadcasted_iota(jnp.int32, (BLOCK, LANES), 1) < HEAD_DIM
    heads = []
    for pair in range(N_HEADS // 2):
        kv = (2 * pair) // GQA_GROUP
        qp = q[:, pair * LANES:(pair + 1) * LANES]
        qr = pltpu.roll(qp, HEAD_DIM, 1)
        for half in range(2):
            src = qp if half == kv else qr
            heads.append(jnp.where(lane_lo, src, 0.0) if kv == 0 else jnp.where(lane_lo, 0.0, src))
    return jnp.concatenate(heads, axis=0).astype(BF16)


def _transposed_values(v_blocks, vt_tail=None):
    cols = [v[j:j + BLOCK].T for v in v_blocks for j in range(0, v.shape[0], BLOCK)]
    vt = jnp.concatenate(cols + ([] if vt_tail is None else [vt_tail]), axis=1)
    return jnp.concatenate([vt.astype(BF16), jnp.ones((SUM_ROWS, vt.shape[1]), BF16)], axis=0)


def _scores(q, kcat, cap):
    st = lax.dot_general(kcat, _stack_heads(q), (((1,), (1,)), ((), ())), preferred_element_type=F32)
    if cap is None:
        return st
    n_band = cap.shape[0]
    capped = jnp.minimum(st[:n_band], jnp.concatenate([cap] * N_HEADS, axis=1))
    return jnp.concatenate([capped, st[n_band:]], axis=0)


def _weighted_values(st, vt, sink_row):
    m = jnp.maximum(jnp.max(st, axis=0, keepdims=True), sink_row)
    pt = jnp.exp2(st - m).astype(BF16)
    ot = _dot(vt, pt)
    denom = ot[KV_WIDTH:KV_WIDTH + 1] + jnp.exp2(sink_row - m)
    ot = ot[:KV_WIDTH] / denom
    pairs = []
    for pair in range(N_HEADS // 2):
        kv = (2 * pair) // GQA_GROUP
        dims = slice(kv * HEAD_DIM, (kv + 1) * HEAD_DIM)
        both = [ot[dims, (2 * pair + half) * BLOCK:(2 * pair + half + 1) * BLOCK] for half in range(2)]
        pairs.append(jnp.concatenate(both, axis=0).T)
    return jnp.concatenate(pairs, axis=1)


def _attend_blocks(n_blocks, scores_of, values_of, sink_row, attn_scr):
    st = scores_of(0)
    for j in range(n_blocks):
        st_next = scores_of(j + 1) if j + 1 < n_blocks else None
        attn_scr[j * BLOCK:(j + 1) * BLOCK, :] = _weighted_values(st, values_of(j), sink_row).astype(BF16)
        st = st_next


def _project_out(attn_scr, sc_ref, x_ref, mod_ref, w_ref, o_ref):
    mix = _dot(attn_scr[...], w_ref[:ATTN_WIDTH, :]) + _dot(sc_ref[...], w_ref[ATTN_WIDTH:, :])
    o_ref[...] = x_ref[...] + _mod_chunk(mod_ref, 2) * mix


def _attn_ctx_kernel(sink_ref, q_ref, k_ref, v_ref, sc_ref, x_ref, mod_ref, w_ref, *rest, side_cast):
    (o_ref,), (attn_scr,) = _split_refs(rest, 1, side_cast)
    per_seq = SEQ // BLOCK
    kcats = [k_ref[s * SEQ:(s + 1) * SEQ, :].astype(BF16) for s in range(ATTN_CTX_SEQS)]
    vts = [_transposed_values([v_ref[s * SEQ:(s + 1) * SEQ, :]]) for s in range(ATTN_CTX_SEQS)]
    sink_row = sink_ref[...] * LOG2_E
    _attend_blocks(ATTN_CTX_SEQS * per_seq,
                   lambda j: _scores(q_ref[j * BLOCK:(j + 1) * BLOCK, :], kcats[j // per_seq], None),
                   lambda j: vts[j // per_seq], sink_row, attn_scr)
    _project_out(attn_scr, sc_ref, x_ref, mod_ref, w_ref, o_ref)


def _attn_ctx(q, k, v, sc, x, mod_l, mod_base, sink_row, w_out_bf, side=()):
    T = q.shape[0]
    tq = ATTN_CTX_SEQS * SEQ
    tok = lambda w: pl.BlockSpec((tq, w), lambda i: (i, 0))
    return _host_call(
        _attn_ctx_kernel,
        grid=(T // tq,),
        in_specs=[_resident((1, N_HEADS * BLOCK)), tok(ATTN_WIDTH), tok(KV_WIDTH), tok(KV_WIDTH),
                  tok(CONV_WIDTH), tok(D_MODEL), _mod_spec(lambda i: mod_base + CTX_ROW),
                  _resident((ATTN_WIDTH + CONV_WIDTH, D_MODEL))],
        args=[sink_row, q, k, v, sc, x, mod_l, w_out_bf],
        out_specs=[tok(D_MODEL)], out_shape=[jax.ShapeDtypeStruct((T, D_MODEL), F32)],
        scratch_shapes=[pltpu.VMEM((tq, ATTN_WIDTH), BF16)], name="attn_ctx", side=side, step_of=lambda i: i)


def _band_cap(has_prev, has_next):
    c = lax.broadcasted_iota(jnp.int32, (3 * BLOCK, BLOCK), 0)
    r = lax.broadcasted_iota(jnp.int32, (3 * BLOCK, BLOCK), 1)
    first_prev = r + jnp.where(has_prev, 0, BLOCK)
    last_next = r + 2 * BLOCK - jnp.where(has_next, 0, BLOCK)
    masked = ((c < BLOCK) & (c < first_prev)) | ((c >= 2 * BLOCK) & (c > last_next))
    return jnp.where(masked, NEG_INF, SCORE_CAP)


def _attn_win_kernel(sink_ref, q_ref, kp_ref, kc_ref, kn_ref, vp_ref, vc_ref, vn_ref, ckt_ref, cvt_ref,
                     sc_ref, x_ref, mod_ref, w_ref, *rest, n_steps, side_cast):
    (o_ref,), (attn_scr,) = _split_refs(rest, 1, side_cast)
    i = pl.program_id(1)
    kc, vc = kc_ref[...], vc_ref[...]
    inner = range(0, ATTN_Q_BLOCKS * BLOCK, BLOCK)
    k_blocks = [kp_ref[...]] + [kc[j:j + BLOCK] for j in inner] + [kn_ref[...]]
    v_blocks = [vp_ref[...]] + [vc[j:j + BLOCK] for j in inner] + [vn_ref[...]]
    sink_row = sink_ref[...] * LOG2_E
    ckt = ckt_ref[0]
    ck = jnp.concatenate([ckt[:, j:j + BLOCK].T for j in range(0, PAST_LEN, BLOCK)], axis=0)

    def scores_of(j):
        kcat = jnp.concatenate(k_blocks[j:j + 3] + [ck], axis=0).astype(BF16)
        cap = _band_cap(i > 0 if j == 0 else True, i < n_steps - 1 if j == ATTN_Q_BLOCKS - 1 else True)
        return _scores(q_ref[j * BLOCK:(j + 1) * BLOCK, :], kcat, cap)

    _attend_blocks(ATTN_Q_BLOCKS, scores_of, lambda j: _transposed_values(v_blocks[j:j + 3], cvt_ref[0]),
                   sink_row, attn_scr)
    _project_out(attn_scr, sc_ref, x_ref, mod_ref, w_ref, o_ref)


def _attn_win(q, k, v, ck, cv, sc, x, mod_l, mod_base, sink_row, w_out_bf, side=()):
    T = q.shape[0]
    tq = ATTN_Q_BLOCKS * BLOCK
    n_steps = DEC_SEQ // tq
    nb = DEC_SEQ // BLOCK
    cur = lambda w: pl.BlockSpec((tq, w), lambda b, i: (b * n_steps + i, 0))
    prev = pl.BlockSpec((BLOCK, KV_WIDTH), lambda b, i: (b * nb + jnp.maximum(ATTN_Q_BLOCKS * i - 1, 0), 0))
    nxt = pl.BlockSpec((BLOCK, KV_WIDTH),
                       lambda b, i: (b * nb + jnp.minimum(ATTN_Q_BLOCKS * (i + 1), nb - 1), 0))
    ctx = pl.BlockSpec((1, KV_WIDTH, PAST_LEN), lambda b, i: (b, 0, 0))
    in_specs = [_resident((1, N_HEADS * BLOCK)), cur(ATTN_WIDTH),
                prev, cur(KV_WIDTH), nxt, prev, cur(KV_WIDTH), nxt, ctx, ctx,
                cur(CONV_WIDTH), cur(D_MODEL), _mod_spec(lambda b, i: mod_base + b),
                _resident((ATTN_WIDTH + CONV_WIDTH, D_MODEL))]
    args = [sink_row, q, k, k, k, v, v, v, ck, cv, sc, x, mod_l, w_out_bf]
    return _host_call(
        functools.partial(_attn_win_kernel, n_steps=n_steps),
        grid=(T // DEC_SEQ, n_steps), in_specs=in_specs, args=args,
        out_specs=[cur(D_MODEL)], out_shape=[jax.ShapeDtypeStruct((T, D_MODEL), F32)],
        scratch_shapes=[pltpu.VMEM((tq, ATTN_WIDTH), BF16)], name="attn_win", side=side,
        step_of=lambda b, i: b * n_steps + i)


def _gelu_tanh(x):
    k = 0.7978845608028654
    half = 0.5 * x
    return half + half * jnp.tanh(x * (k + (k * 0.044715) * (x * x)))


def _gmlp_kernel(x_ref, mod_ref, g_ref, win_ref, vg_ref, ws_ref, bs_ref, wout_ref, *rest, side_cast):
    (o_ref,), (gated_scr, h_scr) = _split_refs(rest, 1, side_cast)
    n_chunks = x_ref.shape[0] // CHUNK
    parts = _modulated_parts([x_ref], h_scr, g_ref[...], _mod_chunk(mod_ref, 0), _mod_chunk(mod_ref, 1), 1)
    u = _gelu_tanh(_dot_rows(parts, win_ref[:, :GMLP_WIDTH]))
    v = _gelu_tanh(_dot(h_scr[...], win_ref[:, GMLP_WIDTH:]))
    ms = jnp.mean(v * v, axis=-1, keepdims=True)
    v = (v * lax.rsqrt(ms + EPS) * vg_ref[...]).astype(BF16)
    for grp in range(GMLP_GROUPS):
        lanes = slice(grp * GMLP_GROUP_DIM, (grp + 1) * GMLP_GROUP_DIM)
        rhs = jnp.concatenate([v[n * CHUNK:(n + 1) * CHUNK, lanes] for n in range(n_chunks)], axis=1)
        s = _dot(ws_ref[grp], rhs)
        for n in range(n_chunks):
            rows = slice(n * CHUNK, (n + 1) * CHUNK)
            s_n = s[:, n * GMLP_GROUP_DIM:(n + 1) * GMLP_GROUP_DIM] + bs_ref[grp]
            gated_scr[rows, lanes] = (u[rows, lanes] * s_n).astype(BF16)
    o_ref[...] = x_ref[...] + _mod_chunk(mod_ref, 2) * _dot(gated_scr[...], wout_ref[...])


def _gmlp(x, mod_l, mod_row, g, win_bf, vg, ws_bf, bs_full, wout_bf, tm, side=()):
    T = x.shape[0]
    tok = pl.BlockSpec((tm, D_MODEL), lambda i: (i, 0))
    in_specs = [tok, _mod_spec(mod_row),
                _resident((1, D_MODEL)), _resident((D_MODEL, 2 * GMLP_WIDTH)), _resident((1, GMLP_WIDTH)),
                _resident((GMLP_GROUPS, CHUNK, CHUNK)), _resident((GMLP_GROUPS, CHUNK, GMLP_GROUP_DIM)),
                _resident((GMLP_WIDTH, D_MODEL))]
    args = [x, mod_l, g, win_bf, vg, ws_bf, bs_full, wout_bf]
    return _host_call(
        _gmlp_kernel, grid=(T // tm,), in_specs=in_specs, args=args,
        out_specs=[tok], out_shape=[jax.ShapeDtypeStruct((T, D_MODEL), F32)],
        scratch_shapes=[pltpu.VMEM((tm, GMLP_WIDTH), BF16), pltpu.VMEM((tm, D_MODEL), BF16)],
        name="gmlp", side=side, step_of=lambda i: i)


def _ffn_kernel(*refs, tiles_per_seq, seq_len, side_cast):
    halo = _halo_rows(tiles_per_seq)
    n_x = 3 if halo else 1
    x_refs, refs = refs[:n_x], refs[n_x:]
    mod_ref, g_ref, wup_ref, cw_ref, wdn_ref = refs[:5]
    (o_ref,), (act_scr, h_scr) = _split_refs(refs[5:], 1, side_cast)
    tm = o_ref.shape[0]
    parts = _modulated_parts(x_refs, h_scr, g_ref[...], _mod_chunk(mod_ref, 3), _mod_chunk(mod_ref, 4),
                             tiles_per_seq)
    for j in range(0, D_FF, FFN_COLS):
        width = min(FFN_COLS, D_FF - j)
        masks = _seq_end_masks(tm, width, seq_len, tiles_per_seq)
        gate_cols = slice(j, j + width)
        val_cols = slice(D_FF + j, D_FF + j + width)
        if j == 0:
            up = lambda cols: _dot_rows(parts, wup_ref[:, cols])
        else:
            h = h_scr[...]
            up = lambda cols: _dot(h, wup_ref[:, cols])
        zg = _token_conv3(up(gate_cols), cw_ref[0, :, gate_cols], tm, halo, masks)
        zv = _token_conv3(up(val_cols), cw_ref[0, :, val_cols], tm, halo, masks)
        act_scr[:, gate_cols] = (jax.nn.silu(zg) * zv).astype(BF16)
    x = x_refs[n_x // 2][...]
    o_ref[...] = x + _mod_chunk(mod_ref, 5) * _dot(act_scr[...], wdn_ref[...])


def _ffn(x, mod_l, mod_row, g, wup_bf, conv_w, wdn_bf, layer, tm, seq_len, side=()):
    T = x.shape[0]
    tiles_per_seq = max(seq_len // tm, 1)
    assert tm % seq_len == 0 or seq_len % tm == 0
    x_specs = _token_specs(T, tm, tiles_per_seq)
    in_specs = x_specs + [_mod_spec(mod_row), _resident((1, D_MODEL)), _resident(wup_bf.shape),
                          _layer_slab(conv_w.shape, layer), _resident(wdn_bf.shape)]
    scratch = [pltpu.VMEM((tm, D_FF), BF16), pltpu.VMEM((tm + 2 * _halo_rows(tiles_per_seq), D_MODEL), BF16)]
    return _host_call(
        functools.partial(_ffn_kernel, tiles_per_seq=tiles_per_seq, seq_len=seq_len),
        grid=(T // tm,), in_specs=in_specs, args=[x] * len(x_specs) + [mod_l, g, wup_bf, conv_w, wdn_bf],
        out_specs=[pl.BlockSpec((tm, D_MODEL), lambda i: (i, 0))], out_shape=[jax.ShapeDtypeStruct((T, D_MODEL), F32)],
        scratch_shapes=scratch, name="conv_ffn", side=side, step_of=lambda i: i)


def _rope_tables(n_tokens):
    t = np.arange(n_tokens)
    n_freq = HEAD_DIM // 4
    inv = (ROPE_BASE ** (-np.arange(n_freq, dtype=np.float32) / n_freq)).astype(np.float32)
    row_ang = (t // GRID_W).astype(np.float32)[:, None] * inv
    col_ang = (t % GRID_W).astype(np.float32)[:, None] * inv
    ang = np.concatenate([row_ang, row_ang, col_ang, col_ang], axis=1)
    ang = np.tile(ang, (1, LANES // HEAD_DIM)).astype(np.float64)
    first = (np.arange(LANES) % (2 * n_freq)) < n_freq
    cos, sin = np.cos(ang), np.sin(ang)
    tables = (cos, np.where(first, -sin, 0.0), np.where(first, 0.0, sin))
    return tuple(jnp.asarray(tab.astype(np.float32)) for tab in tables)


class _Bf16Weights:
    def __init__(self, **stacks):
        self.stacks, self.ready = stacks, {}

    def hosted(self, keys, call):
        due = [(name, slab) for name, slab in keys
               if slab < self.stacks[name].shape[0] and (name, slab) not in self.ready]
        outs = call([(self.stacks[name], slab) for name, slab in due])
        for key, cast in zip(due, outs[len(outs) - len(due):]):
            self.ready[key] = cast
        return outs[:len(outs) - len(due)]

    def get(self, name, slab):
        if (name, slab) not in self.ready:
            self.ready[(name, slab)] = _to_bf16(self.stacks[name][slab:slab + 1])[0]
        return self.ready[(name, slab)]


def kernel(x_prompt, x_sample, cache_k, cache_v, c, c_ctx, ada_w, ada_b, norm_mix_g, norm_ffn_g, w_in_even,
           q_norm_g, k_norm_g, sink_logit, short_conv_w, w_out_even, w_in_odd, gmlp_norm_g, w_spatial,
           b_spatial, w_out_odd, w_up, ffn_conv_w, w_down):
    n_p, n_s = BATCH * SEQ, DEC_BATCH * DEC_SEQ
    xp = x_prompt.reshape(n_p, D_MODEL)
    xs = x_sample.reshape(n_s, D_MODEL)
    cond = jnp.concatenate([c, c_ctx[None, :], jnp.zeros((COND_ROWS - DEC_BATCH - 1, D_MODEL), F32)], axis=0)
    bf = _Bf16Weights(in_even=w_in_even, out_even=w_out_even, in_odd=w_in_odd, out_odd=w_out_odd,
                      spatial=w_spatial.reshape(N_ODD, GMLP_GROUPS * CHUNK, CHUNK), up=w_up, down=w_down)

    mod_l, = bf.hosted([("in_even", 0)], lambda side: _adaln(cond, ada_w, ada_b, side))
    rope_tabs = _rope_tables(DEC_SEQ)

    tm_p, tm_s = 1024, 1024
    row_vec = lambda a: a.reshape(1, -1)
    new_k, new_v = [], []
    for l in range(DEPTH):
        base = l * COND_ROWS
        row_p = lambda i, base=base: base + CTX_ROW
        row_s = lambda i, base=base: base + i // (DEC_SEQ // tm_s)
        g_mix = row_vec(norm_mix_g[l])
        nxt = (l + 1) // 2
        if l % 2 == 0:
            e = l // 2
            w_in = bf.get("in_even", e)
            qg = row_vec(jnp.tile(q_norm_g[e], N_HEADS))
            kg = row_vec(jnp.tile(k_norm_g[e], N_KV_HEADS))
            sink = jnp.repeat(sink_logit[e], BLOCK).reshape(1, N_HEADS * BLOCK)
            cw = short_conv_w[e]
            qp, kp, vp, scp, k_layer, v_layer = bf.hosted([("out_even", e)], lambda side: _in_even(
                xp, mod_l, row_p, g_mix, w_in, qg, kg, cw, None, tm_p, SEQ, side, cache_out=True))
            new_k.append(k_layer)
            new_v.append(v_layer)
            w_out = bf.get("out_even", e)
            odd_next = [("in_odd", nxt), ("out_odd", nxt), ("spatial", nxt)]
            xp, = bf.hosted(odd_next, lambda side: _attn_ctx(qp, kp, vp, scp, xp, mod_l, base, sink, w_out, side))
            qs, ks, vs, scs = bf.hosted([("down", l), ("down", l + 1)], lambda side: _in_even(
                xs, mod_l, row_s, g_mix, w_in, qg, kg, cw, rope_tabs, tm_s, DEC_SEQ, side))
            ck = jnp.transpose(cache_k[:, e], (0, 2, 3, 1)).reshape(DEC_BATCH, KV_WIDTH, PAST_LEN)
            cv = jnp.transpose(cache_v[:, e], (0, 2, 3, 1)).reshape(DEC_BATCH, KV_WIDTH, PAST_LEN)
            xs, = bf.hosted([("up", l), ("up", l + 1)], lambda side: _attn_win(
                qs, ks, vs, ck, cv, scs, xs, mod_l, base, sink, w_out, side))
        else:
            o = l // 2
            w_in, w_out = bf.get("in_odd", o), bf.get("out_odd", o)
            vg = row_vec(gmlp_norm_g[o])
            ws = bf.get("spatial", o).reshape(GMLP_GROUPS, CHUNK, CHUNK)
            bs_full = jnp.broadcast_to(b_spatial[o][:, :, None], (GMLP_GROUPS, CHUNK, GMLP_GROUP_DIM))
            xp, = bf.hosted([("down", l)], lambda side: _gmlp(
                xp, mod_l, row_p, g_mix, w_in, vg, ws, bs_full, w_out, tm_p, side))
            xs, = bf.hosted([("up", l)], lambda side: _gmlp(
                xs, mod_l, row_s, g_mix, w_in, vg, ws, bs_full, w_out, tm_s, side))
        g_ffn = row_vec(norm_ffn_g[l])
        w_up_l, w_down_l = bf.get("up", l), bf.get("down", l)
        xp, = _ffn(xp, mod_l, row_p, g_ffn, w_up_l, ffn_conv_w, w_down_l, l, tm_p, SEQ)
        xs, = _ffn(xs, mod_l, row_s, g_ffn, w_up_l, ffn_conv_w, w_down_l, l, tm_s, DEC_SEQ)
    per_layer = lambda parts: jnp.transpose(parts[0] if len(parts) == 1 else jnp.concatenate(parts, axis=1),
                                            (0, 1, 4, 2, 3))
    return (xp.reshape(BATCH, SEQ, D_MODEL), xs.reshape(DEC_BATCH, DEC_SEQ, D_MODEL),
            per_layer(new_k), per_layer(new_v))
```

```python
import functools

import jax
import jax.numpy as jnp
import numpy as np
from jax import lax
from jax.experimental import pallas as pl
from jax.experimental.pallas import tpu as pltpu

F32 = jnp.float32
BF16 = jnp.bfloat16

D_MODEL = 1024
BATCH = 16
SEQ = 256
DEPTH = 2
DEC_BATCH = 4
DEC_SEQ = 2048
PAST_LEN = 512
GRID_W = 64
N_HEADS = 8
N_KV_HEADS = 2
HEAD_DIM = 64
GQA_GROUP = N_HEADS // N_KV_HEADS
ATTN_WIDTH = N_HEADS * HEAD_DIM
KV_WIDTH = N_KV_HEADS * HEAD_DIM
WINDOW = 128
BLOCK = 128
ROPE_BASE = 10000.0
CONV_WIDTH = 512
CHUNK = 128
GMLP_WIDTH = 1024
GMLP_GROUPS = 8
GMLP_GROUP_DIM = GMLP_WIDTH // GMLP_GROUPS
D_FF = 2816
EPS = 1e-6
NEG_INF = -1e30
N_EVEN = (DEPTH + 1) // 2
N_ODD = DEPTH // 2
LOG2_E = 1.4426950408889634

LANES = 128
SUBLANES_F32 = 8
SUBLANES_BF16 = 16
MXU_WIDTH = 256
VMEM_LIMIT_BYTES = 56 * 1024 * 1024

COND_ROWS = 8
CTX_ROW = DEC_BATCH
TOKEN_HALO = SUBLANES_BF16
FFN_COLS = 2 * MXU_WIDTH
LEAD_PARTS = 4
CAST_BLOCK_BYTES = 3 * 1024 * 1024
ADA_TN = 1536
ATTN_Q_BLOCKS = 8
ATTN_CTX_SEQS = 4
SUM_ROWS = SUBLANES_BF16
SCORE_CAP = 3.0e38


def _params(n_axes):
    return pltpu.CompilerParams(dimension_semantics=("parallel",) * n_axes,
                                vmem_limit_bytes=VMEM_LIMIT_BYTES)


def _resident(shape):
    zeros = (0,) * len(shape)
    return pl.BlockSpec(shape, lambda *_: zeros, pipeline_mode=pl.Buffered(1))


def _layer_slab(shape, layer):
    return pl.BlockSpec((1,) + tuple(shape[1:]), lambda *_: (layer, 0, 0), pipeline_mode=pl.Buffered(1))


def _mod_spec(mod_row):
    return pl.BlockSpec((1, 1, 6 * D_MODEL), lambda *idx: (mod_row(*idx), 0, 0))


def _dot(a, b):
    return jnp.dot(a, b, preferred_element_type=F32)


def _cast_kernel(x_ref, o_ref):
    o_ref[...] = x_ref[...].astype(BF16)


def _to_bf16(w):
    n, rows, cols = w.shape
    fits = [r for r in range(SUBLANES_BF16, rows + 1, SUBLANES_BF16)
            if rows % r == 0 and r * cols * 4 <= CAST_BLOCK_BYTES]
    tr = max(fits)
    spec = pl.BlockSpec((1, tr, cols), lambda i, j: (i, j, 0))
    return pl.pallas_call(
        _cast_kernel, grid=(n, rows // tr), in_specs=[spec], out_specs=spec,
        out_shape=jax.ShapeDtypeStruct(w.shape, BF16), compiler_params=_params(2), name="to_bf16",
    )(w)


def _side_cast_specs(side, n_steps, step_of):
    w, layer = side
    _, rows, cols = w.shape
    tr = rows // n_steps
    assert tr * n_steps == rows and tr % SUBLANES_BF16 == 0
    return (pl.BlockSpec((1, tr, cols), lambda *idx: (layer, step_of(*idx), 0)),
            pl.BlockSpec((tr, cols), lambda *idx: (step_of(*idx), 0)),
            jax.ShapeDtypeStruct((rows, cols), BF16))


def _split_refs(rest, n_out, side_cast):
    n = side_cast
    side_ins, outs, side_outs, scratch = rest[:n], rest[n:n + n_out], rest[n + n_out:2 * n + n_out], rest[2 * n + n_out:]
    for side_in, side_out in zip(side_ins, side_outs):
        side_out[...] = side_in[0].astype(BF16)
    return outs, scratch


def _host_call(kernel, *, grid, in_specs, args, out_specs, out_shape, scratch_shapes, name, side, step_of):
    in_specs, args, out_specs, out_shape = list(in_specs), list(args), list(out_specs), list(out_shape)
    n_steps = 1
    for extent in grid:
        n_steps *= extent
    sides = [_side_cast_specs(one, n_steps, step_of) for one in side]
    in_specs += [spec for spec, _, _ in sides]
    args += [stack for stack, _ in side]
    out_specs += [spec for _, spec, _ in sides]
    out_shape += [shape for _, _, shape in sides]
    return pl.pallas_call(
        functools.partial(kernel, side_cast=len(side)),
        grid=grid, in_specs=in_specs, out_specs=out_specs, out_shape=out_shape, scratch_shapes=scratch_shapes,
        compiler_params=_params(len(grid)), name=name,
    )(*args)


def _modulate(x, g, shift, scale):
    ms = jnp.mean(x * x, axis=-1, keepdims=True)
    return (x * lax.rsqrt(ms + EPS) * g) * (1.0 + scale) + shift


def _mod_chunk(mod_ref, k):
    return mod_ref[0, :, k * D_MODEL:(k + 1) * D_MODEL]


def _halo_rows(tiles_per_seq):
    return TOKEN_HALO if tiles_per_seq > 1 else 0


def _token_specs(T, tm, tiles_per_seq):
    tok = pl.BlockSpec((tm, D_MODEL), lambda i: (i, 0))
    if tiles_per_seq == 1:
        return [tok]
    per_tile = tm // TOKEN_HALO
    last = T // TOKEN_HALO - 1
    prev = pl.BlockSpec((TOKEN_HALO, D_MODEL), lambda i: (jnp.maximum(i * per_tile - 1, 0), 0))
    nxt = pl.BlockSpec((TOKEN_HALO, D_MODEL), lambda i: (jnp.minimum((i + 1) * per_tile, last), 0))
    return [prev, tok, nxt]


def _modulated_parts(x_refs, h_scr, g, shift, scale, tiles_per_seq):
    halo = _halo_rows(tiles_per_seq)
    x_ref = x_refs[len(x_refs) // 2]
    tm = x_ref.shape[0]
    step = tm // LEAD_PARTS
    parts = []
    for p in range(LEAD_PARTS):
        lo, hi = halo + p * step, halo + (p + 1) * step
        piece = _modulate(x_ref[p * step:(p + 1) * step, :], g, shift, scale).astype(BF16)
        if halo and p == 0:
            pos = pl.program_id(0) % tiles_per_seq
            edge = jnp.where(pos > 0, _modulate(x_refs[0][...], g, shift, scale), 0.0).astype(BF16)
            piece, lo = jnp.concatenate([edge, piece], axis=0), 0
        if halo and p == LEAD_PARTS - 1:
            pos = pl.program_id(0) % tiles_per_seq
            edge = jnp.where(pos < tiles_per_seq - 1, _modulate(x_refs[2][...], g, shift, scale), 0.0).astype(BF16)
            piece, hi = jnp.concatenate([piece, edge], axis=0), tm + 2 * halo
        h_scr[lo:hi, :] = piece
        parts.append(piece)
    return parts


def _dot_rows(parts, w):
    return jnp.concatenate([_dot(p, w) for p in parts], axis=0)


def _seq_end_masks(tm, cols, seq_len, tiles_per_seq):
    if tiles_per_seq > 1:
        return None
    seq_row = lax.broadcasted_iota(jnp.int32, (tm, cols), 0) % seq_len
    return seq_row != 0, seq_row != seq_len - 1


def _token_conv3(z, w, tm, halo, masks):
    rows = z.shape[0]
    mid = slice(halo, halo + tm)
    dn = pltpu.roll(z, 1, 0)[mid]
    up = pltpu.roll(z, rows - 1, 0)[mid]
    if masks is not None:
        dn = jnp.where(masks[0], dn, 0.0)
        up = jnp.where(masks[1], up, 0.0)
    return dn * w[0:1] + z[mid] * w[1:2] + up * w[2:3]


def _adaln_kernel(cond_ref, w_ref, b_ref, *rest, side_cast):
    (o_ref,), _ = _split_refs(rest, 1, side_cast)
    a = jax.nn.silu(cond_ref[...]).astype(BF16)
    rows = _dot(a, w_ref[0].astype(BF16)) + b_ref[0]
    for r in range(COND_ROWS):
        o_ref[r] = rows[r:r + 1, :]


def _adaln(cond, ada_w, ada_b, side=()):
    n_out = 6 * D_MODEL
    n_col = n_out // ADA_TN
    return _host_call(
        _adaln_kernel,
        grid=(DEPTH, n_col),
        in_specs=[pl.BlockSpec((COND_ROWS, D_MODEL), lambda l, j: (0, 0)),
                  pl.BlockSpec((1, D_MODEL, ADA_TN), lambda l, j: (l, 0, j)),
                  pl.BlockSpec((1, 1, ADA_TN), lambda l, j: (l, 0, j))],
        args=[cond, ada_w, ada_b.reshape(DEPTH, 1, n_out)],
        out_specs=[pl.BlockSpec((COND_ROWS, 1, ADA_TN), lambda l, j: (l, 0, j))],
        out_shape=[jax.ShapeDtypeStruct((DEPTH * COND_ROWS, 1, n_out), F32)],
        scratch_shapes=[], name="adaln", side=side, step_of=lambda l, j: l * n_col + j)


def _head_rms(z, gain):
    n = z.shape[1]
    w = min(n, MXU_WIDTH)
    r = lax.broadcasted_iota(jnp.int32, (w, w), 0) // HEAD_DIM
    c = lax.broadcasted_iota(jnp.int32, (w, w), 1) // HEAD_DIM
    ones = (r == c).astype(BF16)
    sq = (z * z).astype(BF16)
    ss = jnp.concatenate([_dot(sq[:, k:k + w], ones) for k in range(0, n, w)], axis=1)
    return z * lax.rsqrt(ss * (1.0 / HEAD_DIM) + EPS) * gain


def _rope(z, cos, sin_lo, sin_hi):
    outs = []
    for k in range(0, z.shape[1], LANES):
        blk = z[:, k:k + LANES]
        outs.append(blk * cos + pltpu.roll(blk, LANES - 16, 1) * sin_lo + pltpu.roll(blk, 16, 1) * sin_hi)
    return jnp.concatenate(outs, axis=1)


def _in_even_kernel(*refs, rope, tiles_per_seq, seq_len, cache_out, side_cast):
    halo = _halo_rows(tiles_per_seq)
    n_x = 3 if halo else 1
    x_refs, refs = refs[:n_x], refs[n_x:]
    mod_ref, g_ref, w_ref, qg_ref, kg_ref, cw_ref = refs[:6]
    refs = refs[6:]
    if rope:
        cos_ref, slo_ref, shi_ref = refs[:3]
        refs = refs[3:]
    outs, (h_scr,) = _split_refs(refs, 6 if cache_out else 4, side_cast)
    q_ref, k_ref, v_ref, sc_ref = outs[:4]
    tm = q_ref.shape[0]
    parts = _modulated_parts(x_refs, h_scr, g_ref[...], _mod_chunk(mod_ref, 0), _mod_chunk(mod_ref, 1),
                             tiles_per_seq)
    c0 = 0
    c1 = c0 + ATTN_WIDTH
    c2 = c1 + 2 * KV_WIDTH
    c3 = c2 + CONV_WIDTH
    c4 = c3 + CONV_WIDTH
    c5 = c4 + CONV_WIDTH
    zq = _dot_rows(parts, w_ref[:, c0:c1])[halo:halo + tm]
    h_all = h_scr[...]
    h = h_scr[halo:halo + tm, :]
    q = _head_rms(zq, qg_ref[...] * (HEAD_DIM ** -0.5 * LOG2_E))
    zkv = _dot(h, w_ref[:, c1:c2])
    k = _head_rms(zkv[:, :KV_WIDTH], kg_ref[...])
    if rope:
        cos, slo, shi = cos_ref[...], slo_ref[...], shi_ref[...]
        q = _rope(q, cos, slo, shi)
        k = _rope(k, cos, slo, shi)
    v = zkv[:, KV_WIDTH:]
    q_ref[...] = q
    k_ref[...] = k
    v_ref[...] = v
    if cache_out:
        for cache_ref, rows in zip(outs[4:], (k, v)):
            for s in range(tm // seq_len):
                seq_t = jnp.concatenate([rows[j:j + BLOCK, :].T for j in range(s * seq_len, (s + 1) * seq_len, BLOCK)],
                                        axis=1)
                cache_ref[s, 0] = seq_t.reshape(N_KV_HEADS, HEAD_DIM, seq_len)
    ch = _dot(h_all, w_ref[:, c3:c4]) * _dot(h_all, w_ref[:, c4:c5])
    conv = _token_conv3(ch, cw_ref[...], tm, halo, _seq_end_masks(tm, CONV_WIDTH, seq_len, tiles_per_seq))
    sc_ref[...] = (_dot(h, w_ref[:, c2:c3]) * conv).astype(BF16)


def _in_even(x, mod_l, mod_row, g, w_bf, qg, kg, conv_w, rope_tabs, tm, seq_len, side=(), cache_out=False):
    T = x.shape[0]
    n_in = w_bf.shape[1]
    tiles_per_seq = max(seq_len // tm, 1)
    assert tm % seq_len == 0 or seq_len % tm == 0
    tok = lambda w: pl.BlockSpec((tm, w), lambda i: (i, 0))
    x_specs = _token_specs(T, tm, tiles_per_seq)
    in_specs = x_specs + [_mod_spec(mod_row), _resident((1, D_MODEL)), _resident((D_MODEL, n_in)),
                          _resident((1, ATTN_WIDTH)), _resident((1, KV_WIDTH)), _resident((3, CONV_WIDTH))]
    args = [x] * len(x_specs) + [mod_l, g, w_bf, qg, kg, conv_w]
    if rope_tabs is not None:
        in_specs += [pl.BlockSpec((tm, LANES), lambda i: (i % tiles_per_seq, 0))] * 3
        args += list(rope_tabs)
    outs = ((ATTN_WIDTH, F32), (KV_WIDTH, F32), (KV_WIDTH, F32), (CONV_WIDTH, BF16))
    out_specs = [tok(w) for w, _ in outs]
    out_shape = [jax.ShapeDtypeStruct((T, w), dt) for w, dt in outs]
    if cache_out:
        n_seq = tm // seq_len
        cache_block = (n_seq, 1, N_KV_HEADS, HEAD_DIM, seq_len)
        out_specs += [pl.BlockSpec(cache_block, lambda i: (i, 0, 0, 0, 0))] * 2
        out_shape += [jax.ShapeDtypeStruct((T // seq_len,) + cache_block[1:], F32)] * 2
    scratch = [pltpu.VMEM((tm + 2 * _halo_rows(tiles_per_seq), D_MODEL), BF16)]
    return _host_call(
        functools.partial(_in_even_kernel, rope=rope_tabs is not None, tiles_per_seq=tiles_per_seq, seq_len=seq_len,
                          cache_out=cache_out),
        grid=(T // tm,), in_specs=in_specs, args=args, out_specs=out_specs, out_shape=out_shape,
        scratch_shapes=scratch, name="in_even", side=side, step_of=lambda i: i)


def _stack_heads(q):
    lane_lo = lax.broadcasted_iota(jnp.int32, (BLOCK, LANES), 1) < HEAD_DIM
    heads = []
    for pair in range(N_HEADS // 2):
        kv = (2 * pair) // GQA_GROUP
        qp = q[:, pair * LANES:(pair + 1) * LANES]
        qr = pltpu.roll(qp, HEAD_DIM, 1)
        for half in range(2):
            src = qp if half == kv else qr
            heads.append(jnp.where(lane_lo, src, 0.0) if kv == 0 else jnp.where(lane_lo, 0.0, src))
    return jnp.concatenate(heads, axis=0).astype(BF16)


def _transposed_values(v_blocks, vt_tail=None):
    cols = [v[j:j + BLOCK].T for v in v_blocks for j in range(0, v.shape[0], BLOCK)]
    vt = jnp.concatenate(cols + ([] if vt_tail is None else [vt_tail]), axis=1)
    return jnp.concatenate([vt.astype(BF16), jnp.ones((SUM_ROWS, vt.shape[1]), BF16)], axis=0)


def _scores(q, kcat, cap):
    st = lax.dot_general(kcat, _stack_heads(q), (((1,), (1,)), ((), ())), preferred_element_type=F32)
    if cap is None:
        return st
    n_band = cap.shape[0]
    capped = jnp.minimum(st[:n_band], jnp.concatenate([cap] * N_HEADS, axis=1))
    return jnp.concatenate([capped, st[n_band:]], axis=0)


def _weighted_values(st, vt, sink_row):
    m = jnp.maximum(jnp.max(st, axis=0, keepdims=True), sink_row)
    pt = jnp.exp2(st - m).astype(BF16)
    ot = _dot(vt, pt)
    denom = ot[KV_WIDTH:KV_WIDTH + 1] + jnp.exp2(sink_row - m)
    ot = ot[:KV_WIDTH] / denom
    pairs = []
    for pair in range(N_HEADS // 2):
        kv = (2 * pair) // GQA_GROUP
        dims = slice(kv * HEAD_DIM, (kv + 1) * HEAD_DIM)
        both = [ot[dims, (2 * pair + half) * BLOCK:(2 * pair + half + 1) * BLOCK] for half in range(2)]
        pairs.append(jnp.concatenate(both, axis=0).T)
    return jnp.concatenate(pairs, axis=1)


def _attend_blocks(n_blocks, scores_of, values_of, sink_row, attn_scr):
    st = scores_of(0)
    for j in range(n_blocks):
        st_next = scores_of(j + 1) if j + 1 < n_blocks else None
        attn_scr[j * BLOCK:(j + 1) * BLOCK, :] = _weighted_values(st, values_of(j), sink_row).astype(BF16)
        st = st_next


def _project_out(attn_scr, sc_ref, x_ref, mod_ref, w_ref, o_ref):
    mix = _dot(attn_scr[...], w_ref[:ATTN_WIDTH, :]) + _dot(sc_ref[...], w_ref[ATTN_WIDTH:, :])
    o_ref[...] = x_ref[...] + _mod_chunk(mod_ref, 2) * mix


def _attn_ctx_kernel(sink_ref, q_ref, k_ref, v_ref, sc_ref, x_ref, mod_ref, w_ref, *rest, side_cast):
    (o_ref,), (attn_scr,) = _split_refs(rest, 1, side_cast)
    per_seq = SEQ // BLOCK
    kcats = [k_ref[s * SEQ:(s + 1) * SEQ, :].astype(BF16) for s in range(ATTN_CTX_SEQS)]
    vts = [_transposed_values([v_ref[s * SEQ:(s + 1) * SEQ, :]]) for s in range(ATTN_CTX_SEQS)]
    sink_row = sink_ref[...] * LOG2_E
    _attend_blocks(ATTN_CTX_SEQS * per_seq,
                   lambda j: _scores(q_ref[j * BLOCK:(j + 1) * BLOCK, :], kcats[j // per_seq], None),
                   lambda j: vts[j // per_seq], sink_row, attn_scr)
    _project_out(attn_scr, sc_ref, x_ref, mod_ref, w_ref, o_ref)


def _attn_ctx(q, k, v, sc, x, mod_l, mod_base, sink_row, w_out_bf, side=()):
    T = q.shape[0]
    tq = ATTN_CTX_SEQS * SEQ
    tok = lambda w: pl.BlockSpec((tq, w), lambda i: (i, 0))
    return _host_call(
        _attn_ctx_kernel,
        grid=(T // tq,),
        in_specs=[_resident((1, N_HEADS * BLOCK)), tok(ATTN_WIDTH), tok(KV_WIDTH), tok(KV_WIDTH),
                  tok(CONV_WIDTH), tok(D_MODEL), _mod_spec(lambda i: mod_base + CTX_ROW),
                  _resident((ATTN_WIDTH + CONV_WIDTH, D_MODEL))],
        args=[sink_row, q, k, v, sc, x, mod_l, w_out_bf],
        out_specs=[tok(D_MODEL)], out_shape=[jax.ShapeDtypeStruct((T, D_MODEL), F32)],
        scratch_shapes=[pltpu.VMEM((tq, ATTN_WIDTH), BF16)], name="attn_ctx", side=side, step_of=lambda i: i)


def _band_cap(has_prev, has_next):
    c = lax.broadcasted_iota(jnp.int32, (3 * BLOCK, BLOCK), 0)
    r = lax.broadcasted_iota(jnp.int32, (3 * BLOCK, BLOCK), 1)
    first_prev = r + jnp.where(has_prev, 0, BLOCK)
    last_next = r + 2 * BLOCK - jnp.where(has_next, 0, BLOCK)
    masked = ((c < BLOCK) & (c < first_prev)) | ((c >= 2 * BLOCK) & (c > last_next))
    return jnp.where(masked, NEG_INF, SCORE_CAP)


def _attn_win_kernel(sink_ref, q_ref, kp_ref, kc_ref, kn_ref, vp_ref, vc_ref, vn_ref, ckt_ref, cvt_ref,
                     sc_ref, x_ref, mod_ref, w_ref, *rest, n_steps, side_cast):
    (o_ref,), (attn_scr,) = _split_refs(rest, 1, side_cast)
    i = pl.program_id(1)
    kc, vc = kc_ref[...], vc_ref[...]
    inner = range(0, ATTN_Q_BLOCKS * BLOCK, BLOCK)
    k_blocks = [kp_ref[...]] + [kc[j:j + BLOCK] for j in inner] + [kn_ref[...]]
    v_blocks = [vp_ref[...]] + [vc[j:j + BLOCK] for j in inner] + [vn_ref[...]]
    sink_row = sink_ref[...] * LOG2_E
    ckt = ckt_ref[0]
    ck = jnp.concatenate([ckt[:, j:j + BLOCK].T for j in range(0, PAST_LEN, BLOCK)], axis=0)

    def scores_of(j):
        kcat = jnp.concatenate(k_blocks[j:j + 3] + [ck], axis=0).astype(BF16)
        cap = _band_cap(i > 0 if j == 0 else True, i < n_steps - 1 if j == ATTN_Q_BLOCKS - 1 else True)
        return _scores(q_ref[j * BLOCK:(j + 1) * BLOCK, :], kcat, cap)

    _attend_blocks(ATTN_Q_BLOCKS, scores_of, lambda j: _transposed_values(v_blocks[j:j + 3], cvt_ref[0]),
                   sink_row, attn_scr)
    _project_out(attn_scr, sc_ref, x_ref, mod_ref, w_ref, o_ref)


def _attn_win(q, k, v, ck, cv, sc, x, mod_l, mod_base, sink_row, w_out_bf, side=()):
    T = q.shape[0]
    tq = ATTN_Q_BLOCKS * BLOCK
    n_steps = DEC_SEQ // tq
    nb = DEC_SEQ // BLOCK
    cur = lambda w: pl.BlockSpec((tq, w), lambda b, i: (b * n_steps + i, 0))
    prev = pl.BlockSpec((BLOCK, KV_WIDTH), lambda b, i: (b * nb + jnp.maximum(ATTN_Q_BLOCKS * i - 1, 0), 0))
    nxt = pl.BlockSpec((BLOCK, KV_WIDTH),
                       lambda b, i: (b * nb + jnp.minimum(ATTN_Q_BLOCKS * (i + 1), nb - 1), 0))
    ctx = pl.BlockSpec((1, KV_WIDTH, PAST_LEN), lambda b, i: (b, 0, 0))
    in_specs = [_resident((1, N_HEADS * BLOCK)), cur(ATTN_WIDTH),
                prev, cur(KV_WIDTH), nxt, prev, cur(KV_WIDTH), nxt, ctx, ctx,
                cur(CONV_WIDTH), cur(D_MODEL), _mod_spec(lambda b, i: mod_base + b),
                _resident((ATTN_WIDTH + CONV_WIDTH, D_MODEL))]
    args = [sink_row, q, k, k, k, v, v, v, ck, cv, sc, x, mod_l, w_out_bf]
    return _host_call(
        functools.partial(_attn_win_kernel, n_steps=n_steps),
        grid=(T // DEC_SEQ, n_steps), in_specs=in_specs, args=args,
        out_specs=[cur(D_MODEL)], out_shape=[jax.ShapeDtypeStruct((T, D_MODEL), F32)],
        scratch_shapes=[pltpu.VMEM((tq, ATTN_WIDTH), BF16)], name="attn_win", side=side,
        step_of=lambda b, i: b * n_steps + i)


def _gelu_tanh(x):
    k = 0.7978845608028654
    half = 0.5 * x
    return half + half * jnp.tanh(x * (k + (k * 0.044715) * (x * x)))


def _gmlp_kernel(x_ref, mod_ref, g_ref, win_ref, vg_ref, ws_ref, bs_ref, wout_ref, *rest, side_cast):
    (o_ref,), (gated_scr, h_scr) = _split_refs(rest, 1, side_cast)
    n_chunks = x_ref.shape[0] // CHUNK
    parts = _modulated_parts([x_ref], h_scr, g_ref[...], _mod_chunk(mod_ref, 0), _mod_chunk(mod_ref, 1), 1)
    u = _gelu_tanh(_dot_rows(parts, win_ref[:, :GMLP_WIDTH]))
    v = _gelu_tanh(_dot(h_scr[...], win_ref[:, GMLP_WIDTH:]))
    ms = jnp.mean(v * v, axis=-1, keepdims=True)
    v = (v * lax.rsqrt(ms + EPS) * vg_ref[...]).astype(BF16)
    for grp in range(GMLP_GROUPS):
        lanes = slice(grp * GMLP_GROUP_DIM, (grp + 1) * GMLP_GROUP_DIM)
        rhs = jnp.concatenate([v[n * CHUNK:(n + 1) * CHUNK, lanes] for n in range(n_chunks)], axis=1)
        s = _dot(ws_ref[grp], rhs)
        for n in range(n_chunks):
            rows = slice(n * CHUNK, (n + 1) * CHUNK)
            s_n = s[:, n * GMLP_GROUP_DIM:(n + 1) * GMLP_GROUP_DIM] + bs_ref[grp]
            gated_scr[rows, lanes] = (u[rows, lanes] * s_n).astype(BF16)
    o_ref[...] = x_ref[...] + _mod_chunk(mod_ref, 2) * _dot(gated_scr[...], wout_ref[...])


def _gmlp(x, mod_l, mod_row, g, win_bf, vg, ws_bf, bs_full, wout_bf, tm, side=()):
    T = x.shape[0]
    tok = pl.BlockSpec((tm, D_MODEL), lambda i: (i, 0))
    in_specs = [tok, _mod_spec(mod_row),
                _resident((1, D_MODEL)), _resident((D_MODEL, 2 * GMLP_WIDTH)), _resident((1, GMLP_WIDTH)),
                _resident((GMLP_GROUPS, CHUNK, CHUNK)), _resident((GMLP_GROUPS, CHUNK, GMLP_GROUP_DIM)),
                _resident((GMLP_WIDTH, D_MODEL))]
    args = [x, mod_l, g, win_bf, vg, ws_bf, bs_full, wout_bf]
    return _host_call(
        _gmlp_kernel, grid=(T // tm,), in_specs=in_specs, args=args,
        out_specs=[tok], out_shape=[jax.ShapeDtypeStruct((T, D_MODEL), F32)],
        scratch_shapes=[pltpu.VMEM((tm, GMLP_WIDTH), BF16), pltpu.VMEM((tm, D_MODEL), BF16)],
        name="gmlp", side=side, step_of=lambda i: i)


def _ffn_kernel(*refs, tiles_per_seq, seq_len, side_cast):
    halo = _halo_rows(tiles_per_seq)
    n_x = 3 if halo else 1
    x_refs, refs = refs[:n_x], refs[n_x:]
    mod_ref, g_ref, wup_ref, cw_ref, wdn_ref = refs[:5]
    (o_ref,), (act_scr, h_scr) = _split_refs(refs[5:], 1, side_cast)
    tm = o_ref.shape[0]
    parts = _modulated_parts(x_refs, h_scr, g_ref[...], _mod_chunk(mod_ref, 3), _mod_chunk(mod_ref, 4),
                             tiles_per_seq)
    for j in range(0, D_FF, FFN_COLS):
        width = min(FFN_COLS, D_FF - j)
        masks = _seq_end_masks(tm, width, seq_len, tiles_per_seq)
        gate_cols = slice(j, j + width)
        val_cols = slice(D_FF + j, D_FF + j + width)
        if j == 0:
            up = lambda cols: _dot_rows(parts, wup_ref[:, cols])
        else:
            h = h_scr[...]
            up = lambda cols: _dot(h, wup_ref[:, cols])
        zg = _token_conv3(up(gate_cols), cw_ref[0, :, gate_cols], tm, halo, masks)
        zv = _token_conv3(up(val_cols), cw_ref[0, :, val_cols], tm, halo, masks)
        act_scr[:, gate_cols] = (jax.nn.silu(zg) * zv).astype(BF16)
    x = x_refs[n_x // 2][...]
    o_ref[...] = x + _mod_chunk(mod_ref, 5) * _dot(act_scr[...], wdn_ref[...])


def _ffn(x, mod_l, mod_row, g, wup_bf, conv_w, wdn_bf, layer, tm, seq_len, side=()):
    T = x.shape[0]
    tiles_per_seq = max(seq_len // tm, 1)
    assert tm % seq_len == 0 or seq_len % tm == 0
    x_specs = _token_specs(T, tm, tiles_per_seq)
    in_specs = x_specs + [_mod_spec(mod_row), _resident((1, D_MODEL)), _resident(wup_bf.shape),
                          _layer_slab(conv_w.shape, layer), _resident(wdn_bf.shape)]
    scratch = [pltpu.VMEM((tm, D_FF), BF16), pltpu.VMEM((tm + 2 * _halo_rows(tiles_per_seq), D_MODEL), BF16)]
    return _host_call(
        functools.partial(_ffn_kernel, tiles_per_seq=tiles_per_seq, seq_len=seq_len),
        grid=(T // tm,), in_specs=in_specs, args=[x] * len(x_specs) + [mod_l, g, wup_bf, conv_w, wdn_bf],
        out_specs=[pl.BlockSpec((tm, D_MODEL), lambda i: (i, 0))], out_shape=[jax.ShapeDtypeStruct((T, D_MODEL), F32)],
        scratch_shapes=scratch, name="conv_ffn", side=side, step_of=lambda i: i)


def _rope_tables(n_tokens):
    t = np.arange(n_tokens)
    n_freq = HEAD_DIM // 4
    inv = (ROPE_BASE ** (-np.arange(n_freq, dtype=np.float32) / n_freq)).astype(np.float32)
    row_ang = (t // GRID_W).astype(np.float32)[:, None] * inv
    col_ang = (t % GRID_W).astype(np.float32)[:, None] * inv
    ang = np.concatenate([row_ang, row_ang, col_ang, col_ang], axis=1)
    ang = np.tile(ang, (1, LANES // HEAD_DIM)).astype(np.float64)
    first = (np.arange(LANES) % (2 * n_freq)) < n_freq
    cos, sin = np.cos(ang), np.sin(ang)
    tables = (cos, np.where(first, -sin, 0.0), np.where(first, 0.0, sin))
    return tuple(jnp.asarray(tab.astype(np.float32)) for tab in tables)


class _Bf16Weights:
    def __init__(self, **stacks):
        self.stacks, self.ready = stacks, {}

    def hosted(self, keys, call):
        due = [(name, slab) for name, slab in keys
               if slab < self.stacks[name].shape[0] and (name, slab) not in self.ready]
        outs = call([(self.stacks[name], slab) for name, slab in due])
        for key, cast in zip(due, outs[len(outs) - len(due):]):
            self.ready[key] = cast
        return outs[:len(outs) - len(due)]

    def get(self, name, slab):
        if (name, slab) not in self.ready:
            self.ready[(name, slab)] = _to_bf16(self.stacks[name][slab:slab + 1])[0]
        return self.ready[(name, slab)]


def kernel(x_prompt, x_sample, cache_k, cache_v, c, c_ctx, ada_w, ada_b, norm_mix_g, norm_ffn_g, w_in_even,
           q_norm_g, k_norm_g, sink_logit, short_conv_w, w_out_even, w_in_odd, gmlp_norm_g, w_spatial,
           b_spatial, w_out_odd, w_up, ffn_conv_w, w_down):
    n_p, n_s = BATCH * SEQ, DEC_BATCH * DEC_SEQ
    xp = x_prompt.reshape(n_p, D_MODEL)
    xs = x_sample.reshape(n_s, D_MODEL)
    cond = jnp.concatenate([c, c_ctx[None, :], jnp.zeros((COND_ROWS - DEC_BATCH - 1, D_MODEL), F32)], axis=0)
    bf = _Bf16Weights(in_even=w_in_even, out_even=w_out_even, in_odd=w_in_odd, out_odd=w_out_odd,
                      spatial=w_spatial.reshape(N_ODD, GMLP_GROUPS * CHUNK, CHUNK), up=w_up, down=w_down)

    mod_l, = bf.hosted([("in_even", 0)], lambda side: _adaln(cond, ada_w, ada_b, side))
    rope_tabs = _rope_tables(DEC_SEQ)

    tm_p, tm_s = 1024, 1024
    row_vec = lambda a: a.reshape(1, -1)
    new_k, new_v = [], []
    for l in range(DEPTH):
        base = l * COND_ROWS
        row_p = lambda i, base=base: base + CTX_ROW
        row_s = lambda i, base=base: base + i // (DEC_SEQ // tm_s)
        g_mix = row_vec(norm_mix_g[l])
        nxt = (l + 1) // 2
        if l % 2 == 0:
            e = l // 2
            w_in = bf.get("in_even", e)
            qg = row_vec(jnp.tile(q_norm_g[e], N_HEADS))
            kg = row_vec(jnp.tile(k_norm_g[e], N_KV_HEADS))
            sink = jnp.repeat(sink_logit[e], BLOCK).reshape(1, N_HEADS * BLOCK)
            cw = short_conv_w[e]
            qp, kp, vp, scp, k_layer, v_layer = bf.hosted([("out_even", e)], lambda side: _in_even(
                xp, mod_l, row_p, g_mix, w_in, qg, kg, cw, None, tm_p, SEQ, side, cache_out=True))
            new_k.append(k_layer)
            new_v.append(v_layer)
            w_out = bf.get("out_even", e)
            odd_next = [("in_odd", nxt), ("out_odd", nxt), ("spatial", nxt)]
            xp, = bf.hosted(odd_next, lambda side: _attn_ctx(qp, kp, vp, scp, xp, mod_l, base, sink, w_out, side))
            qs, ks, vs, scs = bf.hosted([("down", l), ("down", l + 1)], lambda side: _in_even(
                xs, mod_l, row_s, g_mix, w_in, qg, kg, cw, rope_tabs, tm_s, DEC_SEQ, side))
            ck = jnp.transpose(cache_k[:, e], (0, 2, 3, 1)).reshape(DEC_BATCH, KV_WIDTH, PAST_LEN)
            cv = jnp.transpose(cache_v[:, e], (0, 2, 3, 1)).reshape(DEC_BATCH, KV_WIDTH, PAST_LEN)
            xs, = bf.hosted([("up", l), ("up", l + 1)], lambda side: _attn_win(
                qs, ks, vs, ck, cv, scs, xs, mod_l, base, sink, w_out, side))
        else:
            o = l // 2
            w_in, w_out = bf.get("in_odd", o), bf.get("out_odd", o)
            vg = row_vec(gmlp_norm_g[o])
            ws = bf.get("spatial", o).reshape(GMLP_GROUPS, CHUNK, CHUNK)
            bs_full = jnp.broadcast_to(b_spatial[o][:, :, None], (GMLP_GROUPS, CHUNK, GMLP_GROUP_DIM))
            xp, = bf.hosted([("down", l)], lambda side: _gmlp(
                xp, mod_l, row_p, g_mix, w_in, vg, ws, bs_full, w_out, tm_p, side))
            xs, = bf.hosted([("up", l)], lambda side: _gmlp(
                xs, mod_l, row_s, g_mix, w_in, vg, ws, bs_full, w_out, tm_s, side))
        g_ffn = row_vec(norm_ffn_g[l])
        w_up_l, w_down_l = bf.get("up", l), bf.get("down", l)
        xp, = _ffn(xp, mod_l, row_p, g_ffn, w_up_l, ffn_conv_w, w_down_l, l, tm_p, SEQ)
        xs, = _ffn(xs, mod_l, row_s, g_ffn, w_up_l, ffn_conv_w, w_down_l, l, tm_s, DEC_SEQ)
    per_layer = lambda parts: jnp.transpose(parts[0] if len(parts) == 1 else jnp.concatenate(parts, axis=1),
                                            (0, 1, 4, 2, 3))
    return (xp.reshape(BATCH, SEQ, D_MODEL), xs.reshape(DEC_BATCH, DEC_SEQ, D_MODEL),
            per_layer(new_k), per_layer(new_v))
```

```python
import functools

import jax
import jax.numpy as jnp
import numpy as np
from jax import lax
from jax.experimental import pallas as pl
from jax.experimental.pallas import tpu as pltpu

F32 = jnp.float32
BF16 = jnp.bfloat16

D_MODEL = 1024
BATCH = 16
SEQ = 256
DEPTH = 2
DEC_BATCH = 4
DEC_SEQ = 2048
PAST_LEN = 512
GRID_W = 64
N_HEADS = 8
N_KV_HEADS = 2
HEAD_DIM = 64
GQA_GROUP = N_HEADS // N_KV_HEADS
ATTN_WIDTH = N_HEADS * HEAD_DIM
KV_WIDTH = N_KV_HEADS * HEAD_DIM
WINDOW = 128
BLOCK = 128
assert WINDOW == BLOCK
ROPE_BASE = 10000.0
CONV_WIDTH = 512
CHUNK = 128
GMLP_WIDTH = 1024
GMLP_GROUPS = 8
GMLP_GROUP_DIM = GMLP_WIDTH // GMLP_GROUPS
D_FF = 2816
EPS = 1e-6
NEG_INF = -1e30
N_ODD = DEPTH // 2
LOG2_E = 1.4426950408889634

LANES = 128
SUBLANES_BF16 = 16
MXU_WIDTH = 256
VMEM_LIMIT_BYTES = 56 * 1024 * 1024

COND_ROWS = 8
CTX_ROW = DEC_BATCH
TOKEN_HALO = SUBLANES_BF16
FFN_COLS = 2 * MXU_WIDTH
GMLP_U_COLS = MXU_WIDTH
LEAD_PARTS = 4
CAST_BLOCK_BYTES = 3 * 1024 * 1024
ADA_TN = 1536
ATTN_Q_BLOCKS = 4
ATTN_CTX_SEQS = 2
SUM_ROWS = SUBLANES_BF16
SCORE_CAP = 3.0e38


def _params(n_axes):
    return pltpu.CompilerParams(dimension_semantics=("parallel",) * n_axes,
                                vmem_limit_bytes=VMEM_LIMIT_BYTES)


def _resident(shape):
    zeros = (0,) * len(shape)
    return pl.BlockSpec(shape, lambda *_: zeros, pipeline_mode=pl.Buffered(1))


def _layer_slab(shape, layer):
    return pl.BlockSpec((1,) + tuple(shape[1:]), lambda *_: (layer, 0, 0), pipeline_mode=pl.Buffered(1))


def _mod_spec(mod_row):
    return pl.BlockSpec((1, 1, 6 * D_MODEL), lambda *idx: (mod_row(*idx), 0, 0))


def _dot(a, b):
    return jnp.dot(a, b, preferred_element_type=F32)


def _cast_kernel(x_ref, o_ref):
    o_ref[...] = x_ref[...].astype(BF16)


def _to_bf16(w):
    n, rows, cols = w.shape
    fits = [r for r in range(SUBLANES_BF16, rows + 1, SUBLANES_BF16)
            if rows % r == 0 and r * cols * 4 <= CAST_BLOCK_BYTES]
    tr = max(fits)
    spec = pl.BlockSpec((1, tr, cols), lambda i, j: (i, j, 0))
    return pl.pallas_call(
        _cast_kernel, grid=(n, rows // tr), in_specs=[spec], out_specs=spec,
        out_shape=jax.ShapeDtypeStruct(w.shape, BF16), compiler_params=_params(2), name="to_bf16",
    )(w)


def _side_cast_specs(side, n_steps, step_of):
    w, layer = side
    _, rows, cols = w.shape
    tr = rows // n_steps
    assert tr * n_steps == rows and tr % SUBLANES_BF16 == 0
    return (pl.BlockSpec((1, tr, cols), lambda *idx: (layer, step_of(*idx), 0)),
            pl.BlockSpec((tr, cols), lambda *idx: (step_of(*idx), 0)),
            jax.ShapeDtypeStruct((rows, cols), BF16))


def _split_refs(rest, n_out, side_cast):
    n = side_cast
    side_ins, outs, side_outs, scratch = rest[:n], rest[n:n + n_out], rest[n + n_out:2 * n + n_out], rest[2 * n + n_out:]
    for side_in, side_out in zip(side_ins, side_outs):
        side_out[...] = side_in[0].astype(BF16)
    return outs, scratch


def _host_call(kernel, *, grid, in_specs, args, out_specs, out_shape, scratch_shapes, name, side, step_of):
    in_specs, args, out_specs, out_shape = list(in_specs), list(args), list(out_specs), list(out_shape)
    n_steps = 1
    for extent in grid:
        n_steps *= extent
    sides = [_side_cast_specs(one, n_steps, step_of) for one in side]
    in_specs += [spec for spec, _, _ in sides]
    args += [stack for stack, _ in side]
    out_specs += [spec for _, spec, _ in sides]
    out_shape += [shape for _, _, shape in sides]
    return pl.pallas_call(
        functools.partial(kernel, side_cast=len(side)),
        grid=grid, in_specs=in_specs, out_specs=out_specs, out_shape=out_shape, scratch_shapes=scratch_shapes,
        compiler_params=_params(len(grid)), name=name,
    )(*args)


def _modulate(x, g, shift, scale):
    ms = jnp.mean(x * x, axis=-1, keepdims=True)
    return (x * lax.rsqrt(ms + EPS) * g) * (1.0 + scale) + shift


def _mod_chunk(mod_ref, k):
    return mod_ref[0, :, k * D_MODEL:(k + 1) * D_MODEL]


def _halo_rows(tiles_per_seq):
    return TOKEN_HALO if tiles_per_seq > 1 else 0


def _token_specs(T, tm, tiles_per_seq):
    tok = pl.BlockSpec((tm, D_MODEL), lambda i: (i, 0))
    if tiles_per_seq == 1:
        return [tok]
    per_tile = tm // TOKEN_HALO
    last = T // TOKEN_HALO - 1
    prev = pl.BlockSpec((TOKEN_HALO, D_MODEL), lambda i: (jnp.maximum(i * per_tile - 1, 0), 0))
    nxt = pl.BlockSpec((TOKEN_HALO, D_MODEL), lambda i: (jnp.minimum((i + 1) * per_tile, last), 0))
    return [prev, tok, nxt]


def _modulated_parts(x_refs, h_scr, g, shift, scale, tiles_per_seq):
    halo = _halo_rows(tiles_per_seq)
    x_ref = x_refs[len(x_refs) // 2]
    tm = x_ref.shape[0]
    step = tm // LEAD_PARTS
    parts = []
    for p in range(LEAD_PARTS):
        lo, hi = halo + p * step, halo + (p + 1) * step
        piece = _modulate(x_ref[p * step:(p + 1) * step, :], g, shift, scale).astype(BF16)
        if halo and p == 0:
            pos = pl.program_id(0) % tiles_per_seq
            edge = jnp.where(pos > 0, _modulate(x_refs[0][...], g, shift, scale), 0.0).astype(BF16)
            piece, lo = jnp.concatenate([edge, piece], axis=0), 0
        if halo and p == LEAD_PARTS - 1:
            pos = pl.program_id(0) % tiles_per_seq
            edge = jnp.where(pos < tiles_per_seq - 1, _modulate(x_refs[2][...], g, shift, scale), 0.0).astype(BF16)
            piece, hi = jnp.concatenate([piece, edge], axis=0), tm + 2 * halo
        h_scr[lo:hi, :] = piece
        parts.append(piece)
    return parts


def _dot_rows(parts, w):
    return jnp.concatenate([_dot(p, w) for p in parts], axis=0)


def _seq_end_masks(tm, cols, seq_len, tiles_per_seq):
    if tiles_per_seq > 1:
        return None
    seq_row = lax.broadcasted_iota(jnp.int32, (tm, cols), 0) % seq_len
    return seq_row != 0, seq_row != seq_len - 1


def _token_conv3(z, w, tm, halo, masks):
    rows = z.shape[0]
    mid = slice(halo, halo + tm)
    dn = pltpu.roll(z, 1, 0)[mid]
    up = pltpu.roll(z, rows - 1, 0)[mid]
    if masks is not None:
        dn = jnp.where(masks[0], dn, 0.0)
        up = jnp.where(masks[1], up, 0.0)
    return dn * w[0:1] + z[mid] * w[1:2] + up * w[2:3]


def _adaln_kernel(cond_ref, w_ref, b_ref, *rest, side_cast):
    (o_ref,), _ = _split_refs(rest, 1, side_cast)
    a = jax.nn.silu(cond_ref[...]).astype(BF16)
    rows = _dot(a, w_ref[0].astype(BF16)) + b_ref[0]
    for r in range(COND_ROWS):
        o_ref[r] = rows[r:r + 1, :]


def _adaln(cond, ada_w, ada_b, side=()):
    n_out = 6 * D_MODEL
    n_col = n_out // ADA_TN
    return _host_call(
        _adaln_kernel,
        grid=(DEPTH, n_col),
        in_specs=[pl.BlockSpec((COND_ROWS, D_MODEL), lambda l, j: (0, 0)),
                  pl.BlockSpec((1, D_MODEL, ADA_TN), lambda l, j: (l, 0, j)),
                  pl.BlockSpec((1, 1, ADA_TN), lambda l, j: (l, 0, j))],
        args=[cond, ada_w, ada_b.reshape(DEPTH, 1, n_out)],
        out_specs=[pl.BlockSpec((COND_ROWS, 1, ADA_TN), lambda l, j: (l, 0, j))],
        out_shape=[jax.ShapeDtypeStruct((DEPTH * COND_ROWS, 1, n_out), F32)],
        scratch_shapes=[], name="adaln", side=side, step_of=lambda l, j: l * n_col + j)


def _head_rms(z, gain):
    n = z.shape[1]
    w = min(n, MXU_WIDTH)
    r = lax.broadcasted_iota(jnp.int32, (w, w), 0) // HEAD_DIM
    c = lax.broadcasted_iota(jnp.int32, (w, w), 1) // HEAD_DIM
    ones = (r == c).astype(BF16)
    sq = (z * z).astype(BF16)
    ss = jnp.concatenate([_dot(sq[:, k:k + w], ones) for k in range(0, n, w)], axis=1)
    return z * lax.rsqrt(ss * (1.0 / HEAD_DIM) + EPS) * gain


def _rope(z, cos, sin_lo, sin_hi):
    outs = []
    for k in range(0, z.shape[1], LANES):
        blk = z[:, k:k + LANES]
        outs.append(blk * cos + pltpu.roll(blk, LANES - 16, 1) * sin_lo + pltpu.roll(blk, 16, 1) * sin_hi)
    return jnp.concatenate(outs, axis=1)


def _in_even_kernel(*refs, rope, tiles_per_seq, seq_len, cache_out, side_cast):
    halo = _halo_rows(tiles_per_seq)
    n_x = 3 if halo else 1
    x_refs, refs = refs[:n_x], refs[n_x:]
    mod_ref, g_ref, w_ref, qg_ref, kg_ref, cw_ref = refs[:6]
    refs = refs[6:]
    if rope:
        cos_ref, slo_ref, shi_ref = refs[:3]
        refs = refs[3:]
    outs, (h_scr,) = _split_refs(refs, 6 if cache_out else 4, side_cast)
    q_ref, k_ref, v_ref, sc_ref = outs[:4]
    tm = q_ref.shape[0]
    parts = _modulated_parts(x_refs, h_scr, g_ref[...], _mod_chunk(mod_ref, 0), _mod_chunk(mod_ref, 1),
                             tiles_per_seq)
    c0 = 0
    c1 = c0 + ATTN_WIDTH
    c2 = c1 + 2 * KV_WIDTH
    c3 = c2 + CONV_WIDTH
    c4 = c3 + CONV_WIDTH
    c5 = c4 + CONV_WIDTH
    zq = _dot_rows(parts, w_ref[:, c0:c1])[halo:halo + tm]
    h_all = h_scr[...]
    h = h_scr[halo:halo + tm, :]
    q = _head_rms(zq, qg_ref[...] * (HEAD_DIM ** -0.5 * LOG2_E))
    zkv = _dot(h, w_ref[:, c1:c2])
    k = _head_rms(zkv[:, :KV_WIDTH], kg_ref[...])
    if rope:
        cos, slo, shi = cos_ref[...], slo_ref[...], shi_ref[...]
        q = _rope(q, cos, slo, shi)
        k = _rope(k, cos, slo, shi)
    v = zkv[:, KV_WIDTH:]
    q_ref[...] = q
    k_ref[...] = k
    v_ref[...] = v
    if cache_out:
        for cache_ref, rows in zip(outs[4:], (k, v)):
            for s in range(tm // seq_len):
                seq_t = jnp.concatenate([rows[j:j + BLOCK, :].T for j in range(s * seq_len, (s + 1) * seq_len, BLOCK)],
                                        axis=1)
                cache_ref[s, 0] = seq_t.reshape(N_KV_HEADS, HEAD_DIM, seq_len)
    ch = _dot(h_all, w_ref[:, c3:c4]) * _dot(h_all, w_ref[:, c4:c5])
    conv = _token_conv3(ch, cw_ref[...], tm, halo, _seq_end_masks(tm, CONV_WIDTH, seq_len, tiles_per_seq))
    sc_ref[...] = (_dot(h, w_ref[:, c2:c3]) * conv).astype(BF16)


def _in_even(x, mod_l, mod_row, g, w_bf, qg, kg, conv_w, rope_tabs, tm, seq_len, side=(), cache_out=False):
    T = x.shape[0]
    n_in = w_bf.shape[1]
    tiles_per_seq = max(seq_len // tm, 1)
    assert tm % seq_len == 0 or seq_len % tm == 0
    tok = lambda w: pl.BlockSpec((tm, w), lambda i: (i, 0))
    x_specs = _token_specs(T, tm, tiles_per_seq)
    in_specs = x_specs + [_mod_spec(mod_row), _resident((1, D_MODEL)), _resident((D_MODEL, n_in)),
                          _resident((1, ATTN_WIDTH)), _resident((1, KV_WIDTH)), _resident((3, CONV_WIDTH))]
    args = [x] * len(x_specs) + [mod_l, g, w_bf, qg, kg, conv_w]
    if rope_tabs is not None:
        in_specs += [pl.BlockSpec((tm, LANES), lambda i: (i % tiles_per_seq, 0))] * 3
        args += list(rope_tabs)
    outs = ((ATTN_WIDTH, F32), (KV_WIDTH, F32), (KV_WIDTH, F32), (CONV_WIDTH, BF16))
    out_specs = [tok(w) for w, _ in outs]
    out_shape = [jax.ShapeDtypeStruct((T, w), dt) for w, dt in outs]
    if cache_out:
        n_seq = tm // seq_len
        cache_block = (n_seq, 1, N_KV_HEADS, HEAD_DIM, seq_len)
        out_specs += [pl.BlockSpec(cache_block, lambda i: (i, 0, 0, 0, 0))] * 2
        out_shape += [jax.ShapeDtypeStruct((T // seq_len,) + cache_block[1:], F32)] * 2
    scratch = [pltpu.VMEM((tm + 2 * _halo_rows(tiles_per_seq), D_MODEL), BF16)]
    return _host_call(
        functools.partial(_in_even_kernel, rope=rope_tabs is not None, tiles_per_seq=tiles_per_seq, seq_len=seq_len,
                          cache_out=cache_out),
        grid=(T // tm,), in_specs=in_specs, args=args, out_specs=out_specs, out_shape=out_shape,
        scratch_shapes=scratch, name="in_even", side=side, step_of=lambda i: i)


def _stack_heads(q):
    lane_lo = lax.broadcasted_iota(jnp.int32, (BLOCK, LANES), 1) < HEAD_DIM
    heads = []
    for pair in range(N_HEADS // 2):
        kv = (2 * pair) // GQA_GROUP
        qp = q[:, pair * LANES:(pair + 1) * LANES]
        qr = pltpu.roll(qp, HEAD_DIM, 1)
        for half in range(2):
            src = qp if half == kv else qr
            heads.append(jnp.where(lane_lo, src, 0.0) if kv == 0 else jnp.where(lane_lo, 0.0, src))
    return jnp.concatenate(heads, axis=0).astype(BF16)


def _transposed_values(v_blocks, vt_tail=None):
    cols = [v[j:j + BLOCK].T for v in v_blocks for j in range(0, v.shape[0], BLOCK)]
    vt = jnp.concatenate(cols + ([] if vt_tail is None else [vt_tail]), axis=1)
    return jnp.concatenate([vt.astype(BF16), jnp.ones((SUM_ROWS, vt.shape[1]), BF16)], axis=0)


def _scores(q, kcat, cap):
    st = lax.dot_general(kcat, _stack_heads(q), (((1,), (1,)), ((), ())), preferred_element_type=F32)
    if cap is None:
        return st
    n_band = cap.shape[0]
    capped = jnp.minimum(st[:n_band], jnp.concatenate([cap] * N_HEADS, axis=1))
    return jnp.concatenate([capped, st[n_band:]], axis=0)


def _weighted_values(st, vt, sink_row):
    m = jnp.maximum(jnp.max(st, axis=0, keepdims=True), sink_row)
    pt = jnp.exp2(st - m).astype(BF16)
    ot = _dot(vt, pt)
    denom = ot[KV_WIDTH:KV_WIDTH + 1] + jnp.exp2(sink_row - m)
    ot = ot[:KV_WIDTH] / denom
    pairs = []
    for pair in range(N_HEADS // 2):
        kv = (2 * pair) // GQA_GROUP
        dims = slice(kv * HEAD_DIM, (kv + 1) * HEAD_DIM)
        both = [ot[dims, (2 * pair + half) * BLOCK:(2 * pair + half + 1) * BLOCK] for half in range(2)]
        pairs.append(jnp.concatenate(both, axis=0).T)
    return jnp.concatenate(pairs, axis=1)


def _attend_blocks(n_blocks, scores_of, values_of, sink_row, attn_scr):
    st = scores_of(0)
    for j in range(n_blocks):
        st_next = scores_of(j + 1) if j + 1 < n_blocks else None
        attn_scr[j * BLOCK:(j + 1) * BLOCK, :] = _weighted_values(st, values_of(j), sink_row).astype(BF16)
        st = st_next


def _project_out(attn_scr, sc_ref, x_ref, mod_ref, w_ref, o_ref):
    mix = _dot(attn_scr[...], w_ref[:ATTN_WIDTH, :]) + _dot(sc_ref[...], w_ref[ATTN_WIDTH:, :])
    o_ref[...] = x_ref[...] + _mod_chunk(mod_ref, 2) * mix


def _attn_ctx_kernel(sink_ref, q_ref, k_ref, v_ref, sc_ref, x_ref, mod_ref, w_ref, *rest, side_cast):
    (o_ref,), (attn_scr,) = _split_refs(rest, 1, side_cast)
    per_seq = SEQ // BLOCK
    kcats = [k_ref[s * SEQ:(s + 1) * SEQ, :].astype(BF16) for s in range(ATTN_CTX_SEQS)]
    vts = [_transposed_values([v_ref[s * SEQ:(s + 1) * SEQ, :]]) for s in range(ATTN_CTX_SEQS)]
    sink_row = sink_ref[...] * LOG2_E
    _attend_blocks(ATTN_CTX_SEQS * per_seq,
                   lambda j: _scores(q_ref[j * BLOCK:(j + 1) * BLOCK, :], kcats[j // per_seq], None),
                   lambda j: vts[j // per_seq], sink_row, attn_scr)
    _project_out(attn_scr, sc_ref, x_ref, mod_ref, w_ref, o_ref)


def _attn_ctx(q, k, v, sc, x, mod_l, mod_base, sink_row, w_out_bf, side=()):
    T = q.shape[0]
    tq = ATTN_CTX_SEQS * SEQ
    tok = lambda w: pl.BlockSpec((tq, w), lambda i: (i, 0))
    return _host_call(
        _attn_ctx_kernel,
        grid=(T // tq,),
        in_specs=[_resident((1, N_HEADS * BLOCK)), tok(ATTN_WIDTH), tok(KV_WIDTH), tok(KV_WIDTH),
                  tok(CONV_WIDTH), tok(D_MODEL), _mod_spec(lambda i: mod_base + CTX_ROW),
                  _resident((ATTN_WIDTH + CONV_WIDTH, D_MODEL))],
        args=[sink_row, q, k, v, sc, x, mod_l, w_out_bf],
        out_specs=[tok(D_MODEL)], out_shape=[jax.ShapeDtypeStruct((T, D_MODEL), F32)],
        scratch_shapes=[pltpu.VMEM((tq, ATTN_WIDTH), BF16)], name="attn_ctx", side=side, step_of=lambda i: i)


def _band_cap(has_prev, has_next):
    c = lax.broadcasted_iota(jnp.int32, (3 * BLOCK, BLOCK), 0)
    r = lax.broadcasted_iota(jnp.int32, (3 * BLOCK, BLOCK), 1)
    first_prev = r + jnp.where(has_prev, 0, BLOCK)
    last_next = r + 2 * BLOCK - jnp.where(has_next, 0, BLOCK)
    masked = ((c < BLOCK) & (c < first_prev)) | ((c >= 2 * BLOCK) & (c > last_next))
    return jnp.where(masked, NEG_INF, SCORE_CAP)


def _attn_win_kernel(sink_ref, q_ref, kp_ref, kc_ref, kn_ref, vp_ref, vc_ref, vn_ref, ckt_ref, cvt_ref,
                     sc_ref, x_ref, mod_ref, w_ref, *rest, n_steps, side_cast):
    (o_ref,), (attn_scr,) = _split_refs(rest, 1, side_cast)
    i = pl.program_id(1)
    kc, vc = kc_ref[...], vc_ref[...]
    inner = range(0, ATTN_Q_BLOCKS * BLOCK, BLOCK)
    k_blocks = [kp_ref[...]] + [kc[j:j + BLOCK] for j in inner] + [kn_ref[...]]
    v_blocks = [vp_ref[...]] + [vc[j:j + BLOCK] for j in inner] + [vn_ref[...]]
    sink_row = sink_ref[...] * LOG2_E
    ckt = ckt_ref[0]
    ck = jnp.concatenate([ckt[:, j:j + BLOCK].T for j in range(0, PAST_LEN, BLOCK)], axis=0)

    def scores_of(j):
        kcat = jnp.concatenate(k_blocks[j:j + 3] + [ck], axis=0).astype(BF16)
        cap = _band_cap(i > 0 if j == 0 else True, i < n_steps - 1 if j == ATTN_Q_BLOCKS - 1 else True)
        return _scores(q_ref[j * BLOCK:(j + 1) * BLOCK, :], kcat, cap)

    _attend_blocks(ATTN_Q_BLOCKS, scores_of, lambda j: _transposed_values(v_blocks[j:j + 3], cvt_ref[0]),
                   sink_row, attn_scr)
    _project_out(attn_scr, sc_ref, x_ref, mod_ref, w_ref, o_ref)


def _attn_win(q, k, v, ck, cv, sc, x, mod_l, mod_base, sink_row, w_out_bf, side=()):
    T = q.shape[0]
    tq = ATTN_Q_BLOCKS * BLOCK
    n_steps = DEC_SEQ // tq
    nb = DEC_SEQ // BLOCK
    cur = lambda w: pl.BlockSpec((tq, w), lambda b, i: (b * n_steps + i, 0))
    prev = pl.BlockSpec((BLOCK, KV_WIDTH), lambda b, i: (b * nb + jnp.maximum(ATTN_Q_BLOCKS * i - 1, 0), 0))
    nxt = pl.BlockSpec((BLOCK, KV_WIDTH),
                       lambda b, i: (b * nb + jnp.minimum(ATTN_Q_BLOCKS * (i + 1), nb - 1), 0))
    ctx = pl.BlockSpec((1, KV_WIDTH, PAST_LEN), lambda b, i: (b, 0, 0))
    in_specs = [_resident((1, N_HEADS * BLOCK)), cur(ATTN_WIDTH),
                prev, cur(KV_WIDTH), nxt, prev, cur(KV_WIDTH), nxt, ctx, ctx,
                cur(CONV_WIDTH), cur(D_MODEL), _mod_spec(lambda b, i: mod_base + b),
                _resident((ATTN_WIDTH + CONV_WIDTH, D_MODEL))]
    args = [sink_row, q, k, k, k, v, v, v, ck, cv, sc, x, mod_l, w_out_bf]
    return _host_call(
        functools.partial(_attn_win_kernel, n_steps=n_steps),
        grid=(T // DEC_SEQ, n_steps), in_specs=in_specs, args=args,
        out_specs=[cur(D_MODEL)], out_shape=[jax.ShapeDtypeStruct((T, D_MODEL), F32)],
        scratch_shapes=[pltpu.VMEM((tq, ATTN_WIDTH), BF16)], name="attn_win", side=side,
        step_of=lambda b, i: b * n_steps + i)


def _gelu_tanh(x):
    k = 0.7978845608028654
    half = 0.5 * x
    return half + half * jnp.tanh(x * (k + (k * 0.044715) * (x * x)))


def _gmlp_kernel(x_ref, mod_ref, g_ref, win_ref, vg_ref, ws_ref, bs_ref, wout_ref, *rest, side_cast):
    (o_ref,), (gated_scr, h_scr) = _split_refs(rest, 1, side_cast)
    n_chunks = x_ref.shape[0] // CHUNK
    parts = _modulated_parts([x_ref], h_scr, g_ref[...], _mod_chunk(mod_ref, 0), _mod_chunk(mod_ref, 1), 1)
    v = _gelu_tanh(_dot_rows(parts, win_ref[:, GMLP_WIDTH:]))
    ms = jnp.mean(v * v, axis=-1, keepdims=True)
    v = (v * lax.rsqrt(ms + EPS) * vg_ref[...]).astype(BF16)
    h = h_scr[...]
    for first in range(0, GMLP_WIDTH, GMLP_U_COLS):
        u = _gelu_tanh(_dot(h, win_ref[:, first:first + GMLP_U_COLS]))
        for grp in range(first // GMLP_GROUP_DIM, (first + GMLP_U_COLS) // GMLP_GROUP_DIM):
            lanes = slice(grp * GMLP_GROUP_DIM, (grp + 1) * GMLP_GROUP_DIM)
            u_lanes = slice(lanes.start - first, lanes.stop - first)
            rhs = jnp.concatenate([v[n * CHUNK:(n + 1) * CHUNK, lanes] for n in range(n_chunks)], axis=1)
            s = _dot(ws_ref[grp], rhs)
            for n in range(n_chunks):
                rows = slice(n * CHUNK, (n + 1) * CHUNK)
                s_n = s[:, n * GMLP_GROUP_DIM:(n + 1) * GMLP_GROUP_DIM] + bs_ref[grp]
                gated_scr[rows, lanes] = (u[rows, u_lanes] * s_n).astype(BF16)
    o_ref[...] = x_ref[...] + _mod_chunk(mod_ref, 2) * _dot(gated_scr[...], wout_ref[...])


def _gmlp(x, mod_l, mod_row, g, win_bf, vg, ws_bf, bs_full, wout_bf, tm, side=()):
    T = x.shape[0]
    tok = pl.BlockSpec((tm, D_MODEL), lambda i: (i, 0))
    in_specs = [tok, _mod_spec(mod_row),
                _resident((1, D_MODEL)), _resident((D_MODEL, 2 * GMLP_WIDTH)), _resident((1, GMLP_WIDTH)),
                _resident((GMLP_GROUPS, CHUNK, CHUNK)), _resident((GMLP_GROUPS, CHUNK, GMLP_GROUP_DIM)),
                _resident((GMLP_WIDTH, D_MODEL))]
    args = [x, mod_l, g, win_bf, vg, ws_bf, bs_full, wout_bf]
    return _host_call(
        _gmlp_kernel, grid=(T // tm,), in_specs=in_specs, args=args,
        out_specs=[tok], out_shape=[jax.ShapeDtypeStruct((T, D_MODEL), F32)],
        scratch_shapes=[pltpu.VMEM((tm, GMLP_WIDTH), BF16), pltpu.VMEM((tm, D_MODEL), BF16)],
        name="gmlp", side=side, step_of=lambda i: i)


def _ffn_kernel(*refs, tiles_per_seq, seq_len, side_cast):
    halo = _halo_rows(tiles_per_seq)
    n_x = 3 if halo else 1
    x_refs, refs = refs[:n_x], refs[n_x:]
    mod_ref, g_ref, wup_ref, cw_ref, wdn_ref = refs[:5]
    (o_ref,), (act_scr, h_scr) = _split_refs(refs[5:], 1, side_cast)
    tm = o_ref.shape[0]
    parts = _modulated_parts(x_refs, h_scr, g_ref[...], _mod_chunk(mod_ref, 3), _mod_chunk(mod_ref, 4),
                             tiles_per_seq)
    for j in range(0, D_FF, FFN_COLS):
        width = min(FFN_COLS, D_FF - j)
        masks = _seq_end_masks(tm, width, seq_len, tiles_per_seq)
        gate_cols = slice(j, j + width)
        val_cols = slice(D_FF + j, D_FF + j + width)
        if j == 0:
            up = lambda cols: _dot_rows(parts, wup_ref[:, cols])
        else:
            h = h_scr[...]
            up = lambda cols: _dot(h, wup_ref[:, cols])
        zg = _token_conv3(up(gate_cols), cw_ref[0, :, gate_cols], tm, halo, masks)
        zv = _token_conv3(up(val_cols), cw_ref[0, :, val_cols], tm, halo, masks)
        act_scr[:, gate_cols] = (jax.nn.silu(zg) * zv).astype(BF16)
    x = x_refs[n_x // 2][...]
    o_ref[...] = x + _mod_chunk(mod_ref, 5) * _dot(act_scr[...], wdn_ref[...])


def _ffn(x, mod_l, mod_row, g, wup_bf, conv_w, wdn_bf, layer, tm, seq_len, side=()):
    T = x.shape[0]
    tiles_per_seq = max(seq_len // tm, 1)
    assert tm % seq_len == 0 or seq_len % tm == 0
    x_specs = _token_specs(T, tm, tiles_per_seq)
    in_specs = x_specs + [_mod_spec(mod_row), _resident((1, D_MODEL)), _resident(wup_bf.shape),
                          _layer_slab(conv_w.shape, layer), _resident(wdn_bf.shape)]
    scratch = [pltpu.VMEM((tm, D_FF), BF16), pltpu.VMEM((tm + 2 * _halo_rows(tiles_per_seq), D_MODEL), BF16)]
    return _host_call(
        functools.partial(_ffn_kernel, tiles_per_seq=tiles_per_seq, seq_len=seq_len),
        grid=(T // tm,), in_specs=in_specs, args=[x] * len(x_specs) + [mod_l, g, wup_bf, conv_w, wdn_bf],
        out_specs=[pl.BlockSpec((tm, D_MODEL), lambda i: (i, 0))], out_shape=[jax.ShapeDtypeStruct((T, D_MODEL), F32)],
        scratch_shapes=scratch, name="conv_ffn", side=side, step_of=lambda i: i)


def _rope_tables(n_tokens):
    t = np.arange(n_tokens)
    n_freq = HEAD_DIM // 4
    inv = (ROPE_BASE ** (-np.arange(n_freq, dtype=np.float32) / n_freq)).astype(np.float32)
    row_ang = (t // GRID_W).astype(np.float32)[:, None] * inv
    col_ang = (t % GRID_W).astype(np.float32)[:, None] * inv
    ang = np.concatenate([row_ang, row_ang, col_ang, col_ang], axis=1)
    ang = np.tile(ang, (1, LANES // HEAD_DIM)).astype(np.float64)
    first = (np.arange(LANES) % (2 * n_freq)) < n_freq
    cos, sin = np.cos(ang), np.sin(ang)
    tables = (cos, np.where(first, -sin, 0.0), np.where(first, 0.0, sin))
    return tuple(jnp.asarray(tab.astype(np.float32)) for tab in tables)


class _Bf16Weights:
    def __init__(self, **stacks):
        self.stacks, self.ready = stacks, {}

    def hosted(self, keys, call):
        due = [(name, slab) for name, slab in keys
               if slab < self.stacks[name].shape[0] and (name, slab) not in self.ready]
        outs = call([(self.stacks[name], slab) for name, slab in due])
        for key, cast in zip(due, outs[len(outs) - len(due):]):
            self.ready[key] = cast
        return outs[:len(outs) - len(due)]

    def get(self, name, slab):
        if (name, slab) not in self.ready:
            self.ready[(name, slab)] = _to_bf16(self.stacks[name][slab:slab + 1])[0]
        return self.ready[(name, slab)]


def kernel(x_prompt, x_sample, cache_k, cache_v, c, c_ctx, ada_w, ada_b, norm_mix_g, norm_ffn_g, w_in_even,
           q_norm_g, k_norm_g, sink_logit, short_conv_w, w_out_even, w_in_odd, gmlp_norm_g, w_spatial,
           b_spatial, w_out_odd, w_up, ffn_conv_w, w_down):
    n_p, n_s = BATCH * SEQ, DEC_BATCH * DEC_SEQ
    xp = x_prompt.reshape(n_p, D_MODEL)
    xs = x_sample.reshape(n_s, D_MODEL)
    cond = jnp.concatenate([c, c_ctx[None, :], jnp.zeros((COND_ROWS - DEC_BATCH - 1, D_MODEL), F32)], axis=0)
    bf = _Bf16Weights(in_even=w_in_even, out_even=w_out_even, in_odd=w_in_odd, out_odd=w_out_odd,
                      spatial=w_spatial.reshape(N_ODD, GMLP_GROUPS * CHUNK, CHUNK), up=w_up, down=w_down)

    mod_l, = bf.hosted([("in_even", 0)], lambda side: _adaln(cond, ada_w, ada_b, side))
    rope_tabs = _rope_tables(DEC_SEQ)

    tm_p, tm_s = 1024, 1024
    row_vec = lambda a: a.reshape(1, -1)
    new_k, new_v = [], []
    for l in range(DEPTH):
        base = l * COND_ROWS
        row_p = lambda i, base=base: base + CTX_ROW
        row_s = lambda i, base=base: base + i // (DEC_SEQ // tm_s)
        g_mix = row_vec(norm_mix_g[l])
        nxt = (l + 1) // 2
        if l % 2 == 0:
            e = l // 2
            w_in = bf.get("in_even", e)
            qg = row_vec(jnp.tile(q_norm_g[e], N_HEADS))
            kg = row_vec(jnp.tile(k_norm_g[e], N_KV_HEADS))
            sink = jnp.repeat(sink_logit[e], BLOCK).reshape(1, N_HEADS * BLOCK)
            cw = short_conv_w[e]
            qp, kp, vp, scp, k_layer, v_layer = bf.hosted([("out_even", e)], lambda side: _in_even(
                xp, mod_l, row_p, g_mix, w_in, qg, kg, cw, None, tm_p, SEQ, side, cache_out=True))
            new_k.append(k_layer)
            new_v.append(v_layer)
            w_out = bf.get("out_even", e)
            odd_next = [("in_odd", nxt), ("out_odd", nxt), ("spatial", nxt)]
            xp, = bf.hosted(odd_next, lambda side: _attn_ctx(qp, kp, vp, scp, xp, mod_l, base, sink, w_out, side))
            qs, ks, vs, scs = bf.hosted([("down", l), ("down", l + 1)], lambda side: _in_even(
                xs, mod_l, row_s, g_mix, w_in, qg, kg, cw, rope_tabs, tm_s, DEC_SEQ, side))
            ck = jnp.transpose(cache_k[:, e], (0, 2, 3, 1)).reshape(DEC_BATCH, KV_WIDTH, PAST_LEN)
            cv = jnp.transpose(cache_v[:, e], (0, 2, 3, 1)).reshape(DEC_BATCH, KV_WIDTH, PAST_LEN)
            xs, = bf.hosted([("up", l), ("up", l + 1)], lambda side: _attn_win(
                qs, ks, vs, ck, cv, scs, xs, mod_l, base, sink, w_out, side))
        else:
            o = l // 2
            w_in, w_out = bf.get("in_odd", o), bf.get("out_odd", o)
            vg = row_vec(gmlp_norm_g[o])
            ws = bf.get("spatial", o).reshape(GMLP_GROUPS, CHUNK, CHUNK)
            bs_full = jnp.broadcast_to(b_spatial[o][:, :, None], (GMLP_GROUPS, CHUNK, GMLP_GROUP_DIM))
            xp, = bf.hosted([("down", l)], lambda side: _gmlp(
                xp, mod_l, row_p, g_mix, w_in, vg, ws, bs_full, w_out, tm_p, side))
            xs, = bf.hosted([("up", l)], lambda side: _gmlp(
                xs, mod_l, row_s, g_mix, w_in, vg, ws, bs_full, w_out, tm_s, side))
        g_ffn = row_vec(norm_ffn_g[l])
        w_up_l, w_down_l = bf.get("up", l), bf.get("down", l)
        xp, = _ffn(xp, mod_l, row_p, g_ffn, w_up_l, ffn_conv_w, w_down_l, l, tm_p, SEQ)
        xs, = _ffn(xs, mod_l, row_s, g_ffn, w_up_l, ffn_conv_w, w_down_l, l, tm_s, DEC_SEQ)
    per_layer = lambda parts: jnp.transpose(parts[0] if len(parts) == 1 else jnp.concatenate(parts, axis=1),
                                            (0, 1, 4, 2, 3))
    return (xp.reshape(BATCH, SEQ, D_MODEL), xs.reshape(DEC_BATCH, DEC_SEQ, D_MODEL),
            per_layer(new_k), per_layer(new_v))
```

```python
import functools

import jax
import jax.numpy as jnp
import numpy as np
from jax import lax
from jax.experimental import pallas as pl
from jax.experimental.pallas import tpu as pltpu

F32 = jnp.float32
BF16 = jnp.bfloat16

D_MODEL = 1024
BATCH = 16
SEQ = 256
DEPTH = 2
DEC_BATCH = 4
DEC_SEQ = 2048
PAST_LEN = 512
GRID_W = 64
N_HEADS = 8
N_KV_HEADS = 2
HEAD_DIM = 64
GQA_GROUP = N_HEADS // N_KV_HEADS
ATTN_WIDTH = N_HEADS * HEAD_DIM
KV_WIDTH = N_KV_HEADS * HEAD_DIM
WINDOW = 128
BLOCK = 128
assert WINDOW == BLOCK
ROPE_BASE = 10000.0
CONV_WIDTH = 512
CHUNK = 128
GMLP_WIDTH = 1024
GMLP_GROUPS = 8
GMLP_GROUP_DIM = GMLP_WIDTH // GMLP_GROUPS
D_FF = 2816
EPS = 1e-6
NEG_INF = -1e30
N_ODD = DEPTH // 2
LOG2_E = 1.4426950408889634

LANES = 128
SUBLANES_BF16 = 16
MXU_WIDTH = 256
VMEM_LIMIT_BYTES = 56 * 1024 * 1024

COND_ROWS = 8
CTX_ROW = DEC_BATCH
TOKEN_HALO = SUBLANES_BF16
FFN_COLS = 2 * MXU_WIDTH
GMLP_U_COLS = MXU_WIDTH
LEAD_PARTS = 4
CAST_BLOCK_BYTES = 3 * 1024 * 1024
ADA_TN = 768
ATTN_Q_BLOCKS = 4
ATTN_CTX_SEQS = 2
SUM_ROWS = SUBLANES_BF16
SCORE_CAP = 3.0e38


def _params(n_axes):
    return pltpu.CompilerParams(dimension_semantics=("parallel",) * n_axes,
                                vmem_limit_bytes=VMEM_LIMIT_BYTES)


def _resident(shape):
    zeros = (0,) * len(shape)
    return pl.BlockSpec(shape, lambda *_: zeros, pipeline_mode=pl.Buffered(1))


def _layer_slab(shape, layer):
    return pl.BlockSpec((1,) + tuple(shape[1:]), lambda *_: (layer, 0, 0), pipeline_mode=pl.Buffered(1))


def _mod_spec(mod_row):
    return pl.BlockSpec((1, 1, 6 * D_MODEL), lambda *idx: (mod_row(*idx), 0, 0))


def _dot(a, b):
    return jnp.dot(a, b, preferred_element_type=F32)


def _cast_kernel(x_ref, o_ref):
    o_ref[...] = x_ref[...].astype(BF16)


def _to_bf16(w):
    n, rows, cols = w.shape
    fits = [r for r in range(SUBLANES_BF16, rows + 1, SUBLANES_BF16)
            if rows % r == 0 and r * cols * 4 <= CAST_BLOCK_BYTES]
    tr = max(fits)
    spec = pl.BlockSpec((1, tr, cols), lambda i, j: (i, j, 0))
    return pl.pallas_call(
        _cast_kernel, grid=(n, rows // tr), in_specs=[spec], out_specs=spec,
        out_shape=jax.ShapeDtypeStruct(w.shape, BF16), compiler_params=_params(2), name="to_bf16",
    )(w)


def _side_cast_specs(side, n_steps, step_of):
    w, layer = side
    _, rows, cols = w.shape
    tr = rows // n_steps
    assert tr * n_steps == rows and tr % SUBLANES_BF16 == 0
    return (pl.BlockSpec((1, tr, cols), lambda *idx: (layer, step_of(*idx), 0)),
            pl.BlockSpec((tr, cols), lambda *idx: (step_of(*idx), 0)),
            jax.ShapeDtypeStruct((rows, cols), BF16))


def _split_refs(rest, n_out, side_cast):
    n = side_cast
    side_ins, outs, side_outs, scratch = rest[:n], rest[n:n + n_out], rest[n + n_out:2 * n + n_out], rest[2 * n + n_out:]
    for side_in, side_out in zip(side_ins, side_outs):
        side_out[...] = side_in[0].astype(BF16)
    return outs, scratch


def _host_call(kernel, *, grid, in_specs, args, out_specs, out_shape, scratch_shapes, name, side, step_of):
    in_specs, args, out_specs, out_shape = list(in_specs), list(args), list(out_specs), list(out_shape)
    n_steps = 1
    for extent in grid:
        n_steps *= extent
    sides = [_side_cast_specs(one, n_steps, step_of) for one in side]
    in_specs += [spec for spec, _, _ in sides]
    args += [stack for stack, _ in side]
    out_specs += [spec for _, spec, _ in sides]
    out_shape += [shape for _, _, shape in sides]
    return pl.pallas_call(
        functools.partial(kernel, side_cast=len(side)),
        grid=grid, in_specs=in_specs, out_specs=out_specs, out_shape=out_shape, scratch_shapes=scratch_shapes,
        compiler_params=_params(len(grid)), name=name,
    )(*args)


def _modulate(x, g, shift, scale):
    ms = jnp.mean(x * x, axis=-1, keepdims=True)
    return (x * lax.rsqrt(ms + EPS) * g) * (1.0 + scale) + shift


def _mod_chunk(mod_ref, k):
    return mod_ref[0, :, k * D_MODEL:(k + 1) * D_MODEL]


def _halo_rows(tiles_per_seq):
    return TOKEN_HALO if tiles_per_seq > 1 else 0


def _token_specs(T, tm, tiles_per_seq):
    tok = pl.BlockSpec((tm, D_MODEL), lambda i: (i, 0))
    if tiles_per_seq == 1:
        return [tok]
    per_tile = tm // TOKEN_HALO
    last = T // TOKEN_HALO - 1
    prev = pl.BlockSpec((TOKEN_HALO, D_MODEL), lambda i: (jnp.maximum(i * per_tile - 1, 0), 0))
    nxt = pl.BlockSpec((TOKEN_HALO, D_MODEL), lambda i: (jnp.minimum((i + 1) * per_tile, last), 0))
    return [prev, tok, nxt]


def _modulated_parts(x_refs, h_scr, g, shift, scale, tiles_per_seq):
    halo = _halo_rows(tiles_per_seq)
    x_ref = x_refs[len(x_refs) // 2]
    tm = x_ref.shape[0]
    step = tm // LEAD_PARTS
    parts = []
    for p in range(LEAD_PARTS):
        lo, hi = halo + p * step, halo + (p + 1) * step
        piece = _modulate(x_ref[p * step:(p + 1) * step, :], g, shift, scale).astype(BF16)
        if halo and p == 0:
            pos = pl.program_id(0) % tiles_per_seq
            edge = jnp.where(pos > 0, _modulate(x_refs[0][...], g, shift, scale), 0.0).astype(BF16)
            piece, lo = jnp.concatenate([edge, piece], axis=0), 0
        if halo and p == LEAD_PARTS - 1:
            pos = pl.program_id(0) % tiles_per_seq
            edge = jnp.where(pos < tiles_per_seq - 1, _modulate(x_refs[2][...], g, shift, scale), 0.0).astype(BF16)
            piece, hi = jnp.concatenate([piece, edge], axis=0), tm + 2 * halo
        h_scr[lo:hi, :] = piece
        parts.append(piece)
    return parts


def _dot_rows(parts, w):
    return jnp.concatenate([_dot(p, w) for p in parts], axis=0)


def _seq_end_masks(tm, cols, seq_len, tiles_per_seq):
    if tiles_per_seq > 1:
        return None
    seq_row = lax.broadcasted_iota(jnp.int32, (tm, cols), 0) % seq_len
    return seq_row != 0, seq_row != seq_len - 1


def _token_conv3(z, w, tm, halo, masks):
    rows = z.shape[0]
    mid = slice(halo, halo + tm)
    dn = pltpu.roll(z, 1, 0)[mid]
    up = pltpu.roll(z, rows - 1, 0)[mid]
    if masks is not None:
        dn = jnp.where(masks[0], dn, 0.0)
        up = jnp.where(masks[1], up, 0.0)
    return dn * w[0:1] + z[mid] * w[1:2] + up * w[2:3]


def _adaln_kernel(cond_ref, w_ref, b_ref, *rest, side_cast):
    (o_ref,), _ = _split_refs(rest, 1, side_cast)
    a = jax.nn.silu(cond_ref[...]).astype(BF16)
    rows = _dot(a, w_ref[0].astype(BF16)) + b_ref[0]
    for r in range(COND_ROWS):
        o_ref[r] = rows[r:r + 1, :]


def _adaln(cond, ada_w, ada_b, side=()):
    n_out = 6 * D_MODEL
    n_col = n_out // ADA_TN
    return _host_call(
        _adaln_kernel,
        grid=(DEPTH, n_col),
        in_specs=[pl.BlockSpec((COND_ROWS, D_MODEL), lambda l, j: (0, 0)),
                  pl.BlockSpec((1, D_MODEL, ADA_TN), lambda l, j: (l, 0, j)),
                  pl.BlockSpec((1, 1, ADA_TN), lambda l, j: (l, 0, j))],
        args=[cond, ada_w, ada_b.reshape(DEPTH, 1, n_out)],
        out_specs=[pl.BlockSpec((COND_ROWS, 1, ADA_TN), lambda l, j: (l, 0, j))],
        out_shape=[jax.ShapeDtypeStruct((DEPTH * COND_ROWS, 1, n_out), F32)],
        scratch_shapes=[], name="adaln", side=side, step_of=lambda l, j: l * n_col + j)


def _head_rms(z, gain):
    n = z.shape[1]
    w = min(n, MXU_WIDTH)
    r = lax.broadcasted_iota(jnp.int32, (w, w), 0) // HEAD_DIM
    c = lax.broadcasted_iota(jnp.int32, (w, w), 1) // HEAD_DIM
    ones = (r == c).astype(BF16)
    sq = (z * z).astype(BF16)
    ss = jnp.concatenate([_dot(sq[:, k:k + w], ones) for k in range(0, n, w)], axis=1)
    return z * lax.rsqrt(ss * (1.0 / HEAD_DIM) + EPS) * gain


def _rope(z, cos, sin_lo, sin_hi):
    outs = []
    for k in range(0, z.shape[1], LANES):
        blk = z[:, k:k + LANES]
        outs.append(blk * cos + pltpu.roll(blk, LANES - 16, 1) * sin_lo + pltpu.roll(blk, 16, 1) * sin_hi)
    return jnp.concatenate(outs, axis=1)


def _in_even_kernel(*refs, rope, tiles_per_seq, seq_len, cache_out, side_cast):
    halo = _halo_rows(tiles_per_seq)
    n_x = 3 if halo else 1
    x_refs, refs = refs[:n_x], refs[n_x:]
    mod_ref, g_ref, w_ref, qg_ref, kg_ref, cw_ref = refs[:6]
    refs = refs[6:]
    if rope:
        cos_ref, slo_ref, shi_ref = refs[:3]
        refs = refs[3:]
    outs, (h_scr,) = _split_refs(refs, 6 if cache_out else 4, side_cast)
    q_ref, k_ref, v_ref, sc_ref = outs[:4]
    tm = q_ref.shape[0]
    parts = _modulated_parts(x_refs, h_scr, g_ref[...], _mod_chunk(mod_ref, 0), _mod_chunk(mod_ref, 1),
                             tiles_per_seq)
    c0 = 0
    c1 = c0 + ATTN_WIDTH
    c2 = c1 + 2 * KV_WIDTH
    c3 = c2 + CONV_WIDTH
    c4 = c3 + CONV_WIDTH
    c5 = c4 + CONV_WIDTH
    zq = _dot_rows(parts, w_ref[:, c0:c1])[halo:halo + tm]
    h_all = h_scr[...]
    h = h_scr[halo:halo + tm, :]
    q = _head_rms(zq, qg_ref[...] * (HEAD_DIM ** -0.5 * LOG2_E))
    zkv = _dot(h, w_ref[:, c1:c2])
    k = _head_rms(zkv[:, :KV_WIDTH], kg_ref[...])
    if rope:
        cos, slo, shi = cos_ref[...], slo_ref[...], shi_ref[...]
        q = _rope(q, cos, slo, shi)
        k = _rope(k, cos, slo, shi)
    v = zkv[:, KV_WIDTH:]
    q_ref[...] = q
    k_ref[...] = k
    v_ref[...] = v
    if cache_out:
        for cache_ref, rows in zip(outs[4:], (k, v)):
            for s in range(tm // seq_len):
                seq_t = jnp.concatenate([rows[j:j + BLOCK, :].T for j in range(s * seq_len, (s + 1) * seq_len, BLOCK)],
                                        axis=1)
                cache_ref[s, 0] = seq_t.reshape(N_KV_HEADS, HEAD_DIM, seq_len)
    ch = _dot(h_all, w_ref[:, c3:c4]) * _dot(h_all, w_ref[:, c4:c5])
    conv = _token_conv3(ch, cw_ref[...], tm, halo, _seq_end_masks(tm, CONV_WIDTH, seq_len, tiles_per_seq))
    sc_ref[...] = (_dot(h, w_ref[:, c2:c3]) * conv).astype(BF16)


def _in_even(x, mod_l, mod_row, g, w_bf, qg, kg, conv_w, rope_tabs, tm, seq_len, side=(), cache_out=False):
    T = x.shape[0]
    n_in = w_bf.shape[1]
    tiles_per_seq = max(seq_len // tm, 1)
    assert tm % seq_len == 0 or seq_len % tm == 0
    tok = lambda w: pl.BlockSpec((tm, w), lambda i: (i, 0))
    x_specs = _token_specs(T, tm, tiles_per_seq)
    in_specs = x_specs + [_mod_spec(mod_row), _resident((1, D_MODEL)), _resident((D_MODEL, n_in)),
                          _resident((1, ATTN_WIDTH)), _resident((1, KV_WIDTH)), _resident((3, CONV_WIDTH))]
    args = [x] * len(x_specs) + [mod_l, g, w_bf, qg, kg, conv_w]
    if rope_tabs is not None:
        in_specs += [pl.BlockSpec((tm, LANES), lambda i: (i % tiles_per_seq, 0))] * 3
        args += list(rope_tabs)
    outs = ((ATTN_WIDTH, F32), (KV_WIDTH, F32), (KV_WIDTH, F32), (CONV_WIDTH, BF16))
    out_specs = [tok(w) for w, _ in outs]
    out_shape = [jax.ShapeDtypeStruct((T, w), dt) for w, dt in outs]
    if cache_out:
        n_seq = tm // seq_len
        cache_block = (n_seq, 1, N_KV_HEADS, HEAD_DIM, seq_len)
        out_specs += [pl.BlockSpec(cache_block, lambda i: (i, 0, 0, 0, 0))] * 2
        out_shape += [jax.ShapeDtypeStruct((T // seq_len,) + cache_block[1:], F32)] * 2
    scratch = [pltpu.VMEM((tm + 2 * _halo_rows(tiles_per_seq), D_MODEL), BF16)]
    return _host_call(
        functools.partial(_in_even_kernel, rope=rope_tabs is not None, tiles_per_seq=tiles_per_seq, seq_len=seq_len,
                          cache_out=cache_out),
        grid=(T // tm,), in_specs=in_specs, args=args, out_specs=out_specs, out_shape=out_shape,
        scratch_shapes=scratch, name="in_even", side=side, step_of=lambda i: i)


def _stack_heads(q):
    lane_lo = lax.broadcasted_iota(jnp.int32, (BLOCK, LANES), 1) < HEAD_DIM
    heads = []
    for pair in range(N_HEADS // 2):
        kv = (2 * pair) // GQA_GROUP
        qp = q[:, pair * LANES:(pair + 1) * LANES]
        qr = pltpu.roll(qp, HEAD_DIM, 1)
        for half in range(2):
            src = qp if half == kv else qr
            heads.append(jnp.where(lane_lo, src, 0.0) if kv == 0 else jnp.where(lane_lo, 0.0, src))
    return jnp.concatenate(heads, axis=0).astype(BF16)


def _transposed_values(v_blocks, vt_tail=None):
    cols = [v[j:j + BLOCK].T for v in v_blocks for j in range(0, v.shape[0], BLOCK)]
    vt = jnp.concatenate(cols + ([] if vt_tail is None else [vt_tail]), axis=1)
    return jnp.concatenate([vt.astype(BF16), jnp.ones((SUM_ROWS, vt.shape[1]), BF16)], axis=0)


def _scores(q, kcat, cap):
    st = lax.dot_general(kcat, _stack_heads(q), (((1,), (1,)), ((), ())), preferred_element_type=F32)
    if cap is None:
        return st
    n_band = cap.shape[0]
    capped = jnp.minimum(st[:n_band], jnp.concatenate([cap] * N_HEADS, axis=1))
    return jnp.concatenate([capped, st[n_band:]], axis=0)


def _weighted_values(st, vt, sink_row):
    m = jnp.maximum(jnp.max(st, axis=0, keepdims=True), sink_row)
    pt = jnp.exp2(st - m).astype(BF16)
    ot = _dot(vt, pt)
    denom = ot[KV_WIDTH:KV_WIDTH + 1] + jnp.exp2(sink_row - m)
    ot = ot[:KV_WIDTH] / denom
    pairs = []
    for pair in range(N_HEADS // 2):
        kv = (2 * pair) // GQA_GROUP
        dims = slice(kv * HEAD_DIM, (kv + 1) * HEAD_DIM)
        both = [ot[dims, (2 * pair + half) * BLOCK:(2 * pair + half + 1) * BLOCK] for half in range(2)]
        pairs.append(jnp.concatenate(both, axis=0).T)
    return jnp.concatenate(pairs, axis=1)


def _attend_blocks(n_blocks, scores_of, values_of, sink_row, attn_scr):
    st = scores_of(0)
    for j in range(n_blocks):
        st_next = scores_of(j + 1) if j + 1 < n_blocks else None
        attn_scr[j * BLOCK:(j + 1) * BLOCK, :] = _weighted_values(st, values_of(j), sink_row).astype(BF16)
        st = st_next


def _project_out(attn_scr, sc_ref, x_ref, mod_ref, w_ref, o_ref):
    mix = _dot(attn_scr[...], w_ref[:ATTN_WIDTH, :]) + _dot(sc_ref[...], w_ref[ATTN_WIDTH:, :])
    o_ref[...] = x_ref[...] + _mod_chunk(mod_ref, 2) * mix


def _attn_ctx_kernel(sink_ref, q_ref, k_ref, v_ref, sc_ref, x_ref, mod_ref, w_ref, *rest, side_cast):
    (o_ref,), (attn_scr,) = _split_refs(rest, 1, side_cast)
    per_seq = SEQ // BLOCK
    kcats = [k_ref[s * SEQ:(s + 1) * SEQ, :].astype(BF16) for s in range(ATTN_CTX_SEQS)]
    vts = [_transposed_values([v_ref[s * SEQ:(s + 1) * SEQ, :]]) for s in range(ATTN_CTX_SEQS)]
    sink_row = sink_ref[...] * LOG2_E
    _attend_blocks(ATTN_CTX_SEQS * per_seq,
                   lambda j: _scores(q_ref[j * BLOCK:(j + 1) * BLOCK, :], kcats[j // per_seq], None),
                   lambda j: vts[j // per_seq], sink_row, attn_scr)
    _project_out(attn_scr, sc_ref, x_ref, mod_ref, w_ref, o_ref)


def _attn_ctx(q, k, v, sc, x, mod_l, mod_base, sink_row, w_out_bf, side=()):
    T = q.shape[0]
    tq = ATTN_CTX_SEQS * SEQ
    tok = lambda w: pl.BlockSpec((tq, w), lambda i: (i, 0))
    return _host_call(
        _attn_ctx_kernel,
        grid=(T // tq,),
        in_specs=[_resident((1, N_HEADS * BLOCK)), tok(ATTN_WIDTH), tok(KV_WIDTH), tok(KV_WIDTH),
                  tok(CONV_WIDTH), tok(D_MODEL), _mod_spec(lambda i: mod_base + CTX_ROW),
                  _resident((ATTN_WIDTH + CONV_WIDTH, D_MODEL))],
        args=[sink_row, q, k, v, sc, x, mod_l, w_out_bf],
        out_specs=[tok(D_MODEL)], out_shape=[jax.ShapeDtypeStruct((T, D_MODEL), F32)],
        scratch_shapes=[pltpu.VMEM((tq, ATTN_WIDTH), BF16)], name="attn_ctx", side=side, step_of=lambda i: i)


def _band_cap(has_prev, has_next):
    c = lax.broadcasted_iota(jnp.int32, (3 * BLOCK, BLOCK), 0)
    r = lax.broadcasted_iota(jnp.int32, (3 * BLOCK, BLOCK), 1)
    first_prev = r + jnp.where(has_prev, 0, BLOCK)
    last_next = r + 2 * BLOCK - jnp.where(has_next, 0, BLOCK)
    masked = ((c < BLOCK) & (c < first_prev)) | ((c >= 2 * BLOCK) & (c > last_next))
    return jnp.where(masked, NEG_INF, SCORE_CAP)


def _attn_win_kernel(sink_ref, q_ref, kp_ref, kc_ref, kn_ref, vp_ref, vc_ref, vn_ref, ckt_ref, cvt_ref,
                     sc_ref, x_ref, mod_ref, w_ref, *rest, n_steps, side_cast):
    (o_ref,), (attn_scr,) = _split_refs(rest, 1, side_cast)
    i = pl.program_id(1)
    kc, vc = kc_ref[...], vc_ref[...]
    inner = range(0, ATTN_Q_BLOCKS * BLOCK, BLOCK)
    k_blocks = [kp_ref[...]] + [kc[j:j + BLOCK] for j in inner] + [kn_ref[...]]
    v_blocks = [vp_ref[...]] + [vc[j:j + BLOCK] for j in inner] + [vn_ref[...]]
    sink_row = sink_ref[...] * LOG2_E
    ckt = ckt_ref[0]
    ck = jnp.concatenate([ckt[:, j:j + BLOCK].T for j in range(0, PAST_LEN, BLOCK)], axis=0)

    def scores_of(j):
        kcat = jnp.concatenate(k_blocks[j:j + 3] + [ck], axis=0).astype(BF16)
        cap = _band_cap(i > 0 if j == 0 else True, i < n_steps - 1 if j == ATTN_Q_BLOCKS - 1 else True)
        return _scores(q_ref[j * BLOCK:(j + 1) * BLOCK, :], kcat, cap)

    _attend_blocks(ATTN_Q_BLOCKS, scores_of, lambda j: _transposed_values(v_blocks[j:j + 3], cvt_ref[0]),
                   sink_row, attn_scr)
    _project_out(attn_scr, sc_ref, x_ref, mod_ref, w_ref, o_ref)


def _attn_win(q, k, v, ck, cv, sc, x, mod_l, mod_base, sink_row, w_out_bf, side=()):
    T = q.shape[0]
    tq = ATTN_Q_BLOCKS * BLOCK
    n_steps = DEC_SEQ // tq
    nb = DEC_SEQ // BLOCK
    cur = lambda w: pl.BlockSpec((tq, w), lambda b, i: (b * n_steps + i, 0))
    prev = pl.BlockSpec((BLOCK, KV_WIDTH), lambda b, i: (b * nb + jnp.maximum(ATTN_Q_BLOCKS * i - 1, 0), 0))
    nxt = pl.BlockSpec((BLOCK, KV_WIDTH),
                       lambda b, i: (b * nb + jnp.minimum(ATTN_Q_BLOCKS * (i + 1), nb - 1), 0))
    ctx = pl.BlockSpec((1, KV_WIDTH, PAST_LEN), lambda b, i: (b, 0, 0))
    in_specs = [_resident((1, N_HEADS * BLOCK)), cur(ATTN_WIDTH),
                prev, cur(KV_WIDTH), nxt, prev, cur(KV_WIDTH), nxt, ctx, ctx,
                cur(CONV_WIDTH), cur(D_MODEL), _mod_spec(lambda b, i: mod_base + b),
                _resident((ATTN_WIDTH + CONV_WIDTH, D_MODEL))]
    args = [sink_row, q, k, k, k, v, v, v, ck, cv, sc, x, mod_l, w_out_bf]
    return _host_call(
        functools.partial(_attn_win_kernel, n_steps=n_steps),
        grid=(T // DEC_SEQ, n_steps), in_specs=in_specs, args=args,
        out_specs=[cur(D_MODEL)], out_shape=[jax.ShapeDtypeStruct((T, D_MODEL), F32)],
        scratch_shapes=[pltpu.VMEM((tq, ATTN_WIDTH), BF16)], name="attn_win", side=side,
        step_of=lambda b, i: b * n_steps + i)


def _gelu_tanh(x):
    k = 0.7978845608028654
    half = 0.5 * x
    return half + half * jnp.tanh(x * (k + (k * 0.044715) * (x * x)))


def _gmlp_kernel(x_ref, mod_ref, g_ref, win_ref, vg_ref, ws_ref, bs_ref, wout_ref, *rest, side_cast):
    (o_ref,), (gated_scr, h_scr) = _split_refs(rest, 1, side_cast)
    n_chunks = x_ref.shape[0] // CHUNK
    parts = _modulated_parts([x_ref], h_scr, g_ref[...], _mod_chunk(mod_ref, 0), _mod_chunk(mod_ref, 1), 1)
    v = _gelu_tanh(_dot_rows(parts, win_ref[:, GMLP_WIDTH:]))
    ms = jnp.mean(v * v, axis=-1, keepdims=True)
    v = (v * lax.rsqrt(ms + EPS) * vg_ref[...]).astype(BF16)
    h = h_scr[...]
    for first in range(0, GMLP_WIDTH, GMLP_U_COLS):
        u = _gelu_tanh(_dot(h, win_ref[:, first:first + GMLP_U_COLS]))
        for grp in range(first // GMLP_GROUP_DIM, (first + GMLP_U_COLS) // GMLP_GROUP_DIM):
            lanes = slice(grp * GMLP_GROUP_DIM, (grp + 1) * GMLP_GROUP_DIM)
            u_lanes = slice(lanes.start - first, lanes.stop - first)
            rhs = jnp.concatenate([v[n * CHUNK:(n + 1) * CHUNK, lanes] for n in range(n_chunks)], axis=1)
            s = _dot(ws_ref[grp], rhs)
            for n in range(n_chunks):
                rows = slice(n * CHUNK, (n + 1) * CHUNK)
                s_n = s[:, n * GMLP_GROUP_DIM:(n + 1) * GMLP_GROUP_DIM] + bs_ref[grp]
                gated_scr[rows, lanes] = (u[rows, u_lanes] * s_n).astype(BF16)
    o_ref[...] = x_ref[...] + _mod_chunk(mod_ref, 2) * _dot(gated_scr[...], wout_ref[...])


def _gmlp(x, mod_l, mod_row, g, win_bf, vg, ws_bf, bs_full, wout_bf, tm, side=()):
    T = x.shape[0]
    tok = pl.BlockSpec((tm, D_MODEL), lambda i: (i, 0))
    in_specs = [tok, _mod_spec(mod_row),
                _resident((1, D_MODEL)), _resident((D_MODEL, 2 * GMLP_WIDTH)), _resident((1, GMLP_WIDTH)),
                _resident((GMLP_GROUPS, CHUNK, CHUNK)), _resident((GMLP_GROUPS, CHUNK, GMLP_GROUP_DIM)),
                _resident((GMLP_WIDTH, D_MODEL))]
    args = [x, mod_l, g, win_bf, vg, ws_bf, bs_full, wout_bf]
    return _host_call(
        _gmlp_kernel, grid=(T // tm,), in_specs=in_specs, args=args,
        out_specs=[tok], out_shape=[jax.ShapeDtypeStruct((T, D_MODEL), F32)],
        scratch_shapes=[pltpu.VMEM((tm, GMLP_WIDTH), BF16), pltpu.VMEM((tm, D_MODEL), BF16)],
        name="gmlp", side=side, step_of=lambda i: i)


def _ffn_kernel(*refs, tiles_per_seq, seq_len, side_cast):
    halo = _halo_rows(tiles_per_seq)
    n_x = 3 if halo else 1
    x_refs, refs = refs[:n_x], refs[n_x:]
    mod_ref, g_ref, wup_ref, cw_ref, wdn_ref = refs[:5]
    (o_ref,), (act_scr, h_scr) = _split_refs(refs[5:], 1, side_cast)
    tm = o_ref.shape[0]
    parts = _modulated_parts(x_refs, h_scr, g_ref[...], _mod_chunk(mod_ref, 3), _mod_chunk(mod_ref, 4),
                             tiles_per_seq)
    for j in range(0, D_FF, FFN_COLS):
        width = min(FFN_COLS, D_FF - j)
        masks = _seq_end_masks(tm, width, seq_len, tiles_per_seq)
        gate_cols = slice(j, j + width)
        val_cols = slice(D_FF + j, D_FF + j + width)
        if j == 0:
            up = lambda cols: _dot_rows(parts, wup_ref[:, cols])
        else:
            h = h_scr[...]
            up = lambda cols: _dot(h, wup_ref[:, cols])
        zg = _token_conv3(up(gate_cols), cw_ref[0, :, gate_cols], tm, halo, masks)
        zv = _token_conv3(up(val_cols), cw_ref[0, :, val_cols], tm, halo, masks)
        act_scr[:, gate_cols] = (jax.nn.silu(zg) * zv).astype(BF16)
    x = x_refs[n_x // 2][...]
    o_ref[...] = x + _mod_chunk(mod_ref, 5) * _dot(act_scr[...], wdn_ref[...])


def _ffn(x, mod_l, mod_row, g, wup_bf, conv_w, wdn_bf, layer, tm, seq_len, side=()):
    T = x.shape[0]
    tiles_per_seq = max(seq_len // tm, 1)
    assert tm % seq_len == 0 or seq_len % tm == 0
    x_specs = _token_specs(T, tm, tiles_per_seq)
    in_specs = x_specs + [_mod_spec(mod_row), _resident((1, D_MODEL)), _resident(wup_bf.shape),
                          _layer_slab(conv_w.shape, layer), _resident(wdn_bf.shape)]
    scratch = [pltpu.VMEM((tm, D_FF), BF16), pltpu.VMEM((tm + 2 * _halo_rows(tiles_per_seq), D_MODEL), BF16)]
    return _host_call(
        functools.partial(_ffn_kernel, tiles_per_seq=tiles_per_seq, seq_len=seq_len),
        grid=(T // tm,), in_specs=in_specs, args=[x] * len(x_specs) + [mod_l, g, wup_bf, conv_w, wdn_bf],
        out_specs=[pl.BlockSpec((tm, D_MODEL), lambda i: (i, 0))], out_shape=[jax.ShapeDtypeStruct((T, D_MODEL), F32)],
        scratch_shapes=scratch, name="conv_ffn", side=side, step_of=lambda i: i)


def _rope_tables(n_tokens):
    t = np.arange(n_tokens)
    n_freq = HEAD_DIM // 4
    inv = (ROPE_BASE ** (-np.arange(n_freq, dtype=np.float32) / n_freq)).astype(np.float32)
    row_ang = (t // GRID_W).astype(np.float32)[:, None] * inv
    col_ang = (t % GRID_W).astype(np.float32)[:, None] * inv
    ang = np.concatenate([row_ang, row_ang, col_ang, col_ang], axis=1)
    ang = np.tile(ang, (1, LANES // HEAD_DIM)).astype(np.float64)
    first = (np.arange(LANES) % (2 * n_freq)) < n_freq
    cos, sin = np.cos(ang), np.sin(ang)
    tables = (cos, np.where(first, -sin, 0.0), np.where(first, 0.0, sin))
    return tuple(jnp.asarray(tab.astype(np.float32)) for tab in tables)


class _Bf16Weights:
    def __init__(self, **stacks):
        self.stacks, self.ready = stacks, {}

    def hosted(self, keys, call):
        due = [(name, slab) for name, slab in keys
               if slab < self.stacks[name].shape[0] and (name, slab) not in self.ready]
        outs = call([(self.stacks[name], slab) for name, slab in due])
        for key, cast in zip(due, outs[len(outs) - len(due):]):
            self.ready[key] = cast
        return outs[:len(outs) - len(due)]

    def get(self, name, slab):
        if (name, slab) not in self.ready:
            self.ready[(name, slab)] = _to_bf16(self.stacks[name][slab:slab + 1])[0]
        return self.ready[(name, slab)]


def kernel(x_prompt, x_sample, cache_k, cache_v, c, c_ctx, ada_w, ada_b, norm_mix_g, norm_ffn_g, w_in_even,
           q_norm_g, k_norm_g, sink_logit, short_conv_w, w_out_even, w_in_odd, gmlp_norm_g, w_spatial,
           b_spatial, w_out_odd, w_up, ffn_conv_w, w_down):
    n_p, n_s = BATCH * SEQ, DEC_BATCH * DEC_SEQ
    xp = x_prompt.reshape(n_p, D_MODEL)
    xs = x_sample.reshape(n_s, D_MODEL)
    cond = jnp.concatenate([c, c_ctx[None, :], jnp.zeros((COND_ROWS - DEC_BATCH - 1, D_MODEL), F32)], axis=0)
    bf = _Bf16Weights(in_even=w_in_even, out_even=w_out_even, in_odd=w_in_odd, out_odd=w_out_odd,
                      spatial=w_spatial.reshape(N_ODD, GMLP_GROUPS * CHUNK, CHUNK), up=w_up, down=w_down)

    mod_l, = bf.hosted([("in_even", 0)], lambda side: _adaln(cond, ada_w, ada_b, side))
    rope_tabs = _rope_tables(DEC_SEQ)

    tm_p, tm_s = 1024, 1024
    row_vec = lambda a: a.reshape(1, -1)
    new_k, new_v = [], []
    for l in range(DEPTH):
        base = l * COND_ROWS
        row_p = lambda i, base=base: base + CTX_ROW
        row_s = lambda i, base=base: base + i // (DEC_SEQ // tm_s)
        g_mix = row_vec(norm_mix_g[l])
        nxt = (l + 1) // 2
        if l % 2 == 0:
            e = l // 2
            w_in = bf.get("in_even", e)
            qg = row_vec(jnp.tile(q_norm_g[e], N_HEADS))
            kg = row_vec(jnp.tile(k_norm_g[e], N_KV_HEADS))
            sink = jnp.repeat(sink_logit[e], BLOCK).reshape(1, N_HEADS * BLOCK)
            cw = short_conv_w[e]
            qp, kp, vp, scp, k_layer, v_layer = bf.hosted([("out_even", e)], lambda side: _in_even(
                xp, mod_l, row_p, g_mix, w_in, qg, kg, cw, None, tm_p, SEQ, side, cache_out=True))
            new_k.append(k_layer)
            new_v.append(v_layer)
            w_out = bf.get("out_even", e)
            odd_next = [("in_odd", nxt), ("out_odd", nxt), ("spatial", nxt)]
            xp, = bf.hosted(odd_next, lambda side: _attn_ctx(qp, kp, vp, scp, xp, mod_l, base, sink, w_out, side))
            qs, ks, vs, scs = bf.hosted([("down", l), ("down", l + 1)], lambda side: _in_even(
                xs, mod_l, row_s, g_mix, w_in, qg, kg, cw, rope_tabs, tm_s, DEC_SEQ, side))
            ck = jnp.transpose(cache_k[:, e], (0, 2, 3, 1)).reshape(DEC_BATCH, KV_WIDTH, PAST_LEN)
            cv = jnp.transpose(cache_v[:, e], (0, 2, 3, 1)).reshape(DEC_BATCH, KV_WIDTH, PAST_LEN)
            xs, = bf.hosted([("up", l), ("up", l + 1)], lambda side: _attn_win(
                qs, ks, vs, ck, cv, scs, xs, mod_l, base, sink, w_out, side))
        else:
            o = l // 2
            w_in, w_out = bf.get("in_odd", o), bf.get("out_odd", o)
            vg = row_vec(gmlp_norm_g[o])
            ws = bf.get("spatial", o).reshape(GMLP_GROUPS, CHUNK, CHUNK)
            bs_full = jnp.broadcast_to(b_spatial[o][:, :, None], (GMLP_GROUPS, CHUNK, GMLP_GROUP_DIM))
            xp, = bf.hosted([("down", l)], lambda side: _gmlp(
                xp, mod_l, row_p, g_mix, w_in, vg, ws, bs_full, w_out, tm_p, side))
            xs, = bf.hosted([("up", l)], lambda side: _gmlp(
                xs, mod_l, row_s, g_mix, w_in, vg, ws, bs_full, w_out, tm_s, side))
        g_ffn = row_vec(norm_ffn_g[l])
        w_up_l, w_down_l = bf.get("up", l), bf.get("down", l)
        xp, = _ffn(xp, mod_l, row_p, g_ffn, w_up_l, ffn_conv_w, w_down_l, l, tm_p, SEQ)
        xs, = _ffn(xs, mod_l, row_s, g_ffn, w_up_l, ffn_conv_w, w_down_l, l, tm_s, DEC_SEQ)
    per_layer = lambda parts: jnp.transpose(parts[0] if len(parts) == 1 else jnp.concatenate(parts, axis=1),
                                            (0, 1, 4, 2, 3))
    return (xp.reshape(BATCH, SEQ, D_MODEL), xs.reshape(DEC_BATCH, DEC_SEQ, D_MODEL),
            per_layer(new_k), per_layer(new_v))
```

```python
import functools

import jax
import jax.numpy as jnp
import numpy as np
from jax import lax
from jax.experimental import pallas as pl
from jax.experimental.pallas import tpu as pltpu

F32 = jnp.float32
BF16 = jnp.bfloat16

D_MODEL = 1024
BATCH = 16
SEQ = 256
DEPTH = 2
DEC_BATCH = 4
DEC_SEQ = 2048
PAST_LEN = 512
GRID_W = 64
N_HEADS = 8
N_KV_HEADS = 2
HEAD_DIM = 64
GQA_GROUP = N_HEADS // N_KV_HEADS
ATTN_WIDTH = N_HEADS * HEAD_DIM
KV_WIDTH = N_KV_HEADS * HEAD_DIM
WINDOW = 128
BLOCK = 128
assert WINDOW == BLOCK
ROPE_BASE = 10000.0
CONV_WIDTH = 512
CHUNK = 128
GMLP_WIDTH = 1024
GMLP_GROUPS = 8
GMLP_GROUP_DIM = GMLP_WIDTH // GMLP_GROUPS
D_FF = 2816
EPS = 1e-6
NEG_INF = -1e30
N_ODD = DEPTH // 2
LOG2_E = 1.4426950408889634

LANES = 128
SUBLANES_BF16 = 16
MXU_WIDTH = 256
VMEM_LIMIT_BYTES = 56 * 1024 * 1024

COND_ROWS = 8
CTX_ROW = DEC_BATCH
TOKEN_HALO = SUBLANES_BF16
FFN_COLS = 2 * MXU_WIDTH
GMLP_U_COLS = MXU_WIDTH
LEAD_PARTS = 4
CAST_BLOCK_BYTES = 3 * 1024 * 1024
ADA_TN = 1536
ATTN_Q_BLOCKS = 4
ATTN_CTX_SEQS = 2
SUM_ROWS = SUBLANES_BF16
SCORE_CAP = 3.0e38


def _params(n_axes):
    return pltpu.CompilerParams(dimension_semantics=("parallel",) * n_axes,
                                vmem_limit_bytes=VMEM_LIMIT_BYTES)


def _resident(shape):
    zeros = (0,) * len(shape)
    return pl.BlockSpec(shape, lambda *_: zeros, pipeline_mode=pl.Buffered(1))


def _layer_slab(shape, layer):
    return pl.BlockSpec((1,) + tuple(shape[1:]), lambda *_: (layer, 0, 0), pipeline_mode=pl.Buffered(1))


def _mod_spec(mod_row):
    return pl.BlockSpec((1, 1, 6 * D_MODEL), lambda *idx: (mod_row(*idx), 0, 0))


def _dot(a, b):
    return jnp.dot(a, b, preferred_element_type=F32)


def _cast_kernel(x_ref, o_ref):
    o_ref[...] = x_ref[...].astype(BF16)


def _to_bf16(w):
    n, rows, cols = w.shape
    fits = [r for r in range(SUBLANES_BF16, rows + 1, SUBLANES_BF16)
            if rows % r == 0 and r * cols * 4 <= CAST_BLOCK_BYTES]
    tr = max(fits)
    spec = pl.BlockSpec((1, tr, cols), lambda i, j: (i, j, 0))
    return pl.pallas_call(
        _cast_kernel, grid=(n, rows // tr), in_specs=[spec], out_specs=spec,
        out_shape=jax.ShapeDtypeStruct(w.shape, BF16), compiler_params=_params(2), name="to_bf16",
    )(w)


def _side_cast_specs(side, n_steps, step_of):
    w, layer = side
    _, rows, cols = w.shape
    tr = rows // n_steps
    assert tr * n_steps == rows and tr % SUBLANES_BF16 == 0
    return (pl.BlockSpec((1, tr, cols), lambda *idx: (layer, step_of(*idx), 0)),
            pl.BlockSpec((tr, cols), lambda *idx: (step_of(*idx), 0)),
            jax.ShapeDtypeStruct((rows, cols), BF16))


def _split_refs(rest, n_out, side_cast):
    n = side_cast
    side_ins, outs, side_outs, scratch = rest[:n], rest[n:n + n_out], rest[n + n_out:2 * n + n_out], rest[2 * n + n_out:]
    for side_in, side_out in zip(side_ins, side_outs):
        side_out[...] = side_in[0].astype(BF16)
    return outs, scratch


def _host_call(kernel, *, grid, in_specs, args, out_specs, out_shape, scratch_shapes, name, side, step_of):
    in_specs, args, out_specs, out_shape = list(in_specs), list(args), list(out_specs), list(out_shape)
    n_steps = 1
    for extent in grid:
        n_steps *= extent
    sides = [_side_cast_specs(one, n_steps, step_of) for one in side]
    in_specs += [spec for spec, _, _ in sides]
    args += [stack for stack, _ in side]
    out_specs += [spec for _, spec, _ in sides]
    out_shape += [shape for _, _, shape in sides]
    return pl.pallas_call(
        functools.partial(kernel, side_cast=len(side)),
        grid=grid, in_specs=in_specs, out_specs=out_specs, out_shape=out_shape, scratch_shapes=scratch_shapes,
        compiler_params=_params(len(grid)), name=name,
    )(*args)


def _modulate(x, g, shift, scale):
    ms = jnp.mean(x * x, axis=-1, keepdims=True)
    return (x * lax.rsqrt(ms + EPS) * g) * (1.0 + scale) + shift


def _mod_chunk(mod_ref, k):
    return mod_ref[0, :, k * D_MODEL:(k + 1) * D_MODEL]


def _halo_rows(tiles_per_seq):
    return TOKEN_HALO if tiles_per_seq > 1 else 0


def _token_specs(T, tm, tiles_per_seq):
    tok = pl.BlockSpec((tm, D_MODEL), lambda i: (i, 0))
    if tiles_per_seq == 1:
        return [tok]
    per_tile = tm // TOKEN_HALO
    last = T // TOKEN_HALO - 1
    prev = pl.BlockSpec((TOKEN_HALO, D_MODEL), lambda i: (jnp.maximum(i * per_tile - 1, 0), 0))
    nxt = pl.BlockSpec((TOKEN_HALO, D_MODEL), lambda i: (jnp.minimum((i + 1) * per_tile, last), 0))
    return [prev, tok, nxt]


def _modulated_parts(x_refs, h_scr, g, shift, scale, tiles_per_seq):
    halo = _halo_rows(tiles_per_seq)
    x_ref = x_refs[len(x_refs) // 2]
    tm = x_ref.shape[0]
    step = tm // LEAD_PARTS
    parts = []
    for p in range(LEAD_PARTS):
        lo, hi = halo + p * step, halo + (p + 1) * step
        piece = _modulate(x_ref[p * step:(p + 1) * step, :], g, shift, scale).astype(BF16)
        if halo and p == 0:
            pos = pl.program_id(0) % tiles_per_seq
            edge = jnp.where(pos > 0, _modulate(x_refs[0][...], g, shift, scale), 0.0).astype(BF16)
            piece, lo = jnp.concatenate([edge, piece], axis=0), 0
        if halo and p == LEAD_PARTS - 1:
            pos = pl.program_id(0) % tiles_per_seq
            edge = jnp.where(pos < tiles_per_seq - 1, _modulate(x_refs[2][...], g, shift, scale), 0.0).astype(BF16)
            piece, hi = jnp.concatenate([piece, edge], axis=0), tm + 2 * halo
        h_scr[lo:hi, :] = piece
        parts.append(piece)
    return parts


def _dot_rows(parts, w):
    return jnp.concatenate([_dot(p, w) for p in parts], axis=0)


def _seq_end_masks(tm, cols, seq_len, tiles_per_seq):
    if tiles_per_seq > 1:
        return None
    seq_row = lax.broadcasted_iota(jnp.int32, (tm, cols), 0) % seq_len
    return seq_row != 0, seq_row != seq_len - 1


def _token_conv3(z, w, tm, halo, masks):
    rows = z.shape[0]
    mid = slice(halo, halo + tm)
    dn = pltpu.roll(z, 1, 0)[mid]
    up = pltpu.roll(z, rows - 1, 0)[mid]
    if masks is not None:
        dn = jnp.where(masks[0], dn, 0.0)
        up = jnp.where(masks[1], up, 0.0)
    return dn * w[0:1] + z[mid] * w[1:2] + up * w[2:3]


def _adaln_kernel(cond_ref, w_ref, b_ref, *rest, side_cast):
    (o_ref,), _ = _split_refs(rest, 1, side_cast)
    a = jax.nn.silu(cond_ref[...]).astype(BF16)
    rows = _dot(a, w_ref[0].astype(BF16)) + b_ref[0]
    for r in range(COND_ROWS):
        o_ref[r] = rows[r:r + 1, :]


def _adaln(cond, ada_w, ada_b, side=()):
    n_out = 6 * D_MODEL
    n_col = n_out // ADA_TN
    return _host_call(
        _adaln_kernel,
        grid=(DEPTH, n_col),
        in_specs=[pl.BlockSpec((COND_ROWS, D_MODEL), lambda l, j: (0, 0)),
                  pl.BlockSpec((1, D_MODEL, ADA_TN), lambda l, j: (l, 0, j)),
                  pl.BlockSpec((1, 1, ADA_TN), lambda l, j: (l, 0, j))],
        args=[cond, ada_w, ada_b.reshape(DEPTH, 1, n_out)],
        out_specs=[pl.BlockSpec((COND_ROWS, 1, ADA_TN), lambda l, j: (l, 0, j))],
        out_shape=[jax.ShapeDtypeStruct((DEPTH * COND_ROWS, 1, n_out), F32)],
        scratch_shapes=[], name="adaln", side=side, step_of=lambda l, j: l * n_col + j)


def _head_rms(z, gain):
    n = z.shape[1]
    w = min(n, MXU_WIDTH)
    r = lax.broadcasted_iota(jnp.int32, (w, w), 0) // HEAD_DIM
    c = lax.broadcasted_iota(jnp.int32, (w, w), 1) // HEAD_DIM
    ones = (r == c).astype(BF16)
    sq = (z * z).astype(BF16)
    ss = jnp.concatenate([_dot(sq[:, k:k + w], ones) for k in range(0, n, w)], axis=1)
    return z * lax.rsqrt(ss * (1.0 / HEAD_DIM) + EPS) * gain


def _rope(z, cos, sin_lo, sin_hi):
    outs = []
    for k in range(0, z.shape[1], LANES):
        blk = z[:, k:k + LANES]
        outs.append(blk * cos + pltpu.roll(blk, LANES - 16, 1) * sin_lo + pltpu.roll(blk, 16, 1) * sin_hi)
    return jnp.concatenate(outs, axis=1)


def _in_even_kernel(*refs, rope, tiles_per_seq, seq_len, cache_out, side_cast):
    halo = _halo_rows(tiles_per_seq)
    n_x = 3 if halo else 1
    x_refs, refs = refs[:n_x], refs[n_x:]
    mod_ref, g_ref, w_ref, qg_ref, kg_ref, cw_ref = refs[:6]
    refs = refs[6:]
    if rope:
        cos_ref, slo_ref, shi_ref = refs[:3]
        refs = refs[3:]
    outs, (h_scr,) = _split_refs(refs, 6 if cache_out else 4, side_cast)
    q_ref, k_ref, v_ref, sc_ref = outs[:4]
    tm = q_ref.shape[0]
    parts = _modulated_parts(x_refs, h_scr, g_ref[...], _mod_chunk(mod_ref, 0), _mod_chunk(mod_ref, 1),
                             tiles_per_seq)
    c0 = 0
    c1 = c0 + ATTN_WIDTH
    c2 = c1 + 2 * KV_WIDTH
    c3 = c2 + CONV_WIDTH
    c4 = c3 + CONV_WIDTH
    c5 = c4 + CONV_WIDTH
    zqkv = _dot_rows(parts, w_ref[:, c0:c2])[halo:halo + tm]
    zq, zkv = zqkv[:, :ATTN_WIDTH], zqkv[:, ATTN_WIDTH:]
    h_all = h_scr[...]
    h = h_scr[halo:halo + tm, :]
    q = _head_rms(zq, qg_ref[...] * (HEAD_DIM ** -0.5 * LOG2_E))
    k = _head_rms(zkv[:, :KV_WIDTH], kg_ref[...])
    if rope:
        cos, slo, shi = cos_ref[...], slo_ref[...], shi_ref[...]
        q = _rope(q, cos, slo, shi)
        k = _rope(k, cos, slo, shi)
    v = zkv[:, KV_WIDTH:]
    q_ref[...] = q
    k_ref[...] = k
    v_ref[...] = v
    if cache_out:
        for cache_ref, rows in zip(outs[4:], (k, v)):
            for s in range(tm // seq_len):
                seq_t = jnp.concatenate([rows[j:j + BLOCK, :].T for j in range(s * seq_len, (s + 1) * seq_len, BLOCK)],
                                        axis=1)
                cache_ref[s, 0] = seq_t.reshape(N_KV_HEADS, HEAD_DIM, seq_len)
    ch = _dot(h_all, w_ref[:, c3:c4]) * _dot(h_all, w_ref[:, c4:c5])
    conv = _token_conv3(ch, cw_ref[...], tm, halo, _seq_end_masks(tm, CONV_WIDTH, seq_len, tiles_per_seq))
    sc_ref[...] = (_dot(h, w_ref[:, c2:c3]) * conv).astype(BF16)


def _in_even(x, mod_l, mod_row, g, w_bf, qg, kg, conv_w, rope_tabs, tm, seq_len, side=(), cache_out=False):
    T = x.shape[0]
    n_in = w_bf.shape[1]
    tiles_per_seq = max(seq_len // tm, 1)
    assert tm % seq_len == 0 or seq_len % tm == 0
    tok = lambda w: pl.BlockSpec((tm, w), lambda i: (i, 0))
    x_specs = _token_specs(T, tm, tiles_per_seq)
    in_specs = x_specs + [_mod_spec(mod_row), _resident((1, D_MODEL)), _resident((D_MODEL, n_in)),
                          _resident((1, ATTN_WIDTH)), _resident((1, KV_WIDTH)), _resident((3, CONV_WIDTH))]
    args = [x] * len(x_specs) + [mod_l, g, w_bf, qg, kg, conv_w]
    if rope_tabs is not None:
        in_specs += [pl.BlockSpec((tm, LANES), lambda i: (i % tiles_per_seq, 0))] * 3
        args += list(rope_tabs)
    outs = ((ATTN_WIDTH, F32), (KV_WIDTH, F32), (KV_WIDTH, F32), (CONV_WIDTH, BF16))
    out_specs = [tok(w) for w, _ in outs]
    out_shape = [jax.ShapeDtypeStruct((T, w), dt) for w, dt in outs]
    if cache_out:
        n_seq = tm // seq_len
        cache_block = (n_seq, 1, N_KV_HEADS, HEAD_DIM, seq_len)
        out_specs += [pl.BlockSpec(cache_block, lambda i: (i, 0, 0, 0, 0))] * 2
        out_shape += [jax.ShapeDtypeStruct((T // seq_len,) + cache_block[1:], F32)] * 2
    scratch = [pltpu.VMEM((tm + 2 * _halo_rows(tiles_per_seq), D_MODEL), BF16)]
    return _host_call(
        functools.partial(_in_even_kernel, rope=rope_tabs is not None, tiles_per_seq=tiles_per_seq, seq_len=seq_len,
                          cache_out=cache_out),
        grid=(T // tm,), in_specs=in_specs, args=args, out_specs=out_specs, out_shape=out_shape,
        scratch_shapes=scratch, name="in_even", side=side, step_of=lambda i: i)


def _stack_heads(q):
    lane_lo = lax.broadcasted_iota(jnp.int32, (BLOCK, LANES), 1) < HEAD_DIM
    heads = []
    for pair in range(N_HEADS // 2):
        kv = (2 * pair) // GQA_GROUP
        qp = q[:, pair * LANES:(pair + 1) * LANES]
        qr = pltpu.roll(qp, HEAD_DIM, 1)
        for half in range(2):
            src = qp if half == kv else qr
            heads.append(jnp.where(lane_lo, src, 0.0) if kv == 0 else jnp.where(lane_lo, 0.0, src))
    return jnp.concatenate(heads, axis=0).astype(BF16)


def _transposed_values(v_blocks, vt_tail=None):
    cols = [v[j:j + BLOCK].T for v in v_blocks for j in range(0, v.shape[0], BLOCK)]
    vt = jnp.concatenate(cols + ([] if vt_tail is None else [vt_tail]), axis=1)
    return jnp.concatenate([vt.astype(BF16), jnp.ones((SUM_ROWS, vt.shape[1]), BF16)], axis=0)


def _scores(q, kcat, cap):
    st = lax.dot_general(kcat, _stack_heads(q), (((1,), (1,)), ((), ())), preferred_element_type=F32)
    if cap is None:
        return st
    n_band = cap.shape[0]
    capped = jnp.minimum(st[:n_band], jnp.concatenate([cap] * N_HEADS, axis=1))
    return jnp.concatenate([capped, st[n_band:]], axis=0)


def _weighted_values(st, vt, sink_row):
    m = jnp.maximum(jnp.max(st, axis=0, keepdims=True), sink_row)
    pt = jnp.exp2(st - m).astype(BF16)
    ot = _dot(vt, pt)
    denom = ot[KV_WIDTH:KV_WIDTH + 1] + jnp.exp2(sink_row - m)
    ot = ot[:KV_WIDTH] / denom
    pairs = []
    for pair in range(N_HEADS // 2):
        kv = (2 * pair) // GQA_GROUP
        dims = slice(kv * HEAD_DIM, (kv + 1) * HEAD_DIM)
        both = [ot[dims, (2 * pair + half) * BLOCK:(2 * pair + half + 1) * BLOCK] for half in range(2)]
        pairs.append(jnp.concatenate(both, axis=0).T)
    return jnp.concatenate(pairs, axis=1)


def _attend_blocks(n_blocks, scores_of, values_of, sink_row, attn_scr):
    st = scores_of(0)
    for j in range(n_blocks):
        st_next = scores_of(j + 1) if j + 1 < n_blocks else None
        attn_scr[j * BLOCK:(j + 1) * BLOCK, :] = _weighted_values(st, values_of(j), sink_row).astype(BF16)
        st = st_next


def _project_out(attn_scr, sc_ref, x_ref, mod_ref, w_ref, o_ref):
    mix = _dot(attn_scr[...], w_ref[:ATTN_WIDTH, :]) + _dot(sc_ref[...], w_ref[ATTN_WIDTH:, :])
    o_ref[...] = x_ref[...] + _mod_chunk(mod_ref, 2) * mix


def _attn_ctx_kernel(sink_ref, q_ref, k_ref, v_ref, sc_ref, x_ref, mod_ref, w_ref, *rest, side_cast):
    (o_ref,), (attn_scr,) = _split_refs(rest, 1, side_cast)
    per_seq = SEQ // BLOCK
    kcats = [k_ref[s * SEQ:(s + 1) * SEQ, :].astype(BF16) for s in range(ATTN_CTX_SEQS)]
    vts = [_transposed_values([v_ref[s * SEQ:(s + 1) * SEQ, :]]) for s in range(ATTN_CTX_SEQS)]
    sink_row = sink_ref[...] * LOG2_E
    _attend_blocks(ATTN_CTX_SEQS * per_seq,
                   lambda j: _scores(q_ref[j * BLOCK:(j + 1) * BLOCK, :], kcats[j // per_seq], None),
                   lambda j: vts[j // per_seq], sink_row, attn_scr)
    _project_out(attn_scr, sc_ref, x_ref, mod_ref, w_ref, o_ref)


def _attn_ctx(q, k, v, sc, x, mod_l, mod_base, sink_row, w_out_bf, side=()):
    T = q.shape[0]
    tq = ATTN_CTX_SEQS * SEQ
    tok = lambda w: pl.BlockSpec((tq, w), lambda i: (i, 0))
    return _host_call(
        _attn_ctx_kernel,
        grid=(T // tq,),
        in_specs=[_resident((1, N_HEADS * BLOCK)), tok(ATTN_WIDTH), tok(KV_WIDTH), tok(KV_WIDTH),
                  tok(CONV_WIDTH), tok(D_MODEL), _mod_spec(lambda i: mod_base + CTX_ROW),
                  _resident((ATTN_WIDTH + CONV_WIDTH, D_MODEL))],
        args=[sink_row, q, k, v, sc, x, mod_l, w_out_bf],
        out_specs=[tok(D_MODEL)], out_shape=[jax.ShapeDtypeStruct((T, D_MODEL), F32)],
        scratch_shapes=[pltpu.VMEM((tq, ATTN_WIDTH), BF16)], name="attn_ctx", side=side, step_of=lambda i: i)


def _band_cap(has_prev, has_next):
    c = lax.broadcasted_iota(jnp.int32, (3 * BLOCK, BLOCK), 0)
    r = lax.broadcasted_iota(jnp.int32, (3 * BLOCK, BLOCK), 1)
    first_prev = r + jnp.where(has_prev, 0, BLOCK)
    last_next = r + 2 * BLOCK - jnp.where(has_next, 0, BLOCK)
    masked = ((c < BLOCK) & (c < first_prev)) | ((c >= 2 * BLOCK) & (c > last_next))
    return jnp.where(masked, NEG_INF, SCORE_CAP)


def _attn_win_kernel(sink_ref, q_ref, kp_ref, kc_ref, kn_ref, vp_ref, vc_ref, vn_ref, ckt_ref, cvt_ref,
                     sc_ref, x_ref, mod_ref, w_ref, *rest, n_steps, side_cast):
    (o_ref,), (attn_scr,) = _split_refs(rest, 1, side_cast)
    i = pl.program_id(1)
    kc, vc = kc_ref[...], vc_ref[...]
    inner = range(0, ATTN_Q_BLOCKS * BLOCK, BLOCK)
    k_blocks = [kp_ref[...]] + [kc[j:j + BLOCK] for j in inner] + [kn_ref[...]]
    v_blocks = [vp_ref[...]] + [vc[j:j + BLOCK] for j in inner] + [vn_ref[...]]
    sink_row = sink_ref[...] * LOG2_E
    ckt = ckt_ref[0]
    ck = jnp.concatenate([ckt[:, j:j + BLOCK].T for j in range(0, PAST_LEN, BLOCK)], axis=0)

    def scores_of(j):
        kcat = jnp.concatenate(k_blocks[j:j + 3] + [ck], axis=0).astype(BF16)
        cap = _band_cap(i > 0 if j == 0 else True, i < n_steps - 1 if j == ATTN_Q_BLOCKS - 1 else True)
        return _scores(q_ref[j * BLOCK:(j + 1) * BLOCK, :], kcat, cap)

    _attend_blocks(ATTN_Q_BLOCKS, scores_of, lambda j: _transposed_values(v_blocks[j:j + 3], cvt_ref[0]),
                   sink_row, attn_scr)
    _project_out(attn_scr, sc_ref, x_ref, mod_ref, w_ref, o_ref)


def _attn_win(q, k, v, ck, cv, sc, x, mod_l, mod_base, sink_row, w_out_bf, side=()):
    T = q.shape[0]
    tq = ATTN_Q_BLOCKS * BLOCK
    n_steps = DEC_SEQ // tq
    nb = DEC_SEQ // BLOCK
    cur = lambda w: pl.BlockSpec((tq, w), lambda b, i: (b * n_steps + i, 0))
    prev = pl.BlockSpec((BLOCK, KV_WIDTH), lambda b, i: (b * nb + jnp.maximum(ATTN_Q_BLOCKS * i - 1, 0), 0))
    nxt = pl.BlockSpec((BLOCK, KV_WIDTH),
                       lambda b, i: (b * nb + jnp.minimum(ATTN_Q_BLOCKS * (i + 1), nb - 1), 0))
    ctx = pl.BlockSpec((1, KV_WIDTH, PAST_LEN), lambda b, i: (b, 0, 0))
    in_specs = [_resident((1, N_HEADS * BLOCK)), cur(ATTN_WIDTH),
                prev, cur(KV_WIDTH), nxt, prev, cur(KV_WIDTH), nxt, ctx, ctx,
                cur(CONV_WIDTH), cur(D_MODEL), _mod_spec(lambda b, i: mod_base + b),
                _resident((ATTN_WIDTH + CONV_WIDTH, D_MODEL))]
    args = [sink_row, q, k, k, k, v, v, v, ck, cv, sc, x, mod_l, w_out_bf]
    return _host_call(
        functools.partial(_attn_win_kernel, n_steps=n_steps),
        grid=(T // DEC_SEQ, n_steps), in_specs=in_specs, args=args,
        out_specs=[cur(D_MODEL)], out_shape=[jax.ShapeDtypeStruct((T, D_MODEL), F32)],
        scratch_shapes=[pltpu.VMEM((tq, ATTN_WIDTH), BF16)], name="attn_win", side=side,
        step_of=lambda b, i: b * n_steps + i)


def _gelu_tanh(x):
    k = 0.7978845608028654
    half = 0.5 * x
    return half + half * jnp.tanh(x * (k + (k * 0.044715) * (x * x)))


def _gmlp_kernel(x_ref, mod_ref, g_ref, win_ref, vg_ref, ws_ref, bs_ref, wout_ref, *rest, side_cast):
    (o_ref,), (gated_scr, h_scr) = _split_refs(rest, 1, side_cast)
    n_chunks = x_ref.shape[0] // CHUNK
    parts = _modulated_parts([x_ref], h_scr, g_ref[...], _mod_chunk(mod_ref, 0), _mod_chunk(mod_ref, 1), 1)
    v = _gelu_tanh(_dot_rows(parts, win_ref[:, GMLP_WIDTH:]))
    ms = jnp.mean(v * v, axis=-1, keepdims=True)
    v = (v * lax.rsqrt(ms + EPS) * vg_ref[...]).astype(BF16)
    h = h_scr[...]
    for first in range(0, GMLP_WIDTH, GMLP_U_COLS):
        u = _gelu_tanh(_dot(h, win_ref[:, first:first + GMLP_U_COLS]))
        for grp in range(first // GMLP_GROUP_DIM, (first + GMLP_U_COLS) // GMLP_GROUP_DIM):
            lanes = slice(grp * GMLP_GROUP_DIM, (grp + 1) * GMLP_GROUP_DIM)
            u_lanes = slice(lanes.start - first, lanes.stop - first)
            rhs = jnp.concatenate([v[n * CHUNK:(n + 1) * CHUNK, lanes] for n in range(n_chunks)], axis=1)
            s = _dot(ws_ref[grp], rhs)
            for n in range(n_chunks):
                rows = slice(n * CHUNK, (n + 1) * CHUNK)
                s_n = s[:, n * GMLP_GROUP_DIM:(n + 1) * GMLP_GROUP_DIM] + bs_ref[grp]
                gated_scr[rows, lanes] = (u[rows, u_lanes] * s_n).astype(BF16)
    o_ref[...] = x_ref[...] + _mod_chunk(mod_ref, 2) * _dot(gated_scr[...], wout_ref[...])


def _gmlp(x, mod_l, mod_row, g, win_bf, vg, ws_bf, bs_full, wout_bf, tm, side=()):
    T = x.shape[0]
    tok = pl.BlockSpec((tm, D_MODEL), lambda i: (i, 0))
    in_specs = [tok, _mod_spec(mod_row),
                _resident((1, D_MODEL)), _resident((D_MODEL, 2 * GMLP_WIDTH)), _resident((1, GMLP_WIDTH)),
                _resident((GMLP_GROUPS, CHUNK, CHUNK)), _resident((GMLP_GROUPS, CHUNK, GMLP_GROUP_DIM)),
                _resident((GMLP_WIDTH, D_MODEL))]
    args = [x, mod_l, g, win_bf, vg, ws_bf, bs_full, wout_bf]
    return _host_call(
        _gmlp_kernel, grid=(T // tm,), in_specs=in_specs, args=args,
        out_specs=[tok], out_shape=[jax.ShapeDtypeStruct((T, D_MODEL), F32)],
        scratch_shapes=[pltpu.VMEM((tm, GMLP_WIDTH), BF16), pltpu.VMEM((tm, D_MODEL), BF16)],
        name="gmlp", side=side, step_of=lambda i: i)


def _ffn_kernel(*refs, tiles_per_seq, seq_len, side_cast):
    halo = _halo_rows(tiles_per_seq)
    n_x = 3 if halo else 1
    x_refs, refs = refs[:n_x], refs[n_x:]
    mod_ref, g_ref, wup_ref, cw_ref, wdn_ref = refs[:5]
    (o_ref,), (act_scr, h_scr) = _split_refs(refs[5:], 1, side_cast)
    tm = o_ref.shape[0]
    parts = _modulated_parts(x_refs, h_scr, g_ref[...], _mod_chunk(mod_ref, 3), _mod_chunk(mod_ref, 4),
                             tiles_per_seq)
    for j in range(0, D_FF, FFN_COLS):
        width = min(FFN_COLS, D_FF - j)
        masks = _seq_end_masks(tm, width, seq_len, tiles_per_seq)
        gate_cols = slice(j, j + width)
        val_cols = slice(D_FF + j, D_FF + j + width)
        if j == 0:
            up = lambda cols: _dot_rows(parts, wup_ref[:, cols])
        else:
            h = h_scr[...]
            up = lambda cols: _dot(h, wup_ref[:, cols])
        zg = _token_conv3(up(gate_cols), cw_ref[0, :, gate_cols], tm, halo, masks)
        zv = _token_conv3(up(val_cols), cw_ref[0, :, val_cols], tm, halo, masks)
        act_scr[:, gate_cols] = (jax.nn.silu(zg) * zv).astype(BF16)
    x = x_refs[n_x // 2][...]
    o_ref[...] = x + _mod_chunk(mod_ref, 5) * _dot(act_scr[...], wdn_ref[...])


def _ffn(x, mod_l, mod_row, g, wup_bf, conv_w, wdn_bf, layer, tm, seq_len, side=()):
    T = x.shape[0]
    tiles_per_seq = max(seq_len // tm, 1)
    assert tm % seq_len == 0 or seq_len % tm == 0
    x_specs = _token_specs(T, tm, tiles_per_seq)
    in_specs = x_specs + [_mod_spec(mod_row), _resident((1, D_MODEL)), _resident(wup_bf.shape),
                          _layer_slab(conv_w.shape, layer), _resident(wdn_bf.shape)]
    scratch = [pltpu.VMEM((tm, D_FF), BF16), pltpu.VMEM((tm + 2 * _halo_rows(tiles_per_seq), D_MODEL), BF16)]
    return _host_call(
        functools.partial(_ffn_kernel, tiles_per_seq=tiles_per_seq, seq_len=seq_len),
        grid=(T // tm,), in_specs=in_specs, args=[x] * len(x_specs) + [mod_l, g, wup_bf, conv_w, wdn_bf],
        out_specs=[pl.BlockSpec((tm, D_MODEL), lambda i: (i, 0))], out_shape=[jax.ShapeDtypeStruct((T, D_MODEL), F32)],
        scratch_shapes=scratch, name="conv_ffn", side=side, step_of=lambda i: i)


def _rope_tables(n_tokens):
    t = np.arange(n_tokens)
    n_freq = HEAD_DIM // 4
    inv = (ROPE_BASE ** (-np.arange(n_freq, dtype=np.float32) / n_freq)).astype(np.float32)
    row_ang = (t // GRID_W).astype(np.float32)[:, None] * inv
    col_ang = (t % GRID_W).astype(np.float32)[:, None] * inv
    ang = np.concatenate([row_ang, row_ang, col_ang, col_ang], axis=1)
    ang = np.tile(ang, (1, LANES // HEAD_DIM)).astype(np.float64)
    first = (np.arange(LANES) % (2 * n_freq)) < n_freq
    cos, sin = np.cos(ang), np.sin(ang)
    tables = (cos, np.where(first, -sin, 0.0), np.where(first, 0.0, sin))
    return tuple(jnp.asarray(tab.astype(np.float32)) for tab in tables)


class _Bf16Weights:
    def __init__(self, **stacks):
        self.stacks, self.ready = stacks, {}

    def hosted(self, keys, call):
        due = [(name, slab) for name, slab in keys
               if slab < self.stacks[name].shape[0] and (name, slab) not in self.ready]
        outs = call([(self.stacks[name], slab) for name, slab in due])
        for key, cast in zip(due, outs[len(outs) - len(due):]):
            self.ready[key] = cast
        return outs[:len(outs) - len(due)]

    def get(self, name, slab):
        if (name, slab) not in self.ready:
            self.ready[(name, slab)] = _to_bf16(self.stacks[name][slab:slab + 1])[0]
        return self.ready[(name, slab)]


def kernel(x_prompt, x_sample, cache_k, cache_v, c, c_ctx, ada_w, ada_b, norm_mix_g, norm_ffn_g, w_in_even,
           q_norm_g, k_norm_g, sink_logit, short_conv_w, w_out_even, w_in_odd, gmlp_norm_g, w_spatial,
           b_spatial, w_out_odd, w_up, ffn_conv_w, w_down):
    n_p, n_s = BATCH * SEQ, DEC_BATCH * DEC_SEQ
    xp = x_prompt.reshape(n_p, D_MODEL)
    xs = x_sample.reshape(n_s, D_MODEL)
    cond = jnp.concatenate([c, c_ctx[None, :], jnp.zeros((COND_ROWS - DEC_BATCH - 1, D_MODEL), F32)], axis=0)
    bf = _Bf16Weights(in_even=w_in_even, out_even=w_out_even, in_odd=w_in_odd, out_odd=w_out_odd,
                      spatial=w_spatial.reshape(N_ODD, GMLP_GROUPS * CHUNK, CHUNK), up=w_up, down=w_down)

    mod_l, = bf.hosted([("in_even", 0)], lambda side: _adaln(cond, ada_w, ada_b, side))
    rope_tabs = _rope_tables(DEC_SEQ)

    tm_p, tm_s = 1024, 1024
    row_vec = lambda a: a.reshape(1, -1)
    new_k, new_v = [], []
    for l in range(DEPTH):
        base = l * COND_ROWS
        row_p = lambda i, base=base: base + CTX_ROW
        row_s = lambda i, base=base: base + i // (DEC_SEQ // tm_s)
        g_mix = row_vec(norm_mix_g[l])
        nxt = (l + 1) // 2
        if l % 2 == 0:
            e = l // 2
            w_in = bf.get("in_even", e)
            qg = row_vec(jnp.tile(q_norm_g[e], N_HEADS))
            kg = row_vec(jnp.tile(k_norm_g[e], N_KV_HEADS))
            sink = jnp.repeat(sink_logit[e], BLOCK).reshape(1, N_HEADS * BLOCK)
            cw = short_conv_w[e]
            qp, kp, vp, scp, k_layer, v_layer = bf.hosted([("out_even", e)], lambda side: _in_even(
                xp, mod_l, row_p, g_mix, w_in, qg, kg, cw, None, tm_p, SEQ, side, cache_out=True))
            new_k.append(k_layer)
            new_v.append(v_layer)
            w_out = bf.get("out_even", e)
            odd_next = [("in_odd", nxt), ("out_odd", nxt), ("spatial", nxt)]
            xp, = bf.hosted(odd_next, lambda side: _attn_ctx(qp, kp, vp, scp, xp, mod_l, base, sink, w_out, side))
            qs, ks, vs, scs = bf.hosted([("down", l), ("down", l + 1)], lambda side: _in_even(
                xs, mod_l, row_s, g_mix, w_in, qg, kg, cw, rope_tabs, tm_s, DEC_SEQ, side))
            ck = jnp.transpose(cache_k[:, e], (0, 2, 3, 1)).reshape(DEC_BATCH, KV_WIDTH, PAST_LEN)
            cv = jnp.transpose(cache_v[:, e], (0, 2, 3, 1)).reshape(DEC_BATCH, KV_WIDTH, PAST_LEN)
            xs, = bf.hosted([("up", l), ("up", l + 1)], lambda side: _attn_win(
                qs, ks, vs, ck, cv, scs, xs, mod_l, base, sink, w_out, side))
        else:
            o = l // 2
            w_in, w_out = bf.get("in_odd", o), bf.get("out_odd", o)
            vg = row_vec(gmlp_norm_g[o])
            ws = bf.get("spatial", o).reshape(GMLP_GROUPS, CHUNK, CHUNK)
            bs_full = jnp.broadcast_to(b_spatial[o][:, :, None], (GMLP_GROUPS, CHUNK, GMLP_GROUP_DIM))
            xp, = bf.hosted([("down", l)], lambda side: _gmlp(
                xp, mod_l, row_p, g_mix, w_in, vg, ws, bs_full, w_out, tm_p, side))
            xs, = bf.hosted([("up", l)], lambda side: _gmlp(
                xs, mod_l, row_s, g_mix, w_in, vg, ws, bs_full, w_out, tm_s, side))
        g_ffn = row_vec(norm_ffn_g[l])
        w_up_l, w_down_l = bf.get("up", l), bf.get("down", l)
        xp, = _ffn(xp, mod_l, row_p, g_ffn, w_up_l, ffn_conv_w, w_down_l, l, tm_p, SEQ)
        xs, = _ffn(xs, mod_l, row_s, g_ffn, w_up_l, ffn_conv_w, w_down_l, l, tm_s, DEC_SEQ)
    per_layer = lambda parts: jnp.transpose(parts[0] if len(parts) == 1 else jnp.concatenate(parts, axis=1),
                                            (0, 1, 4, 2, 3))
    return (xp.reshape(BATCH, SEQ, D_MODEL), xs.reshape(DEC_BATCH, DEC_SEQ, D_MODEL),
            per_layer(new_k), per_layer(new_v))
```

```python
import functools

import jax
import jax.numpy as jnp
import numpy as np
from jax import lax
from jax.experimental import pallas as pl
from jax.experimental.pallas import tpu as pltpu

F32 = jnp.float32
BF16 = jnp.bfloat16

D_MODEL = 1024
BATCH = 16
SEQ = 256
DEPTH = 2
DEC_BATCH = 4
DEC_SEQ = 2048
PAST_LEN = 512
GRID_W = 64
N_HEADS = 8
N_KV_HEADS = 2
HEAD_DIM = 64
GQA_GROUP = N_HEADS // N_KV_HEADS
ATTN_WIDTH = N_HEADS * HEAD_DIM
KV_WIDTH = N_KV_HEADS * HEAD_DIM
WINDOW = 128
BLOCK = 128
assert WINDOW == BLOCK
ROPE_BASE = 10000.0
CONV_WIDTH = 512
CHUNK = 128
GMLP_WIDTH = 1024
GMLP_GROUPS = 8
GMLP_GROUP_DIM = GMLP_WIDTH // GMLP_GROUPS
D_FF = 2816
EPS = 1e-6
NEG_INF = -1e30
N_ODD = DEPTH // 2
LOG2_E = 1.4426950408889634

LANES = 128
SUBLANES_BF16 = 16
MXU_WIDTH = 256
VMEM_LIMIT_BYTES = 56 * 1024 * 1024

COND_ROWS = 8
CTX_ROW = DEC_BATCH
TOKEN_HALO = SUBLANES_BF16
FFN_COLS = 2 * MXU_WIDTH
GMLP_U_COLS = MXU_WIDTH
LEAD_PARTS = 4
GMLP_LEAD_PARTS = 2
CAST_BLOCK_BYTES = 3 * 1024 * 1024
ADA_TN = 1536
ATTN_Q_BLOCKS = 4
ATTN_CTX_SEQS = 2
SUM_ROWS = SUBLANES_BF16
SCORE_CAP = 3.0e38


def _params(n_axes):
    return pltpu.CompilerParams(dimension_semantics=("parallel",) * n_axes,
                                vmem_limit_bytes=VMEM_LIMIT_BYTES)


def _resident(shape):
    zeros = (0,) * len(shape)
    return pl.BlockSpec(shape, lambda *_: zeros, pipeline_mode=pl.Buffered(1))


def _layer_slab(shape, layer):
    return pl.BlockSpec((1,) + tuple(shape[1:]), lambda *_: (layer, 0, 0), pipeline_mode=pl.Buffered(1))


def _mod_spec(mod_row):
    return pl.BlockSpec((1, 1, 6 * D_MODEL), lambda *idx: (mod_row(*idx), 0, 0))


def _dot(a, b):
    return jnp.dot(a, b, preferred_element_type=F32)


def _cast_kernel(x_ref, o_ref):
    o_ref[...] = x_ref[...].astype(BF16)


def _to_bf16(w):
    n, rows, cols = w.shape
    fits = [r for r in range(SUBLANES_BF16, rows + 1, SUBLANES_BF16)
            if rows % r == 0 and r * cols * 4 <= CAST_BLOCK_BYTES]
    tr = max(fits)
    spec = pl.BlockSpec((1, tr, cols), lambda i, j: (i, j, 0))
    return pl.pallas_call(
        _cast_kernel, grid=(n, rows // tr), in_specs=[spec], out_specs=spec,
        out_shape=jax.ShapeDtypeStruct(w.shape, BF16), compiler_params=_params(2), name="to_bf16",
    )(w)


def _side_cast_specs(side, n_steps, step_of):
    w, layer = side
    _, rows, cols = w.shape
    tr = rows // n_steps
    assert tr * n_steps == rows and tr % SUBLANES_BF16 == 0
    return (pl.BlockSpec((1, tr, cols), lambda *idx: (layer, step_of(*idx), 0)),
            pl.BlockSpec((tr, cols), lambda *idx: (step_of(*idx), 0)),
            jax.ShapeDtypeStruct((rows, cols), BF16))


def _split_refs(rest, n_out, side_cast):
    n = side_cast
    side_ins, outs, side_outs, scratch = rest[:n], rest[n:n + n_out], rest[n + n_out:2 * n + n_out], rest[2 * n + n_out:]
    for side_in, side_out in zip(side_ins, side_outs):
        side_out[...] = side_in[0].astype(BF16)
    return outs, scratch


def _host_call(kernel, *, grid, in_specs, args, out_specs, out_shape, scratch_shapes, name, side, step_of):
    in_specs, args, out_specs, out_shape = list(in_specs), list(args), list(out_specs), list(out_shape)
    n_steps = 1
    for extent in grid:
        n_steps *= extent
    sides = [_side_cast_specs(one, n_steps, step_of) for one in side]
    in_specs += [spec for spec, _, _ in sides]
    args += [stack for stack, _ in side]
    out_specs += [spec for _, spec, _ in sides]
    out_shape += [shape for _, _, shape in sides]
    return pl.pallas_call(
        functools.partial(kernel, side_cast=len(side)),
        grid=grid, in_specs=in_specs, out_specs=out_specs, out_shape=out_shape, scratch_shapes=scratch_shapes,
        compiler_params=_params(len(grid)), name=name,
    )(*args)


def _modulate(x, g, shift, scale):
    ms = jnp.mean(x * x, axis=-1, keepdims=True)
    return (x * lax.rsqrt(ms + EPS) * g) * (1.0 + scale) + shift


def _mod_chunk(mod_ref, k):
    return mod_ref[0, :, k * D_MODEL:(k + 1) * D_MODEL]


def _halo_rows(tiles_per_seq):
    return TOKEN_HALO if tiles_per_seq > 1 else 0


def _token_specs(T, tm, tiles_per_seq):
    tok = pl.BlockSpec((tm, D_MODEL), lambda i: (i, 0))
    if tiles_per_seq == 1:
        return [tok]
    per_tile = tm // TOKEN_HALO
    last = T // TOKEN_HALO - 1
    prev = pl.BlockSpec((TOKEN_HALO, D_MODEL), lambda i: (jnp.maximum(i * per_tile - 1, 0), 0))
    nxt = pl.BlockSpec((TOKEN_HALO, D_MODEL), lambda i: (jnp.minimum((i + 1) * per_tile, last), 0))
    return [prev, tok, nxt]


def _modulated_parts(x_refs, h_scr, g, shift, scale, tiles_per_seq, n_parts=LEAD_PARTS):
    halo = _halo_rows(tiles_per_seq)
    x_ref = x_refs[len(x_refs) // 2]
    tm = x_ref.shape[0]
    step = tm // n_parts
    parts = []
    for p in range(n_parts):
        lo, hi = halo + p * step, halo + (p + 1) * step
        piece = _modulate(x_ref[p * step:(p + 1) * step, :], g, shift, scale).astype(BF16)
        if halo and p == 0:
            pos = pl.program_id(0) % tiles_per_seq
            edge = jnp.where(pos > 0, _modulate(x_refs[0][...], g, shift, scale), 0.0).astype(BF16)
            piece, lo = jnp.concatenate([edge, piece], axis=0), 0
        if halo and p == n_parts - 1:
            pos = pl.program_id(0) % tiles_per_seq
            edge = jnp.where(pos < tiles_per_seq - 1, _modulate(x_refs[2][...], g, shift, scale), 0.0).astype(BF16)
            piece, hi = jnp.concatenate([piece, edge], axis=0), tm + 2 * halo
        h_scr[lo:hi, :] = piece
        parts.append(piece)
    return parts


def _dot_rows(parts, w):
    return jnp.concatenate([_dot(p, w) for p in parts], axis=0)


def _seq_end_masks(tm, cols, seq_len, tiles_per_seq):
    if tiles_per_seq > 1:
        return None
    seq_row = lax.broadcasted_iota(jnp.int32, (tm, cols), 0) % seq_len
    return seq_row != 0, seq_row != seq_len - 1


def _token_conv3(z, w, tm, halo, masks):
    rows = z.shape[0]
    mid = slice(halo, halo + tm)
    dn = pltpu.roll(z, 1, 0)[mid]
    up = pltpu.roll(z, rows - 1, 0)[mid]
    if masks is not None:
        dn = jnp.where(masks[0], dn, 0.0)
        up = jnp.where(masks[1], up, 0.0)
    return dn * w[0:1] + z[mid] * w[1:2] + up * w[2:3]


def _adaln_kernel(cond_ref, w_ref, b_ref, *rest, side_cast):
    (o_ref,), _ = _split_refs(rest, 1, side_cast)
    a = jax.nn.silu(cond_ref[...]).astype(BF16)
    rows = _dot(a, w_ref[0].astype(BF16)) + b_ref[0]
    for r in range(COND_ROWS):
        o_ref[r] = rows[r:r + 1, :]


def _adaln(cond, ada_w, ada_b, side=()):
    n_out = 6 * D_MODEL
    n_col = n_out // ADA_TN
    return _host_call(
        _adaln_kernel,
        grid=(DEPTH, n_col),
        in_specs=[pl.BlockSpec((COND_ROWS, D_MODEL), lambda l, j: (0, 0)),
                  pl.BlockSpec((1, D_MODEL, ADA_TN), lambda l, j: (l, 0, j)),
                  pl.BlockSpec((1, 1, ADA_TN), lambda l, j: (l, 0, j))],
        args=[cond, ada_w, ada_b.reshape(DEPTH, 1, n_out)],
        out_specs=[pl.BlockSpec((COND_ROWS, 1, ADA_TN), lambda l, j: (l, 0, j))],
        out_shape=[jax.ShapeDtypeStruct((DEPTH * COND_ROWS, 1, n_out), F32)],
        scratch_shapes=[], name="adaln", side=side, step_of=lambda l, j: l * n_col + j)


def _head_rms(z, gain):
    n = z.shape[1]
    w = min(n, MXU_WIDTH)
    r = lax.broadcasted_iota(jnp.int32, (w, w), 0) // HEAD_DIM
    c = lax.broadcasted_iota(jnp.int32, (w, w), 1) // HEAD_DIM
    ones = (r == c).astype(BF16)
    sq = (z * z).astype(BF16)
    ss = jnp.concatenate([_dot(sq[:, k:k + w], ones) for k in range(0, n, w)], axis=1)
    return z * lax.rsqrt(ss * (1.0 / HEAD_DIM) + EPS) * gain


def _rope(z, cos, sin_lo, sin_hi):
    outs = []
    for k in range(0, z.shape[1], LANES):
        blk = z[:, k:k + LANES]
        outs.append(blk * cos + pltpu.roll(blk, LANES - 16, 1) * sin_lo + pltpu.roll(blk, 16, 1) * sin_hi)
    return jnp.concatenate(outs, axis=1)


def _in_even_kernel(*refs, rope, tiles_per_seq, seq_len, cache_out, side_cast):
    halo = _halo_rows(tiles_per_seq)
    n_x = 3 if halo else 1
    x_refs, refs = refs[:n_x], refs[n_x:]
    mod_ref, g_ref, w_ref, qg_ref, kg_ref, cw_ref = refs[:6]
    refs = refs[6:]
    if rope:
        cos_ref, slo_ref, shi_ref = refs[:3]
        refs = refs[3:]
    outs, (h_scr,) = _split_refs(refs, 6 if cache_out else 4, side_cast)
    q_ref, k_ref, v_ref, sc_ref = outs[:4]
    tm = q_ref.shape[0]
    parts = _modulated_parts(x_refs, h_scr, g_ref[...], _mod_chunk(mod_ref, 0), _mod_chunk(mod_ref, 1),
                             tiles_per_seq)
    c0 = 0
    c1 = c0 + ATTN_WIDTH
    c2 = c1 + 2 * KV_WIDTH
    c3 = c2 + CONV_WIDTH
    c4 = c3 + CONV_WIDTH
    c5 = c4 + CONV_WIDTH
    zqkv = _dot_rows(parts, w_ref[:, c0:c2])[halo:halo + tm]
    zq, zkv = zqkv[:, :ATTN_WIDTH], zqkv[:, ATTN_WIDTH:]
    h_all = h_scr[...]
    h = h_scr[halo:halo + tm, :]
    q = _head_rms(zq, qg_ref[...] * (HEAD_DIM ** -0.5 * LOG2_E))
    k = _head_rms(zkv[:, :KV_WIDTH], kg_ref[...])
    if rope:
        cos, slo, shi = cos_ref[...], slo_ref[...], shi_ref[...]
        q = _rope(q, cos, slo, shi)
        k = _rope(k, cos, slo, shi)
    v = zkv[:, KV_WIDTH:]
    q_ref[...] = q
    k_ref[...] = k
    v_ref[...] = v
    if cache_out:
        for cache_ref, rows in zip(outs[4:], (k, v)):
            for s in range(tm // seq_len):
                seq_t = jnp.concatenate([rows[j:j + BLOCK, :].T for j in range(s * seq_len, (s + 1) * seq_len, BLOCK)],
                                        axis=1)
                cache_ref[s, 0] = seq_t.reshape(N_KV_HEADS, HEAD_DIM, seq_len)
    ch = _dot(h_all, w_ref[:, c3:c4]) * _dot(h_all, w_ref[:, c4:c5])
    conv = _token_conv3(ch, cw_ref[...], tm, halo, _seq_end_masks(tm, CONV_WIDTH, seq_len, tiles_per_seq))
    sc_ref[...] = (_dot(h, w_ref[:, c2:c3]) * conv).astype(BF16)


def _in_even(x, mod_l, mod_row, g, w_bf, qg, kg, conv_w, rope_tabs, tm, seq_len, side=(), cache_out=False):
    T = x.shape[0]
    n_in = w_bf.shape[1]
    tiles_per_seq = max(seq_len // tm, 1)
    assert tm % seq_len == 0 or seq_len % tm == 0
    tok = lambda w: pl.BlockSpec((tm, w), lambda i: (i, 0))
    x_specs = _token_specs(T, tm, tiles_per_seq)
    in_specs = x_specs + [_mod_spec(mod_row), _resident((1, D_MODEL)), _resident((D_MODEL, n_in)),
                          _resident((1, ATTN_WIDTH)), _resident((1, KV_WIDTH)), _resident((3, CONV_WIDTH))]
    args = [x] * len(x_specs) + [mod_l, g, w_bf, qg, kg, conv_w]
    if rope_tabs is not None:
        in_specs += [pl.BlockSpec((tm, LANES), lambda i: (i % tiles_per_seq, 0))] * 3
        args += list(rope_tabs)
    outs = ((ATTN_WIDTH, F32), (KV_WIDTH, F32), (KV_WIDTH, F32), (CONV_WIDTH, BF16))
    out_specs = [tok(w) for w, _ in outs]
    out_shape = [jax.ShapeDtypeStruct((T, w), dt) for w, dt in outs]
    if cache_out:
        n_seq = tm // seq_len
        cache_block = (n_seq, 1, N_KV_HEADS, HEAD_DIM, seq_len)
        out_specs += [pl.BlockSpec(cache_block, lambda i: (i, 0, 0, 0, 0))] * 2
        out_shape += [jax.ShapeDtypeStruct((T // seq_len,) + cache_block[1:], F32)] * 2
    scratch = [pltpu.VMEM((tm + 2 * _halo_rows(tiles_per_seq), D_MODEL), BF16)]
    return _host_call(
        functools.partial(_in_even_kernel, rope=rope_tabs is not None, tiles_per_seq=tiles_per_seq, seq_len=seq_len,
                          cache_out=cache_out),
        grid=(T // tm,), in_specs=in_specs, args=args, out_specs=out_specs, out_shape=out_shape,
        scratch_shapes=scratch, name="in_even", side=side, step_of=lambda i: i)


def _stack_heads(q):
    lane_lo = lax.broadcasted_iota(jnp.int32, (BLOCK, LANES), 1) < HEAD_DIM
    heads = []
    for pair in range(N_HEADS // 2):
        kv = (2 * pair) // GQA_GROUP
        qp = q[:, pair * LANES:(pair + 1) * LANES]
        qr = pltpu.roll(qp, HEAD_DIM, 1)
        for half in range(2):
            src = qp if half == kv else qr
            heads.append(jnp.where(lane_lo, src, 0.0) if kv == 0 else jnp.where(lane_lo, 0.0, src))
    return jnp.concatenate(heads, axis=0).astype(BF16)


def _transposed_values(v_blocks, vt_tail=None):
    cols = [v[j:j + BLOCK].T for v in v_blocks for j in range(0, v.shape[0], BLOCK)]
    vt = jnp.concatenate(cols + ([] if vt_tail is None else [vt_tail]), axis=1)
    return jnp.concatenate([vt.astype(BF16), jnp.ones((SUM_ROWS, vt.shape[1]), BF16)], axis=0)


def _scores(q, kcat, cap):
    st = lax.dot_general(kcat, _stack_heads(q), (((1,), (1,)), ((), ())), preferred_element_type=F32)
    if cap is None:
        return st
    n_band = cap.shape[0]
    capped = jnp.minimum(st[:n_band], jnp.concatenate([cap] * N_HEADS, axis=1))
    return jnp.concatenate([capped, st[n_band:]], axis=0)


def _weighted_values(st, vt, sink_row):
    m = jnp.maximum(jnp.max(st, axis=0, keepdims=True), sink_row)
    pt = jnp.exp2(st - m).astype(BF16)
    ot = _dot(vt, pt)
    denom = ot[KV_WIDTH:KV_WIDTH + 1] + jnp.exp2(sink_row - m)
    ot = ot[:KV_WIDTH] / denom
    pairs = []
    for pair in range(N_HEADS // 2):
        kv = (2 * pair) // GQA_GROUP
        dims = slice(kv * HEAD_DIM, (kv + 1) * HEAD_DIM)
        both = [ot[dims, (2 * pair + half) * BLOCK:(2 * pair + half + 1) * BLOCK] for half in range(2)]
        pairs.append(jnp.concatenate(both, axis=0).T)
    return jnp.concatenate(pairs, axis=1)


def _attend_blocks(n_blocks, scores_of, values_of, sink_row, attn_scr):
    st = scores_of(0)
    for j in range(n_blocks):
        st_next = scores_of(j + 1) if j + 1 < n_blocks else None
        attn_scr[j * BLOCK:(j + 1) * BLOCK, :] = _weighted_values(st, values_of(j), sink_row).astype(BF16)
        st = st_next


def _project_out(attn_scr, sc_ref, x_ref, mod_ref, w_ref, o_ref):
    mix = _dot(attn_scr[...], w_ref[:ATTN_WIDTH, :]) + _dot(sc_ref[...], w_ref[ATTN_WIDTH:, :])
    o_ref[...] = x_ref[...] + _mod_chunk(mod_ref, 2) * mix


def _attn_ctx_kernel(sink_ref, q_ref, k_ref, v_ref, sc_ref, x_ref, mod_ref, w_ref, *rest, side_cast):
    (o_ref,), (attn_scr,) = _split_refs(rest, 1, side_cast)
    per_seq = SEQ // BLOCK
    kcats = [k_ref[s * SEQ:(s + 1) * SEQ, :].astype(BF16) for s in range(ATTN_CTX_SEQS)]
    vts = [_transposed_values([v_ref[s * SEQ:(s + 1) * SEQ, :]]) for s in range(ATTN_CTX_SEQS)]
    sink_row = sink_ref[...] * LOG2_E
    _attend_blocks(ATTN_CTX_SEQS * per_seq,
                   lambda j: _scores(q_ref[j * BLOCK:(j + 1) * BLOCK, :], kcats[j // per_seq], None),
                   lambda j: vts[j // per_seq], sink_row, attn_scr)
    _project_out(attn_scr, sc_ref, x_ref, mod_ref, w_ref, o_ref)


def _attn_ctx(q, k, v, sc, x, mod_l, mod_base, sink_row, w_out_bf, side=()):
    T = q.shape[0]
    tq = ATTN_CTX_SEQS * SEQ
    tok = lambda w: pl.BlockSpec((tq, w), lambda i: (i, 0))
    return _host_call(
        _attn_ctx_kernel,
        grid=(T // tq,),
        in_specs=[_resident((1, N_HEADS * BLOCK)), tok(ATTN_WIDTH), tok(KV_WIDTH), tok(KV_WIDTH),
                  tok(CONV_WIDTH), tok(D_MODEL), _mod_spec(lambda i: mod_base + CTX_ROW),
                  _resident((ATTN_WIDTH + CONV_WIDTH, D_MODEL))],
        args=[sink_row, q, k, v, sc, x, mod_l, w_out_bf],
        out_specs=[tok(D_MODEL)], out_shape=[jax.ShapeDtypeStruct((T, D_MODEL), F32)],
        scratch_shapes=[pltpu.VMEM((tq, ATTN_WIDTH), BF16)], name="attn_ctx", side=side, step_of=lambda i: i)


def _band_cap(has_prev, has_next):
    c = lax.broadcasted_iota(jnp.int32, (3 * BLOCK, BLOCK), 0)
    r = lax.broadcasted_iota(jnp.int32, (3 * BLOCK, BLOCK), 1)
    first_prev = r + jnp.where(has_prev, 0, BLOCK)
    last_next = r + 2 * BLOCK - jnp.where(has_next, 0, BLOCK)
    masked = ((c < BLOCK) & (c < first_prev)) | ((c >= 2 * BLOCK) & (c > last_next))
    return jnp.where(masked, NEG_INF, SCORE_CAP)


def _attn_win_kernel(sink_ref, q_ref, kp_ref, kc_ref, kn_ref, vp_ref, vc_ref, vn_ref, ckt_ref, cvt_ref,
                     sc_ref, x_ref, mod_ref, w_ref, *rest, n_steps, side_cast):
    (o_ref,), (attn_scr,) = _split_refs(rest, 1, side_cast)
    i = pl.program_id(1)
    kc, vc = kc_ref[...], vc_ref[...]
    inner = range(0, ATTN_Q_BLOCKS * BLOCK, BLOCK)
    k_blocks = [kp_ref[...]] + [kc[j:j + BLOCK] for j in inner] + [kn_ref[...]]
    v_blocks = [vp_ref[...]] + [vc[j:j + BLOCK] for j in inner] + [vn_ref[...]]
    sink_row = sink_ref[...] * LOG2_E
    ckt = ckt_ref[0]
    ck = jnp.concatenate([ckt[:, j:j + BLOCK].T for j in range(0, PAST_LEN, BLOCK)], axis=0)

    def scores_of(j):
        kcat = jnp.concatenate(k_blocks[j:j + 3] + [ck], axis=0).astype(BF16)
        cap = _band_cap(i > 0 if j == 0 else True, i < n_steps - 1 if j == ATTN_Q_BLOCKS - 1 else True)
        return _scores(q_ref[j * BLOCK:(j + 1) * BLOCK, :], kcat, cap)

    _attend_blocks(ATTN_Q_BLOCKS, scores_of, lambda j: _transposed_values(v_blocks[j:j + 3], cvt_ref[0]),
                   sink_row, attn_scr)
    _project_out(attn_scr, sc_ref, x_ref, mod_ref, w_ref, o_ref)


def _attn_win(q, k, v, ck, cv, sc, x, mod_l, mod_base, sink_row, w_out_bf, side=()):
    T = q.shape[0]
    tq = ATTN_Q_BLOCKS * BLOCK
    n_steps = DEC_SEQ // tq
    nb = DEC_SEQ // BLOCK
    cur = lambda w: pl.BlockSpec((tq, w), lambda b, i: (b * n_steps + i, 0))
    prev = pl.BlockSpec((BLOCK, KV_WIDTH), lambda b, i: (b * nb + jnp.maximum(ATTN_Q_BLOCKS * i - 1, 0), 0))
    nxt = pl.BlockSpec((BLOCK, KV_WIDTH),
                       lambda b, i: (b * nb + jnp.minimum(ATTN_Q_BLOCKS * (i + 1), nb - 1), 0))
    ctx = pl.BlockSpec((1, KV_WIDTH, PAST_LEN), lambda b, i: (b, 0, 0))
    in_specs = [_resident((1, N_HEADS * BLOCK)), cur(ATTN_WIDTH),
                prev, cur(KV_WIDTH), nxt, prev, cur(KV_WIDTH), nxt, ctx, ctx,
                cur(CONV_WIDTH), cur(D_MODEL), _mod_spec(lambda b, i: mod_base + b),
                _resident((ATTN_WIDTH + CONV_WIDTH, D_MODEL))]
    args = [sink_row, q, k, k, k, v, v, v, ck, cv, sc, x, mod_l, w_out_bf]
    return _host_call(
        functools.partial(_attn_win_kernel, n_steps=n_steps),
        grid=(T // DEC_SEQ, n_steps), in_specs=in_specs, args=args,
        out_specs=[cur(D_MODEL)], out_shape=[jax.ShapeDtypeStruct((T, D_MODEL), F32)],
        scratch_shapes=[pltpu.VMEM((tq, ATTN_WIDTH), BF16)], name="attn_win", side=side,
        step_of=lambda b, i: b * n_steps + i)


def _gelu_tanh(x):
    k = 0.7978845608028654
    half = 0.5 * x
    return half + half * jnp.tanh(x * (k + (k * 0.044715) * (x * x)))


def _gmlp_kernel(x_ref, mod_ref, g_ref, win_ref, vg_ref, ws_ref, bs_ref, wout_ref, *rest, side_cast):
    (o_ref,), (gated_scr, h_scr) = _split_refs(rest, 1, side_cast)
    n_chunks = x_ref.shape[0] // CHUNK
    parts = _modulated_parts([x_ref], h_scr, g_ref[...], _mod_chunk(mod_ref, 0), _mod_chunk(mod_ref, 1), 1,
                             GMLP_LEAD_PARTS)
    v = _gelu_tanh(_dot_rows(parts, win_ref[:, GMLP_WIDTH:]))
    ms = jnp.mean(v * v, axis=-1, keepdims=True)
    v = (v * lax.rsqrt(ms + EPS) * vg_ref[...]).astype(BF16)
    h = h_scr[...]
    for first in range(0, GMLP_WIDTH, GMLP_U_COLS):
        u = _gelu_tanh(_dot(h, win_ref[:, first:first + GMLP_U_COLS]))
        for grp in range(first // GMLP_GROUP_DIM, (first + GMLP_U_COLS) // GMLP_GROUP_DIM):
            lanes = slice(grp * GMLP_GROUP_DIM, (grp + 1) * GMLP_GROUP_DIM)
            u_lanes = slice(lanes.start - first, lanes.stop - first)
            rhs = jnp.concatenate([v[n * CHUNK:(n + 1) * CHUNK, lanes] for n in range(n_chunks)], axis=1)
            s = _dot(ws_ref[grp], rhs)
            for n in range(n_chunks):
                rows = slice(n * CHUNK, (n + 1) * CHUNK)
                s_n = s[:, n * GMLP_GROUP_DIM:(n + 1) * GMLP_GROUP_DIM] + bs_ref[grp]
                gated_scr[rows, lanes] = (u[rows, u_lanes] * s_n).astype(BF16)
    o_ref[...] = x_ref[...] + _mod_chunk(mod_ref, 2) * _dot(gated_scr[...], wout_ref[...])


def _gmlp(x, mod_l, mod_row, g, win_bf, vg, ws_bf, bs_full, wout_bf, tm, side=()):
    T = x.shape[0]
    tok = pl.BlockSpec((tm, D_MODEL), lambda i: (i, 0))
    in_specs = [tok, _mod_spec(mod_row),
                _resident((1, D_MODEL)), _resident((D_MODEL, 2 * GMLP_WIDTH)), _resident((1, GMLP_WIDTH)),
                _resident((GMLP_GROUPS, CHUNK, CHUNK)), _resident((GMLP_GROUPS, CHUNK, GMLP_GROUP_DIM)),
                _resident((GMLP_WIDTH, D_MODEL))]
    args = [x, mod_l, g, win_bf, vg, ws_bf, bs_full, wout_bf]
    return _host_call(
        _gmlp_kernel, grid=(T // tm,), in_specs=in_specs, args=args,
        out_specs=[tok], out_shape=[jax.ShapeDtypeStruct((T, D_MODEL), F32)],
        scratch_shapes=[pltpu.VMEM((tm, GMLP_WIDTH), BF16), pltpu.VMEM((tm, D_MODEL), BF16)],
        name="gmlp", side=side, step_of=lambda i: i)


def _ffn_kernel(*refs, tiles_per_seq, seq_len, side_cast):
    halo = _halo_rows(tiles_per_seq)
    n_x = 3 if halo else 1
    x_refs, refs = refs[:n_x], refs[n_x:]
    mod_ref, g_ref, wup_ref, cw_ref, wdn_ref = refs[:5]
    (o_ref,), (act_scr, h_scr) = _split_refs(refs[5:], 1, side_cast)
    tm = o_ref.shape[0]
    parts = _modulated_parts(x_refs, h_scr, g_ref[...], _mod_chunk(mod_ref, 3), _mod_chunk(mod_ref, 4),
                             tiles_per_seq)
    for j in range(0, D_FF, FFN_COLS):
        width = min(FFN_COLS, D_FF - j)
        masks = _seq_end_masks(tm, width, seq_len, tiles_per_seq)
        gate_cols = slice(j, j + width)
        val_cols = slice(D_FF + j, D_FF + j + width)
        if j == 0:
            up = lambda cols: _dot_rows(parts, wup_ref[:, cols])
        else:
            h = h_scr[...]
            up = lambda cols: _dot(h, wup_ref[:, cols])
        zg = _token_conv3(up(gate_cols), cw_ref[0, :, gate_cols], tm, halo, masks)
        zv = _token_conv3(up(val_cols), cw_ref[0, :, val_cols], tm, halo, masks)
        act_scr[:, gate_cols] = (jax.nn.silu(zg) * zv).astype(BF16)
    x = x_refs[n_x // 2][...]
    o_ref[...] = x + _mod_chunk(mod_ref, 5) * _dot(act_scr[...], wdn_ref[...])


def _ffn(x, mod_l, mod_row, g, wup_bf, conv_w, wdn_bf, layer, tm, seq_len, side=()):
    T = x.shape[0]
    tiles_per_seq = max(seq_len // tm, 1)
    assert tm % seq_len == 0 or seq_len % tm == 0
    x_specs = _token_specs(T, tm, tiles_per_seq)
    in_specs = x_specs + [_mod_spec(mod_row), _resident((1, D_MODEL)), _resident(wup_bf.shape),
                          _layer_slab(conv_w.shape, layer), _resident(wdn_bf.shape)]
    scratch = [pltpu.VMEM((tm, D_FF), BF16), pltpu.VMEM((tm + 2 * _halo_rows(tiles_per_seq), D_MODEL), BF16)]
    return _host_call(
        functools.partial(_ffn_kernel, tiles_per_seq=tiles_per_seq, seq_len=seq_len),
        grid=(T // tm,), in_specs=in_specs, args=[x] * len(x_specs) + [mod_l, g, wup_bf, conv_w, wdn_bf],
        out_specs=[pl.BlockSpec((tm, D_MODEL), lambda i: (i, 0))], out_shape=[jax.ShapeDtypeStruct((T, D_MODEL), F32)],
        scratch_shapes=scratch, name="conv_ffn", side=side, step_of=lambda i: i)


def _rope_tables(n_tokens):
    t = np.arange(n_tokens)
    n_freq = HEAD_DIM // 4
    inv = (ROPE_BASE ** (-np.arange(n_freq, dtype=np.float32) / n_freq)).astype(np.float32)
    row_ang = (t // GRID_W).astype(np.float32)[:, None] * inv
    col_ang = (t % GRID_W).astype(np.float32)[:, None] * inv
    ang = np.concatenate([row_ang, row_ang, col_ang, col_ang], axis=1)
    ang = np.tile(ang, (1, LANES // HEAD_DIM)).astype(np.float64)
    first = (np.arange(LANES) % (2 * n_freq)) < n_freq
    cos, sin = np.cos(ang), np.sin(ang)
    tables = (cos, np.where(first, -sin, 0.0), np.where(first, 0.0, sin))
    return tuple(jnp.asarray(tab.astype(np.float32)) for tab in tables)


class _Bf16Weights:
    def __init__(self, **stacks):
        self.stacks, self.ready = stacks, {}

    def hosted(self, keys, call):
        due = [(name, slab) for name, slab in keys
               if slab < self.stacks[name].shape[0] and (name, slab) not in self.ready]
        outs = call([(self.stacks[name], slab) for name, slab in due])
        for key, cast in zip(due, outs[len(outs) - len(due):]):
            self.ready[key] = cast
        return outs[:len(outs) - len(due)]

    def get(self, name, slab):
        if (name, slab) not in self.ready:
            self.ready[(name, slab)] = _to_bf16(self.stacks[name][slab:slab + 1])[0]
        return self.ready[(name, slab)]


def kernel(x_prompt, x_sample, cache_k, cache_v, c, c_ctx, ada_w, ada_b, norm_mix_g, norm_ffn_g, w_in_even,
           q_norm_g, k_norm_g, sink_logit, short_conv_w, w_out_even, w_in_odd, gmlp_norm_g, w_spatial,
           b_spatial, w_out_odd, w_up, ffn_conv_w, w_down):
    n_p, n_s = BATCH * SEQ, DEC_BATCH * DEC_SEQ
    xp = x_prompt.reshape(n_p, D_MODEL)
    xs = x_sample.reshape(n_s, D_MODEL)
    cond = jnp.concatenate([c, c_ctx[None, :], jnp.zeros((COND_ROWS - DEC_BATCH - 1, D_MODEL), F32)], axis=0)
    bf = _Bf16Weights(in_even=w_in_even, out_even=w_out_even, in_odd=w_in_odd, out_odd=w_out_odd,
                      spatial=w_spatial.reshape(N_ODD, GMLP_GROUPS * CHUNK, CHUNK), up=w_up, down=w_down)

    mod_l, = bf.hosted([("in_even", 0)], lambda side: _adaln(cond, ada_w, ada_b, side))
    rope_tabs = _rope_tables(DEC_SEQ)

    tm_p, tm_s = 1024, 1024
    row_vec = lambda a: a.reshape(1, -1)
    new_k, new_v = [], []
    for l in range(DEPTH):
        base = l * COND_ROWS
        row_p = lambda i, base=base: base + CTX_ROW
        row_s = lambda i, base=base: base + i // (DEC_SEQ // tm_s)
        g_mix = row_vec(norm_mix_g[l])
        nxt = (l + 1) // 2
        if l % 2 == 0:
            e = l // 2
            w_in = bf.get("in_even", e)
            qg = row_vec(jnp.tile(q_norm_g[e], N_HEADS))
            kg = row_vec(jnp.tile(k_norm_g[e], N_KV_HEADS))
            sink = jnp.repeat(sink_logit[e], BLOCK).reshape(1, N_HEADS * BLOCK)
            cw = short_conv_w[e]
            qp, kp, vp, scp, k_layer, v_layer = bf.hosted([("out_even", e)], lambda side: _in_even(
                xp, mod_l, row_p, g_mix, w_in, qg, kg, cw, None, tm_p, SEQ, side, cache_out=True))
            new_k.append(k_layer)
            new_v.append(v_layer)
            w_out = bf.get("out_even", e)
            odd_next = [("in_odd", nxt), ("out_odd", nxt), ("spatial", nxt)]
            xp, = bf.hosted(odd_next, lambda side: _attn_ctx(qp, kp, vp, scp, xp, mod_l, base, sink, w_out, side))
            qs, ks, vs, scs = bf.hosted([("down", l), ("down", l + 1)], lambda side: _in_even(
                xs, mod_l, row_s, g_mix, w_in, qg, kg, cw, rope_tabs, tm_s, DEC_SEQ, side))
            ck = jnp.transpose(cache_k[:, e], (0, 2, 3, 1)).reshape(DEC_BATCH, KV_WIDTH, PAST_LEN)
            cv = jnp.transpose(cache_v[:, e], (0, 2, 3, 1)).reshape(DEC_BATCH, KV_WIDTH, PAST_LEN)
            xs, = bf.hosted([("up", l), ("up", l + 1)], lambda side: _attn_win(
                qs, ks, vs, ck, cv, scs, xs, mod_l, base, sink, w_out, side))
        else:
            o = l // 2
            w_in, w_out = bf.get("in_odd", o), bf.get("out_odd", o)
            vg = row_vec(gmlp_norm_g[o])
            ws = bf.get("spatial", o).reshape(GMLP_GROUPS, CHUNK, CHUNK)
            bs_full = jnp.broadcast_to(b_spatial[o][:, :, None], (GMLP_GROUPS, CHUNK, GMLP_GROUP_DIM))
            xp, = bf.hosted([("down", l)], lambda side: _gmlp(
                xp, mod_l, row_p, g_mix, w_in, vg, ws, bs_full, w_out, tm_p, side))
            xs, = bf.hosted([("up", l)], lambda side: _gmlp(
                xs, mod_l, row_s, g_mix, w_in, vg, ws, bs_full, w_out, tm_s, side))
        g_ffn = row_vec(norm_ffn_g[l])
        w_up_l, w_down_l = bf.get("up", l), bf.get("down", l)
        xp, = _ffn(xp, mod_l, row_p, g_ffn, w_up_l, ffn_conv_w, w_down_l, l, tm_p, SEQ)
        xs, = _ffn(xs, mod_l, row_s, g_ffn, w_up_l, ffn_conv_w, w_down_l, l, tm_s, DEC_SEQ)
    per_layer = lambda parts: jnp.transpose(parts[0] if len(parts) == 1 else jnp.concatenate(parts, axis=1),
                                            (0, 1, 4, 2, 3))
    return (xp.reshape(BATCH, SEQ, D_MODEL), xs.reshape(DEC_BATCH, DEC_SEQ, D_MODEL),
            per_layer(new_k), per_layer(new_v))
```
